```python
import math
import jax, jax.numpy as jnp
from jax import lax
import numpy as np

D_MODEL = 2048
BATCH = 8
SEQ = 2048
DEPTH = 2

HEAD_DIM = 64
BLOCK = 128
A_Q_HEADS = 12
A_KV_HEADS = 4
A_GROUP = A_Q_HEADS // A_KV_HEADS
WINDOW = 128
B_HEADS = 4
C_HEADS = 12
A_WIDTH = A_Q_HEADS * HEAD_DIM
B_WIDTH = B_HEADS * 2 * HEAD_DIM
C_WIDTH = C_HEADS * HEAD_DIM
N_BRANCH = 3
IN_SPLITS = (A_WIDTH, A_KV_HEADS * HEAD_DIM, A_KV_HEADS * HEAD_DIM,
             B_WIDTH, B_WIDTH, B_WIDTH,
             C_WIDTH, C_WIDTH, C_WIDTH, C_HEADS,
             N_BRANCH * D_MODEL)
D_IN = sum(IN_SPLITS)
D_FF = ((8 * D_MODEL // 3 + 255) // 256) * 256
N_EXPERTS = 8
TOP_K = 2
D_FF_EXPERT = 7 * D_MODEL // 2
N_DENSE = (DEPTH + 1) // 2
N_MOE = DEPTH // 2
RMS_EPS = 1e-6
NEG_INF = -1e30

kernel_name = "hybrid_gated_swa_diff_fox_moe"


def rmsnorm(x, g):
    xf = x.astype(jnp.float32)
    xf = xf * lax.rsqrt(jnp.mean(xf * xf, axis=-1, keepdims=True) + RMS_EPS)
    return xf.astype(x.dtype) * g


def alibi_slopes(n):
    return jnp.exp2(-8.0 * (jnp.arange(n, dtype=jnp.float32) + 1.0) / n)


def split_cols(z, sizes):
    offs = np.cumsum(np.array(sizes))[:-1].tolist()
    return jnp.split(z, offs, axis=-1)


def window_attention_sinks(q, k, v, sinks, slopes):
    b, s, hq, d = q.shape
    hkv = k.shape[2]
    g = hq // hkv
    nb = s // BLOCK
    qb = q.reshape(b, nb, BLOCK, hkv, g, d)
    k_cur = k.reshape(b, nb, BLOCK, hkv, d)
    v_cur = v.reshape(b, nb, BLOCK, hkv, d)
    pad = ((0, 0), (1, 0), (0, 0), (0, 0), (0, 0))
    kb = jnp.concatenate([jnp.pad(k_cur, pad)[:, :-1], k_cur], axis=2)
    vb = jnp.concatenate([jnp.pad(v_cur, pad)[:, :-1], v_cur], axis=2)
    logits = jnp.einsum('bnqhgd,bnkhd->bnhgqk', qb, kb).astype(jnp.float32) * (d ** -0.5)
    dist = jnp.arange(BLOCK)[:, None] + BLOCK - jnp.arange(2 * BLOCK)[None, :]
    valid = (dist >= 0) & (dist < WINDOW)
    real = (jnp.arange(nb)[:, None] > 0) | (jnp.arange(2 * BLOCK)[None, :] >= BLOCK)
    mask = valid[None, :, :] & real[:, None, :]
    logits = logits - slopes.reshape(hkv, g, 1, 1) * dist.astype(jnp.float32)
    logits = jnp.where(mask[None, :, None, None], logits, NEG_INF)
    sink = jnp.broadcast_to(sinks.astype(jnp.float32).reshape(hkv, g, 1, 1), logits.shape[:-1] + (1,))
    p = jax.nn.softmax(jnp.concatenate([logits, sink], axis=-1), axis=-1)[..., :-1]
    o = jnp.einsum('bnhgqk,bnkhd->bnqhgd', p.astype(v.dtype), vb)
    return o.reshape(b, s, hq * d)


def diff_attention(q, k, v, lam, slopes):
    b, s, h, _, d = q.shape
    nb = s // BLOCK
    kpos = jnp.arange(s)

    def one_block(i):
        qi = lax.dynamic_slice_in_dim(q, i * BLOCK, BLOCK, axis=1)
        logits = jnp.einsum('bqhmd,bkhmd->bhmqk', qi, k).astype(jnp.float32) * (d ** -0.5)
        dist = (i * BLOCK + jnp.arange(BLOCK))[:, None] - kpos[None, :]
        logits = logits - slopes[:, None, None, None] * dist.astype(jnp.float32)
        logits = jnp.where(dist >= 0, logits, NEG_INF)
        p = jax.nn.softmax(logits, axis=-1)
        a = p[:, :, 0] - lam * p[:, :, 1]
        return jnp.einsum('bhqk,bkhe->bqhe', a.astype(v.dtype), v)

    o = lax.map(one_block, jnp.arange(nb))
    return o.transpose(1, 0, 2, 3, 4).reshape(b, s, h, 2 * d)


def forgetting_attention(q, k, v, log_f):
    b, s, h, d = q.shape
    nb = s // BLOCK
    c = jnp.cumsum(log_f, axis=1).transpose(0, 2, 1)
    kpos = jnp.arange(s)

    def one_block(i):
        qi = lax.dynamic_slice_in_dim(q, i * BLOCK, BLOCK, axis=1)
        ci = lax.dynamic_slice_in_dim(c, i * BLOCK, BLOCK, axis=2)
        logits = jnp.einsum('bqhd,bkhd->bhqk', qi, k).astype(jnp.float32) * (d ** -0.5)
        logits = logits + ci[:, :, :, None] - c[:, :, None, :]
        causal = (i * BLOCK + jnp.arange(BLOCK))[:, None] >= kpos[None, :]
        logits = jnp.where(causal, logits, NEG_INF)
        p = jax.nn.softmax(logits, axis=-1)
        return jnp.einsum('bhqk,bkhd->bqhd', p.astype(v.dtype), v)

    o = lax.map(one_block, jnp.arange(nb))
    return o.transpose(1, 0, 2, 3, 4).reshape(b, s, h * d)


def token_mixer(h, w_in, b_forget, b_gate, sinks, lq1, lk1, lq2, lk2, subln,
                w_br_a, w_br_b, w_br_c, w_out, layer_idx):
    b, s, _ = h.shape
    z = h @ w_in
    qa, ka, va, qb, kb, vb, qc, kc, vc, fc, gt = split_cols(z, IN_SPLITS)
    oa = window_attention_sinks(qa.reshape(b, s, A_Q_HEADS, HEAD_DIM),
                                ka.reshape(b, s, A_KV_HEADS, HEAD_DIM),
                                va.reshape(b, s, A_KV_HEADS, HEAD_DIM),
                                sinks, alibi_slopes(A_Q_HEADS))
    lam_init = 0.8 - 0.6 * math.exp(-0.3 * layer_idx)
    lam = (jnp.exp(jnp.sum(lq1.astype(jnp.float32) * lk1.astype(jnp.float32)))
           - jnp.exp(jnp.sum(lq2.astype(jnp.float32) * lk2.astype(jnp.float32))) + lam_init)
    ob = diff_attention(qb.reshape(b, s, B_HEADS, 2, HEAD_DIM),
                        kb.reshape(b, s, B_HEADS, 2, HEAD_DIM),
                        vb.reshape(b, s, B_HEADS, 2 * HEAD_DIM),
                        lam, alibi_slopes(B_HEADS))
    ob = (rmsnorm(ob, subln) * (1.0 - lam_init)).reshape(b, s, B_WIDTH)
    log_f = jax.nn.log_sigmoid((fc + b_forget).astype(jnp.float32))
    oc = forgetting_attention(qc.reshape(b, s, C_HEADS, HEAD_DIM),
                              kc.reshape(b, s, C_HEADS, HEAD_DIM),
                              vc.reshape(b, s, C_HEADS, HEAD_DIM), log_f)
    gates = jax.nn.sigmoid((gt + b_gate).astype(jnp.float32)).astype(h.dtype)
    gates = gates.reshape(b, s, N_BRANCH, D_MODEL)
    merged = (gates[:, :, 0] * (oa @ w_br_a) + gates[:, :, 1] * (ob @ w_br_b)
              + gates[:, :, 2] * (oc @ w_br_c))
    return merged @ w_out


def swiglu(h, wg, wu, wd):
    return (jax.nn.silu(h @ wg) * (h @ wu)) @ wd


def moe_swiglu(h, w_router, e_gate, e_up, e_down):
    b, s, dm = h.shape
    t = h.reshape(b * s, dm)
    logits = (t @ w_router).astype(jnp.float32)
    top_val, top_idx = lax.top_k(logits, TOP_K)
    top_w = jax.nn.softmax(top_val, axis=-1)
    gates = jnp.sum(jax.nn.one_hot(top_idx, N_EXPERTS, dtype=jnp.float32) * top_w[..., None], axis=1)
    y = jnp.zeros_like(t)
    for e in range(N_EXPERTS):
        y = y + gates[:, e:e + 1].astype(t.dtype) * swiglu(t, e_gate[e], e_up[e], e_down[e])
    return y.reshape(b, s, dm)


def setup_inputs(seed: int = 0) -> dict:
    key = jax.random.key(seed)
    ks = iter(jax.random.split(key, 32))

    def nrm(shape, scale):
        return jax.random.normal(next(ks), shape, jnp.float32) * scale

    def gain(shape):
        return 1.0 + nrm(shape, 0.02)

    return {
        "x": nrm((BATCH, SEQ, D_MODEL), 1.0),
        "mix_pre_norm": gain((DEPTH, D_MODEL)),
        "w_in": nrm((DEPTH, D_MODEL, D_IN), D_MODEL ** -0.5),
        "b_forget": 3.0 + nrm((DEPTH, C_HEADS), 0.5),
        "b_gate": nrm((DEPTH, N_BRANCH * D_MODEL), 0.02),
        "attn_sinks": nrm((DEPTH, A_Q_HEADS), 1.0),
        "lam_q1": nrm((DEPTH, HEAD_DIM), 0.1),
        "lam_k1": nrm((DEPTH, HEAD_DIM), 0.1),
        "lam_q2": nrm((DEPTH, HEAD_DIM), 0.1),
        "lam_k2": nrm((DEPTH, HEAD_DIM), 0.1),
        "diff_subln": gain((DEPTH, 2 * HEAD_DIM)),
        "w_br_a": nrm((DEPTH, A_WIDTH, D_MODEL), A_WIDTH ** -0.5),
        "w_br_b": nrm((DEPTH, B_WIDTH, D_MODEL), B_WIDTH ** -0.5),
        "w_br_c": nrm((DEPTH, C_WIDTH, D_MODEL), C_WIDTH ** -0.5),
        "w_out": nrm((DEPTH, D_MODEL, D_MODEL), D_MODEL ** -0.5),
        "mix_post_norm": gain((DEPTH, D_MODEL)),
        "ffn_pre_norm": gain((DEPTH, D_MODEL)),
        "ffn_post_norm": gain((DEPTH, D_MODEL)),
        "dense_w_gate": nrm((N_DENSE, D_MODEL, D_FF), D_MODEL ** -0.5),
        "dense_w_up": nrm((N_DENSE, D_MODEL, D_FF), D_MODEL ** -0.5),
        "dense_w_down": nrm((N_DENSE, D_FF, D_MODEL), D_FF ** -0.5),
        "w_router": nrm((N_MOE, D_MODEL, N_EXPERTS), D_MODEL ** -0.5),
        "moe_w_gate": nrm((N_MOE, N_EXPERTS, D_MODEL, D_FF_EXPERT), D_MODEL ** -0.5),
        "moe_w_up": nrm((N_MOE, N_EXPERTS, D_MODEL, D_FF_EXPERT), D_MODEL ** -0.5),
        "moe_w_down": nrm((N_MOE, N_EXPERTS, D_FF_EXPERT, D_MODEL), D_FF_EXPERT ** -0.5),
    }


def reference(x, mix_pre_norm, w_in, b_forget, b_gate, attn_sinks, lam_q1, lam_k1,
              lam_q2, lam_k2, diff_subln, w_br_a, w_br_b, w_br_c, w_out,
              mix_post_norm, ffn_pre_norm, ffn_post_norm, dense_w_gate, dense_w_up,
              dense_w_down, w_router, moe_w_gate, moe_w_up, moe_w_down):
    for l in range(DEPTH):
        h = rmsnorm(x, mix_pre_norm[l])
        m = token_mixer(h, w_in[l], b_forget[l], b_gate[l], attn_sinks[l],
                        lam_q1[l], lam_k1[l], lam_q2[l], lam_k2[l], diff_subln[l],
                        w_br_a[l], w_br_b[l], w_br_c[l], w_out[l], l)
        x = x + rmsnorm(m, mix_post_norm[l])
        h = rmsnorm(x, ffn_pre_norm[l])
        if l % 2 == 0:
            i = l // 2
            f = swiglu(h, dense_w_gate[i], dense_w_up[i], dense_w_down[i])
        else:
            i = l // 2
            f = moe_swiglu(h, w_router[i], moe_w_gate[i], moe_w_up[i], moe_w_down[i])
        x = x + rmsnorm(f, ffn_post_norm[l])
    return x
```

```python
import functools
import math

import jax
import jax.numpy as jnp
import numpy as np
from jax import lax
from jax.experimental import pallas as pl
from jax.experimental.pallas import tpu as pltpu

F32 = jnp.float32
BF16 = jnp.bfloat16

HEAD_DIM = 64
LANES = 128
A_Q_HEADS = 12
A_KV_HEADS = 4
A_GROUP = A_Q_HEADS // A_KV_HEADS
WINDOW = 128
B_HEADS = 4
C_HEADS = 12
N_BRANCH = 3
N_EXPERTS = 8
TOP_K = 2
RMS_EPS = 1e-6
NEG_INF = -1e30
VMEM_LIMIT = 56 * 1024 * 1024

A_SLOT_HEAD = (0, 3, 1, 4, 2, 5, 6, 9, 7, 10, 8, 11)


def _params(*sem):
    return pltpu.CompilerParams(dimension_semantics=sem, vmem_limit_bytes=VMEM_LIMIT)


def _rms(x, g):
    var = jnp.mean(x * x, axis=-1, keepdims=True)
    return x * lax.rsqrt(var + RMS_EPS) * g


def _sigmoid(x):
    return 1.0 / (1.0 + jnp.exp(-x))


def _dot(a, b):
    return jnp.dot(a, b, preferred_element_type=F32)


def _dot_nt(a, b):
    return lax.dot_general(a, b, (((1,), (1,)), ((), ())), preferred_element_type=F32)


def _rmsnorm_kernel(x_ref, g_ref, o_ref):
    o_ref[...] = _rms(x_ref[...], g_ref[...]).astype(o_ref.dtype)


def rmsnorm_rows(x, g, *, tm=512):
    t, d = x.shape
    tm = min(tm, t)
    return pl.pallas_call(
        _rmsnorm_kernel,
        out_shape=jax.ShapeDtypeStruct((t, d), BF16),
        grid=(t // tm,),
        in_specs=[pl.BlockSpec((tm, d), lambda i: (i, 0)),
                  pl.BlockSpec((1, d), lambda i: (0, 0))],
        out_specs=pl.BlockSpec((tm, d), lambda i: (i, 0)),
        compiler_params=_params("parallel"),
        name="rmsnorm_rows",
    )(x, g.reshape(1, d))


def _mm_kernel(a_ref, b_ref, o_ref):
    o_ref[...] = _dot(a_ref[...], b_ref[...]).astype(o_ref.dtype)


def _mm_gate_kernel(a_ref, b_ref, bias_ref, o_ref):
    z = _dot(a_ref[...], b_ref[...]) + bias_ref[...]
    o_ref[...] = _sigmoid(z).astype(o_ref.dtype)


def matmul(a, b, *, bias=None, out_dtype=BF16, tm=1024, tn=512):
    m, k = a.shape
    n = b.shape[1]
    tm, tn = min(tm, m), min(tn, n)
    assert m % tm == 0 and n % tn == 0
    in_specs = [pl.BlockSpec((tm, k), lambda i, j: (i, 0)),
                pl.BlockSpec((k, tn), lambda i, j: (0, j))]
    args = [a, b]
    kern = _mm_kernel
    if bias is not None:
        in_specs.append(pl.BlockSpec((1, tn), lambda i, j: (0, j)))
        args.append(bias.reshape(1, n).astype(F32))
        kern = _mm_gate_kernel
    return pl.pallas_call(
        kern,
        out_shape=jax.ShapeDtypeStruct((m, n), out_dtype),
        grid=(m // tm, n // tn),
        in_specs=in_specs,
        out_specs=pl.BlockSpec((tm, tn), lambda i, j: (i, j)),
        compiler_params=_params("parallel", "parallel"),
        name="matmul_gate" if bias is not None else "matmul",
    )(*args)


def _forget_kernel(h_ref, wt_ref, b_ref, c_ref, *, seq, blk):
    z = _dot_nt(wt_ref[...], h_ref[0]) + b_ref[...]
    log_f = jnp.minimum(z, 0.0) - jnp.log1p(jnp.exp(-jnp.abs(z)))
    r = lax.broadcasted_iota(jnp.int32, (blk, blk), 0)
    c = lax.broadcasted_iota(jnp.int32, (blk, blk), 1)
    upper = (r <= c).astype(F32)
    carry = jnp.zeros((LANES, 1), F32)
    for j in range(seq // blk):
        part = lax.dot_general(log_f[:, j * blk:(j + 1) * blk], upper,
                               (((1,), (0,)), ((), ())),
                               precision=lax.Precision.HIGHEST,
                               preferred_element_type=F32) + carry
        c_ref[0, :, j * blk:(j + 1) * blk] = part
        carry = part[:, blk - 1:blk]


def forget_cumsum(h3, w_f, b_f):
    b, s, d = h3.shape
    nh = w_f.shape[1]
    wt = jnp.zeros((LANES, d), BF16).at[:nh].set(w_f.T.astype(BF16))
    bias = jnp.zeros((LANES, 1), F32).at[:nh, 0].set(b_f.astype(F32))
    blk = min(256, s)
    return pl.pallas_call(
        functools.partial(_forget_kernel, seq=s, blk=blk),
        out_shape=jax.ShapeDtypeStruct((b, LANES, s), F32),
        grid=(b,),
        in_specs=[pl.BlockSpec((1, s, d), lambda i: (i, 0, 0)),
                  pl.BlockSpec((LANES, d), lambda i: (0, 0)),
                  pl.BlockSpec((LANES, 1), lambda i: (0, 0))],
        out_specs=pl.BlockSpec((1, LANES, s), lambda i: (i, 0, 0)),
        compiler_params=_params("parallel"),
        name="forget_cumsum",
    )(h3, wt, bias)


def _half_mask(tq, half):
    lane = lax.broadcasted_iota(jnp.int32, (tq, LANES), 1)
    return (lane >= HEAD_DIM * half) & (lane < HEAD_DIM * (half + 1))


def _online_block(qs, k, v, bias_rows, carry, causal):
    tq, tk = qs[0].shape[0], k.shape[0]
    if causal:
        r = lax.broadcasted_iota(jnp.int32, (tq, tk), 0)
        c = lax.broadcasted_iota(jnp.int32, (tq, tk), 1)
        keep = r >= c
    out = []
    for q, bias, (m, l, acc) in zip(qs, bias_rows, carry):
        s = _dot_nt(q, k) + bias
        if causal:
            s = jnp.where(keep, s, NEG_INF)
        m_new = jnp.maximum(m, jnp.max(s, axis=-1, keepdims=True))
        alpha = jnp.exp(m - m_new)
        p = jnp.exp(s - m_new)
        l = alpha * l + jnp.sum(p, axis=-1, keepdims=True)
        acc = alpha * acc + _dot(p.astype(v.dtype), v)
        out.append((m_new, l, acc))
    return tuple(out)


def _causal_streams(qs, k_ref, v_ref, bias_fn, qi, tq):
    init = tuple((jnp.full((tq, 1), NEG_INF, F32), jnp.zeros((tq, 1), F32),
                  jnp.zeros((tq, LANES), F32)) for _ in qs)

    def load(kb):
        start = pl.multiple_of(kb * tq, tq)
        return k_ref[0, pl.ds(start, tq), :], v_ref[0, pl.ds(start, tq), :], start

    def body(kb, carry):
        k, v, start = load(kb)
        return _online_block(qs, k, v, bias_fn(start), carry, causal=False)

    carry = lax.fori_loop(0, qi, body, init)
    k, v, start = load(qi)
    carry = _online_block(qs, k, v, bias_fn(start), carry, causal=True)
    return [(l, acc) for (_, l, acc) in carry]


def _fox_kernel(q_ref, k_ref, v_ref, c_ref, o_ref, *, tq):
    qi = pl.program_id(2)
    q = q_ref[0] * jnp.asarray(HEAD_DIM ** -0.5, q_ref.dtype)
    masks = [_half_mask(tq, h) for h in range(2)]
    qs = [jnp.where(mk, q, jnp.zeros_like(q)) for mk in masks]

    def bias_fn(start):
        return [-c_ref[0, h, :, pl.ds(start, tq)] for h in range(2)]

    (l0, a0), (l1, a1) = _causal_streams(qs, k_ref, v_ref, bias_fn, qi, tq)
    o_ref[0] = jnp.where(masks[0], a0 / l0, a1 / l1).astype(o_ref.dtype)


def _diff_kernel(slope_ref, lam_ref, q_ref, k_ref, v_ref, g_ref, o_ref, *, tq, out_scale):
    qi = pl.program_id(2)
    slope = slope_ref[pl.program_id(1)]
    lam = lam_ref[0]
    q = q_ref[0] * jnp.asarray(HEAD_DIM ** -0.5, q_ref.dtype)
    qs = [jnp.where(_half_mask(tq, h), q, jnp.zeros_like(q)) for h in range(2)]

    def bias_fn(start):
        pos = start + lax.broadcasted_iota(jnp.int32, (1, tq), 1)
        row = slope * pos.astype(F32)
        return [row, row]

    (l0, a0), (l1, a1) = _causal_streams(qs, k_ref, v_ref, bias_fn, qi, tq)
    o = a0 / l0 - lam * (a1 / l1)
    o_ref[0] = (_rms(o, g_ref[...]) * out_scale).astype(o_ref.dtype)


def fox_attention(z3, c4, *, q_col, k_col, v_col, tq=256):
    b, s, _ = z3.shape
    tq = min(tq, s)
    pairs = C_HEADS // 2
    return pl.pallas_call(
        functools.partial(_fox_kernel, tq=tq),
        out_shape=jax.ShapeDtypeStruct((b, s, pairs * LANES), BF16),
        grid=(b, pairs, s // tq),
        in_specs=[pl.BlockSpec((1, tq, LANES), lambda bi, p, qi: (bi, qi, q_col + p)),
                  pl.BlockSpec((1, s, LANES), lambda bi, p, qi: (bi, 0, k_col + p)),
                  pl.BlockSpec((1, s, LANES), lambda bi, p, qi: (bi, 0, v_col + p)),
                  pl.BlockSpec((1, 2, 1, s), lambda bi, p, qi: (bi, p, 0, 0))],
        out_specs=pl.BlockSpec((1, tq, LANES), lambda bi, p, qi: (bi, qi, p)),
        compiler_params=_params("parallel", "parallel", "parallel"),
        name="fox_attention",
    )(z3, z3, z3, c4)


def diff_attention(z3, slopes, lam, subln, *, q_col, k_col, v_col, out_scale, tq=256):
    b, s, _ = z3.shape
    tq = min(tq, s)
    smem = pl.BlockSpec(memory_space=pltpu.SMEM)
    return pl.pallas_call(
        functools.partial(_diff_kernel, tq=tq, out_scale=out_scale),
        out_shape=jax.ShapeDtypeStruct((b, s, B_HEADS * LANES), BF16),
        grid=(b, B_HEADS, s // tq),
        in_specs=[smem, smem,
                  pl.BlockSpec((1, tq, LANES), lambda bi, h, qi: (bi, qi, q_col + h)),
                  pl.BlockSpec((1, s, LANES), lambda bi, h, qi: (bi, 0, k_col + h)),
                  pl.BlockSpec((1, s, LANES), lambda bi, h, qi: (bi, 0, v_col + h)),
                  pl.BlockSpec((1, LANES), lambda bi, h, qi: (0, 0))],
        out_specs=pl.BlockSpec((1, tq, LANES), lambda bi, h, qi: (bi, qi, h)),
        compiler_params=_params("parallel", "parallel", "parallel"),
        name="diff_attention",
    )(slopes, lam, z3, z3, z3, subln.reshape(1, LANES).astype(F32))


def _swa_kernel(slope_ref, sink_ref, q_ref, k_ref, v_ref, o_ref, *, blk):
    qi = pl.program_id(1)
    cur = pl.multiple_of(qi * blk, blk)
    prev = pl.multiple_of(jnp.maximum(qi - 1, 0) * blk, blk)
    r = lax.broadcasted_iota(jnp.int32, (blk, 2 * blk), 0)
    c = lax.broadcasted_iota(jnp.int32, (blk, 2 * blk), 1)
    dist = r + blk - c
    keep = (dist >= 0) & (dist < WINDOW) & ((qi > 0) | (c >= blk))
    dist_f = dist.astype(F32)
    scale = jnp.asarray(HEAD_DIM ** -0.5, q_ref.dtype)
    masks = [_half_mask(blk, h) for h in range(2)]
    for tile in range(A_Q_HEADS // 2):
        q = q_ref[0, :, tile * LANES:(tile + 1) * LANES] * scale
        halves = []
        for half in range(2):
            slot = 2 * tile + half
            kv_tile = (A_SLOT_HEAD[slot] // A_GROUP) // 2
            lanes = slice(kv_tile * LANES, (kv_tile + 1) * LANES)
            k = jnp.concatenate([k_ref[0, pl.ds(prev, blk), lanes],
                                 k_ref[0, pl.ds(cur, blk), lanes]], axis=0)
            v = jnp.concatenate([v_ref[0, pl.ds(prev, blk), lanes],
                                 v_ref[0, pl.ds(cur, blk), lanes]], axis=0)
            qh = jnp.where(masks[half], q, jnp.zeros_like(q))
            s = _dot_nt(qh, k) - slope_ref[slot] * dist_f
            s = jnp.where(keep, s, NEG_INF)
            sink = sink_ref[slot]
            m = jnp.maximum(jnp.max(s, axis=-1, keepdims=True), sink)
            p = jnp.exp(s - m)
            denom = jnp.sum(p, axis=-1, keepdims=True) + jnp.exp(sink - m)
            halves.append(_dot(p.astype(v.dtype), v) / denom)
        o_ref[0, :, tile * LANES:(tile + 1) * LANES] = jnp.where(
            masks[0], halves[0], halves[1]).astype(o_ref.dtype)


def swa_attention(z3, slopes, sinks, *, q_tile, k_tile, v_tile):
    b, s, _ = z3.shape
    blk = WINDOW
    qw = A_Q_HEADS * HEAD_DIM
    kw = A_KV_HEADS * HEAD_DIM
    smem = pl.BlockSpec(memory_space=pltpu.SMEM)
    return pl.pallas_call(
        functools.partial(_swa_kernel, blk=blk),
        out_shape=jax.ShapeDtypeStruct((b, s, qw), BF16),
        grid=(b, s // blk),
        in_specs=[smem, smem,
                  pl.BlockSpec((1, blk, qw), lambda bi, qi: (bi, qi, q_tile)),
                  pl.BlockSpec((1, s, kw), lambda bi, qi: (bi, 0, k_tile)),
                  pl.BlockSpec((1, s, kw), lambda bi, qi: (bi, 0, v_tile))],
        out_specs=pl.BlockSpec((1, blk, qw), lambda bi, qi: (bi, qi, 0)),
        compiler_params=_params("parallel", "parallel"),
        name="swa_attention",
    )(slopes, sinks, z3, z3, z3)


def _merge_kernel(oa_ref, ob_ref, oc_ref, wa_ref, wb_ref, wc_ref,
                  ga_ref, gb_ref, gc_ref, o_ref):
    acc = ga_ref[...].astype(F32) * _dot(oa_ref[...], wa_ref[...])
    acc += gb_ref[...].astype(F32) * _dot(ob_ref[...], wb_ref[...])
    acc += gc_ref[...].astype(F32) * _dot(oc_ref[...], wc_ref[...])
    o_ref[...] = acc.astype(o_ref.dtype)


def gated_merge(oa, ob, oc, wa, wb, wc, gates, *, tm=1024, tn=512):
    t = oa.shape[0]
    d = wa.shape[1]
    tm, tn = min(tm, t), min(tn, d)
    nj = d // tn
    row = lambda w: pl.BlockSpec((tm, w), lambda i, j: (i, 0))
    col = lambda w: pl.BlockSpec((w, tn), lambda i, j: (0, j))
    gate = lambda br: pl.BlockSpec((tm, tn), lambda i, j: (i, br * nj + j))
    return pl.pallas_call(
        _merge_kernel,
        out_shape=jax.ShapeDtypeStruct((t, d), BF16),
        grid=(t // tm, nj),
        in_specs=[row(oa.shape[1]), row(ob.shape[1]), row(oc.shape[1]),
                  col(wa.shape[0]), col(wb.shape[0]), col(wc.shape[0]),
                  gate(0), gate(1), gate(2)],
        out_specs=pl.BlockSpec((tm, tn), lambda i, j: (i, j)),
        compiler_params=_params("parallel", "parallel"),
        name="gated_merge",
    )(oa, ob, oc, wa, wb, wc, gates, gates, gates)


def _proj_res_kernel(a_ref, w_ref, x_ref, gp_ref, gn_ref, xo_ref, ho_ref, acc_ref, *, nk):
    kk = pl.program_id(1)
    part = _dot(a_ref[...], w_ref[...])

    @pl.when(kk == 0)
    def _():
        acc_ref[...] = part

    @pl.when(kk > 0)
    def _():
        acc_ref[...] += part

    @pl.when(kk == nk - 1)
    def _():
        x_new = x_ref[...] + _rms(acc_ref[...], gp_ref[...])
        xo_ref[...] = x_new
        ho_ref[...] = _rms(x_new, gn_ref[...]).astype(ho_ref.dtype)


def proj_residual(a, w, x, g_post, g_next, *, tm, tk):
    t, k = a.shape
    d = w.shape[1]
    tm, tk = min(tm, t), min(tk, k)
    nk = k // tk
    vec = pl.BlockSpec((1, d), lambda i, kk: (0, 0))
    return pl.pallas_call(
        functools.partial(_proj_res_kernel, nk=nk),
        out_shape=(jax.ShapeDtypeStruct((t, d), F32), jax.ShapeDtypeStruct((t, d), BF16)),
        grid=(t // tm, nk),
        in_specs=[pl.BlockSpec((tm, tk), lambda i, kk: (i, kk)),
                  pl.BlockSpec((tk, d), lambda i, kk: (kk, 0)),
                  pl.BlockSpec((tm, d), lambda i, kk: (i, 0)),
                  vec, vec],
        out_specs=(pl.BlockSpec((tm, d), lambda i, kk: (i, 0)),
                   pl.BlockSpec((tm, d), lambda i, kk: (i, 0))),
        scratch_shapes=[pltpu.VMEM((tm, d), F32)],
        compiler_params=_params("parallel", "arbitrary"),
        name="proj_residual",
    )(a, w, x, g_post.reshape(1, d), g_next.reshape(1, d))


def _swiglu_up_kernel(h_ref, wg_ref, wu_ref, o_ref):
    h = h_ref[...]
    g = _dot(h, wg_ref[...])
    u = _dot(h, wu_ref[...])
    o_ref[...] = (g * _sigmoid(g) * u).astype(o_ref.dtype)


def swiglu_up(h, wg, wu, *, tm=1024, tn=512):
    t, d = h.shape
    f = wg.shape[1]
    tm, tn = min(tm, t), min(tn, f)
    return pl.pallas_call(
        _swiglu_up_kernel,
        out_shape=jax.ShapeDtypeStruct((t, f), BF16),
        grid=(t // tm, f // tn),
        in_specs=[pl.BlockSpec((tm, d), lambda i, j: (i, 0)),
                  pl.BlockSpec((d, tn), lambda i, j: (0, j)),
                  pl.BlockSpec((d, tn), lambda i, j: (0, j))],
        out_specs=pl.BlockSpec((tm, tn), lambda i, j: (i, j)),
        compiler_params=_params("parallel", "parallel"),
        name="swiglu_up",
    )(h, wg, wu)


def _router_kernel(x_ref, g_ref, w_ref, idx_ref, wt_ref):
    h = _rms(x_ref[...], g_ref[...])
    logits = jnp.dot(h, w_ref[...], precision=lax.Precision.HIGHEST,
                     preferred_element_type=F32)
    lane = lax.broadcasted_iota(jnp.int32, logits.shape, 1)
    lane_f = lane.astype(F32)
    lg = jnp.where(lane < N_EXPERTS, logits, -jnp.inf)
    m1 = jnp.max(lg, axis=-1, keepdims=True)
    i1 = jnp.min(jnp.where(lg == m1, lane_f, float(LANES)), axis=-1, keepdims=True)
    lg2 = jnp.where(lane_f == i1, -jnp.inf, lg)
    m2 = jnp.max(lg2, axis=-1, keepdims=True)
    i2 = jnp.min(jnp.where(lg2 == m2, lane_f, float(LANES)), axis=-1, keepdims=True)
    e2 = jnp.exp(m2 - m1)
    w1 = 1.0 / (1.0 + e2)
    w2 = e2 / (1.0 + e2)
    idx_ref[...] = jnp.where(lane == 0, i1, jnp.where(lane == 1, i2, 0.0)).astype(jnp.int32)
    wt_ref[...] = jnp.where(lane == 0, w1, jnp.where(lane == 1, w2, 0.0))


def router_top2(x, g, w_router, *, tm=512):
    t, d = x.shape
    tm = min(tm, t)
    wp = jnp.zeros((d, LANES), F32).at[:, :N_EXPERTS].set(w_router.astype(F32))
    out = pl.BlockSpec((tm, LANES), lambda i: (i, 0))
    idx, wt = pl.pallas_call(
        _router_kernel,
        out_shape=(jax.ShapeDtypeStruct((t, LANES), jnp.int32),
                   jax.ShapeDtypeStruct((t, LANES), F32)),
        grid=(t // tm,),
        in_specs=[pl.BlockSpec((tm, d), lambda i: (i, 0)),
                  pl.BlockSpec((1, d), lambda i: (0, 0)),
                  pl.BlockSpec((d, LANES), lambda i: (0, 0))],
        out_specs=(out, out),
        compiler_params=_params("parallel"),
        name="router_top2",
    )(x, g.reshape(1, d), wp)
    return idx[:, :TOP_K], wt[:, :TOP_K]


def _moe_up_kernel(te_ref, na_ref, x_ref, wg_ref, wu_ref, o_ref):
    @pl.when(pl.program_id(1) < na_ref[0])
    def _():
        x = x_ref[...]
        g = _dot(x, wg_ref[0])
        u = _dot(x, wu_ref[0])
        o_ref[...] = (g * _sigmoid(g) * u).astype(o_ref.dtype)


def _moe_down_kernel(te_ref, na_ref, u_ref, wd_ref, rw_ref, o_ref):
    @pl.when(pl.program_id(1) < na_ref[0])
    def _():
        o_ref[...] = (rw_ref[...] * _dot(u_ref[...], wd_ref[0])).astype(o_ref.dtype)


def _active_tile(i, na_ref):
    return jnp.minimum(i, na_ref[0] - 1)


def moe_up(xs, tile_expert, n_active, wg, wu, *, tm, tn=1024):
    p, d = xs.shape
    f = wg.shape[2]
    tn = min(tn, f)
    grid_spec = pltpu.PrefetchScalarGridSpec(
        num_scalar_prefetch=2,
        grid=(f // tn, p // tm),
        in_specs=[pl.BlockSpec((tm, d), lambda j, i, te, na: (_active_tile(i, na), 0)),
                  pl.BlockSpec((1, d, tn), lambda j, i, te, na: (te[_active_tile(i, na)], 0, j)),
                  pl.BlockSpec((1, d, tn), lambda j, i, te, na: (te[_active_tile(i, na)], 0, j))],
        out_specs=pl.BlockSpec((tm, tn), lambda j, i, te, na: (_active_tile(i, na), j)),
    )
    return pl.pallas_call(
        _moe_up_kernel,
        out_shape=jax.ShapeDtypeStruct((p, f), BF16),
        grid_spec=grid_spec,
        compiler_params=_params("arbitrary", "arbitrary"),
        name="moe_up",
    )(tile_expert, n_active, xs, wg, wu)


def moe_down(u, tile_expert, n_active, wd, row_w, *, tm, tn=512):
    p, f = u.shape
    d = wd.shape[2]
    tn = min(tn, d)
    grid_spec = pltpu.PrefetchScalarGridSpec(
        num_scalar_prefetch=2,
        grid=(d // tn, p // tm),
        in_specs=[pl.BlockSpec((tm, f), lambda j, i, te, na: (_active_tile(i, na), 0)),
                  pl.BlockSpec((1, f, tn), lambda j, i, te, na: (te[_active_tile(i, na)], 0, j)),
                  pl.BlockSpec((tm, 1), lambda j, i, te, na: (_active_tile(i, na), 0))],
        out_specs=pl.BlockSpec((tm, tn), lambda j, i, te, na: (_active_tile(i, na), j)),
    )
    return pl.pallas_call(
        _moe_down_kernel,
        out_shape=jax.ShapeDtypeStruct((p, d), BF16),
        grid_spec=grid_spec,
        compiler_params=_params("arbitrary", "arbitrary"),
        name="moe_down",
    )(tile_expert, n_active, u, wd, row_w)


def _combine_kernel(ya_ref, yb_ref, x_ref, g_ref, o_ref):
    y = ya_ref[...].astype(F32) + yb_ref[...].astype(F32)
    o_ref[...] = x_ref[...] + _rms(y, g_ref[...])


def combine_residual(ya, yb, x, g, *, tm=512):
    t, d = x.shape
    tm = min(tm, t)
    blk = pl.BlockSpec((tm, d), lambda i: (i, 0))
    return pl.pallas_call(
        _combine_kernel,
        out_shape=jax.ShapeDtypeStruct((t, d), F32),
        grid=(t // tm,),
        in_specs=[blk, blk, blk, pl.BlockSpec((1, d), lambda i: (0, 0))],
        out_specs=blk,
        compiler_params=_params("parallel"),
        name="combine_residual",
    )(ya, yb, x, g.reshape(1, d))


def _alibi_slopes(n):
    return jnp.exp2(-8.0 * (jnp.arange(n, dtype=F32) + 1.0) / n)


def _token_mixer(x2, h, batch, seq, layer_idx, w_in, b_forget, b_gate, sinks,
                 lq1, lk1, lq2, lk2, subln, w_br_a, w_br_b, w_br_c, w_out,
                 g_post, g_next):
    t, d = x2.shape
    a_w = A_Q_HEADS * HEAD_DIM
    kv_w = A_KV_HEADS * HEAD_DIM
    b_w = B_HEADS * 2 * HEAD_DIM
    c_w = C_HEADS * HEAD_DIM
    n_attn = a_w + 2 * kv_w + 3 * b_w + 3 * c_w

    slot_cols = np.concatenate([np.arange(HEAD_DIM) + HEAD_DIM * hd for hd in A_SLOT_HEAD])
    w_attn = jnp.concatenate([w_in[:, :a_w][:, slot_cols], w_in[:, a_w:n_attn]],
                             axis=1).astype(BF16)
    w_f = w_in[:, n_attn:n_attn + C_HEADS]
    w_g = w_in[:, n_attn + C_HEADS:].astype(BF16)

    z3 = matmul(h, w_attn).reshape(batch, seq, n_attn)
    gates = matmul(h, w_g, bias=b_gate)
    c = forget_cumsum(h.reshape(batch, seq, d), w_f, b_forget)
    c4 = c[:, :C_HEADS].reshape(batch, C_HEADS, 1, seq)

    slot = np.array(A_SLOT_HEAD)
    oa = swa_attention(z3, _alibi_slopes(A_Q_HEADS)[slot], sinks.astype(F32)[slot],
                       q_tile=0, k_tile=a_w // kv_w, v_tile=a_w // kv_w + 1)

    lam_init = 0.8 - 0.6 * math.exp(-0.3 * layer_idx)
    lam = (jnp.exp(jnp.sum(lq1.astype(F32) * lk1.astype(F32)))
           - jnp.exp(jnp.sum(lq2.astype(F32) * lk2.astype(F32))) + lam_init)
    b0 = (a_w + 2 * kv_w) // LANES
    ob = diff_attention(z3, _alibi_slopes(B_HEADS), lam.reshape(1), subln,
                        q_col=b0, k_col=b0 + B_HEADS, v_col=b0 + 2 * B_HEADS,
                        out_scale=1.0 - lam_init)
    c0 = b0 + 3 * B_HEADS
    oc = fox_attention(z3, c4, q_col=c0, k_col=c0 + C_HEADS // 2, v_col=c0 + C_HEADS)

    merged = gated_merge(oa.reshape(t, a_w), ob.reshape(t, b_w), oc.reshape(t, c_w),
                         w_br_a[slot_cols].astype(BF16), w_br_b.astype(BF16),
                         w_br_c.astype(BF16), gates)
    return proj_residual(merged, w_out.astype(BF16), x2, g_post, g_next, tm=256, tk=d)


def _moe_dispatch(idx, wts, tm):
    t = idx.shape[0]
    n_rows = t * TOP_K
    n_tiles = (n_rows + N_EXPERTS * (tm - 1)) // tm
    e_flat = idx.reshape(n_rows)
    onehot = (e_flat[:, None] == jnp.arange(N_EXPERTS)[None, :]).astype(jnp.int32)
    rank = jnp.sum((jnp.cumsum(onehot, axis=0) - 1) * onehot, axis=1)
    counts = jnp.sum(onehot, axis=0)
    padded = ((counts + tm - 1) // tm) * tm
    ends = jnp.cumsum(padded)
    starts = ends - padded
    pos = starts[e_flat] + rank
    row_token = jnp.zeros((n_tiles * tm,), jnp.int32).at[pos].set(
        jnp.arange(n_rows, dtype=jnp.int32) // TOP_K)
    row_w = jnp.zeros((n_tiles * tm,), F32).at[pos].set(wts.reshape(n_rows))
    tile_start = jnp.arange(n_tiles, dtype=jnp.int32) * tm
    tile_expert = jnp.minimum(jnp.searchsorted(ends, tile_start, side="right"),
                              N_EXPERTS - 1).astype(jnp.int32)
    n_active = (ends[-1] // tm).astype(jnp.int32).reshape(1)
    return pos.reshape(t, TOP_K), row_token, row_w.reshape(-1, 1), tile_expert, n_active


def _moe_ffn(x2, h, g_pre, g_post, w_router, e_gate, e_up, e_down, *, tm=512):
    idx, wts = router_top2(x2, g_pre, w_router)
    tm = min(tm, x2.shape[0])
    pos, row_token, row_w, tile_expert, n_active = _moe_dispatch(idx, wts, tm)
    xs = jnp.take(h, row_token, axis=0)
    u = moe_up(xs, tile_expert, n_active, e_gate.astype(BF16), e_up.astype(BF16), tm=tm)
    ys = moe_down(u, tile_expert, n_active, e_down.astype(BF16), row_w, tm=tm)
    ya = jnp.take(ys, pos[:, 0], axis=0)
    yb = jnp.take(ys, pos[:, 1], axis=0)
    return combine_residual(ya, yb, x2, g_post)


def kernel(x, mix_pre_norm, w_in, b_forget, b_gate, attn_sinks, lam_q1, lam_k1, lam_q2, lam_k2, diff_subln, w_br_a, w_br_b, w_br_c, w_out, mix_post_norm, ffn_pre_norm, ffn_post_norm, dense_w_gate, dense_w_up, dense_w_down, w_router, moe_w_gate, moe_w_up, moe_w_down):
    batch, seq, d = x.shape
    depth = w_in.shape[0]
    x2 = x.reshape(batch * seq, d)
    h = rmsnorm_rows(x2, mix_pre_norm[0])
    for l in range(depth):
        x2, h = _token_mixer(x2, h, batch, seq, l, w_in[l], b_forget[l], b_gate[l],
                             attn_sinks[l], lam_q1[l], lam_k1[l], lam_q2[l], lam_k2[l],
                             diff_subln[l], w_br_a[l], w_br_b[l], w_br_c[l], w_out[l],
                             mix_post_norm[l], ffn_pre_norm[l])
        i = l // 2
        if l % 2 == 0:
            u = swiglu_up(h, dense_w_gate[i].astype(BF16), dense_w_up[i].astype(BF16))
            g_next = mix_pre_norm[l + 1] if l + 1 < depth else jnp.ones((d,), F32)
            f = dense_w_down.shape[1]
            x2, h = proj_residual(u, dense_w_down[i].astype(BF16), x2, ffn_post_norm[l],
                                  g_next, tm=512, tk=f // 4)
        else:
            x2 = _moe_ffn(x2, h, ffn_pre_norm[l], ffn_post_norm[l], w_router[i],
                          moe_w_gate[i], moe_w_up[i], moe_w_down[i])
            if l + 1 < depth:
                h = rmsnorm_rows(x2, mix_pre_norm[l + 1])
    return x2.reshape(batch, seq, d)
```

```python
import functools
import math

import jax
import jax.numpy as jnp
import numpy as np
from jax import lax
from jax.experimental import pallas as pl
from jax.experimental.pallas import tpu as pltpu

F32 = jnp.float32
BF16 = jnp.bfloat16

HEAD_DIM = 64
LANES = 128
A_Q_HEADS = 12
A_KV_HEADS = 4
A_GROUP = A_Q_HEADS // A_KV_HEADS
WINDOW = 128
B_HEADS = 4
C_HEADS = 12
N_BRANCH = 3
N_EXPERTS = 8
TOP_K = 2
RMS_EPS = 1e-6
NEG_INF = -1e30
VMEM_LIMIT = 56 * 1024 * 1024

A_SLOT_HEAD = (0, 3, 1, 4, 2, 5, 6, 9, 7, 10, 8, 11)


def _params(*sem):
    return pltpu.CompilerParams(dimension_semantics=sem, vmem_limit_bytes=VMEM_LIMIT)


def _rms(x, g):
    var = jnp.mean(x * x, axis=-1, keepdims=True)
    return x * lax.rsqrt(var + RMS_EPS) * g


def _sigmoid(x):
    return 0.5 * jnp.tanh(0.5 * x) + 0.5


def _dot(a, b):
    return jnp.dot(a, b, preferred_element_type=F32)


def _dot_nt(a, b):
    return lax.dot_general(a, b, (((1,), (1,)), ((), ())), preferred_element_type=F32)


def _rmsnorm_kernel(x_ref, g_ref, o_ref):
    o_ref[...] = _rms(x_ref[...], g_ref[...]).astype(o_ref.dtype)


def rmsnorm_rows(x, g, *, tm=512):
    t, d = x.shape
    tm = min(tm, t)
    return pl.pallas_call(
        _rmsnorm_kernel,
        out_shape=jax.ShapeDtypeStruct((t, d), BF16),
        grid=(t // tm,),
        in_specs=[pl.BlockSpec((tm, d), lambda i: (i, 0)),
                  pl.BlockSpec((1, d), lambda i: (0, 0))],
        out_specs=pl.BlockSpec((tm, d), lambda i: (i, 0)),
        compiler_params=_params("parallel"),
        name="rmsnorm_rows",
    )(x, g.reshape(1, d))


def _mm_kernel(a_ref, b_ref, o_ref):
    o_ref[...] = _dot(a_ref[...], b_ref[...]).astype(o_ref.dtype)


def _mm_gate_kernel(a_ref, b_ref, bias_ref, o_ref):
    z = _dot(a_ref[...], b_ref[...]) + bias_ref[...]
    o_ref[...] = _sigmoid(z).astype(o_ref.dtype)


def matmul(a, b, *, bias=None, out_dtype=BF16, tm=1024, tn=512):
    m, k = a.shape
    n = b.shape[1]
    tm, tn = min(tm, m), min(tn, n)
    assert m % tm == 0 and n % tn == 0
    in_specs = [pl.BlockSpec((tm, k), lambda i, j: (i, 0)),
                pl.BlockSpec((k, tn), lambda i, j: (0, j))]
    args = [a, b]
    kern = _mm_kernel
    if bias is not None:
        in_specs.append(pl.BlockSpec((1, tn), lambda i, j: (0, j)))
        args.append(bias.reshape(1, n).astype(F32))
        kern = _mm_gate_kernel
    return pl.pallas_call(
        kern,
        out_shape=jax.ShapeDtypeStruct((m, n), out_dtype),
        grid=(m // tm, n // tn),
        in_specs=in_specs,
        out_specs=pl.BlockSpec((tm, tn), lambda i, j: (i, j)),
        compiler_params=_params("parallel", "parallel"),
        name="matmul_gate" if bias is not None else "matmul",
    )(*args)


def _forget_kernel(h_ref, wt_ref, b_ref, c_ref, *, seq, blk):
    z = _dot_nt(wt_ref[...], h_ref[0]) + b_ref[...]
    log_f = jnp.minimum(z, 0.0) - jnp.log1p(jnp.exp(-jnp.abs(z)))
    r = lax.broadcasted_iota(jnp.int32, (blk, blk), 0)
    c = lax.broadcasted_iota(jnp.int32, (blk, blk), 1)
    upper = (r <= c).astype(F32)
    carry = jnp.zeros((LANES, 1), F32)
    for j in range(seq // blk):
        part = lax.dot_general(log_f[:, j * blk:(j + 1) * blk], upper,
                               (((1,), (0,)), ((), ())),
                               precision=lax.Precision.HIGHEST,
                               preferred_element_type=F32) + carry
        c_ref[0, :, j * blk:(j + 1) * blk] = part
        carry = part[:, blk - 1:blk]


def forget_cumsum(h3, w_f, b_f):
    b, s, d = h3.shape
    nh = w_f.shape[1]
    wt = jnp.zeros((LANES, d), BF16).at[:nh].set(w_f.T.astype(BF16))
    bias = jnp.zeros((LANES, 1), F32).at[:nh, 0].set(b_f.astype(F32))
    blk = min(256, s)
    return pl.pallas_call(
        functools.partial(_forget_kernel, seq=s, blk=blk),
        out_shape=jax.ShapeDtypeStruct((b, LANES, s), F32),
        grid=(b,),
        in_specs=[pl.BlockSpec((1, s, d), lambda i: (i, 0, 0)),
                  pl.BlockSpec((LANES, d), lambda i: (0, 0)),
                  pl.BlockSpec((LANES, 1), lambda i: (0, 0))],
        out_specs=pl.BlockSpec((1, LANES, s), lambda i: (i, 0, 0)),
        compiler_params=_params("parallel"),
        name="forget_cumsum",
    )(h3, wt, bias)


def _half_mask(tq, half):
    lane = lax.broadcasted_iota(jnp.int32, (tq, LANES), 1)
    return (lane >= HEAD_DIM * half) & (lane < HEAD_DIM * (half + 1))


def _causal_logits(qh, k_ref, lanes, qi, tq, bias):
    n0 = qi * tq
    r = lax.broadcasted_iota(jnp.int32, (tq, tq), 0)
    c = lax.broadcasted_iota(jnp.int32, (tq, tq), 1)
    diag = _dot_nt(qh, k_ref[0, n0:n0 + tq, lanes]) + bias(n0, tq)
    parts = [jnp.where(r >= c, diag, NEG_INF)]
    if qi > 0:
        parts.append(_dot_nt(qh, k_ref[0, :n0, lanes]) + bias(0, n0))
    return parts


def _softmax_parts(parts):
    m = functools.reduce(jnp.maximum, [s.max(axis=-1, keepdims=True) for s in parts])
    exps = [jnp.exp(s - m) for s in parts]
    denom = functools.reduce(jnp.add, [e.sum(axis=-1, keepdims=True) for e in exps])
    return exps, denom


def _value_parts(v_ref, lanes, qi, tq):
    n0 = qi * tq
    return [v_ref[0, n0:n0 + tq, lanes]] + ([v_ref[0, :n0, lanes]] if qi > 0 else [])


def _weighted_values(weights, values):
    return functools.reduce(jnp.add, [_dot(w.astype(v.dtype), v) for w, v in zip(weights, values)])


def _dispatch_on_tile(branch, nq):
    qi = pl.program_id(2)
    for i in range(nq):
        pl.when(qi == i)(functools.partial(branch, qi=i))


def _fox_kernel(q_ref, k_ref, v_ref, c_ref, o_ref, *, tq, nq, npair):
    masks = [_half_mask(tq, h) for h in range(2)]

    def branch(qi):
        for p in range(npair):
            lanes = slice(p * LANES, (p + 1) * LANES)
            q = q_ref[0, :, lanes] * jnp.asarray(HEAD_DIM ** -0.5, q_ref.dtype)
            values = _value_parts(v_ref, lanes, qi, tq)
            outs = []
            for h in range(2):
                qh = jnp.where(masks[h], q, jnp.zeros_like(q))
                bias = lambda start, size: -c_ref[0, 2 * p + h, :, start:start + size]
                exps, denom = _softmax_parts(_causal_logits(qh, k_ref, lanes, qi, tq, bias))
                outs.append(_weighted_values(exps, values) / denom)
            o_ref[0, :, lanes] = jnp.where(masks[0], outs[0], outs[1]).astype(o_ref.dtype)

    _dispatch_on_tile(branch, nq)


def _diff_kernel(slope_ref, lam_ref, q_ref, k_ref, v_ref, g_ref, o_ref, *,
                 tq, nq, nhead, out_scale):
    lam = lam_ref[0]
    masks = [_half_mask(tq, h) for h in range(2)]

    def branch(qi):
        for j in range(nhead):
            lanes = slice(j * LANES, (j + 1) * LANES)
            slope = slope_ref[pl.program_id(1) * nhead + j]
            q = q_ref[0, :, lanes] * jnp.asarray(HEAD_DIM ** -0.5, q_ref.dtype)

            def bias(start, size):
                pos = start + lax.broadcasted_iota(jnp.int32, (1, size), 1)
                return slope * pos.astype(F32)

            maps = []
            for h in range(2):
                qh = jnp.where(masks[h], q, jnp.zeros_like(q))
                maps.append(_softmax_parts(_causal_logits(qh, k_ref, lanes, qi, tq, bias)))
            (e0, d0), (e1, d1) = maps
            w0, w1 = 1.0 / d0, lam / d1
            weights = [a * w0 - b * w1 for a, b in zip(e0, e1)]
            o = _weighted_values(weights, _value_parts(v_ref, lanes, qi, tq))
            o_ref[0, :, lanes] = (_rms(o, g_ref[...]) * out_scale).astype(o_ref.dtype)

    _dispatch_on_tile(branch, nq)


def fox_attention(z3, c4, *, q_col, k_col, v_col, tq=256, npair=3):
    b, s, _ = z3.shape
    tq = min(tq, s)
    w = npair * LANES
    groups = C_HEADS // 2 // npair
    assert q_col % npair == 0 and k_col % npair == 0 and v_col % npair == 0
    return pl.pallas_call(
        functools.partial(_fox_kernel, tq=tq, nq=s // tq, npair=npair),
        out_shape=jax.ShapeDtypeStruct((b, s, C_HEADS // 2 * LANES), BF16),
        grid=(b, groups, s // tq),
        in_specs=[pl.BlockSpec((1, tq, w), lambda bi, g, qi: (bi, qi, q_col // npair + g)),
                  pl.BlockSpec((1, s, w), lambda bi, g, qi: (bi, 0, k_col // npair + g)),
                  pl.BlockSpec((1, s, w), lambda bi, g, qi: (bi, 0, v_col // npair + g)),
                  pl.BlockSpec((1, 2 * npair, 1, s), lambda bi, g, qi: (bi, g, 0, 0))],
        out_specs=pl.BlockSpec((1, tq, w), lambda bi, g, qi: (bi, qi, g)),
        compiler_params=_params("parallel", "parallel", "arbitrary"),
        name="fox_attention",
    )(z3, z3, z3, c4)


def diff_attention(z3, slopes, lam, subln, *, q_col, k_col, v_col, out_scale, tq=256, nhead=2):
    b, s, _ = z3.shape
    tq = min(tq, s)
    w = nhead * LANES
    assert q_col % nhead == 0 and k_col % nhead == 0 and v_col % nhead == 0
    smem = pl.BlockSpec(memory_space=pltpu.SMEM)
    return pl.pallas_call(
        functools.partial(_diff_kernel, tq=tq, nq=s // tq, nhead=nhead, out_scale=out_scale),
        out_shape=jax.ShapeDtypeStruct((b, s, B_HEADS * LANES), BF16),
        grid=(b, B_HEADS // nhead, s // tq),
        in_specs=[smem, smem,
                  pl.BlockSpec((1, tq, w), lambda bi, g, qi: (bi, qi, q_col // nhead + g)),
                  pl.BlockSpec((1, s, w), lambda bi, g, qi: (bi, 0, k_col // nhead + g)),
                  pl.BlockSpec((1, s, w), lambda bi, g, qi: (bi, 0, v_col // nhead + g)),
                  pl.BlockSpec((1, LANES), lambda bi, g, qi: (0, 0))],
        out_specs=pl.BlockSpec((1, tq, w), lambda bi, g, qi: (bi, qi, g)),
        compiler_params=_params("parallel", "parallel", "arbitrary"),
        name="diff_attention",
    )(slopes, lam, z3, z3, z3, subln.reshape(1, LANES).astype(F32))


def _swa_kernel(slope_ref, sink_ref, q_ref, k_ref, v_ref, o_ref, *, blk):
    qi = pl.program_id(1)
    cur = pl.multiple_of(qi * blk, blk)
    prev = pl.multiple_of(jnp.maximum(qi - 1, 0) * blk, blk)
    r = lax.broadcasted_iota(jnp.int32, (blk, 2 * blk), 0)
    c = lax.broadcasted_iota(jnp.int32, (blk, 2 * blk), 1)
    dist = r + blk - c
    keep = (dist >= 0) & (dist < WINDOW) & ((qi > 0) | (c >= blk))
    dist_f = dist.astype(F32)
    scale = jnp.asarray(HEAD_DIM ** -0.5, q_ref.dtype)
    masks = [_half_mask(blk, h) for h in range(2)]
    for tile in range(A_Q_HEADS // 2):
        q = q_ref[0, :, tile * LANES:(tile + 1) * LANES] * scale
        halves = []
        for half in range(2):
            slot = 2 * tile + half
            kv_tile = (A_SLOT_HEAD[slot] // A_GROUP) // 2
            lanes = slice(kv_tile * LANES, (kv_tile + 1) * LANES)
            k = jnp.concatenate([k_ref[0, pl.ds(prev, blk), lanes],
                                 k_ref[0, pl.ds(cur, blk), lanes]], axis=0)
            v = jnp.concatenate([v_ref[0, pl.ds(prev, blk), lanes],
                                 v_ref[0, pl.ds(cur, blk), lanes]], axis=0)
            qh = jnp.where(masks[half], q, jnp.zeros_like(q))
            s = _dot_nt(qh, k) - slope_ref[slot] * dist_f
            s = jnp.where(keep, s, NEG_INF)
            sink = sink_ref[slot]
            m = jnp.maximum(jnp.max(s, axis=-1, keepdims=True), sink)
            p = jnp.exp(s - m)
            denom = jnp.sum(p, axis=-1, keepdims=True) + jnp.exp(sink - m)
            halves.append(_dot(p.astype(v.dtype), v) / denom)
        o_ref[0, :, tile * LANES:(tile + 1) * LANES] = jnp.where(
            masks[0], halves[0], halves[1]).astype(o_ref.dtype)


def swa_attention(z3, slopes, sinks, *, q_tile, k_tile, v_tile):
    b, s, _ = z3.shape
    blk = WINDOW
    qw = A_Q_HEADS * HEAD_DIM
    kw = A_KV_HEADS * HEAD_DIM
    smem = pl.BlockSpec(memory_space=pltpu.SMEM)
    return pl.pallas_call(
        functools.partial(_swa_kernel, blk=blk),
        out_shape=jax.ShapeDtypeStruct((b, s, qw), BF16),
        grid=(b, s // blk),
        in_specs=[smem, smem,
                  pl.BlockSpec((1, blk, qw), lambda bi, qi: (bi, qi, q_tile)),
                  pl.BlockSpec((1, s, kw), lambda bi, qi: (bi, 0, k_tile)),
                  pl.BlockSpec((1, s, kw), lambda bi, qi: (bi, 0, v_tile))],
        out_specs=pl.BlockSpec((1, blk, qw), lambda bi, qi: (bi, qi, 0)),
        compiler_params=_params("parallel", "parallel"),
        name="swa_attention",
    )(slopes, sinks, z3, z3, z3)


def _merge_kernel(oa_ref, ob_ref, oc_ref, wa_ref, wb_ref, wc_ref,
                  ga_ref, gb_ref, gc_ref, o_ref):
    acc = ga_ref[...].astype(F32) * _dot(oa_ref[...], wa_ref[...])
    acc += gb_ref[...].astype(F32) * _dot(ob_ref[...], wb_ref[...])
    acc += gc_ref[...].astype(F32) * _dot(oc_ref[...], wc_ref[...])
    o_ref[...] = acc.astype(o_ref.dtype)


def gated_merge(oa, ob, oc, wa, wb, wc, gates, *, tm=1024, tn=512):
    t = oa.shape[0]
    d = wa.shape[1]
    tm, tn = min(tm, t), min(tn, d)
    nj = d // tn
    row = lambda w: pl.BlockSpec((tm, w), lambda i, j: (i, 0))
    col = lambda w: pl.BlockSpec((w, tn), lambda i, j: (0, j))
    gate = lambda br: pl.BlockSpec((tm, tn), lambda i, j: (i, br * nj + j))
    return pl.pallas_call(
        _merge_kernel,
        out_shape=jax.ShapeDtypeStruct((t, d), BF16),
        grid=(t // tm, nj),
        in_specs=[row(oa.shape[1]), row(ob.shape[1]), row(oc.shape[1]),
                  col(wa.shape[0]), col(wb.shape[0]), col(wc.shape[0]),
                  gate(0), gate(1), gate(2)],
        out_specs=pl.BlockSpec((tm, tn), lambda i, j: (i, j)),
        compiler_params=_params("parallel", "parallel"),
        name="gated_merge",
    )(oa, ob, oc, wa, wb, wc, gates, gates, gates)


def _proj_res_kernel(a_ref, w_ref, x_ref, gp_ref, gn_ref, xo_ref, ho_ref, acc_ref, *, nk):
    kk = pl.program_id(1)
    part = _dot(a_ref[...], w_ref[...])

    @pl.when(kk == 0)
    def _():
        acc_ref[...] = part

    @pl.when(kk > 0)
    def _():
        acc_ref[...] += part

    @pl.when(kk == nk - 1)
    def _():
        x_new = x_ref[...] + _rms(acc_ref[...], gp_ref[...])
        xo_ref[...] = x_new
        ho_ref[...] = _rms(x_new, gn_ref[...]).astype(ho_ref.dtype)


def proj_residual(a, w, x, g_post, g_next, *, tm, tk):
    t, k = a.shape
    d = w.shape[1]
    tm, tk = min(tm, t), min(tk, k)
    nk = k // tk
    vec = pl.BlockSpec((1, d), lambda i, kk: (0, 0))
    return pl.pallas_call(
        functools.partial(_proj_res_kernel, nk=nk),
        out_shape=(jax.ShapeDtypeStruct((t, d), F32), jax.ShapeDtypeStruct((t, d), BF16)),
        grid=(t // tm, nk),
        in_specs=[pl.BlockSpec((tm, tk), lambda i, kk: (i, kk)),
                  pl.BlockSpec((tk, d), lambda i, kk: (kk, 0)),
                  pl.BlockSpec((tm, d), lambda i, kk: (i, 0)),
                  vec, vec],
        out_specs=(pl.BlockSpec((tm, d), lambda i, kk: (i, 0)),
                   pl.BlockSpec((tm, d), lambda i, kk: (i, 0))),
        scratch_shapes=[pltpu.VMEM((tm, d), F32)],
        compiler_params=_params("parallel", "arbitrary"),
        name="proj_residual",
    )(a, w, x, g_post.reshape(1, d), g_next.reshape(1, d))


def _swiglu_up_kernel(h_ref, wg_ref, wu_ref, o_ref):
    h = h_ref[...]
    g = _dot(h, wg_ref[...])
    u = _dot(h, wu_ref[...])
    o_ref[...] = (g * _sigmoid(g) * u).astype(o_ref.dtype)


def swiglu_up(h, wg, wu, *, tm=1024, tn=512):
    t, d = h.shape
    f = wg.shape[1]
    tm, tn = min(tm, t), min(tn, f)
    return pl.pallas_call(
        _swiglu_up_kernel,
        out_shape=jax.ShapeDtypeStruct((t, f), BF16),
        grid=(t // tm, f // tn),
        in_specs=[pl.BlockSpec((tm, d), lambda i, j: (i, 0)),
                  pl.BlockSpec((d, tn), lambda i, j: (0, j)),
                  pl.BlockSpec((d, tn), lambda i, j: (0, j))],
        out_specs=pl.BlockSpec((tm, tn), lambda i, j: (i, j)),
        compiler_params=_params("parallel", "parallel"),
        name="swiglu_up",
    )(h, wg, wu)


def _router_kernel(x_ref, g_ref, w_ref, idx_ref, wt_ref, h_ref):
    h = _rms(x_ref[...], g_ref[...])
    h_ref[...] = h
    logits = jnp.dot(h, w_ref[...], precision=lax.Precision.HIGHEST,
                     preferred_element_type=F32)
    lane = lax.broadcasted_iota(jnp.int32, logits.shape, 1)
    lane_f = lane.astype(F32)
    lg = jnp.where(lane < N_EXPERTS, logits, -jnp.inf)
    m1 = jnp.max(lg, axis=-1, keepdims=True)
    i1 = jnp.min(jnp.where(lg == m1, lane_f, float(LANES)), axis=-1, keepdims=True)
    lg2 = jnp.where(lane_f == i1, -jnp.inf, lg)
    m2 = jnp.max(lg2, axis=-1, keepdims=True)
    i2 = jnp.min(jnp.where(lg2 == m2, lane_f, float(LANES)), axis=-1, keepdims=True)
    e2 = jnp.exp(m2 - m1)
    w1 = 1.0 / (1.0 + e2)
    w2 = e2 / (1.0 + e2)
    idx_ref[...] = jnp.where(lane == 0, i1, jnp.where(lane == 1, i2, 0.0)).astype(jnp.int32)
    wt_ref[...] = jnp.where(lane == 0, w1, jnp.where(lane == 1, w2, 0.0))


def router_top2(x, g, w_router, *, tm=512):
    t, d = x.shape
    tm = min(tm, t)
    wp = jnp.zeros((d, LANES), F32).at[:, :N_EXPERTS].set(w_router.astype(F32))
    out = pl.BlockSpec((tm, LANES), lambda i: (i, 0))
    idx, wt, h = pl.pallas_call(
        _router_kernel,
        out_shape=(jax.ShapeDtypeStruct((t, LANES), jnp.int32),
                   jax.ShapeDtypeStruct((t, LANES), F32),
                   jax.ShapeDtypeStruct((t, d), F32)),
        grid=(t // tm,),
        in_specs=[pl.BlockSpec((tm, d), lambda i: (i, 0)),
                  pl.BlockSpec((1, d), lambda i: (0, 0)),
                  pl.BlockSpec((d, LANES), lambda i: (0, 0))],
        out_specs=(out, out, pl.BlockSpec((tm, d), lambda i: (i, 0))),
        compiler_params=_params("parallel"),
        name="router_top2",
    )(x, g.reshape(1, d), wp)
    return idx[:, :TOP_K], wt[:, :TOP_K], h


GATHER_CHUNK = 256
GATHER_UNROLL = 8


def _gather_rows_kernel(idx_ref, src_ref, out_ref, sem, *, n_chunks):
    def issue_chunk(c):
        def body(r, carry):
            row = c * GATHER_CHUNK + r
            pltpu.make_async_copy(src_ref.at[pl.ds(idx_ref[row], 1)],
                                  out_ref.at[pl.ds(row, 1)], sem).start()
            return carry
        lax.fori_loop(0, GATHER_CHUNK, body, 0, unroll=GATHER_UNROLL)

    def wait_chunk():
        pltpu.make_async_copy(src_ref.at[pl.ds(0, GATHER_CHUNK)],
                              out_ref.at[pl.ds(0, GATHER_CHUNK)], sem).wait()

    issue_chunk(0)

    def step(c, carry):
        issue_chunk(c)
        wait_chunk()
        return carry

    lax.fori_loop(1, n_chunks, step, 0)
    wait_chunk()


def gather_rows(src, idx):
    n = idx.shape[0]
    d = src.shape[1]
    assert n % GATHER_CHUNK == 0 and src.shape[0] >= GATHER_CHUNK and src.dtype == F32
    return pl.pallas_call(
        functools.partial(_gather_rows_kernel, n_chunks=n // GATHER_CHUNK),
        out_shape=jax.ShapeDtypeStruct((n, d), src.dtype),
        grid_spec=pltpu.PrefetchScalarGridSpec(
            num_scalar_prefetch=1, grid=(1,),
            in_specs=[pl.BlockSpec(memory_space=pl.ANY)],
            out_specs=pl.BlockSpec(memory_space=pl.ANY),
            scratch_shapes=[pltpu.SemaphoreType.DMA]),
        compiler_params=_params("arbitrary"),
        name="gather_rows",
    )(idx, src)


def _moe_up_kernel(te_ref, na_ref, x_ref, wg_ref, wu_ref, o_ref):
    active = pl.program_id(1) < na_ref[0]

    @pl.when(active)
    def _():
        x = x_ref[...].astype(wg_ref.dtype)
        g = _dot(x, wg_ref[0])
        u = _dot(x, wu_ref[0])
        o_ref[...] = (g * _sigmoid(g) * u).astype(o_ref.dtype)

    @pl.when(jnp.logical_not(active))
    def _():
        o_ref[...] = jnp.zeros_like(o_ref)


def _moe_down_kernel(te_ref, na_ref, u_ref, wd_ref, rw_ref, o_ref):
    active = pl.program_id(1) < na_ref[0]

    @pl.when(active)
    def _():
        o_ref[...] = rw_ref[...] * _dot(u_ref[...], wd_ref[0])

    @pl.when(jnp.logical_not(active))
    def _():
        o_ref[...] = jnp.zeros_like(o_ref)


def _active_tile(i, na_ref):
    return jnp.minimum(i, na_ref[0] - 1)


def moe_up(xs, tile_expert, n_active, wg, wu, *, tm, tn=1024):
    p, d = xs.shape
    f = wg.shape[2]
    tn = min(tn, f)
    grid_spec = pltpu.PrefetchScalarGridSpec(
        num_scalar_prefetch=2,
        grid=(f // tn, p // tm),
        in_specs=[pl.BlockSpec((tm, d), lambda j, i, te, na: (_active_tile(i, na), 0)),
                  pl.BlockSpec((1, d, tn), lambda j, i, te, na: (te[_active_tile(i, na)], 0, j)),
                  pl.BlockSpec((1, d, tn), lambda j, i, te, na: (te[_active_tile(i, na)], 0, j))],
        out_specs=pl.BlockSpec((tm, tn), lambda j, i, te, na: (i, j)),
    )
    return pl.pallas_call(
        _moe_up_kernel,
        out_shape=jax.ShapeDtypeStruct((p, f), BF16),
        grid_spec=grid_spec,
        compiler_params=_params("arbitrary", "arbitrary"),
        name="moe_up",
    )(tile_expert, n_active, xs, wg, wu)


def moe_down(u, tile_expert, n_active, wd, row_w, *, tm, tn=512):
    p, f = u.shape
    d = wd.shape[2]
    tn = min(tn, d)
    grid_spec = pltpu.PrefetchScalarGridSpec(
        num_scalar_prefetch=2,
        grid=(d // tn, p // tm),
        in_specs=[pl.BlockSpec((tm, f), lambda j, i, te, na: (_active_tile(i, na), 0)),
                  pl.BlockSpec((1, f, tn), lambda j, i, te, na: (te[_active_tile(i, na)], 0, j)),
                  pl.BlockSpec((tm, 1), lambda j, i, te, na: (_active_tile(i, na), 0))],
        out_specs=pl.BlockSpec((tm, tn), lambda j, i, te, na: (i, j)),
    )
    return pl.pallas_call(
        _moe_down_kernel,
        out_shape=jax.ShapeDtypeStruct((p, d), F32),
        grid_spec=grid_spec,
        compiler_params=_params("arbitrary", "arbitrary"),
        name="moe_down",
    )(tile_expert, n_active, u, wd, row_w)


def _combine_kernel(ya_ref, yb_ref, x_ref, g_ref, o_ref):
    y = ya_ref[...].astype(F32) + yb_ref[...].astype(F32)
    o_ref[...] = x_ref[...] + _rms(y, g_ref[...])


def combine_residual(ya, yb, x, g, *, tm=512):
    t, d = x.shape
    tm = min(tm, t)
    blk = pl.BlockSpec((tm, d), lambda i: (i, 0))
    return pl.pallas_call(
        _combine_kernel,
        out_shape=jax.ShapeDtypeStruct((t, d), F32),
        grid=(t // tm,),
        in_specs=[blk, blk, blk, pl.BlockSpec((1, d), lambda i: (0, 0))],
        out_specs=blk,
        compiler_params=_params("parallel"),
        name="combine_residual",
    )(ya, yb, x, g.reshape(1, d))


def _alibi_slopes(n):
    return jnp.exp2(-8.0 * (jnp.arange(n, dtype=F32) + 1.0) / n)


def _token_mixer(x2, h, batch, seq, layer_idx, w_in, b_forget, b_gate, sinks,
                 lq1, lk1, lq2, lk2, subln, w_br_a, w_br_b, w_br_c, w_out,
                 g_post, g_next):
    t, d = x2.shape
    a_w = A_Q_HEADS * HEAD_DIM
    kv_w = A_KV_HEADS * HEAD_DIM
    b_w = B_HEADS * 2 * HEAD_DIM
    c_w = C_HEADS * HEAD_DIM
    n_attn = a_w + 2 * kv_w + 3 * b_w + 3 * c_w

    slot_cols = np.concatenate([np.arange(HEAD_DIM) + HEAD_DIM * hd for hd in A_SLOT_HEAD])
    a_end = a_w + 2 * kv_w
    b_end = a_end + 3 * b_w
    w_attn = jnp.concatenate([w_in[:, b_end:n_attn], w_in[:, a_end:b_end],
                              w_in[:, :a_w][:, slot_cols], w_in[:, a_w:a_end]],
                             axis=1).astype(BF16)
    w_f = w_in[:, n_attn:n_attn + C_HEADS]
    w_g = w_in[:, n_attn + C_HEADS:].astype(BF16)

    z3 = matmul(h, w_attn).reshape(batch, seq, n_attn)
    gates = matmul(h, w_g, bias=b_gate)
    c = forget_cumsum(h.reshape(batch, seq, d), w_f, b_forget)
    c4 = c[:, :C_HEADS].reshape(batch, C_HEADS, 1, seq)

    oc = fox_attention(z3, c4, q_col=0, k_col=C_HEADS // 2, v_col=C_HEADS)

    lam_init = 0.8 - 0.6 * math.exp(-0.3 * layer_idx)
    lam = (jnp.exp(jnp.sum(lq1.astype(F32) * lk1.astype(F32)))
           - jnp.exp(jnp.sum(lq2.astype(F32) * lk2.astype(F32))) + lam_init)
    b0 = 3 * c_w // LANES
    ob = diff_attention(z3, _alibi_slopes(B_HEADS), lam.reshape(1), subln,
                        q_col=b0, k_col=b0 + B_HEADS, v_col=b0 + 2 * B_HEADS,
                        out_scale=1.0 - lam_init)

    slot = np.array(A_SLOT_HEAD)
    a0 = 3 * c_w + 3 * b_w
    oa = swa_attention(z3, _alibi_slopes(A_Q_HEADS)[slot], sinks.astype(F32)[slot],
                       q_tile=a0 // a_w, k_tile=(a0 + a_w) // kv_w,
                       v_tile=(a0 + a_w) // kv_w + 1)

    merged = gated_merge(oa.reshape(t, a_w), ob.reshape(t, b_w), oc.reshape(t, c_w),
                         w_br_a[slot_cols].astype(BF16), w_br_b.astype(BF16),
                         w_br_c.astype(BF16), gates)
    return proj_residual(merged, w_out.astype(BF16), x2, g_post, g_next, tm=256, tk=d)


def _moe_dispatch(idx, wts, tm):
    t = idx.shape[0]
    n_rows = t * TOP_K
    n_tiles = (n_rows + N_EXPERTS * (tm - 1)) // tm
    e_flat = idx.reshape(n_rows)
    onehot = (e_flat[:, None] == jnp.arange(N_EXPERTS)[None, :]).astype(jnp.int32)
    rank = jnp.sum((jnp.cumsum(onehot, axis=0) - 1) * onehot, axis=1)
    counts = jnp.sum(onehot, axis=0)
    padded = ((counts + tm - 1) // tm) * tm
    ends = jnp.cumsum(padded)
    starts = ends - padded
    pos = starts[e_flat] + rank
    row_token = jnp.zeros((n_tiles * tm,), jnp.int32).at[pos].set(
        jnp.arange(n_rows, dtype=jnp.int32) // TOP_K)
    row_w = jnp.zeros((n_tiles * tm,), F32).at[pos].set(wts.reshape(n_rows))
    tile_start = jnp.arange(n_tiles, dtype=jnp.int32) * tm
    tile_expert = jnp.minimum(jnp.sum(tile_start[:, None] >= ends[None, :], axis=1),
                              N_EXPERTS - 1).astype(jnp.int32)
    n_active = (ends[-1] // tm).astype(jnp.int32).reshape(1)
    return pos.reshape(t, TOP_K), row_token, row_w.reshape(-1, 1), tile_expert, n_active


def _moe_ffn(x2, g_pre, g_post, w_router, e_gate, e_up, e_down, *, tm=512):
    idx, wts, h = router_top2(x2, g_pre, w_router)
    t = x2.shape[0]
    tm = min(tm, t)
    pos, row_token, row_w, tile_expert, n_active = _moe_dispatch(idx, wts, tm)
    xs = gather_rows(h, row_token)
    u = moe_up(xs, tile_expert, n_active, e_gate.astype(BF16), e_up.astype(BF16), tm=tm)
    ys = moe_down(u, tile_expert, n_active, e_down.astype(BF16), row_w, tm=tm)
    y2 = gather_rows(ys, pos.T.reshape(TOP_K * t))
    return combine_residual(y2[:t], y2[t:], x2, g_post)


def kernel(x, mix_pre_norm, w_in, b_forget, b_gate, attn_sinks, lam_q1, lam_k1, lam_q2, lam_k2, diff_subln, w_br_a, w_br_b, w_br_c, w_out, mix_post_norm, ffn_pre_norm, ffn_post_norm, dense_w_gate, dense_w_up, dense_w_down, w_router, moe_w_gate, moe_w_up, moe_w_down):
    batch, seq, d = x.shape
    depth = w_in.shape[0]
    x2 = x.reshape(batch * seq, d)
    h = rmsnorm_rows(x2, mix_pre_norm[0])
    for l in range(depth):
        x2, h = _token_mixer(x2, h, batch, seq, l, w_in[l], b_forget[l], b_gate[l],
                             attn_sinks[l], lam_q1[l], lam_k1[l], lam_q2[l], lam_k2[l],
                             diff_subln[l], w_br_a[l], w_br_b[l], w_br_c[l], w_out[l],
                             mix_post_norm[l], ffn_pre_norm[l])
        i = l // 2
        if l % 2 == 0:
            u = swiglu_up(h, dense_w_gate[i].astype(BF16), dense_w_up[i].astype(BF16))
            g_next = mix_pre_norm[l + 1] if l + 1 < depth else jnp.ones((d,), F32)
            f = dense_w_down.shape[1]
            x2, h = proj_residual(u, dense_w_down[i].astype(BF16), x2, ffn_post_norm[l],
                                  g_next, tm=512, tk=f // 4)
        else:
            x2 = _moe_ffn(x2, ffn_pre_norm[l], ffn_post_norm[l], w_router[i],
                          moe_w_gate[i], moe_w_up[i], moe_w_down[i])
            if l + 1 < depth:
                h = rmsnorm_rows(x2, mix_pre_norm[l + 1])
    return x2.reshape(batch, seq, d)
```

```python
import functools
import math

import jax
import jax.numpy as jnp
import numpy as np
from jax import lax
from jax.experimental import pallas as pl
from jax.experimental.pallas import tpu as pltpu
from jax.experimental.pallas import tpu_sc as plsc

F32 = jnp.float32
BF16 = jnp.bfloat16

HEAD_DIM = 64
LANES = 128
A_Q_HEADS = 12
A_KV_HEADS = 4
A_GROUP = A_Q_HEADS // A_KV_HEADS
WINDOW = 128
B_HEADS = 4
C_HEADS = 12
N_BRANCH = 3
N_EXPERTS = 8
TOP_K = 2
RMS_EPS = 1e-6
NEG_INF = -1e30
VMEM_LIMIT = 56 * 1024 * 1024

A_SLOT_HEAD = (0, 3, 1, 4, 2, 5, 6, 9, 7, 10, 8, 11)


def _params(*sem):
    return pltpu.CompilerParams(dimension_semantics=sem, vmem_limit_bytes=VMEM_LIMIT)


def _rms(x, g):
    var = jnp.mean(x * x, axis=-1, keepdims=True)
    return x * lax.rsqrt(var + RMS_EPS) * g


def _sigmoid(x):
    return 0.5 * jnp.tanh(0.5 * x) + 0.5


def _dot(a, b):
    return jnp.dot(a, b, preferred_element_type=F32)


def _dot_nt(a, b):
    return lax.dot_general(a, b, (((1,), (1,)), ((), ())), preferred_element_type=F32)


def _rmsnorm_kernel(x_ref, g_ref, o_ref):
    o_ref[...] = _rms(x_ref[...], g_ref[...]).astype(o_ref.dtype)


def rmsnorm_rows(x, g, *, tm=512):
    t, d = x.shape
    tm = min(tm, t)
    return pl.pallas_call(
        _rmsnorm_kernel,
        out_shape=jax.ShapeDtypeStruct((t, d), BF16),
        grid=(t // tm,),
        in_specs=[pl.BlockSpec((tm, d), lambda i: (i, 0)),
                  pl.BlockSpec((1, d), lambda i: (0, 0))],
        out_specs=pl.BlockSpec((tm, d), lambda i: (i, 0)),
        compiler_params=_params("parallel"),
        name="rmsnorm_rows",
    )(x, g.reshape(1, d))


def _mm_kernel(a_ref, bt_ref, o_ref):
    o_ref[...] = _dot_nt(a_ref[...], bt_ref[...]).astype(o_ref.dtype)


def _mm_gate_kernel(a_ref, bt_ref, bias_ref, o_ref):
    z = _dot_nt(a_ref[...], bt_ref[...]) + bias_ref[...]
    o_ref[...] = _sigmoid(z).astype(o_ref.dtype)


def matmul(a, bt, *, bias=None, out_dtype=BF16, tm=1024, tn=512):
    m, k = a.shape
    n = bt.shape[0]
    tm, tn = min(tm, m), min(tn, n)
    assert m % tm == 0 and n % tn == 0
    in_specs = [pl.BlockSpec((tm, k), lambda i, j: (i, 0)),
                pl.BlockSpec((tn, k), lambda i, j: (j, 0))]
    args = [a, bt]
    kern = _mm_kernel
    if bias is not None:
        in_specs.append(pl.BlockSpec((1, tn), lambda i, j: (0, j)))
        args.append(bias.reshape(1, n).astype(F32))
        kern = _mm_gate_kernel
    return pl.pallas_call(
        kern,
        out_shape=jax.ShapeDtypeStruct((m, n), out_dtype),
        grid=(m // tm, n // tn),
        in_specs=in_specs,
        out_specs=pl.BlockSpec((tm, tn), lambda i, j: (i, j)),
        compiler_params=_params("parallel", "parallel"),
        name="matmul_gate" if bias is not None else "matmul",
    )(*args)


def _forget_kernel(h_ref, wt_ref, b_ref, c_ref, *, seq, blk):
    z = _dot_nt(wt_ref[...], h_ref[0]) + b_ref[...]
    log_f = jnp.minimum(z, 0.0) - jnp.log1p(jnp.exp(-jnp.abs(z)))
    r = lax.broadcasted_iota(jnp.int32, (blk, blk), 0)
    c = lax.broadcasted_iota(jnp.int32, (blk, blk), 1)
    upper = (r <= c).astype(F32)
    carry = jnp.zeros((LANES, 1), F32)
    for j in range(seq // blk):
        part = lax.dot_general(log_f[:, j * blk:(j + 1) * blk], upper,
                               (((1,), (0,)), ((), ())),
                               precision=lax.Precision.HIGHEST,
                               preferred_element_type=F32) + carry
        c_ref[0, :, j * blk:(j + 1) * blk] = part
        carry = part[:, blk - 1:blk]


def forget_cumsum(h3, w_ft, b_f):
    b, s, d = h3.shape
    nh = w_ft.shape[0]
    wt = jnp.zeros((LANES, d), BF16).at[:nh].set(w_ft.astype(BF16))
    bias = jnp.zeros((LANES, 1), F32).at[:nh, 0].set(b_f.astype(F32))
    blk = min(256, s)
    return pl.pallas_call(
        functools.partial(_forget_kernel, seq=s, blk=blk),
        out_shape=jax.ShapeDtypeStruct((b, LANES, s), F32),
        grid=(b,),
        in_specs=[pl.BlockSpec((1, s, d), lambda i: (i, 0, 0)),
                  pl.BlockSpec((LANES, d), lambda i: (0, 0)),
                  pl.BlockSpec((LANES, 1), lambda i: (0, 0))],
        out_specs=pl.BlockSpec((1, LANES, s), lambda i: (i, 0, 0)),
        compiler_params=_params("parallel"),
        name="forget_cumsum",
    )(h3, wt, bias)


def _half_mask(tq, half):
    lane = lax.broadcasted_iota(jnp.int32, (tq, LANES), 1)
    return (lane >= HEAD_DIM * half) & (lane < HEAD_DIM * (half + 1))


def _causal_logits(qh, k_ref, lanes, qi, tq, bias):
    n0 = qi * tq
    r = lax.broadcasted_iota(jnp.int32, (tq, tq), 0)
    c = lax.broadcasted_iota(jnp.int32, (tq, tq), 1)
    diag = _dot_nt(qh, k_ref[0, n0:n0 + tq, lanes]) + bias(n0, tq)
    parts = [jnp.where(r >= c, diag, NEG_INF)]
    if qi > 0:
        parts.append(_dot_nt(qh, k_ref[0, :n0, lanes]) + bias(0, n0))
    return parts


def _softmax_parts(parts):
    m = functools.reduce(jnp.maximum, [s.max(axis=-1, keepdims=True) for s in parts])
    exps = [jnp.exp(s - m) for s in parts]
    denom = functools.reduce(jnp.add, [e.sum(axis=-1, keepdims=True) for e in exps])
    return exps, denom


def _value_parts(v_ref, lanes, qi, tq):
    n0 = qi * tq
    return [v_ref[0, n0:n0 + tq, lanes]] + ([v_ref[0, :n0, lanes]] if qi > 0 else [])


def _weighted_values(weights, values):
    return functools.reduce(jnp.add, [_dot(w.astype(v.dtype), v) for w, v in zip(weights, values)])


def _dispatch_on_tile(branch, nq):
    qi = pl.program_id(2)
    for i in range(nq):
        pl.when(qi == i)(functools.partial(branch, qi=i))


def _fox_kernel(q_ref, k_ref, v_ref, c_ref, o_ref, *, tq, nq, npair):
    masks = [_half_mask(tq, h) for h in range(2)]

    def branch(qi):
        for p in range(npair):
            lanes = slice(p * LANES, (p + 1) * LANES)
            q = q_ref[0, :, lanes] * jnp.asarray(HEAD_DIM ** -0.5, q_ref.dtype)
            values = _value_parts(v_ref, lanes, qi, tq)
            outs = []
            for h in range(2):
                qh = jnp.where(masks[h], q, jnp.zeros_like(q))
                bias = lambda start, size: -c_ref[0, 2 * p + h, :, start:start + size]
                exps, denom = _softmax_parts(_causal_logits(qh, k_ref, lanes, qi, tq, bias))
                outs.append(_weighted_values(exps, values) / denom)
            o_ref[0, :, lanes] = jnp.where(masks[0], outs[0], outs[1]).astype(o_ref.dtype)

    _dispatch_on_tile(branch, nq)


def _diff_kernel(slope_ref, lam_ref, q_ref, k_ref, v_ref, g_ref, o_ref, *,
                 tq, nq, nhead, out_scale):
    lam = lam_ref[0]
    masks = [_half_mask(tq, h) for h in range(2)]

    def branch(qi):
        for j in range(nhead):
            lanes = slice(j * LANES, (j + 1) * LANES)
            slope = slope_ref[pl.program_id(1) * nhead + j]
            q = q_ref[0, :, lanes] * jnp.asarray(HEAD_DIM ** -0.5, q_ref.dtype)

            def bias(start, size):
                pos = start + lax.broadcasted_iota(jnp.int32, (1, size), 1)
                return slope * pos.astype(F32)

            maps = []
            for h in range(2):
                qh = jnp.where(masks[h], q, jnp.zeros_like(q))
                maps.append(_softmax_parts(_causal_logits(qh, k_ref, lanes, qi, tq, bias)))
            (e0, d0), (e1, d1) = maps
            w0, w1 = 1.0 / d0, lam / d1
            weights = [a * w0 - b * w1 for a, b in zip(e0, e1)]
            o = _weighted_values(weights, _value_parts(v_ref, lanes, qi, tq))
            o_ref[0, :, lanes] = (_rms(o, g_ref[...]) * out_scale).astype(o_ref.dtype)

    _dispatch_on_tile(branch, nq)


def fox_attention(z3, c4, *, q_col, k_col, v_col, tq=256, npair=3):
    b, s, _ = z3.shape
    tq = min(tq, s)
    w = npair * LANES
    groups = C_HEADS // 2 // npair
    assert q_col % npair == 0 and k_col % npair == 0 and v_col % npair == 0
    return pl.pallas_call(
        functools.partial(_fox_kernel, tq=tq, nq=s // tq, npair=npair),
        out_shape=jax.ShapeDtypeStruct((b, s, C_HEADS // 2 * LANES), BF16),
        grid=(b, groups, s // tq),
        in_specs=[pl.BlockSpec((1, tq, w), lambda bi, g, qi: (bi, qi, q_col // npair + g)),
                  pl.BlockSpec((1, s, w), lambda bi, g, qi: (bi, 0, k_col // npair + g)),
                  pl.BlockSpec((1, s, w), lambda bi, g, qi: (bi, 0, v_col // npair + g)),
                  pl.BlockSpec((1, 2 * npair, 1, s), lambda bi, g, qi: (bi, g, 0, 0))],
        out_specs=pl.BlockSpec((1, tq, w), lambda bi, g, qi: (bi, qi, g)),
        compiler_params=_params("parallel", "parallel", "arbitrary"),
        name="fox_attention",
    )(z3, z3, z3, c4)


def diff_attention(z3, slopes, lam, subln, *, q_col, k_col, v_col, out_scale, tq=256, nhead=2):
    b, s, _ = z3.shape
    tq = min(tq, s)
    w = nhead * LANES
    assert q_col % nhead == 0 and k_col % nhead == 0 and v_col % nhead == 0
    smem = pl.BlockSpec(memory_space=pltpu.SMEM)
    return pl.pallas_call(
        functools.partial(_diff_kernel, tq=tq, nq=s // tq, nhead=nhead, out_scale=out_scale),
        out_shape=jax.ShapeDtypeStruct((b, s, B_HEADS * LANES), BF16),
        grid=(b, B_HEADS // nhead, s // tq),
        in_specs=[smem, smem,
                  pl.BlockSpec((1, tq, w), lambda bi, g, qi: (bi, qi, q_col // nhead + g)),
                  pl.BlockSpec((1, s, w), lambda bi, g, qi: (bi, 0, k_col // nhead + g)),
                  pl.BlockSpec((1, s, w), lambda bi, g, qi: (bi, 0, v_col // nhead + g)),
                  pl.BlockSpec((1, LANES), lambda bi, g, qi: (0, 0))],
        out_specs=pl.BlockSpec((1, tq, w), lambda bi, g, qi: (bi, qi, g)),
        compiler_params=_params("parallel", "parallel", "arbitrary"),
        name="diff_attention",
    )(slopes, lam, z3, z3, z3, subln.reshape(1, LANES).astype(F32))


def _swa_kernel(slope_ref, sink_ref, q_ref, k_ref, v_ref, o_ref, *, blk):
    qi = pl.program_id(1)
    cur = pl.multiple_of(qi * blk, blk)
    prev = pl.multiple_of(jnp.maximum(qi - 1, 0) * blk, blk)
    r = lax.broadcasted_iota(jnp.int32, (blk, 2 * blk), 0)
    c = lax.broadcasted_iota(jnp.int32, (blk, 2 * blk), 1)
    dist = r + blk - c
    keep = (dist >= 0) & (dist < WINDOW) & ((qi > 0) | (c >= blk))
    dist_f = dist.astype(F32)
    scale = jnp.asarray(HEAD_DIM ** -0.5, q_ref.dtype)
    masks = [_half_mask(blk, h) for h in range(2)]
    for tile in range(A_Q_HEADS // 2):
        q = q_ref[0, :, tile * LANES:(tile + 1) * LANES] * scale
        halves = []
        for half in range(2):
            slot = 2 * tile + half
            kv_tile = (A_SLOT_HEAD[slot] // A_GROUP) // 2
            lanes = slice(kv_tile * LANES, (kv_tile + 1) * LANES)
            k = jnp.concatenate([k_ref[0, pl.ds(prev, blk), lanes],
                                 k_ref[0, pl.ds(cur, blk), lanes]], axis=0)
            v = jnp.concatenate([v_ref[0, pl.ds(prev, blk), lanes],
                                 v_ref[0, pl.ds(cur, blk), lanes]], axis=0)
            qh = jnp.where(masks[half], q, jnp.zeros_like(q))
            s = _dot_nt(qh, k) - slope_ref[slot] * dist_f
            s = jnp.where(keep, s, NEG_INF)
            sink = sink_ref[slot]
            m = jnp.maximum(jnp.max(s, axis=-1, keepdims=True), sink)
            p = jnp.exp(s - m)
            denom = jnp.sum(p, axis=-1, keepdims=True) + jnp.exp(sink - m)
            halves.append(_dot(p.astype(v.dtype), v) / denom)
        o_ref[0, :, tile * LANES:(tile + 1) * LANES] = jnp.where(
            masks[0], halves[0], halves[1]).astype(o_ref.dtype)


def swa_attention(z3, slopes, sinks, *, q_tile, k_tile, v_tile):
    b, s, _ = z3.shape
    blk = WINDOW
    qw = A_Q_HEADS * HEAD_DIM
    kw = A_KV_HEADS * HEAD_DIM
    smem = pl.BlockSpec(memory_space=pltpu.SMEM)
    return pl.pallas_call(
        functools.partial(_swa_kernel, blk=blk),
        out_shape=jax.ShapeDtypeStruct((b, s, qw), BF16),
        grid=(b, s // blk),
        in_specs=[smem, smem,
                  pl.BlockSpec((1, blk, qw), lambda bi, qi: (bi, qi, q_tile)),
                  pl.BlockSpec((1, s, kw), lambda bi, qi: (bi, 0, k_tile)),
                  pl.BlockSpec((1, s, kw), lambda bi, qi: (bi, 0, v_tile))],
        out_specs=pl.BlockSpec((1, blk, qw), lambda bi, qi: (bi, qi, 0)),
        compiler_params=_params("parallel", "parallel"),
        name="swa_attention",
    )(slopes, sinks, z3, z3, z3)


def _merge_kernel(oa_ref, ob_ref, oc_ref, wa_ref, wb_ref, wc_ref,
                  ga_ref, gb_ref, gc_ref, o_ref):
    acc = ga_ref[...].astype(F32) * _dot(oa_ref[...], wa_ref[...])
    acc += gb_ref[...].astype(F32) * _dot(ob_ref[...], wb_ref[...])
    acc += gc_ref[...].astype(F32) * _dot(oc_ref[...], wc_ref[...])
    o_ref[...] = acc.astype(o_ref.dtype)


def gated_merge(oa, ob, oc, wa, wb, wc, gates, *, tm=1024, tn=512):
    t = oa.shape[0]
    d = wa.shape[1]
    tm, tn = min(tm, t), min(tn, d)
    nj = d // tn
    row = lambda w: pl.BlockSpec((tm, w), lambda i, j: (i, 0))
    col = lambda w: pl.BlockSpec((w, tn), lambda i, j: (0, j))
    gate = lambda br: pl.BlockSpec((tm, tn), lambda i, j: (i, br * nj + j))
    return pl.pallas_call(
        _merge_kernel,
        out_shape=jax.ShapeDtypeStruct((t, d), BF16),
        grid=(t // tm, nj),
        in_specs=[row(oa.shape[1]), row(ob.shape[1]), row(oc.shape[1]),
                  col(wa.shape[0]), col(wb.shape[0]), col(wc.shape[0]),
                  gate(0), gate(1), gate(2)],
        out_specs=pl.BlockSpec((tm, tn), lambda i, j: (i, j)),
        compiler_params=_params("parallel", "parallel"),
        name="gated_merge",
    )(oa, ob, oc, wa, wb, wc, gates, gates, gates)


def _proj_res_kernel(a_ref, w_ref, x_ref, gp_ref, gn_ref, xo_ref, ho_ref, acc_ref, *, nk):
    kk = pl.program_id(1)
    part = _dot(a_ref[...], w_ref[...])

    @pl.when(kk == 0)
    def _():
        acc_ref[...] = part

    @pl.when(kk > 0)
    def _():
        acc_ref[...] += part

    @pl.when(kk == nk - 1)
    def _():
        x_new = x_ref[...] + _rms(acc_ref[...], gp_ref[...])
        xo_ref[...] = x_new
        ho_ref[...] = _rms(x_new, gn_ref[...]).astype(ho_ref.dtype)


def proj_residual(a, w, x, g_post, g_next, *, tm, tk):
    t, k = a.shape
    d = w.shape[1]
    tm, tk = min(tm, t), min(tk, k)
    nk = k // tk
    vec = pl.BlockSpec((1, d), lambda i, kk: (0, 0))
    return pl.pallas_call(
        functools.partial(_proj_res_kernel, nk=nk),
        out_shape=(jax.ShapeDtypeStruct((t, d), F32), jax.ShapeDtypeStruct((t, d), BF16)),
        grid=(t // tm, nk),
        in_specs=[pl.BlockSpec((tm, tk), lambda i, kk: (i, kk)),
                  pl.BlockSpec((tk, d), lambda i, kk: (kk, 0)),
                  pl.BlockSpec((tm, d), lambda i, kk: (i, 0)),
                  vec, vec],
        out_specs=(pl.BlockSpec((tm, d), lambda i, kk: (i, 0)),
                   pl.BlockSpec((tm, d), lambda i, kk: (i, 0))),
        scratch_shapes=[pltpu.VMEM((tm, d), F32)],
        compiler_params=_params("parallel", "arbitrary"),
        name="proj_residual",
    )(a, w, x, g_post.reshape(1, d), g_next.reshape(1, d))


def _swiglu_up_kernel(h_ref, wg_ref, wu_ref, o_ref):
    h = h_ref[...]
    g = _dot(h, wg_ref[...])
    u = _dot(h, wu_ref[...])
    o_ref[...] = (g * _sigmoid(g) * u).astype(o_ref.dtype)


def swiglu_up(h, wg, wu, *, tm=1024, tn=512):
    t, d = h.shape
    f = wg.shape[1]
    tm, tn = min(tm, t), min(tn, f)
    return pl.pallas_call(
        _swiglu_up_kernel,
        out_shape=jax.ShapeDtypeStruct((t, f), BF16),
        grid=(t // tm, f // tn),
        in_specs=[pl.BlockSpec((tm, d), lambda i, j: (i, 0)),
                  pl.BlockSpec((d, tn), lambda i, j: (0, j)),
                  pl.BlockSpec((d, tn), lambda i, j: (0, j))],
        out_specs=pl.BlockSpec((tm, tn), lambda i, j: (i, j)),
        compiler_params=_params("parallel", "parallel"),
        name="swiglu_up",
    )(h, wg, wu)


def _router_kernel(x_ref, g_ref, w_ref, idx_ref, wt_ref, h_ref):
    h = _rms(x_ref[...], g_ref[...])
    h_ref[...] = h
    logits = jnp.dot(h, w_ref[...], precision=lax.Precision.HIGHEST,
                     preferred_element_type=F32)
    lane = lax.broadcasted_iota(jnp.int32, logits.shape, 1)
    lane_f = lane.astype(F32)
    lg = jnp.where(lane < N_EXPERTS, logits, -jnp.inf)
    m1 = jnp.max(lg, axis=-1, keepdims=True)
    i1 = jnp.min(jnp.where(lg == m1, lane_f, float(LANES)), axis=-1, keepdims=True)
    lg2 = jnp.where(lane_f == i1, -jnp.inf, lg)
    m2 = jnp.max(lg2, axis=-1, keepdims=True)
    i2 = jnp.min(jnp.where(lg2 == m2, lane_f, float(LANES)), axis=-1, keepdims=True)
    e2 = jnp.exp(m2 - m1)
    w1 = 1.0 / (1.0 + e2)
    w2 = e2 / (1.0 + e2)
    idx_ref[...] = jnp.where(lane == 0, i1, jnp.where(lane == 1, i2, 0.0)).astype(jnp.int32)
    wt_ref[...] = jnp.where(lane == 0, w1, jnp.where(lane == 1, w2, 0.0))


def router_top2(x, g, w_router, *, tm=512):
    t, d = x.shape
    tm = min(tm, t)
    wp = jnp.zeros((d, LANES), F32).at[:, :N_EXPERTS].set(w_router.astype(F32))
    out = pl.BlockSpec((tm, LANES), lambda i: (i, 0))
    idx, wt, h = pl.pallas_call(
        _router_kernel,
        out_shape=(jax.ShapeDtypeStruct((t, LANES), jnp.int32),
                   jax.ShapeDtypeStruct((t, LANES), F32),
                   jax.ShapeDtypeStruct((t, d), F32)),
        grid=(t // tm,),
        in_specs=[pl.BlockSpec((tm, d), lambda i: (i, 0)),
                  pl.BlockSpec((1, d), lambda i: (0, 0)),
                  pl.BlockSpec((d, LANES), lambda i: (0, 0))],
        out_specs=(out, out, pl.BlockSpec((tm, d), lambda i: (i, 0))),
        compiler_params=_params("parallel"),
        name="router_top2",
    )(x, g.reshape(1, d), wp)
    return idx[:, :TOP_K], wt[:, :TOP_K], h


SC_INDEX_WINDOW = 128
SC_ROWS_PER_COPY = 16


def sc_gather_rows(src, idx):
    n = idx.shape[0]
    d = src.shape[1]
    assert n % SC_INDEX_WINDOW == 0
    mesh = plsc.VectorSubcoreMesh(core_axis_name="c", subcore_axis_name="s")
    dst = jnp.arange(n, dtype=jnp.int32).reshape(1, n)

    @pl.kernel(out_type=jax.ShapeDtypeStruct((n, d), src.dtype), mesh=mesh,
               scratch_types=[pltpu.VMEM((SC_ROWS_PER_COPY, d), src.dtype)])
    def gather(x_hbm, i_hbm, d_hbm, o_hbm, buf):
        def body(i_vmem, d_vmem):
            for s in range(SC_INDEX_WINDOW // SC_ROWS_PER_COPY):
                rows = pl.ds(s * SC_ROWS_PER_COPY, SC_ROWS_PER_COPY)
                pltpu.sync_copy(x_hbm.at[i_vmem.at[0, rows]], buf)
                pltpu.sync_copy(buf, o_hbm.at[d_vmem.at[0, rows]])

        window = pl.BlockSpec((1, SC_INDEX_WINDOW), lambda i: (0, i))
        pltpu.emit_pipeline(
            body, grid=(n // SC_INDEX_WINDOW,),
            in_specs=[window, window], out_specs=[],
            core_axis_name=("c", "s"),
            dimension_semantics=(pltpu.PARALLEL,),
        )(i_hbm, d_hbm)

    return gather(src, idx.reshape(1, n), dst)


def _moe_up_kernel(te_ref, na_ref, x_ref, wg_ref, wu_ref, o_ref):
    active = pl.program_id(1) < na_ref[0]

    @pl.when(active)
    def _():
        x = x_ref[...].astype(wg_ref.dtype)
        g = _dot(x, wg_ref[0])
        u = _dot(x, wu_ref[0])
        o_ref[...] = (g * _sigmoid(g) * u).astype(o_ref.dtype)

    @pl.when(jnp.logical_not(active))
    def _():
        o_ref[...] = jnp.zeros_like(o_ref)


def _moe_down_kernel(te_ref, na_ref, u_ref, wd_ref, rw_ref, o_ref):
    active = pl.program_id(1) < na_ref[0]

    @pl.when(active)
    def _():
        o_ref[...] = rw_ref[...] * _dot(u_ref[...], wd_ref[0])

    @pl.when(jnp.logical_not(active))
    def _():
        o_ref[...] = jnp.zeros_like(o_ref)


def _active_tile(i, na_ref):
    return jnp.minimum(i, na_ref[0] - 1)


def moe_up(xs, tile_expert, n_active, wg, wu, *, tm, tn=1024):
    p, d = xs.shape
    f = wg.shape[2]
    tn = min(tn, f)
    grid_spec = pltpu.PrefetchScalarGridSpec(
        num_scalar_prefetch=2,
        grid=(f // tn, p // tm),
        in_specs=[pl.BlockSpec((tm, d), lambda j, i, te, na: (_active_tile(i, na), 0)),
                  pl.BlockSpec((1, d, tn), lambda j, i, te, na: (te[_active_tile(i, na)], 0, j)),
                  pl.BlockSpec((1, d, tn), lambda j, i, te, na: (te[_active_tile(i, na)], 0, j))],
        out_specs=pl.BlockSpec((tm, tn), lambda j, i, te, na: (i, j)),
    )
    return pl.pallas_call(
        _moe_up_kernel,
        out_shape=jax.ShapeDtypeStruct((p, f), BF16),
        grid_spec=grid_spec,
        compiler_params=_params("arbitrary", "arbitrary"),
        name="moe_up",
    )(tile_expert, n_active, xs, wg, wu)


def moe_down(u, tile_expert, n_active, wd, row_w, *, tm, tn=512):
    p, f = u.shape
    d = wd.shape[2]
    tn = min(tn, d)
    grid_spec = pltpu.PrefetchScalarGridSpec(
        num_scalar_prefetch=2,
        grid=(d // tn, p // tm),
        in_specs=[pl.BlockSpec((tm, f), lambda j, i, te, na: (_active_tile(i, na), 0)),
                  pl.BlockSpec((1, f, tn), lambda j, i, te, na: (te[_active_tile(i, na)], 0, j)),
                  pl.BlockSpec((tm, 1), lambda j, i, te, na: (_active_tile(i, na), 0))],
        out_specs=pl.BlockSpec((tm, tn), lambda j, i, te, na: (i, j)),
    )
    return pl.pallas_call(
        _moe_down_kernel,
        out_shape=jax.ShapeDtypeStruct((p, d), F32),
        grid_spec=grid_spec,
        compiler_params=_params("arbitrary", "arbitrary"),
        name="moe_down",
    )(tile_expert, n_active, u, wd, row_w)


def _combine_kernel(ya_ref, yb_ref, x_ref, g_ref, o_ref):
    y = ya_ref[...].astype(F32) + yb_ref[...].astype(F32)
    o_ref[...] = x_ref[...] + _rms(y, g_ref[...])


def combine_residual(ya, yb, x, g, *, tm=512):
    t, d = x.shape
    tm = min(tm, t)
    blk = pl.BlockSpec((tm, d), lambda i: (i, 0))
    return pl.pallas_call(
        _combine_kernel,
        out_shape=jax.ShapeDtypeStruct((t, d), F32),
        grid=(t // tm,),
        in_specs=[blk, blk, blk, pl.BlockSpec((1, d), lambda i: (0, 0))],
        out_specs=blk,
        compiler_params=_params("parallel"),
        name="combine_residual",
    )(ya, yb, x, g.reshape(1, d))


def _alibi_slopes(n):
    return jnp.exp2(-8.0 * (jnp.arange(n, dtype=F32) + 1.0) / n)


def _token_mixer(x2, h, batch, seq, layer_idx, w_in, b_forget, b_gate, sinks,
                 lq1, lk1, lq2, lk2, subln, w_br_a, w_br_b, w_br_c, w_out,
                 g_post, g_next):
    t, d = x2.shape
    a_w = A_Q_HEADS * HEAD_DIM
    kv_w = A_KV_HEADS * HEAD_DIM
    b_w = B_HEADS * 2 * HEAD_DIM
    c_w = C_HEADS * HEAD_DIM
    n_attn = a_w + 2 * kv_w + 3 * b_w + 3 * c_w

    slot_cols = np.concatenate([np.arange(HEAD_DIM) + HEAD_DIM * hd for hd in A_SLOT_HEAD])
    a_end = a_w + 2 * kv_w
    b_end = a_end + 3 * b_w
    w_in_t = jnp.transpose(w_in, (2, 0, 1))
    rows = lambda lo, hi: w_in_t[lo:hi, layer_idx, :]
    w_attn_t = jnp.concatenate([rows(b_end, n_attn), rows(a_end, b_end),
                                rows(0, a_w)[slot_cols], rows(a_w, a_end)],
                               axis=0).astype(BF16)
    w_f_t = rows(n_attn, n_attn + C_HEADS)
    w_g_t = rows(n_attn + C_HEADS, w_in.shape[2]).astype(BF16)

    z3 = matmul(h, w_attn_t).reshape(batch, seq, n_attn)
    gates = matmul(h, w_g_t, bias=b_gate)
    c = forget_cumsum(h.reshape(batch, seq, d), w_f_t, b_forget)
    c4 = c[:, :C_HEADS].reshape(batch, C_HEADS, 1, seq)

    oc = fox_attention(z3, c4, q_col=0, k_col=C_HEADS // 2, v_col=C_HEADS)

    lam_init = 0.8 - 0.6 * math.exp(-0.3 * layer_idx)
    lam = (jnp.exp(jnp.sum(lq1.astype(F32) * lk1.astype(F32)))
           - jnp.exp(jnp.sum(lq2.astype(F32) * lk2.astype(F32))) + lam_init)
    b0 = 3 * c_w // LANES
    ob = diff_attention(z3, _alibi_slopes(B_HEADS), lam.reshape(1), subln,
                        q_col=b0, k_col=b0 + B_HEADS, v_col=b0 + 2 * B_HEADS,
                        out_scale=1.0 - lam_init)

    slot = np.array(A_SLOT_HEAD)
    a0 = 3 * c_w + 3 * b_w
    oa = swa_attention(z3, _alibi_slopes(A_Q_HEADS)[slot], sinks.astype(F32)[slot],
                       q_tile=a0 // a_w, k_tile=(a0 + a_w) // kv_w,
                       v_tile=(a0 + a_w) // kv_w + 1)

    merged = gated_merge(oa.reshape(t, a_w), ob.reshape(t, b_w), oc.reshape(t, c_w),
                         w_br_a[slot_cols].astype(BF16), w_br_b.astype(BF16),
                         w_br_c.astype(BF16), gates)
    return proj_residual(merged, w_out.astype(BF16), x2, g_post, g_next, tm=256, tk=d)


def _moe_dispatch(idx, wts, tm):
    t = idx.shape[0]
    n_rows = t * TOP_K
    n_tiles = (n_rows + N_EXPERTS * (tm - 1)) // tm
    e_flat = idx.reshape(n_rows)
    onehot = (e_flat[:, None] == jnp.arange(N_EXPERTS)[None, :]).astype(jnp.int32)
    rank = jnp.sum((jnp.cumsum(onehot, axis=0) - 1) * onehot, axis=1)
    counts = jnp.sum(onehot, axis=0)
    padded = ((counts + tm - 1) // tm) * tm
    ends = jnp.cumsum(padded)
    starts = ends - padded
    pos = starts[e_flat] + rank
    row_token = jnp.zeros((n_tiles * tm,), jnp.int32).at[pos].set(
        jnp.arange(n_rows, dtype=jnp.int32) // TOP_K)
    row_w = jnp.zeros((n_tiles * tm,), F32).at[pos].set(wts.reshape(n_rows))
    tile_start = jnp.arange(n_tiles, dtype=jnp.int32) * tm
    tile_expert = jnp.minimum(jnp.sum(tile_start[:, None] >= ends[None, :], axis=1),
                              N_EXPERTS - 1).astype(jnp.int32)
    n_active = (ends[-1] // tm).astype(jnp.int32).reshape(1)
    return pos.reshape(t, TOP_K), row_token, row_w.reshape(-1, 1), tile_expert, n_active


def _moe_ffn(x2, g_pre, g_post, w_router, e_gate, e_up, e_down, *, tm=512):
    idx, wts, h = router_top2(x2, g_pre, w_router)
    t = x2.shape[0]
    tm = min(tm, t)
    pos, row_token, row_w, tile_expert, n_active = _moe_dispatch(idx, wts, tm)
    xs = sc_gather_rows(h, row_token)
    u = moe_up(xs, tile_expert, n_active, e_gate.astype(BF16), e_up.astype(BF16), tm=tm)
    ys = moe_down(u, tile_expert, n_active, e_down.astype(BF16), row_w, tm=tm)
    y2 = sc_gather_rows(ys, pos.T.reshape(TOP_K * t))
    return combine_residual(y2[:t], y2[t:], x2, g_post)


def kernel(x, mix_pre_norm, w_in, b_forget, b_gate, attn_sinks, lam_q1, lam_k1, lam_q2, lam_k2, diff_subln, w_br_a, w_br_b, w_br_c, w_out, mix_post_norm, ffn_pre_norm, ffn_post_norm, dense_w_gate, dense_w_up, dense_w_down, w_router, moe_w_gate, moe_w_up, moe_w_down):
    batch, seq, d = x.shape
    depth = w_in.shape[0]
    x2 = x.reshape(batch * seq, d)
    h = rmsnorm_rows(x2, mix_pre_norm[0])
    for l in range(depth):
        x2, h = _token_mixer(x2, h, batch, seq, l, w_in, b_forget[l], b_gate[l],
                             attn_sinks[l], lam_q1[l], lam_k1[l], lam_q2[l], lam_k2[l],
                             diff_subln[l], w_br_a[l], w_br_b[l], w_br_c[l], w_out[l],
                             mix_post_norm[l], ffn_pre_norm[l])
        i = l // 2
        if l % 2 == 0:
            u = swiglu_up(h, dense_w_gate[i].astype(BF16), dense_w_up[i].astype(BF16))
            g_next = mix_pre_norm[l + 1] if l + 1 < depth else jnp.ones((d,), F32)
            f = dense_w_down.shape[1]
            x2, h = proj_residual(u, dense_w_down[i].astype(BF16), x2, ffn_post_norm[l],
                                  g_next, tm=512, tk=f // 4)
        else:
            x2 = _moe_ffn(x2, ffn_pre_norm[l], ffn_post_norm[l], w_router[i],
                          moe_w_gate[i], moe_w_up[i], moe_w_down[i])
            if l + 1 < depth:
                h = rmsnorm_rows(x2, mix_pre_norm[l + 1])
    return x2.reshape(batch, seq, d)
```

```python
import functools
import math

import jax
import jax.numpy as jnp
import numpy as np
from jax import lax
from jax.experimental import pallas as pl
from jax.experimental.pallas import tpu as pltpu
from jax.experimental.pallas import tpu_sc as plsc

F32 = jnp.float32
BF16 = jnp.bfloat16

HEAD_DIM = 64
LANES = 128
A_Q_HEADS = 12
A_KV_HEADS = 4
A_GROUP = A_Q_HEADS // A_KV_HEADS
WINDOW = 128
B_HEADS = 4
C_HEADS = 12
N_BRANCH = 3
N_EXPERTS = 8
TOP_K = 2
RMS_EPS = 1e-6
NEG_INF = -1e30
VMEM_LIMIT = 56 * 1024 * 1024

A_SLOT_HEAD = (0, 3, 1, 4, 2, 5, 6, 9, 7, 10, 8, 11)


def _params(*sem):
    return pltpu.CompilerParams(dimension_semantics=sem, vmem_limit_bytes=VMEM_LIMIT)


def _rms(x, g):
    var = jnp.mean(x * x, axis=-1, keepdims=True)
    return x * lax.rsqrt(var + RMS_EPS) * g


def _sigmoid(x):
    return 0.5 * jnp.tanh(0.5 * x) + 0.5


def _dot(a, b):
    return jnp.dot(a, b, preferred_element_type=F32)


def _dot_nt(a, b):
    return lax.dot_general(a, b, (((1,), (1,)), ((), ())), preferred_element_type=F32)


def _rmsnorm_kernel(x_ref, g_ref, o_ref):
    o_ref[...] = _rms(x_ref[...], g_ref[...]).astype(o_ref.dtype)


def rmsnorm_rows(x, g, *, tm=512):
    t, d = x.shape
    tm = min(tm, t)
    return pl.pallas_call(
        _rmsnorm_kernel,
        out_shape=jax.ShapeDtypeStruct((t, d), BF16),
        grid=(t // tm,),
        in_specs=[pl.BlockSpec((tm, d), lambda i: (i, 0)),
                  pl.BlockSpec((1, d), lambda i: (0, 0))],
        out_specs=pl.BlockSpec((tm, d), lambda i: (i, 0)),
        compiler_params=_params("parallel"),
        name="rmsnorm_rows",
    )(x, g.reshape(1, d))


def _mm_kernel(a_ref, bt_ref, o_ref):
    o_ref[...] = _dot_nt(a_ref[...], bt_ref[...]).astype(o_ref.dtype)


def _mm_gate_kernel(a_ref, bt_ref, bias_ref, o_ref):
    z = _dot_nt(a_ref[...], bt_ref[...]) + bias_ref[...]
    o_ref[...] = _sigmoid(z).astype(o_ref.dtype)


def matmul(a, bt, *, bias=None, out_dtype=BF16, tm=1024, tn=1024):
    m, k = a.shape
    n = bt.shape[0]
    tm, tn = min(tm, m), min(tn, n)
    assert m % tm == 0 and n % tn == 0
    in_specs = [pl.BlockSpec((tm, k), lambda i, j: (i, 0)),
                pl.BlockSpec((tn, k), lambda i, j: (j, 0))]
    args = [a, bt]
    kern = _mm_kernel
    if bias is not None:
        in_specs.append(pl.BlockSpec((1, tn), lambda i, j: (0, j)))
        args.append(bias.reshape(1, n).astype(F32))
        kern = _mm_gate_kernel
    return pl.pallas_call(
        kern,
        out_shape=jax.ShapeDtypeStruct((m, n), out_dtype),
        grid=(m // tm, n // tn),
        in_specs=in_specs,
        out_specs=pl.BlockSpec((tm, tn), lambda i, j: (i, j)),
        compiler_params=_params("parallel", "parallel"),
        name="matmul_gate" if bias is not None else "matmul",
    )(*args)


def _forget_kernel(h_ref, wt_ref, b_ref, c_ref, *, seq, blk):
    z = _dot_nt(wt_ref[...], h_ref[0]) + b_ref[...]
    log_f = jnp.minimum(z, 0.0) - jnp.log1p(jnp.exp(-jnp.abs(z)))
    r = lax.broadcasted_iota(jnp.int32, (blk, blk), 0)
    c = lax.broadcasted_iota(jnp.int32, (blk, blk), 1)
    upper = (r <= c).astype(F32)
    carry = jnp.zeros((LANES, 1), F32)
    for j in range(seq // blk):
        part = lax.dot_general(log_f[:, j * blk:(j + 1) * blk], upper,
                               (((1,), (0,)), ((), ())),
                               precision=lax.Precision.HIGHEST,
                               preferred_element_type=F32) + carry
        c_ref[0, :, j * blk:(j + 1) * blk] = part
        carry = part[:, blk - 1:blk]


def forget_cumsum(h3, w_ft, b_f):
    b, s, d = h3.shape
    nh = w_ft.shape[0]
    wt = jnp.zeros((LANES, d), BF16).at[:nh].set(w_ft.astype(BF16))
    bias = jnp.zeros((LANES, 1), F32).at[:nh, 0].set(b_f.astype(F32))
    blk = min(256, s)
    return pl.pallas_call(
        functools.partial(_forget_kernel, seq=s, blk=blk),
        out_shape=jax.ShapeDtypeStruct((b, LANES, s), F32),
        grid=(b,),
        in_specs=[pl.BlockSpec((1, s, d), lambda i: (i, 0, 0)),
                  pl.BlockSpec((LANES, d), lambda i: (0, 0)),
                  pl.BlockSpec((LANES, 1), lambda i: (0, 0))],
        out_specs=pl.BlockSpec((1, LANES, s), lambda i: (i, 0, 0)),
        compiler_params=_params("parallel"),
        name="forget_cumsum",
    )(h3, wt, bias)


def _half_mask(tq, half):
    lane = lax.broadcasted_iota(jnp.int32, (tq, LANES), 1)
    return (lane >= HEAD_DIM * half) & (lane < HEAD_DIM * (half + 1))


def _causal_logits(qh, k_ref, lanes, qi, tq, bias):
    n0 = qi * tq
    r = lax.broadcasted_iota(jnp.int32, (tq, tq), 0)
    c = lax.broadcasted_iota(jnp.int32, (tq, tq), 1)
    diag = _dot_nt(qh, k_ref[0, n0:n0 + tq, lanes]) + bias(n0, tq)
    parts = [jnp.where(r >= c, diag, NEG_INF)]
    if qi > 0:
        parts.append(_dot_nt(qh, k_ref[0, :n0, lanes]) + bias(0, n0))
    return parts


def _softmax_parts(parts):
    m = functools.reduce(jnp.maximum, [s.max(axis=-1, keepdims=True) for s in parts])
    exps = [jnp.exp(s - m) for s in parts]
    denom = functools.reduce(jnp.add, [e.sum(axis=-1, keepdims=True) for e in exps])
    return exps, denom


def _value_parts(v_ref, lanes, qi, tq):
    n0 = qi * tq
    return [v_ref[0, n0:n0 + tq, lanes]] + ([v_ref[0, :n0, lanes]] if qi > 0 else [])


def _weighted_values(weights, values):
    return functools.reduce(jnp.add, [_dot(w.astype(v.dtype), v) for w, v in zip(weights, values)])


def _dispatch_on_tile(branch, nq):
    qi = pl.program_id(2)
    for i in range(nq):
        pl.when(qi == i)(functools.partial(branch, qi=i))


def _fox_kernel(q_ref, k_ref, v_ref, c_ref, o_ref, *, tq, nq, npair):
    masks = [_half_mask(tq, h) for h in range(2)]

    def branch(qi):
        for p in range(npair):
            lanes = slice(p * LANES, (p + 1) * LANES)
            q = q_ref[0, :, lanes] * jnp.asarray(HEAD_DIM ** -0.5, q_ref.dtype)
            values = _value_parts(v_ref, lanes, qi, tq)
            outs = []
            for h in range(2):
                qh = jnp.where(masks[h], q, jnp.zeros_like(q))
                bias = lambda start, size: -c_ref[0, 2 * p + h, :, start:start + size]
                exps, denom = _softmax_parts(_causal_logits(qh, k_ref, lanes, qi, tq, bias))
                outs.append(_weighted_values(exps, values) / denom)
            o_ref[0, :, lanes] = jnp.where(masks[0], outs[0], outs[1]).astype(o_ref.dtype)

    _dispatch_on_tile(branch, nq)


def _diff_kernel(slope_ref, lam_ref, q_ref, k_ref, v_ref, g_ref, o_ref, *,
                 tq, nq, nhead, out_scale):
    lam = lam_ref[0]
    masks = [_half_mask(tq, h) for h in range(2)]

    def branch(qi):
        for j in range(nhead):
            lanes = slice(j * LANES, (j + 1) * LANES)
            slope = slope_ref[pl.program_id(1) * nhead + j]
            q = q_ref[0, :, lanes] * jnp.asarray(HEAD_DIM ** -0.5, q_ref.dtype)

            def bias(start, size):
                pos = start + lax.broadcasted_iota(jnp.int32, (1, size), 1)
                return slope * pos.astype(F32)

            maps = []
            for h in range(2):
                qh = jnp.where(masks[h], q, jnp.zeros_like(q))
                maps.append(_softmax_parts(_causal_logits(qh, k_ref, lanes, qi, tq, bias)))
            (e0, d0), (e1, d1) = maps
            w0, w1 = 1.0 / d0, lam / d1
            weights = [a * w0 - b * w1 for a, b in zip(e0, e1)]
            o = _weighted_values(weights, _value_parts(v_ref, lanes, qi, tq))
            o_ref[0, :, lanes] = (_rms(o, g_ref[...]) * out_scale).astype(o_ref.dtype)

    _dispatch_on_tile(branch, nq)


def fox_attention(z3, c4, *, q_col, k_col, v_col, tq=256, npair=3):
    b, s, _ = z3.shape
    tq = min(tq, s)
    w = npair * LANES
    groups = C_HEADS // 2 // npair
    assert q_col % npair == 0 and k_col % npair == 0 and v_col % npair == 0
    return pl.pallas_call(
        functools.partial(_fox_kernel, tq=tq, nq=s // tq, npair=npair),
        out_shape=jax.ShapeDtypeStruct((b, s, C_HEADS // 2 * LANES), BF16),
        grid=(b, groups, s // tq),
        in_specs=[pl.BlockSpec((1, tq, w), lambda bi, g, qi: (bi, qi, q_col // npair + g)),
                  pl.BlockSpec((1, s, w), lambda bi, g, qi: (bi, 0, k_col // npair + g)),
                  pl.BlockSpec((1, s, w), lambda bi, g, qi: (bi, 0, v_col // npair + g)),
                  pl.BlockSpec((1, 2 * npair, 1, s), lambda bi, g, qi: (bi, g, 0, 0))],
        out_specs=pl.BlockSpec((1, tq, w), lambda bi, g, qi: (bi, qi, g)),
        compiler_params=_params("parallel", "parallel", "arbitrary"),
        name="fox_attention",
    )(z3, z3, z3, c4)


def diff_attention(z3, slopes, lam, subln, *, q_col, k_col, v_col, out_scale, tq=256, nhead=2):
    b, s, _ = z3.shape
    tq = min(tq, s)
    w = nhead * LANES
    assert q_col % nhead == 0 and k_col % nhead == 0 and v_col % nhead == 0
    smem = pl.BlockSpec(memory_space=pltpu.SMEM)
    return pl.pallas_call(
        functools.partial(_diff_kernel, tq=tq, nq=s // tq, nhead=nhead, out_scale=out_scale),
        out_shape=jax.ShapeDtypeStruct((b, s, B_HEADS * LANES), BF16),
        grid=(b, B_HEADS // nhead, s // tq),
        in_specs=[smem, smem,
                  pl.BlockSpec((1, tq, w), lambda bi, g, qi: (bi, qi, q_col // nhead + g)),
                  pl.BlockSpec((1, s, w), lambda bi, g, qi: (bi, 0, k_col // nhead + g)),
                  pl.BlockSpec((1, s, w), lambda bi, g, qi: (bi, 0, v_col // nhead + g)),
                  pl.BlockSpec((1, LANES), lambda bi, g, qi: (0, 0))],
        out_specs=pl.BlockSpec((1, tq, w), lambda bi, g, qi: (bi, qi, g)),
        compiler_params=_params("parallel", "parallel", "arbitrary"),
        name="diff_attention",
    )(slopes, lam, z3, z3, z3, subln.reshape(1, LANES).astype(F32))


def _swa_kernel(slope_ref, sink_ref, q_ref, k_ref, v_ref, o_ref, *, blk):
    qi = pl.program_id(1)
    cur = pl.multiple_of(qi * blk, blk)
    prev = pl.multiple_of(jnp.maximum(qi - 1, 0) * blk, blk)
    r = lax.broadcasted_iota(jnp.int32, (blk, 2 * blk), 0)
    c = lax.broadcasted_iota(jnp.int32, (blk, 2 * blk), 1)
    dist = r + blk - c
    keep = (dist >= 0) & (dist < WINDOW) & ((qi > 0) | (c >= blk))
    dist_f = dist.astype(F32)
    scale = jnp.asarray(HEAD_DIM ** -0.5, q_ref.dtype)
    masks = [_half_mask(blk, h) for h in range(2)]
    for tile in range(A_Q_HEADS // 2):
        q = q_ref[0, :, tile * LANES:(tile + 1) * LANES] * scale
        halves = []
        for half in range(2):
            slot = 2 * tile + half
            kv_tile = (A_SLOT_HEAD[slot] // A_GROUP) // 2
            lanes = slice(kv_tile * LANES, (kv_tile + 1) * LANES)
            k = jnp.concatenate([k_ref[0, pl.ds(prev, blk), lanes],
                                 k_ref[0, pl.ds(cur, blk), lanes]], axis=0)
            v = jnp.concatenate([v_ref[0, pl.ds(prev, blk), lanes],
                                 v_ref[0, pl.ds(cur, blk), lanes]], axis=0)
            qh = jnp.where(masks[half], q, jnp.zeros_like(q))
            s = _dot_nt(qh, k) - slope_ref[slot] * dist_f
            s = jnp.where(keep, s, NEG_INF)
            sink = sink_ref[slot]
            m = jnp.maximum(jnp.max(s, axis=-1, keepdims=True), sink)
            p = jnp.exp(s - m)
            denom = jnp.sum(p, axis=-1, keepdims=True) + jnp.exp(sink - m)
            halves.append(_dot(p.astype(v.dtype), v) / denom)
        o_ref[0, :, tile * LANES:(tile + 1) * LANES] = jnp.where(
            masks[0], halves[0], halves[1]).astype(o_ref.dtype)


def swa_attention(z3, slopes, sinks, *, q_tile, k_tile, v_tile):
    b, s, _ = z3.shape
    blk = WINDOW
    qw = A_Q_HEADS * HEAD_DIM
    kw = A_KV_HEADS * HEAD_DIM
    smem = pl.BlockSpec(memory_space=pltpu.SMEM)
    return pl.pallas_call(
        functools.partial(_swa_kernel, blk=blk),
        out_shape=jax.ShapeDtypeStruct((b, s, qw), BF16),
        grid=(b, s // blk),
        in_specs=[smem, smem,
                  pl.BlockSpec((1, blk, qw), lambda bi, qi: (bi, qi, q_tile)),
                  pl.BlockSpec((1, s, kw), lambda bi, qi: (bi, 0, k_tile)),
                  pl.BlockSpec((1, s, kw), lambda bi, qi: (bi, 0, v_tile))],
        out_specs=pl.BlockSpec((1, blk, qw), lambda bi, qi: (bi, qi, 0)),
        compiler_params=_params("parallel", "parallel"),
        name="swa_attention",
    )(slopes, sinks, z3, z3, z3)


def _merge_kernel(oa_ref, ob_ref, oc_ref, wa_ref, wb_ref, wc_ref,
                  ga_ref, gb_ref, gc_ref, o_ref):
    acc = ga_ref[...].astype(F32) * _dot(oa_ref[...], wa_ref[...])
    acc += gb_ref[...].astype(F32) * _dot(ob_ref[...], wb_ref[...])
    acc += gc_ref[...].astype(F32) * _dot(oc_ref[...], wc_ref[...])
    o_ref[...] = acc.astype(o_ref.dtype)


def gated_merge(oa, ob, oc, wa, wb, wc, gates, *, tm=1024, tn=512):
    t = oa.shape[0]
    d = wa.shape[1]
    tm, tn = min(tm, t), min(tn, d)
    nj = d // tn
    row = lambda w: pl.BlockSpec((tm, w), lambda i, j: (i, 0))
    col = lambda w: pl.BlockSpec((w, tn), lambda i, j: (0, j))
    gate = lambda br: pl.BlockSpec((tm, tn), lambda i, j: (i, br * nj + j))
    return pl.pallas_call(
        _merge_kernel,
        out_shape=jax.ShapeDtypeStruct((t, d), BF16),
        grid=(t // tm, nj),
        in_specs=[row(oa.shape[1]), row(ob.shape[1]), row(oc.shape[1]),
                  col(wa.shape[0]), col(wb.shape[0]), col(wc.shape[0]),
                  gate(0), gate(1), gate(2)],
        out_specs=pl.BlockSpec((tm, tn), lambda i, j: (i, j)),
        compiler_params=_params("parallel", "parallel"),
        name="gated_merge",
    )(oa, ob, oc, wa, wb, wc, gates, gates, gates)


def _proj_res_kernel(a_ref, w_ref, x_ref, gp_ref, gn_ref, xo_ref, ho_ref, acc_ref, *, nk):
    kk = pl.program_id(1)
    part = _dot(a_ref[...], w_ref[...])

    @pl.when(kk == 0)
    def _():
        acc_ref[...] = part

    @pl.when(kk > 0)
    def _():
        acc_ref[...] += part

    @pl.when(kk == nk - 1)
    def _():
        x_new = x_ref[...] + _rms(acc_ref[...], gp_ref[...])
        xo_ref[...] = x_new
        ho_ref[...] = _rms(x_new, gn_ref[...]).astype(ho_ref.dtype)


def proj_residual(a, w, x, g_post, g_next, *, tm, tk):
    t, k = a.shape
    d = w.shape[1]
    tm, tk = min(tm, t), min(tk, k)
    nk = k // tk
    vec = pl.BlockSpec((1, d), lambda i, kk: (0, 0))
    w_mode = dict(pipeline_mode=pl.Buffered(1)) if nk == 1 else {}
    return pl.pallas_call(
        functools.partial(_proj_res_kernel, nk=nk),
        out_shape=(jax.ShapeDtypeStruct((t, d), F32), jax.ShapeDtypeStruct((t, d), BF16)),
        grid=(t // tm, nk),
        in_specs=[pl.BlockSpec((tm, tk), lambda i, kk: (i, kk)),
                  pl.BlockSpec((tk, d), lambda i, kk: (kk, 0), **w_mode),
                  pl.BlockSpec((tm, d), lambda i, kk: (i, 0)),
                  vec, vec],
        out_specs=(pl.BlockSpec((tm, d), lambda i, kk: (i, 0)),
                   pl.BlockSpec((tm, d), lambda i, kk: (i, 0))),
        scratch_shapes=[pltpu.VMEM((tm, d), F32)],
        compiler_params=_params("parallel", "arbitrary"),
        name="proj_residual",
    )(a, w, x, g_post.reshape(1, d), g_next.reshape(1, d))


def _swiglu_up_kernel(h_ref, wg_ref, wu_ref, o_ref):
    h = h_ref[...]
    g = _dot(h, wg_ref[...])
    u = _dot(h, wu_ref[...])
    o_ref[...] = (g * _sigmoid(g) * u).astype(o_ref.dtype)


def swiglu_up(h, wg, wu, *, tm=1024, tn=512):
    t, d = h.shape
    f = wg.shape[1]
    tm, tn = min(tm, t), min(tn, f)
    return pl.pallas_call(
        _swiglu_up_kernel,
        out_shape=jax.ShapeDtypeStruct((t, f), BF16),
        grid=(t // tm, f // tn),
        in_specs=[pl.BlockSpec((tm, d), lambda i, j: (i, 0)),
                  pl.BlockSpec((d, tn), lambda i, j: (0, j)),
                  pl.BlockSpec((d, tn), lambda i, j: (0, j))],
        out_specs=pl.BlockSpec((tm, tn), lambda i, j: (i, j)),
        compiler_params=_params("parallel", "parallel"),
        name="swiglu_up",
    )(h, wg, wu)


def _router_kernel(x_ref, g_ref, w_ref, idx_ref, wt_ref, h_ref):
    h = _rms(x_ref[...], g_ref[...])
    h_ref[...] = h
    logits = jnp.dot(h, w_ref[...], precision=lax.Precision.HIGHEST,
                     preferred_element_type=F32)
    lane = lax.broadcasted_iota(jnp.int32, logits.shape, 1)
    lane_f = lane.astype(F32)
    lg = jnp.where(lane < N_EXPERTS, logits, -jnp.inf)
    m1 = jnp.max(lg, axis=-1, keepdims=True)
    i1 = jnp.min(jnp.where(lg == m1, lane_f, float(LANES)), axis=-1, keepdims=True)
    lg2 = jnp.where(lane_f == i1, -jnp.inf, lg)
    m2 = jnp.max(lg2, axis=-1, keepdims=True)
    i2 = jnp.min(jnp.where(lg2 == m2, lane_f, float(LANES)), axis=-1, keepdims=True)
    e2 = jnp.exp(m2 - m1)
    w1 = 1.0 / (1.0 + e2)
    w2 = e2 / (1.0 + e2)
    idx_ref[...] = jnp.where(lane == 0, i1, jnp.where(lane == 1, i2, 0.0)).astype(jnp.int32)
    wt_ref[...] = jnp.where(lane == 0, w1, jnp.where(lane == 1, w2, 0.0))


def router_top2(x, g, w_router, *, tm=512):
    t, d = x.shape
    tm = min(tm, t)
    wp = jnp.zeros((d, LANES), F32).at[:, :N_EXPERTS].set(w_router.astype(F32))
    out = pl.BlockSpec((tm, LANES), lambda i: (i, 0))
    idx, wt, h = pl.pallas_call(
        _router_kernel,
        out_shape=(jax.ShapeDtypeStruct((t, LANES), jnp.int32),
                   jax.ShapeDtypeStruct((t, LANES), F32),
                   jax.ShapeDtypeStruct((t, d), F32)),
        grid=(t // tm,),
        in_specs=[pl.BlockSpec((tm, d), lambda i: (i, 0)),
                  pl.BlockSpec((1, d), lambda i: (0, 0)),
                  pl.BlockSpec((d, LANES), lambda i: (0, 0))],
        out_specs=(out, out, pl.BlockSpec((tm, d), lambda i: (i, 0))),
        compiler_params=_params("parallel"),
        name="router_top2",
    )(x, g.reshape(1, d), wp)
    return idx[:, :TOP_K], wt[:, :TOP_K], h


SC_INDEX_WINDOW = 128
SC_ROWS_PER_COPY = 16


def sc_gather_rows(src, idx):
    n = idx.shape[0]
    d = src.shape[1]
    assert n % SC_INDEX_WINDOW == 0
    mesh = plsc.VectorSubcoreMesh(core_axis_name="c", subcore_axis_name="s")
    dst = jnp.arange(n, dtype=jnp.int32).reshape(1, n)

    @pl.kernel(out_type=jax.ShapeDtypeStruct((n, d), src.dtype), mesh=mesh,
               scratch_types=[pltpu.VMEM((SC_ROWS_PER_COPY, d), src.dtype)])
    def gather(x_hbm, i_hbm, d_hbm, o_hbm, buf):
        def body(i_vmem, d_vmem):
            for s in range(SC_INDEX_WINDOW // SC_ROWS_PER_COPY):
                rows = pl.ds(s * SC_ROWS_PER_COPY, SC_ROWS_PER_COPY)
                pltpu.sync_copy(x_hbm.at[i_vmem.at[0, rows]], buf)
                pltpu.sync_copy(buf, o_hbm.at[d_vmem.at[0, rows]])

        window = pl.BlockSpec((1, SC_INDEX_WINDOW), lambda i: (0, i))
        pltpu.emit_pipeline(
            body, grid=(n // SC_INDEX_WINDOW,),
            in_specs=[window, window], out_specs=[],
            core_axis_name=("c", "s"),
            dimension_semantics=(pltpu.PARALLEL,),
        )(i_hbm, d_hbm)

    return gather(src, idx.reshape(1, n), dst)


def _moe_up_kernel(te_ref, na_ref, x_ref, wg_ref, wu_ref, o_ref):
    active = pl.program_id(1) < na_ref[0]

    @pl.when(active)
    def _():
        x = x_ref[...].astype(wg_ref.dtype)
        g = _dot(x, wg_ref[0])
        u = _dot(x, wu_ref[0])
        o_ref[...] = (g * _sigmoid(g) * u).astype(o_ref.dtype)

    @pl.when(jnp.logical_not(active))
    def _():
        o_ref[...] = jnp.zeros_like(o_ref)


def _moe_down_kernel(te_ref, na_ref, u_ref, wd_ref, o_ref):
    active = pl.program_id(1) < na_ref[0]

    @pl.when(active)
    def _():
        o_ref[...] = _dot(u_ref[...], wd_ref[0])

    @pl.when(jnp.logical_not(active))
    def _():
        o_ref[...] = jnp.zeros_like(o_ref)


def _active_tile(i, na_ref):
    return jnp.minimum(i, na_ref[0] - 1)


def moe_up(xs, tile_expert, n_active, wg, wu, *, tm, tn=1024):
    p, d = xs.shape
    f = wg.shape[2]
    tn = min(tn, f)
    grid_spec = pltpu.PrefetchScalarGridSpec(
        num_scalar_prefetch=2,
        grid=(f // tn, p // tm),
        in_specs=[pl.BlockSpec((tm, d), lambda j, i, te, na: (_active_tile(i, na), 0)),
                  pl.BlockSpec((1, d, tn), lambda j, i, te, na: (te[_active_tile(i, na)], 0, j)),
                  pl.BlockSpec((1, d, tn), lambda j, i, te, na: (te[_active_tile(i, na)], 0, j))],
        out_specs=pl.BlockSpec((tm, tn), lambda j, i, te, na: (i, j)),
    )
    return pl.pallas_call(
        _moe_up_kernel,
        out_shape=jax.ShapeDtypeStruct((p, f), BF16),
        grid_spec=grid_spec,
        compiler_params=_params("arbitrary", "arbitrary"),
        name="moe_up",
    )(tile_expert, n_active, xs, wg, wu)


def moe_down(u, tile_expert, n_active, wd, *, tm, tn=512):
    p, f = u.shape
    d = wd.shape[2]
    tn = min(tn, d)
    grid_spec = pltpu.PrefetchScalarGridSpec(
        num_scalar_prefetch=2,
        grid=(d // tn, p // tm),
        in_specs=[pl.BlockSpec((tm, f), lambda j, i, te, na: (_active_tile(i, na), 0)),
                  pl.BlockSpec((1, f, tn), lambda j, i, te, na: (te[_active_tile(i, na)], 0, j))],
        out_specs=pl.BlockSpec((tm, tn), lambda j, i, te, na: (i, j)),
    )
    return pl.pallas_call(
        _moe_down_kernel,
        out_shape=jax.ShapeDtypeStruct((p, d), F32),
        grid_spec=grid_spec,
        compiler_params=_params("arbitrary", "arbitrary"),
        name="moe_down",
    )(tile_expert, n_active, u, wd)


def _combine_kernel(ya_ref, yb_ref, w_ref, x_ref, g_ref, o_ref):
    w = w_ref[...]
    y = w[:, 0:1] * ya_ref[...] + w[:, 1:2] * yb_ref[...]
    o_ref[...] = x_ref[...] + _rms(y, g_ref[...])


def combine_residual(y2, wts, x, g, *, tm=512):
    t, d = x.shape
    tm = min(tm, t)
    nt = t // tm
    blk = pl.BlockSpec((tm, d), lambda i: (i, 0))
    return pl.pallas_call(
        _combine_kernel,
        out_shape=jax.ShapeDtypeStruct((t, d), F32),
        grid=(nt,),
        in_specs=[blk, pl.BlockSpec((tm, d), lambda i: (i + nt, 0)),
                  pl.BlockSpec((tm, TOP_K), lambda i: (i, 0)), blk,
                  pl.BlockSpec((1, d), lambda i: (0, 0))],
        out_specs=blk,
        compiler_params=_params("parallel"),
        name="combine_residual",
    )(y2, y2, wts, x, g.reshape(1, d))


def _alibi_slopes(n):
    return jnp.exp2(-8.0 * (jnp.arange(n, dtype=F32) + 1.0) / n)


def _token_mixer(x2, h, batch, seq, layer_idx, w_in, b_forget, b_gate, sinks,
                 lq1, lk1, lq2, lk2, subln, w_br_a, w_br_b, w_br_c, w_out,
                 g_post, g_next):
    t, d = x2.shape
    a_w = A_Q_HEADS * HEAD_DIM
    kv_w = A_KV_HEADS * HEAD_DIM
    b_w = B_HEADS * 2 * HEAD_DIM
    c_w = C_HEADS * HEAD_DIM
    n_attn = a_w + 2 * kv_w + 3 * b_w + 3 * c_w

    slot_cols = np.concatenate([np.arange(HEAD_DIM) + HEAD_DIM * hd for hd in A_SLOT_HEAD])
    a_end = a_w + 2 * kv_w
    b_end = a_end + 3 * b_w
    w_in_t = jnp.transpose(w_in, (2, 0, 1))
    rows = lambda lo, hi: w_in_t[lo:hi, layer_idx, :]
    w_attn_t = jnp.concatenate([rows(b_end, n_attn), rows(a_end, b_end),
                                rows(0, a_w)[slot_cols], rows(a_w, a_end)],
                               axis=0).astype(BF16)
    w_f_t = rows(n_attn, n_attn + C_HEADS)
    w_g_t = rows(n_attn + C_HEADS, w_in.shape[2]).astype(BF16)

    z3 = matmul(h, w_attn_t).reshape(batch, seq, n_attn)
    gates = matmul(h, w_g_t, bias=b_gate)
    c = forget_cumsum(h.reshape(batch, seq, d), w_f_t, b_forget)
    c4 = c[:, :C_HEADS].reshape(batch, C_HEADS, 1, seq)

    oc = fox_attention(z3, c4, q_col=0, k_col=C_HEADS // 2, v_col=C_HEADS)

    lam_init = 0.8 - 0.6 * math.exp(-0.3 * layer_idx)
    lam = (jnp.exp(jnp.sum(lq1.astype(F32) * lk1.astype(F32)))
           - jnp.exp(jnp.sum(lq2.astype(F32) * lk2.astype(F32))) + lam_init)
    b0 = 3 * c_w // LANES
    ob = diff_attention(z3, _alibi_slopes(B_HEADS), lam.reshape(1), subln,
                        q_col=b0, k_col=b0 + B_HEADS, v_col=b0 + 2 * B_HEADS,
                        out_scale=1.0 - lam_init)

    slot = np.array(A_SLOT_HEAD)
    a0 = 3 * c_w + 3 * b_w
    oa = swa_attention(z3, _alibi_slopes(A_Q_HEADS)[slot], sinks.astype(F32)[slot],
                       q_tile=a0 // a_w, k_tile=(a0 + a_w) // kv_w,
                       v_tile=(a0 + a_w) // kv_w + 1)

    merged = gated_merge(oa.reshape(t, a_w), ob.reshape(t, b_w), oc.reshape(t, c_w),
                         w_br_a[slot_cols].astype(BF16), w_br_b.astype(BF16),
                         w_br_c.astype(BF16), gates)
    return proj_residual(merged, w_out.astype(BF16), x2, g_post, g_next, tm=256, tk=d)


COUNT_BLOCK = 256


def _moe_dispatch(idx, tm):
    t = idx.shape[0]
    n_rows = t * TOP_K
    n_tiles = (n_rows + N_EXPERTS * (tm - 1)) // tm
    e_flat = idx.reshape(n_rows)
    onehot = e_flat[:, None] == jnp.arange(N_EXPERTS)[None, :]
    blk = min(COUNT_BLOCK, n_rows)
    oh = onehot.astype(F32).reshape(n_rows // blk, blk, N_EXPERTS)
    lower = (jnp.arange(blk)[:, None] >= jnp.arange(blk)[None, :]).astype(F32)
    within = jnp.einsum("ij,bjk->bik", lower, oh)
    before = jnp.cumsum(within[:, -1, :], axis=0) - within[:, -1, :]
    running = (within + before[:, None, :]).reshape(n_rows, N_EXPERTS)
    rank = jnp.sum(jnp.where(onehot, running - 1.0, 0.0), axis=1).astype(jnp.int32)
    counts = jnp.sum(onehot.astype(jnp.int32), axis=0)
    padded = ((counts + tm - 1) // tm) * tm
    ends = jnp.cumsum(padded)
    starts = ends - padded
    pos = starts[e_flat] + rank
    row_token = jnp.zeros((n_tiles * tm,), jnp.int32).at[pos].set(
        jnp.arange(n_rows, dtype=jnp.int32) // TOP_K)
    tile_start = jnp.arange(n_tiles, dtype=jnp.int32) * tm
    tile_expert = jnp.minimum(jnp.sum(tile_start[:, None] >= ends[None, :], axis=1),
                              N_EXPERTS - 1).astype(jnp.int32)
    n_active = (ends[-1] // tm).astype(jnp.int32).reshape(1)
    return pos.reshape(t, TOP_K), row_token, tile_expert, n_active


def _moe_ffn(x2, g_pre, g_post, w_router, e_gate, e_up, e_down, *, tm=512):
    idx, wts, h = router_top2(x2, g_pre, w_router)
    t = x2.shape[0]
    tm = min(tm, t)
    pos, row_token, tile_expert, n_active = _moe_dispatch(idx, tm)
    xs = sc_gather_rows(h, row_token)
    u = moe_up(xs, tile_expert, n_active, e_gate.astype(BF16), e_up.astype(BF16), tm=tm)
    ys = moe_down(u, tile_expert, n_active, e_down.astype(BF16), tm=tm)
    y2 = sc_gather_rows(ys, pos.T.reshape(TOP_K * t))
    return combine_residual(y2, wts, x2, g_post)


def kernel(x, mix_pre_norm, w_in, b_forget, b_gate, attn_sinks, lam_q1, lam_k1, lam_q2, lam_k2, diff_subln, w_br_a, w_br_b, w_br_c, w_out, mix_post_norm, ffn_pre_norm, ffn_post_norm, dense_w_gate, dense_w_up, dense_w_down, w_router, moe_w_gate, moe_w_up, moe_w_down):
    batch, seq, d = x.shape
    depth = w_in.shape[0]
    x2 = x.reshape(batch * seq, d)
    h = rmsnorm_rows(x2, mix_pre_norm[0])
    for l in range(depth):
        x2, h = _token_mixer(x2, h, batch, seq, l, w_in, b_forget[l], b_gate[l],
                             attn_sinks[l], lam_q1[l], lam_k1[l], lam_q2[l], lam_k2[l],
                             diff_subln[l], w_br_a[l], w_br_b[l], w_br_c[l], w_out[l],
                             mix_post_norm[l], ffn_pre_norm[l])
        i = l // 2
        if l % 2 == 0:
            u = swiglu_up(h, dense_w_gate[i].astype(BF16), dense_w_up[i].astype(BF16))
            g_next = mix_pre_norm[l + 1] if l + 1 < depth else jnp.ones((d,), F32)
            f = dense_w_down.shape[1]
            x2, h = proj_residual(u, dense_w_down[i].astype(BF16), x2, ffn_post_norm[l],
                                  g_next, tm=256, tk=f)
        else:
            x2 = _moe_ffn(x2, ffn_pre_norm[l], ffn_post_norm[l], w_router[i],
                          moe_w_gate[i], moe_w_up[i], moe_w_down[i])
            if l + 1 < depth:
                h = rmsnorm_rows(x2, mix_pre_norm[l + 1])
    return x2.reshape(batch, seq, d)
```

```python
import functools
import math

import jax
import jax.numpy as jnp
import numpy as np
from jax import lax
from jax.experimental import pallas as pl
from jax.experimental.pallas import tpu as pltpu
from jax.experimental.pallas import tpu_sc as plsc

F32 = jnp.float32
BF16 = jnp.bfloat16

HEAD_DIM = 64
LANES = 128
A_Q_HEADS = 12
A_KV_HEADS = 4
A_GROUP = A_Q_HEADS // A_KV_HEADS
WINDOW = 128
B_HEADS = 4
C_HEADS = 12
N_BRANCH = 3
N_EXPERTS = 8
TOP_K = 2
RMS_EPS = 1e-6
NEG_INF = -1e30
VMEM_LIMIT = 56 * 1024 * 1024

A_SLOT_HEAD = (0, 3, 1, 4, 2, 5, 6, 9, 7, 10, 8, 11)


def _params(*sem):
    return pltpu.CompilerParams(dimension_semantics=sem, vmem_limit_bytes=VMEM_LIMIT)


def _rms(x, g):
    var = jnp.mean(x * x, axis=-1, keepdims=True)
    return x * lax.rsqrt(var + RMS_EPS) * g


def _sigmoid(x):
    return 0.5 * jnp.tanh(0.5 * x) + 0.5


def _dot(a, b):
    return jnp.dot(a, b, preferred_element_type=F32)


def _dot_nt(a, b):
    return lax.dot_general(a, b, (((1,), (1,)), ((), ())), preferred_element_type=F32)


def _rmsnorm_kernel(x_ref, g_ref, o_ref):
    o_ref[...] = _rms(x_ref[...], g_ref[...]).astype(o_ref.dtype)


def rmsnorm_rows(x, g, *, tm=512):
    t, d = x.shape
    tm = min(tm, t)
    return pl.pallas_call(
        _rmsnorm_kernel,
        out_shape=jax.ShapeDtypeStruct((t, d), BF16),
        grid=(t // tm,),
        in_specs=[pl.BlockSpec((tm, d), lambda i: (i, 0)),
                  pl.BlockSpec((1, d), lambda i: (0, 0))],
        out_specs=pl.BlockSpec((tm, d), lambda i: (i, 0)),
        compiler_params=_params("parallel"),
        name="rmsnorm_rows",
    )(x, g.reshape(1, d))


def _mm_kernel(a_ref, bt_ref, o_ref):
    o_ref[...] = _dot_nt(a_ref[...], bt_ref[...]).astype(o_ref.dtype)


def _mm_gate_kernel(a_ref, bt_ref, bias_ref, o_ref):
    z = _dot_nt(a_ref[...], bt_ref[...]) + bias_ref[...]
    o_ref[...] = _sigmoid(z).astype(o_ref.dtype)


def matmul(a, bt, *, bias=None, out_dtype=BF16, tm=1024, tn=1024):
    m, k = a.shape
    n = bt.shape[0]
    tm, tn = min(tm, m), min(tn, n)
    assert m % tm == 0 and n % tn == 0
    in_specs = [pl.BlockSpec((tm, k), lambda i, j: (i, 0)),
                pl.BlockSpec((tn, k), lambda i, j: (j, 0))]
    args = [a, bt]
    kern = _mm_kernel
    if bias is not None:
        in_specs.append(pl.BlockSpec((1, tn), lambda i, j: (0, j)))
        args.append(bias.reshape(1, n).astype(F32))
        kern = _mm_gate_kernel
    return pl.pallas_call(
        kern,
        out_shape=jax.ShapeDtypeStruct((m, n), out_dtype),
        grid=(m // tm, n // tn),
        in_specs=in_specs,
        out_specs=pl.BlockSpec((tm, tn), lambda i, j: (i, j)),
        compiler_params=_params("parallel", "parallel"),
        name="matmul_gate" if bias is not None else "matmul",
    )(*args)


def _forget_kernel(h_ref, wt_ref, b_ref, c_ref, *, seq, blk):
    z = _dot_nt(wt_ref[...], h_ref[0]) + b_ref[...]
    log_f = jnp.minimum(z, 0.0) - jnp.log1p(jnp.exp(-jnp.abs(z)))
    r = lax.broadcasted_iota(jnp.int32, (blk, blk), 0)
    c = lax.broadcasted_iota(jnp.int32, (blk, blk), 1)
    upper = (r <= c).astype(F32)
    carry = jnp.zeros((LANES, 1), F32)
    for j in range(seq // blk):
        part = lax.dot_general(log_f[:, j * blk:(j + 1) * blk], upper,
                               (((1,), (0,)), ((), ())),
                               precision=lax.Precision.HIGHEST,
                               preferred_element_type=F32) + carry
        c_ref[0, :, j * blk:(j + 1) * blk] = part
        carry = part[:, blk - 1:blk]


def forget_cumsum(h3, w_ft, b_f):
    b, s, d = h3.shape
    nh = w_ft.shape[0]
    wt = jnp.zeros((LANES, d), BF16).at[:nh].set(w_ft.astype(BF16))
    bias = jnp.zeros((LANES, 1), F32).at[:nh, 0].set(b_f.astype(F32))
    blk = min(256, s)
    return pl.pallas_call(
        functools.partial(_forget_kernel, seq=s, blk=blk),
        out_shape=jax.ShapeDtypeStruct((b, LANES, s), F32),
        grid=(b,),
        in_specs=[pl.BlockSpec((1, s, d), lambda i: (i, 0, 0)),
                  pl.BlockSpec((LANES, d), lambda i: (0, 0)),
                  pl.BlockSpec((LANES, 1), lambda i: (0, 0))],
        out_specs=pl.BlockSpec((1, LANES, s), lambda i: (i, 0, 0)),
        compiler_params=_params("parallel"),
        name="forget_cumsum",
    )(h3, wt, bias)


def _half_mask(tq, half):
    lane = lax.broadcasted_iota(jnp.int32, (tq, LANES), 1)
    return (lane >= HEAD_DIM * half) & (lane < HEAD_DIM * (half + 1))


def _causal_logits(qh, k_ref, lanes, qi, tq, bias):
    n0 = qi * tq
    r = lax.broadcasted_iota(jnp.int32, (tq, tq), 0)
    c = lax.broadcasted_iota(jnp.int32, (tq, tq), 1)
    diag = _dot_nt(qh, k_ref[0, n0:n0 + tq, lanes]) + bias(n0, tq)
    parts = [jnp.where(r >= c, diag, NEG_INF)]
    if qi > 0:
        parts.append(_dot_nt(qh, k_ref[0, :n0, lanes]) + bias(0, n0))
    return parts


def _softmax_parts(parts):
    m = functools.reduce(jnp.maximum, [s.max(axis=-1, keepdims=True) for s in parts])
    exps = [jnp.exp(s - m) for s in parts]
    denom = functools.reduce(jnp.add, [e.sum(axis=-1, keepdims=True) for e in exps])
    return exps, denom


def _value_parts(v_ref, lanes, qi, tq):
    n0 = qi * tq
    return [v_ref[0, n0:n0 + tq, lanes]] + ([v_ref[0, :n0, lanes]] if qi > 0 else [])


def _weighted_values(weights, values):
    return functools.reduce(jnp.add, [_dot(w.astype(v.dtype), v) for w, v in zip(weights, values)])


def _dispatch_on_tile(branch, nq):
    qi = pl.program_id(2)
    for i in range(nq):
        pl.when(qi == i)(functools.partial(branch, qi=i))


def _fox_kernel(q_ref, k_ref, v_ref, c_ref, o_ref, *, tq, nq, npair):
    masks = [_half_mask(tq, h) for h in range(2)]

    def branch(qi):
        for p in range(npair):
            lanes = slice(p * LANES, (p + 1) * LANES)
            q = q_ref[0, :, lanes] * jnp.asarray(HEAD_DIM ** -0.5, q_ref.dtype)
            values = _value_parts(v_ref, lanes, qi, tq)
            outs = []
            for h in range(2):
                qh = jnp.where(masks[h], q, jnp.zeros_like(q))
                bias = lambda start, size: -c_ref[0, 2 * p + h, :, start:start + size]
                exps, denom = _softmax_parts(_causal_logits(qh, k_ref, lanes, qi, tq, bias))
                outs.append(_weighted_values(exps, values) / denom)
            o_ref[0, :, lanes] = jnp.where(masks[0], outs[0], outs[1]).astype(o_ref.dtype)

    _dispatch_on_tile(branch, nq)


def _diff_kernel(slope_ref, lam_ref, q_ref, k_ref, v_ref, g_ref, o_ref, *,
                 tq, nq, nhead, out_scale):
    lam = lam_ref[0]
    masks = [_half_mask(tq, h) for h in range(2)]

    def branch(qi):
        for j in range(nhead):
            lanes = slice(j * LANES, (j + 1) * LANES)
            slope = slope_ref[pl.program_id(1) * nhead + j]
            q = q_ref[0, :, lanes] * jnp.asarray(HEAD_DIM ** -0.5, q_ref.dtype)

            def bias(start, size):
                pos = start + lax.broadcasted_iota(jnp.int32, (1, size), 1)
                return slope * pos.astype(F32)

            maps = []
            for h in range(2):
                qh = jnp.where(masks[h], q, jnp.zeros_like(q))
                maps.append(_softmax_parts(_causal_logits(qh, k_ref, lanes, qi, tq, bias)))
            (e0, d0), (e1, d1) = maps
            w0, w1 = 1.0 / d0, lam / d1
            weights = [a * w0 - b * w1 for a, b in zip(e0, e1)]
            o = _weighted_values(weights, _value_parts(v_ref, lanes, qi, tq))
            o_ref[0, :, lanes] = (_rms(o, g_ref[...]) * out_scale).astype(o_ref.dtype)

    _dispatch_on_tile(branch, nq)


def fox_attention(z3, c4, *, q_col, k_col, v_col, tq=256, npair=3):
    b, s, _ = z3.shape
    tq = min(tq, s)
    w = npair * LANES
    groups = C_HEADS // 2 // npair
    assert q_col % npair == 0 and k_col % npair == 0 and v_col % npair == 0
    return pl.pallas_call(
        functools.partial(_fox_kernel, tq=tq, nq=s // tq, npair=npair),
        out_shape=jax.ShapeDtypeStruct((b, s, C_HEADS // 2 * LANES), BF16),
        grid=(b, groups, s // tq),
        in_specs=[pl.BlockSpec((1, tq, w), lambda bi, g, qi: (bi, qi, q_col // npair + g)),
                  pl.BlockSpec((1, s, w), lambda bi, g, qi: (bi, 0, k_col // npair + g)),
                  pl.BlockSpec((1, s, w), lambda bi, g, qi: (bi, 0, v_col // npair + g)),
                  pl.BlockSpec((1, 2 * npair, 1, s), lambda bi, g, qi: (bi, g, 0, 0))],
        out_specs=pl.BlockSpec((1, tq, w), lambda bi, g, qi: (bi, qi, g)),
        compiler_params=_params("parallel", "parallel", "arbitrary"),
        name="fox_attention",
    )(z3, z3, z3, c4)


def diff_attention(z3, slopes, lam, subln, *, q_col, k_col, v_col, out_scale, tq=256, nhead=2):
    b, s, _ = z3.shape
    tq = min(tq, s)
    w = nhead * LANES
    assert q_col % nhead == 0 and k_col % nhead == 0 and v_col % nhead == 0
    smem = pl.BlockSpec(memory_space=pltpu.SMEM)
    return pl.pallas_call(
        functools.partial(_diff_kernel, tq=tq, nq=s // tq, nhead=nhead, out_scale=out_scale),
        out_shape=jax.ShapeDtypeStruct((b, s, B_HEADS * LANES), BF16),
        grid=(b, B_HEADS // nhead, s // tq),
        in_specs=[smem, smem,
                  pl.BlockSpec((1, tq, w), lambda bi, g, qi: (bi, qi, q_col // nhead + g)),
                  pl.BlockSpec((1, s, w), lambda bi, g, qi: (bi, 0, k_col // nhead + g)),
                  pl.BlockSpec((1, s, w), lambda bi, g, qi: (bi, 0, v_col // nhead + g)),
                  pl.BlockSpec((1, LANES), lambda bi, g, qi: (0, 0))],
        out_specs=pl.BlockSpec((1, tq, w), lambda bi, g, qi: (bi, qi, g)),
        compiler_params=_params("parallel", "parallel", "arbitrary"),
        name="diff_attention",
    )(slopes, lam, z3, z3, z3, subln.reshape(1, LANES).astype(F32))


def _swa_kernel(slope_ref, sink_ref, q_ref, k_ref, v_ref, o_ref, *, blk):
    qi = pl.program_id(1)
    cur = pl.multiple_of(qi * blk, blk)
    prev = pl.multiple_of(jnp.maximum(qi - 1, 0) * blk, blk)
    r = lax.broadcasted_iota(jnp.int32, (blk, 2 * blk), 0)
    c = lax.broadcasted_iota(jnp.int32, (blk, 2 * blk), 1)
    dist = r + blk - c
    keep = (dist >= 0) & (dist < WINDOW) & ((qi > 0) | (c >= blk))
    dist_f = dist.astype(F32)
    scale = jnp.asarray(HEAD_DIM ** -0.5, q_ref.dtype)
    masks = [_half_mask(blk, h) for h in range(2)]
    for tile in range(A_Q_HEADS // 2):
        q = q_ref[0, :, tile * LANES:(tile + 1) * LANES] * scale
        halves = []
        for half in range(2):
            slot = 2 * tile + half
            kv_tile = (A_SLOT_HEAD[slot] // A_GROUP) // 2
            lanes = slice(kv_tile * LANES, (kv_tile + 1) * LANES)
            k = jnp.concatenate([k_ref[0, pl.ds(prev, blk), lanes],
                                 k_ref[0, pl.ds(cur, blk), lanes]], axis=0)
            v = jnp.concatenate([v_ref[0, pl.ds(prev, blk), lanes],
                                 v_ref[0, pl.ds(cur, blk), lanes]], axis=0)
            qh = jnp.where(masks[half], q, jnp.zeros_like(q))
            s = _dot_nt(qh, k) - slope_ref[slot] * dist_f
            s = jnp.where(keep, s, NEG_INF)
            sink = sink_ref[slot]
            m = jnp.maximum(jnp.max(s, axis=-1, keepdims=True), sink)
            p = jnp.exp(s - m)
            denom = jnp.sum(p, axis=-1, keepdims=True) + jnp.exp(sink - m)
            halves.append(_dot(p.astype(v.dtype), v) / denom)
        o_ref[0, :, tile * LANES:(tile + 1) * LANES] = jnp.where(
            masks[0], halves[0], halves[1]).astype(o_ref.dtype)


def swa_attention(z3, slopes, sinks, *, q_tile, k_tile, v_tile):
    b, s, _ = z3.shape
    blk = WINDOW
    qw = A_Q_HEADS * HEAD_DIM
    kw = A_KV_HEADS * HEAD_DIM
    smem = pl.BlockSpec(memory_space=pltpu.SMEM)
    return pl.pallas_call(
        functools.partial(_swa_kernel, blk=blk),
        out_shape=jax.ShapeDtypeStruct((b, s, qw), BF16),
        grid=(b, s // blk),
        in_specs=[smem, smem,
                  pl.BlockSpec((1, blk, qw), lambda bi, qi: (bi, qi, q_tile)),
                  pl.BlockSpec((1, s, kw), lambda bi, qi: (bi, 0, k_tile)),
                  pl.BlockSpec((1, s, kw), lambda bi, qi: (bi, 0, v_tile))],
        out_specs=pl.BlockSpec((1, blk, qw), lambda bi, qi: (bi, qi, 0)),
        compiler_params=_params("parallel", "parallel"),
        name="swa_attention",
    )(slopes, sinks, z3, z3, z3)


def _merge_kernel(oa_ref, ob_ref, oc_ref, wa_ref, wb_ref, wc_ref,
                  ga_ref, gb_ref, gc_ref, o_ref):
    acc = ga_ref[...].astype(F32) * _dot(oa_ref[...], wa_ref[...])
    acc += gb_ref[...].astype(F32) * _dot(ob_ref[...], wb_ref[...])
    acc += gc_ref[...].astype(F32) * _dot(oc_ref[...], wc_ref[...])
    o_ref[...] = acc.astype(o_ref.dtype)


def gated_merge(oa, ob, oc, wa, wb, wc, gates, *, tm=1024, tn=512):
    t = oa.shape[0]
    d = wa.shape[1]
    tm, tn = min(tm, t), min(tn, d)
    nj = d // tn
    row = lambda w: pl.BlockSpec((tm, w), lambda i, j: (i, 0))
    col = lambda w: pl.BlockSpec((w, tn), lambda i, j: (0, j))
    gate = lambda br: pl.BlockSpec((tm, tn), lambda i, j: (i, br * nj + j))
    return pl.pallas_call(
        _merge_kernel,
        out_shape=jax.ShapeDtypeStruct((t, d), BF16),
        grid=(t // tm, nj),
        in_specs=[row(oa.shape[1]), row(ob.shape[1]), row(oc.shape[1]),
                  col(wa.shape[0]), col(wb.shape[0]), col(wc.shape[0]),
                  gate(0), gate(1), gate(2)],
        out_specs=pl.BlockSpec((tm, tn), lambda i, j: (i, j)),
        compiler_params=_params("parallel", "parallel"),
        name="gated_merge",
    )(oa, ob, oc, wa, wb, wc, gates, gates, gates)


def _proj_res_kernel(a_ref, w_ref, x_ref, gp_ref, gn_ref, xo_ref, ho_ref, acc_ref, *, nk):
    kk = pl.program_id(1)
    part = _dot(a_ref[...], w_ref[...])

    @pl.when(kk == 0)
    def _():
        acc_ref[...] = part

    @pl.when(kk > 0)
    def _():
        acc_ref[...] += part

    @pl.when(kk == nk - 1)
    def _():
        x_new = x_ref[...] + _rms(acc_ref[...], gp_ref[...])
        xo_ref[...] = x_new
        ho_ref[...] = _rms(x_new, gn_ref[...]).astype(ho_ref.dtype)


def proj_residual(a, w, x, g_post, g_next, *, tm, tk):
    t, k = a.shape
    d = w.shape[1]
    tm, tk = min(tm, t), min(tk, k)
    nk = k // tk
    vec = pl.BlockSpec((1, d), lambda i, kk: (0, 0))
    w_mode = dict(pipeline_mode=pl.Buffered(1)) if nk == 1 else {}
    return pl.pallas_call(
        functools.partial(_proj_res_kernel, nk=nk),
        out_shape=(jax.ShapeDtypeStruct((t, d), F32), jax.ShapeDtypeStruct((t, d), BF16)),
        grid=(t // tm, nk),
        in_specs=[pl.BlockSpec((tm, tk), lambda i, kk: (i, kk)),
                  pl.BlockSpec((tk, d), lambda i, kk: (kk, 0), **w_mode),
                  pl.BlockSpec((tm, d), lambda i, kk: (i, 0)),
                  vec, vec],
        out_specs=(pl.BlockSpec((tm, d), lambda i, kk: (i, 0)),
                   pl.BlockSpec((tm, d), lambda i, kk: (i, 0))),
        scratch_shapes=[pltpu.VMEM((tm, d), F32)],
        compiler_params=_params("parallel", "arbitrary"),
        name="proj_residual",
    )(a, w, x, g_post.reshape(1, d), g_next.reshape(1, d))


def _swiglu_up_kernel(h_ref, wg_ref, wu_ref, o_ref):
    h = h_ref[...]
    g = _dot(h, wg_ref[...])
    u = _dot(h, wu_ref[...])
    o_ref[...] = (g * _sigmoid(g) * u).astype(o_ref.dtype)


def swiglu_up(h, wg, wu, *, tm=1024, tn=512):
    t, d = h.shape
    f = wg.shape[1]
    tm, tn = min(tm, t), min(tn, f)
    return pl.pallas_call(
        _swiglu_up_kernel,
        out_shape=jax.ShapeDtypeStruct((t, f), BF16),
        grid=(t // tm, f // tn),
        in_specs=[pl.BlockSpec((tm, d), lambda i, j: (i, 0)),
                  pl.BlockSpec((d, tn), lambda i, j: (0, j)),
                  pl.BlockSpec((d, tn), lambda i, j: (0, j))],
        out_specs=pl.BlockSpec((tm, tn), lambda i, j: (i, j)),
        compiler_params=_params("parallel", "parallel"),
        name="swiglu_up",
    )(h, wg, wu)


def _router_kernel(x_ref, g_ref, w_ref, idx_ref, wt_ref, h_ref):
    h = _rms(x_ref[...], g_ref[...])
    h_ref[...] = h
    logits = jnp.dot(h, w_ref[...], precision=lax.Precision.HIGHEST,
                     preferred_element_type=F32)
    lane = lax.broadcasted_iota(jnp.int32, logits.shape, 1)
    lane_f = lane.astype(F32)
    lg = jnp.where(lane < N_EXPERTS, logits, -jnp.inf)
    m1 = jnp.max(lg, axis=-1, keepdims=True)
    i1 = jnp.min(jnp.where(lg == m1, lane_f, float(LANES)), axis=-1, keepdims=True)
    lg2 = jnp.where(lane_f == i1, -jnp.inf, lg)
    m2 = jnp.max(lg2, axis=-1, keepdims=True)
    i2 = jnp.min(jnp.where(lg2 == m2, lane_f, float(LANES)), axis=-1, keepdims=True)
    e2 = jnp.exp(m2 - m1)
    w1 = 1.0 / (1.0 + e2)
    w2 = e2 / (1.0 + e2)
    idx_ref[...] = jnp.where(lane == 0, i1, jnp.where(lane == 1, i2, 0.0)).astype(jnp.int32)
    wt_ref[...] = jnp.where(lane == 0, w1, jnp.where(lane == 1, w2, 0.0))


def router_top2(x, g, w_router, *, tm=512):
    t, d = x.shape
    tm = min(tm, t)
    wp = jnp.zeros((d, LANES), F32).at[:, :N_EXPERTS].set(w_router.astype(F32))
    out = pl.BlockSpec((tm, LANES), lambda i: (i, 0))
    idx, wt, h = pl.pallas_call(
        _router_kernel,
        out_shape=(jax.ShapeDtypeStruct((t, LANES), jnp.int32),
                   jax.ShapeDtypeStruct((t, LANES), F32),
                   jax.ShapeDtypeStruct((t, d), F32)),
        grid=(t // tm,),
        in_specs=[pl.BlockSpec((tm, d), lambda i: (i, 0)),
                  pl.BlockSpec((1, d), lambda i: (0, 0)),
                  pl.BlockSpec((d, LANES), lambda i: (0, 0))],
        out_specs=(out, out, pl.BlockSpec((tm, d), lambda i: (i, 0))),
        compiler_params=_params("parallel"),
        name="router_top2",
    )(x, g.reshape(1, d), wp)
    return idx[:, :TOP_K], wt[:, :TOP_K], h


SC_INDEX_WINDOW = 128
SC_ROWS_PER_COPY = 16


def sc_gather_rows(src, idx):
    n = idx.shape[0]
    d = src.shape[1]
    assert n % SC_INDEX_WINDOW == 0
    mesh = plsc.VectorSubcoreMesh(core_axis_name="c", subcore_axis_name="s")
    dst = jnp.arange(n, dtype=jnp.int32).reshape(1, n)

    @pl.kernel(out_type=jax.ShapeDtypeStruct((n, d), src.dtype), mesh=mesh,
               scratch_types=[pltpu.VMEM((SC_ROWS_PER_COPY, d), src.dtype)])
    def gather(x_hbm, i_hbm, d_hbm, o_hbm, buf):
        def body(i_vmem, d_vmem):
            for s in range(SC_INDEX_WINDOW // SC_ROWS_PER_COPY):
                rows = pl.ds(s * SC_ROWS_PER_COPY, SC_ROWS_PER_COPY)
                pltpu.sync_copy(x_hbm.at[i_vmem.at[0, rows]], buf)
                pltpu.sync_copy(buf, o_hbm.at[d_vmem.at[0, rows]])

        window = pl.BlockSpec((1, SC_INDEX_WINDOW), lambda i: (0, i))
        pltpu.emit_pipeline(
            body, grid=(n // SC_INDEX_WINDOW,),
            in_specs=[window, window], out_specs=[],
            core_axis_name=("c", "s"),
            dimension_semantics=(pltpu.PARALLEL,),
        )(i_hbm, d_hbm)

    return gather(src, idx.reshape(1, n), dst)


def _moe_up_kernel(te_ref, na_ref, x_ref, wg_ref, wu_ref, o_ref, wg_bf, wu_bf):
    i = pl.program_id(1)
    active = i < na_ref[0]
    new_expert = jnp.logical_or(i == 0, te_ref[i] != te_ref[jnp.maximum(i - 1, 0)])

    @pl.when(jnp.logical_and(active, new_expert))
    def _():
        wg_bf[...] = wg_ref[0].astype(wg_bf.dtype)
        wu_bf[...] = wu_ref[0].astype(wu_bf.dtype)

    @pl.when(active)
    def _():
        x = x_ref[...].astype(wg_bf.dtype)
        g = _dot(x, wg_bf[...])
        u = _dot(x, wu_bf[...])
        o_ref[...] = (g * _sigmoid(g) * u).astype(o_ref.dtype)

    @pl.when(jnp.logical_not(active))
    def _():
        o_ref[...] = jnp.zeros_like(o_ref)


def _moe_down_kernel(te_ref, na_ref, u_ref, wd_ref, o_ref, wd_bf):
    i = pl.program_id(1)
    active = i < na_ref[0]
    new_expert = jnp.logical_or(i == 0, te_ref[i] != te_ref[jnp.maximum(i - 1, 0)])

    @pl.when(jnp.logical_and(active, new_expert))
    def _():
        wd_bf[...] = wd_ref[0].astype(wd_bf.dtype)

    @pl.when(active)
    def _():
        o_ref[...] = _dot(u_ref[...], wd_bf[...])

    @pl.when(jnp.logical_not(active))
    def _():
        o_ref[...] = jnp.zeros_like(o_ref)


def _active_tile(i, na_ref):
    return jnp.minimum(i, na_ref[0] - 1)


def moe_up(xs, tile_expert, n_active, wg, wu, *, tm, tn=896):
    p, d = xs.shape
    f = wg.shape[2]
    tn = min(tn, f)
    assert f % tn == 0
    grid_spec = pltpu.PrefetchScalarGridSpec(
        num_scalar_prefetch=2,
        grid=(f // tn, p // tm),
        in_specs=[pl.BlockSpec((tm, d), lambda j, i, te, na: (_active_tile(i, na), 0)),
                  pl.BlockSpec((1, d, tn), lambda j, i, te, na: (te[_active_tile(i, na)], 0, j)),
                  pl.BlockSpec((1, d, tn), lambda j, i, te, na: (te[_active_tile(i, na)], 0, j))],
        out_specs=pl.BlockSpec((tm, tn), lambda j, i, te, na: (i, j)),
        scratch_shapes=[pltpu.VMEM((d, tn), BF16), pltpu.VMEM((d, tn), BF16)],
    )
    return pl.pallas_call(
        _moe_up_kernel,
        out_shape=jax.ShapeDtypeStruct((p, f), BF16),
        grid_spec=grid_spec,
        compiler_params=_params("arbitrary", "arbitrary"),
        name="moe_up",
    )(tile_expert, n_active, xs, wg, wu)


def moe_down(u, tile_expert, n_active, wd, *, tm, tn=512):
    p, f = u.shape
    d = wd.shape[2]
    tn = min(tn, d)
    grid_spec = pltpu.PrefetchScalarGridSpec(
        num_scalar_prefetch=2,
        grid=(d // tn, p // tm),
        in_specs=[pl.BlockSpec((tm, f), lambda j, i, te, na: (_active_tile(i, na), 0)),
                  pl.BlockSpec((1, f, tn), lambda j, i, te, na: (te[_active_tile(i, na)], 0, j),
                               pipeline_mode=pl.Buffered(1))],
        out_specs=pl.BlockSpec((tm, tn), lambda j, i, te, na: (i, j)),
        scratch_shapes=[pltpu.VMEM((f, tn), BF16)],
    )
    return pl.pallas_call(
        _moe_down_kernel,
        out_shape=jax.ShapeDtypeStruct((p, d), F32),
        grid_spec=grid_spec,
        compiler_params=_params("arbitrary", "arbitrary"),
        name="moe_down",
    )(tile_expert, n_active, u, wd)


def _combine_kernel(ya_ref, yb_ref, w_ref, x_ref, g_ref, o_ref):
    w = w_ref[...]
    y = w[:, 0:1] * ya_ref[...] + w[:, 1:2] * yb_ref[...]
    o_ref[...] = x_ref[...] + _rms(y, g_ref[...])


def combine_residual(y2, wts, x, g, *, tm=512):
    t, d = x.shape
    tm = min(tm, t)
    nt = t // tm
    blk = pl.BlockSpec((tm, d), lambda i: (i, 0))
    return pl.pallas_call(
        _combine_kernel,
        out_shape=jax.ShapeDtypeStruct((t, d), F32),
        grid=(nt,),
        in_specs=[blk, pl.BlockSpec((tm, d), lambda i: (i + nt, 0)),
                  pl.BlockSpec((tm, TOP_K), lambda i: (i, 0)), blk,
                  pl.BlockSpec((1, d), lambda i: (0, 0))],
        out_specs=blk,
        compiler_params=_params("parallel"),
        name="combine_residual",
    )(y2, y2, wts, x, g.reshape(1, d))


def _alibi_slopes(n):
    return jnp.exp2(-8.0 * (jnp.arange(n, dtype=F32) + 1.0) / n)


def _token_mixer(x2, h, batch, seq, layer_idx, w_in, b_forget, b_gate, sinks,
                 lq1, lk1, lq2, lk2, subln, w_br_a, w_br_b, w_br_c, w_out,
                 g_post, g_next):
    t, d = x2.shape
    a_w = A_Q_HEADS * HEAD_DIM
    kv_w = A_KV_HEADS * HEAD_DIM
    b_w = B_HEADS * 2 * HEAD_DIM
    c_w = C_HEADS * HEAD_DIM
    n_attn = a_w + 2 * kv_w + 3 * b_w + 3 * c_w

    slot_cols = np.concatenate([np.arange(HEAD_DIM) + HEAD_DIM * hd for hd in A_SLOT_HEAD])
    a_end = a_w + 2 * kv_w
    b_end = a_end + 3 * b_w
    w_in_t = jnp.transpose(w_in, (2, 0, 1))
    rows = lambda lo, hi: w_in_t[lo:hi, layer_idx, :]
    w_attn_t = jnp.concatenate([rows(b_end, n_attn), rows(a_end, b_end),
                                rows(0, a_w)[slot_cols], rows(a_w, a_end)],
                               axis=0).astype(BF16)
    w_f_t = rows(n_attn, n_attn + C_HEADS)
    w_g_t = rows(n_attn + C_HEADS, w_in.shape[2]).astype(BF16)

    z3 = matmul(h, w_attn_t).reshape(batch, seq, n_attn)
    gates = matmul(h, w_g_t, bias=b_gate)
    c = forget_cumsum(h.reshape(batch, seq, d), w_f_t, b_forget)
    c4 = c[:, :C_HEADS].reshape(batch, C_HEADS, 1, seq)

    oc = fox_attention(z3, c4, q_col=0, k_col=C_HEADS // 2, v_col=C_HEADS)

    lam_init = 0.8 - 0.6 * math.exp(-0.3 * layer_idx)
    lam = (jnp.exp(jnp.sum(lq1.astype(F32) * lk1.astype(F32)))
           - jnp.exp(jnp.sum(lq2.astype(F32) * lk2.astype(F32))) + lam_init)
    b0 = 3 * c_w // LANES
    ob = diff_attention(z3, _alibi_slopes(B_HEADS), lam.reshape(1), subln,
                        q_col=b0, k_col=b0 + B_HEADS, v_col=b0 + 2 * B_HEADS,
                        out_scale=1.0 - lam_init)

    slot = np.array(A_SLOT_HEAD)
    a0 = 3 * c_w + 3 * b_w
    oa = swa_attention(z3, _alibi_slopes(A_Q_HEADS)[slot], sinks.astype(F32)[slot],
                       q_tile=a0 // a_w, k_tile=(a0 + a_w) // kv_w,
                       v_tile=(a0 + a_w) // kv_w + 1)

    merged = gated_merge(oa.reshape(t, a_w), ob.reshape(t, b_w), oc.reshape(t, c_w),
                         w_br_a[slot_cols].astype(BF16), w_br_b.astype(BF16),
                         w_br_c.astype(BF16), gates)
    return proj_residual(merged, w_out.astype(BF16), x2, g_post, g_next, tm=256, tk=d)


COUNT_BLOCK = 256


def _moe_dispatch(idx, tm):
    t = idx.shape[0]
    n_rows = t * TOP_K
    n_tiles = (n_rows + N_EXPERTS * (tm - 1)) // tm
    e_flat = idx.reshape(n_rows)
    onehot = e_flat[:, None] == jnp.arange(N_EXPERTS)[None, :]
    blk = min(COUNT_BLOCK, n_rows)
    oh = onehot.astype(F32).reshape(n_rows // blk, blk, N_EXPERTS)
    lower = (jnp.arange(blk)[:, None] >= jnp.arange(blk)[None, :]).astype(F32)
    within = jnp.einsum("ij,bjk->bik", lower, oh)
    before = jnp.cumsum(within[:, -1, :], axis=0) - within[:, -1, :]
    running = (within + before[:, None, :]).reshape(n_rows, N_EXPERTS)
    rank = jnp.sum(jnp.where(onehot, running - 1.0, 0.0), axis=1).astype(jnp.int32)
    counts = jnp.sum(onehot.astype(jnp.int32), axis=0)
    padded = ((counts + tm - 1) // tm) * tm
    ends = jnp.cumsum(padded)
    starts = ends - padded
    pos = starts[e_flat] + rank
    row_token = jnp.zeros((n_tiles * tm,), jnp.int32).at[pos].set(
        jnp.arange(n_rows, dtype=jnp.int32) // TOP_K)
    tile_start = jnp.arange(n_tiles, dtype=jnp.int32) * tm
    tile_expert = jnp.minimum(jnp.sum(tile_start[:, None] >= ends[None, :], axis=1),
                              N_EXPERTS - 1).astype(jnp.int32)
    n_active = (ends[-1] // tm).astype(jnp.int32).reshape(1)
    return pos.reshape(t, TOP_K), row_token, tile_expert, n_active


def _moe_ffn(x2, g_pre, g_post, w_router, e_gate, e_up, e_down, *, tm=512):
    idx, wts, h = router_top2(x2, g_pre, w_router)
    t = x2.shape[0]
    tm = min(tm, t)
    pos, row_token, tile_expert, n_active = _moe_dispatch(idx, tm)
    xs = sc_gather_rows(h, row_token)
    u = moe_up(xs, tile_expert, n_active, e_gate, e_up, tm=tm)
    ys = moe_down(u, tile_expert, n_active, e_down, tm=tm)
    y2 = sc_gather_rows(ys, pos.T.reshape(TOP_K * t))
    return combine_residual(y2, wts, x2, g_post)


def kernel(x, mix_pre_norm, w_in, b_forget, b_gate, attn_sinks, lam_q1, lam_k1, lam_q2, lam_k2, diff_subln, w_br_a, w_br_b, w_br_c, w_out, mix_post_norm, ffn_pre_norm, ffn_post_norm, dense_w_gate, dense_w_up, dense_w_down, w_router, moe_w_gate, moe_w_up, moe_w_down):
    batch, seq, d = x.shape
    depth = w_in.shape[0]
    x2 = x.reshape(batch * seq, d)
    h = rmsnorm_rows(x2, mix_pre_norm[0])
    for l in range(depth):
        x2, h = _token_mixer(x2, h, batch, seq, l, w_in, b_forget[l], b_gate[l],
                             attn_sinks[l], lam_q1[l], lam_k1[l], lam_q2[l], lam_k2[l],
                             diff_subln[l], w_br_a[l], w_br_b[l], w_br_c[l], w_out[l],
                             mix_post_norm[l], ffn_pre_norm[l])
        i = l // 2
        if l % 2 == 0:
            u = swiglu_up(h, dense_w_gate[i].astype(BF16), dense_w_up[i].astype(BF16))
            g_next = mix_pre_norm[l + 1] if l + 1 < depth else jnp.ones((d,), F32)
            f = dense_w_down.shape[1]
            x2, h = proj_residual(u, dense_w_down[i].astype(BF16), x2, ffn_post_norm[l],
                                  g_next, tm=256, tk=f)
        else:
            x2 = _moe_ffn(x2, ffn_pre_norm[l], ffn_post_norm[l], w_router[i],
                          moe_w_gate[i], moe_w_up[i], moe_w_down[i])
            if l + 1 < depth:
                h = rmsnorm_rows(x2, mix_pre_norm[l + 1])
    return x2.reshape(batch, seq, d)
```

```python
import functools
import math

import jax
import jax.numpy as jnp
import numpy as np
from jax import lax
from jax.experimental import pallas as pl
from jax.experimental.pallas import tpu as pltpu
from jax.experimental.pallas import tpu_sc as plsc

F32 = jnp.float32
BF16 = jnp.bfloat16

HEAD_DIM = 64
LANES = 128
A_Q_HEADS = 12
A_KV_HEADS = 4
A_GROUP = A_Q_HEADS // A_KV_HEADS
WINDOW = 128
B_HEADS = 4
C_HEADS = 12
N_BRANCH = 3
N_EXPERTS = 8
TOP_K = 2
RMS_EPS = 1e-6
NEG_INF = -1e30
VMEM_LIMIT = 56 * 1024 * 1024

A_SLOT_HEAD = (0, 3, 1, 4, 2, 5, 6, 9, 7, 10, 8, 11)


def _params(*sem):
    return pltpu.CompilerParams(dimension_semantics=sem, vmem_limit_bytes=VMEM_LIMIT)


def _rms(x, g):
    var = jnp.mean(x * x, axis=-1, keepdims=True)
    return x * lax.rsqrt(var + RMS_EPS) * g


def _sigmoid(x):
    return 0.5 * jnp.tanh(0.5 * x) + 0.5


def _dot(a, b):
    return jnp.dot(a, b, preferred_element_type=F32)


def _dot_nt(a, b):
    return lax.dot_general(a, b, (((1,), (1,)), ((), ())), preferred_element_type=F32)


def _round_kernel(*refs):
    n = len(refs) // 2
    for src, dst in zip(refs[:n], refs[n:]):
        dst[...] = src[...].astype(dst.dtype)


ROUND_BLOCK_ELEMS = 2 * 1024 * 1024


def round_to_bf16(*ws):
    shape = ws[0].shape
    flat = [w.reshape(-1, shape[-1]) for w in ws]
    n, c = flat[0].shape
    rows = min(n, 1 << ((ROUND_BLOCK_ELEMS // c).bit_length() - 1))
    assert n % rows == 0 and all(w.shape == shape for w in ws)
    blk = pl.BlockSpec((rows, c), lambda i: (i, 0))
    outs = pl.pallas_call(
        _round_kernel,
        out_shape=tuple(jax.ShapeDtypeStruct((n, c), BF16) for _ in ws),
        grid=(n // rows,),
        in_specs=[blk] * len(ws),
        out_specs=tuple(blk for _ in ws),
        compiler_params=_params("parallel"),
        name="round_to_bf16",
    )(*flat)
    return [o.reshape(shape) for o in outs]


def _rmsnorm_kernel(x_ref, g_ref, o_ref):
    o_ref[...] = _rms(x_ref[...], g_ref[...]).astype(o_ref.dtype)


def rmsnorm_rows(x, g, *, tm=512):
    t, d = x.shape
    tm = min(tm, t)
    return pl.pallas_call(
        _rmsnorm_kernel,
        out_shape=jax.ShapeDtypeStruct((t, d), BF16),
        grid=(t // tm,),
        in_specs=[pl.BlockSpec((tm, d), lambda i: (i, 0)),
                  pl.BlockSpec((1, d), lambda i: (0, 0))],
        out_specs=pl.BlockSpec((tm, d), lambda i: (i, 0)),
        compiler_params=_params("parallel"),
        name="rmsnorm_rows",
    )(x, g.reshape(1, d))


def _mm_kernel(a_ref, bt_ref, o_ref):
    o_ref[...] = _dot_nt(a_ref[...], bt_ref[...]).astype(o_ref.dtype)


def _mm_gate_kernel(a_ref, bt_ref, bias_ref, o_ref):
    z = _dot_nt(a_ref[...], bt_ref[...]) + bias_ref[...]
    o_ref[...] = _sigmoid(z).astype(o_ref.dtype)


def matmul(a, bt, *, bias=None, out_dtype=BF16, tm=1024, tn=1024):
    m, k = a.shape
    n = bt.shape[0]
    tm, tn = min(tm, m), min(tn, n)
    assert m % tm == 0 and n % tn == 0
    in_specs = [pl.BlockSpec((tm, k), lambda i, j: (i, 0)),
                pl.BlockSpec((tn, k), lambda i, j: (j, 0))]
    args = [a, bt]
    kern = _mm_kernel
    if bias is not None:
        in_specs.append(pl.BlockSpec((1, tn), lambda i, j: (0, j)))
        args.append(bias.reshape(1, n).astype(F32))
        kern = _mm_gate_kernel
    return pl.pallas_call(
        kern,
        out_shape=jax.ShapeDtypeStruct((m, n), out_dtype),
        grid=(m // tm, n // tn),
        in_specs=in_specs,
        out_specs=pl.BlockSpec((tm, tn), lambda i, j: (i, j)),
        compiler_params=_params("parallel", "parallel"),
        name="matmul_gate" if bias is not None else "matmul",
    )(*args)


def _forget_kernel(h_ref, wt_ref, b_ref, c_ref, *, seq, blk):
    z = _dot_nt(wt_ref[...], h_ref[0]) + b_ref[...]
    log_f = jnp.minimum(z, 0.0) - jnp.log1p(jnp.exp(-jnp.abs(z)))
    r = lax.broadcasted_iota(jnp.int32, (blk, blk), 0)
    c = lax.broadcasted_iota(jnp.int32, (blk, blk), 1)
    upper = (r <= c).astype(F32)
    carry = jnp.zeros((LANES, 1), F32)
    for j in range(seq // blk):
        part = lax.dot_general(log_f[:, j * blk:(j + 1) * blk], upper,
                               (((1,), (0,)), ((), ())),
                               precision=lax.Precision.HIGHEST,
                               preferred_element_type=F32) + carry
        c_ref[0, :, j * blk:(j + 1) * blk] = part
        carry = part[:, blk - 1:blk]


def forget_cumsum(h3, w_ft, b_f):
    b, s, d = h3.shape
    nh = w_ft.shape[0]
    wt = jnp.zeros((LANES, d), BF16).at[:nh].set(w_ft.astype(BF16))
    bias = jnp.zeros((LANES, 1), F32).at[:nh, 0].set(b_f.astype(F32))
    blk = min(256, s)
    return pl.pallas_call(
        functools.partial(_forget_kernel, seq=s, blk=blk),
        out_shape=jax.ShapeDtypeStruct((b, LANES, s), F32),
        grid=(b,),
        in_specs=[pl.BlockSpec((1, s, d), lambda i: (i, 0, 0)),
                  pl.BlockSpec((LANES, d), lambda i: (0, 0)),
                  pl.BlockSpec((LANES, 1), lambda i: (0, 0))],
        out_specs=pl.BlockSpec((1, LANES, s), lambda i: (i, 0, 0)),
        compiler_params=_params("parallel"),
        name="forget_cumsum",
    )(h3, wt, bias)


def _half_mask(tq, half):
    lane = lax.broadcasted_iota(jnp.int32, (tq, LANES), 1)
    return (lane >= HEAD_DIM * half) & (lane < HEAD_DIM * (half + 1))


def _causal_logits(qh, k_ref, lanes, qi, tq, bias):
    n0 = qi * tq
    r = lax.broadcasted_iota(jnp.int32, (tq, tq), 0)
    c = lax.broadcasted_iota(jnp.int32, (tq, tq), 1)
    diag = _dot_nt(qh, k_ref[0, n0:n0 + tq, lanes]) + bias(n0, tq)
    parts = [jnp.where(r >= c, diag, NEG_INF)]
    if qi > 0:
        parts.append(_dot_nt(qh, k_ref[0, :n0, lanes]) + bias(0, n0))
    return parts


def _softmax_parts(parts):
    m = functools.reduce(jnp.maximum, [s.max(axis=-1, keepdims=True) for s in parts])
    exps = [jnp.exp(s - m) for s in parts]
    denom = functools.reduce(jnp.add, [e.sum(axis=-1, keepdims=True) for e in exps])
    return exps, denom


def _value_parts(v_ref, lanes, qi, tq):
    n0 = qi * tq
    return [v_ref[0, n0:n0 + tq, lanes]] + ([v_ref[0, :n0, lanes]] if qi > 0 else [])


def _weighted_values(weights, values):
    return functools.reduce(jnp.add, [_dot(w.astype(v.dtype), v) for w, v in zip(weights, values)])


def _dispatch_on_tile(branch, nq):
    qi = pl.program_id(2)
    for i in range(nq):
        pl.when(qi == i)(functools.partial(branch, qi=i))


def _fox_kernel(q_ref, k_ref, v_ref, c_ref, o_ref, *, tq, nq, npair):
    masks = [_half_mask(tq, h) for h in range(2)]

    def branch(qi):
        for p in range(npair):
            lanes = slice(p * LANES, (p + 1) * LANES)
            q = q_ref[0, :, lanes] * jnp.asarray(HEAD_DIM ** -0.5, q_ref.dtype)
            values = _value_parts(v_ref, lanes, qi, tq)
            outs = []
            for h in range(2):
                qh = jnp.where(masks[h], q, jnp.zeros_like(q))
                bias = lambda start, size: -c_ref[0, 2 * p + h, :, start:start + size]
                exps, denom = _softmax_parts(_causal_logits(qh, k_ref, lanes, qi, tq, bias))
                outs.append(_weighted_values(exps, values) / denom)
            o_ref[0, :, lanes] = jnp.where(masks[0], outs[0], outs[1]).astype(o_ref.dtype)

    _dispatch_on_tile(branch, nq)


def _diff_kernel(slope_ref, lam_ref, q_ref, k_ref, v_ref, g_ref, o_ref, *,
                 tq, nq, nhead, out_scale):
    lam = lam_ref[0]
    masks = [_half_mask(tq, h) for h in range(2)]

    def branch(qi):
        for j in range(nhead):
            lanes = slice(j * LANES, (j + 1) * LANES)
            slope = slope_ref[pl.program_id(1) * nhead + j]
            q = q_ref[0, :, lanes] * jnp.asarray(HEAD_DIM ** -0.5, q_ref.dtype)

            def bias(start, size):
                pos = start + lax.broadcasted_iota(jnp.int32, (1, size), 1)
                return slope * pos.astype(F32)

            maps = []
            for h in range(2):
                qh = jnp.where(masks[h], q, jnp.zeros_like(q))
                maps.append(_softmax_parts(_causal_logits(qh, k_ref, lanes, qi, tq, bias)))
            (e0, d0), (e1, d1) = maps
            w0, w1 = 1.0 / d0, lam / d1
            weights = [a * w0 - b * w1 for a, b in zip(e0, e1)]
            o = _weighted_values(weights, _value_parts(v_ref, lanes, qi, tq))
            o_ref[0, :, lanes] = (_rms(o, g_ref[...]) * out_scale).astype(o_ref.dtype)

    _dispatch_on_tile(branch, nq)


def fox_attention(z3, c4, *, q_col, k_col, v_col, tq=256, npair=3):
    b, s, _ = z3.shape
    tq = min(tq, s)
    w = npair * LANES
    groups = C_HEADS // 2 // npair
    assert q_col % npair == 0 and k_col % npair == 0 and v_col % npair == 0
    return pl.pallas_call(
        functools.partial(_fox_kernel, tq=tq, nq=s // tq, npair=npair),
        out_shape=jax.ShapeDtypeStruct((b, s, C_HEADS // 2 * LANES), BF16),
        grid=(b, groups, s // tq),
        in_specs=[pl.BlockSpec((1, tq, w), lambda bi, g, qi: (bi, qi, q_col // npair + g)),
                  pl.BlockSpec((1, s, w), lambda bi, g, qi: (bi, 0, k_col // npair + g)),
                  pl.BlockSpec((1, s, w), lambda bi, g, qi: (bi, 0, v_col // npair + g)),
                  pl.BlockSpec((1, 2 * npair, 1, s), lambda bi, g, qi: (bi, g, 0, 0))],
        out_specs=pl.BlockSpec((1, tq, w), lambda bi, g, qi: (bi, qi, g)),
        compiler_params=_params("parallel", "parallel", "arbitrary"),
        name="fox_attention",
    )(z3, z3, z3, c4)


def diff_attention(z3, slopes, lam, subln, *, q_col, k_col, v_col, out_scale, tq=256, nhead=2):
    b, s, _ = z3.shape
    tq = min(tq, s)
    w = nhead * LANES
    assert q_col % nhead == 0 and k_col % nhead == 0 and v_col % nhead == 0
    smem = pl.BlockSpec(memory_space=pltpu.SMEM)
    return pl.pallas_call(
        functools.partial(_diff_kernel, tq=tq, nq=s // tq, nhead=nhead, out_scale=out_scale),
        out_shape=jax.ShapeDtypeStruct((b, s, B_HEADS * LANES), BF16),
        grid=(b, B_HEADS // nhead, s // tq),
        in_specs=[smem, smem,
                  pl.BlockSpec((1, tq, w), lambda bi, g, qi: (bi, qi, q_col // nhead + g)),
                  pl.BlockSpec((1, s, w), lambda bi, g, qi: (bi, 0, k_col // nhead + g)),
                  pl.BlockSpec((1, s, w), lambda bi, g, qi: (bi, 0, v_col // nhead + g)),
                  pl.BlockSpec((1, LANES), lambda bi, g, qi: (0, 0))],
        out_specs=pl.BlockSpec((1, tq, w), lambda bi, g, qi: (bi, qi, g)),
        compiler_params=_params("parallel", "parallel", "arbitrary"),
        name="diff_attention",
    )(slopes, lam, z3, z3, z3, subln.reshape(1, LANES).astype(F32))


def _swa_kernel(slope_ref, sink_ref, q_ref, k_ref, v_ref, o_ref, *, blk):
    qi = pl.program_id(1)
    cur = pl.multiple_of(qi * blk, blk)
    prev = pl.multiple_of(jnp.maximum(qi - 1, 0) * blk, blk)
    r = lax.broadcasted_iota(jnp.int32, (blk, 2 * blk), 0)
    c = lax.broadcasted_iota(jnp.int32, (blk, 2 * blk), 1)
    dist = r + blk - c
    keep = (dist >= 0) & (dist < WINDOW) & ((qi > 0) | (c >= blk))
    dist_f = dist.astype(F32)
    scale = jnp.asarray(HEAD_DIM ** -0.5, q_ref.dtype)
    masks = [_half_mask(blk, h) for h in range(2)]
    for tile in range(A_Q_HEADS // 2):
        q = q_ref[0, :, tile * LANES:(tile + 1) * LANES] * scale
        halves = []
        for half in range(2):
            slot = 2 * tile + half
            kv_tile = (A_SLOT_HEAD[slot] // A_GROUP) // 2
            lanes = slice(kv_tile * LANES, (kv_tile + 1) * LANES)
            k = jnp.concatenate([k_ref[0, pl.ds(prev, blk), lanes],
                                 k_ref[0, pl.ds(cur, blk), lanes]], axis=0)
            v = jnp.concatenate([v_ref[0, pl.ds(prev, blk), lanes],
                                 v_ref[0, pl.ds(cur, blk), lanes]], axis=0)
            qh = jnp.where(masks[half], q, jnp.zeros_like(q))
            s = _dot_nt(qh, k) - slope_ref[slot] * dist_f
            s = jnp.where(keep, s, NEG_INF)
            sink = sink_ref[slot]
            m = jnp.maximum(jnp.max(s, axis=-1, keepdims=True), sink)
            p = jnp.exp(s - m)
            denom = jnp.sum(p, axis=-1, keepdims=True) + jnp.exp(sink - m)
            halves.append(_dot(p.astype(v.dtype), v) / denom)
        o_ref[0, :, tile * LANES:(tile + 1) * LANES] = jnp.where(
            masks[0], halves[0], halves[1]).astype(o_ref.dtype)


def swa_attention(z3, slopes, sinks, *, q_tile, k_tile, v_tile):
    b, s, _ = z3.shape
    blk = WINDOW
    qw = A_Q_HEADS * HEAD_DIM
    kw = A_KV_HEADS * HEAD_DIM
    smem = pl.BlockSpec(memory_space=pltpu.SMEM)
    return pl.pallas_call(
        functools.partial(_swa_kernel, blk=blk),
        out_shape=jax.ShapeDtypeStruct((b, s, qw), BF16),
        grid=(b, s // blk),
        in_specs=[smem, smem,
                  pl.BlockSpec((1, blk, qw), lambda bi, qi: (bi, qi, q_tile)),
                  pl.BlockSpec((1, s, kw), lambda bi, qi: (bi, 0, k_tile)),
                  pl.BlockSpec((1, s, kw), lambda bi, qi: (bi, 0, v_tile))],
        out_specs=pl.BlockSpec((1, blk, qw), lambda bi, qi: (bi, qi, 0)),
        compiler_params=_params("parallel", "parallel"),
        name="swa_attention",
    )(slopes, sinks, z3, z3, z3)


def _merge_out_kernel(oa_ref, ob_ref, oc_ref, ga_ref, gb_ref, gc_ref, wa_ref, wb_ref, wc_ref,
                      wo_ref, x_ref, gp_ref, gn_ref, xo_ref, ho_ref):
    merged = ga_ref[...].astype(F32) * _dot(oa_ref[...], wa_ref[...])
    merged += gb_ref[...].astype(F32) * _dot(ob_ref[...], wb_ref[...])
    merged += gc_ref[...].astype(F32) * _dot(oc_ref[...], wc_ref[...])
    m = _dot(merged.astype(wo_ref.dtype), wo_ref[...])
    x_new = x_ref[...] + _rms(m, gp_ref[...])
    xo_ref[...] = x_new
    ho_ref[...] = _rms(x_new, gn_ref[...]).astype(ho_ref.dtype)


def merge_out_residual(oa, ob, oc, gates, wa, wb, wc, w_out, x, g_post, g_next, *, tm=256):
    t, d = x.shape
    tm = min(tm, t)
    row = lambda w: pl.BlockSpec((tm, w), lambda i: (i, 0))
    gate = lambda br: pl.BlockSpec((tm, d), lambda i: (i, br))
    whole = lambda w: pl.BlockSpec(w.shape, lambda i: (0, 0), pipeline_mode=pl.Buffered(1))
    vec = pl.BlockSpec((1, d), lambda i: (0, 0))
    return pl.pallas_call(
        _merge_out_kernel,
        out_shape=(jax.ShapeDtypeStruct((t, d), F32), jax.ShapeDtypeStruct((t, d), BF16)),
        grid=(t // tm,),
        in_specs=[row(oa.shape[1]), row(ob.shape[1]), row(oc.shape[1]),
                  gate(0), gate(1), gate(2),
                  whole(wa), whole(wb), whole(wc), whole(w_out), row(d), vec, vec],
        out_specs=(row(d), row(d)),
        compiler_params=_params("parallel"),
        name="merge_out_residual",
    )(oa, ob, oc, gates, gates, gates, wa, wb, wc, w_out, x,
      g_post.reshape(1, d), g_next.reshape(1, d))


def _proj_res_kernel(a_ref, w_ref, x_ref, gp_ref, gn_ref, xo_ref, ho_ref, acc_ref, *, nk):
    kk = pl.program_id(1)
    part = _dot(a_ref[...], w_ref[...])

    @pl.when(kk == 0)
    def _():
        acc_ref[...] = part

    @pl.when(kk > 0)
    def _():
        acc_ref[...] += part

    @pl.when(kk == nk - 1)
    def _():
        x_new = x_ref[...] + _rms(acc_ref[...], gp_ref[...])
        xo_ref[...] = x_new
        ho_ref[...] = _rms(x_new, gn_ref[...]).astype(ho_ref.dtype)


def proj_residual(a, w, x, g_post, g_next, *, tm, tk):
    t, k = a.shape
    d = w.shape[1]
    tm, tk = min(tm, t), min(tk, k)
    nk = k // tk
    vec = pl.BlockSpec((1, d), lambda i, kk: (0, 0))
    w_mode = dict(pipeline_mode=pl.Buffered(1)) if nk == 1 else {}
    return pl.pallas_call(
        functools.partial(_proj_res_kernel, nk=nk),
        out_shape=(jax.ShapeDtypeStruct((t, d), F32), jax.ShapeDtypeStruct((t, d), BF16)),
        grid=(t // tm, nk),
        in_specs=[pl.BlockSpec((tm, tk), lambda i, kk: (i, kk)),
                  pl.BlockSpec((tk, d), lambda i, kk: (kk, 0), **w_mode),
                  pl.BlockSpec((tm, d), lambda i, kk: (i, 0)),
                  vec, vec],
        out_specs=(pl.BlockSpec((tm, d), lambda i, kk: (i, 0)),
                   pl.BlockSpec((tm, d), lambda i, kk: (i, 0))),
        scratch_shapes=[pltpu.VMEM((tm, d), F32)],
        compiler_params=_params("parallel", "arbitrary"),
        name="proj_residual",
    )(a, w, x, g_post.reshape(1, d), g_next.reshape(1, d))


def _swiglu_up_kernel(h_ref, wg_ref, wu_ref, o_ref):
    h = h_ref[...]
    g = _dot(h, wg_ref[...])
    u = _dot(h, wu_ref[...])
    o_ref[...] = (g * _sigmoid(g) * u).astype(o_ref.dtype)


def swiglu_up(h, wg, wu, *, tm=1024, tn=512):
    t, d = h.shape
    f = wg.shape[1]
    tm, tn = min(tm, t), min(tn, f)
    return pl.pallas_call(
        _swiglu_up_kernel,
        out_shape=jax.ShapeDtypeStruct((t, f), BF16),
        grid=(t // tm, f // tn),
        in_specs=[pl.BlockSpec((tm, d), lambda i, j: (i, 0)),
                  pl.BlockSpec((d, tn), lambda i, j: (0, j)),
                  pl.BlockSpec((d, tn), lambda i, j: (0, j))],
        out_specs=pl.BlockSpec((tm, tn), lambda i, j: (i, j)),
        compiler_params=_params("parallel", "parallel"),
        name="swiglu_up",
    )(h, wg, wu)


def _router_kernel(x_ref, g_ref, w_ref, idx_ref, wt_ref, h_ref):
    h = _rms(x_ref[...], g_ref[...])
    h_ref[...] = h
    logits = jnp.dot(h, w_ref[...], precision=lax.Precision.HIGHEST,
                     preferred_element_type=F32)
    lane = lax.broadcasted_iota(jnp.int32, logits.shape, 1)
    lane_f = lane.astype(F32)
    lg = jnp.where(lane < N_EXPERTS, logits, -jnp.inf)
    m1 = jnp.max(lg, axis=-1, keepdims=True)
    i1 = jnp.min(jnp.where(lg == m1, lane_f, float(LANES)), axis=-1, keepdims=True)
    lg2 = jnp.where(lane_f == i1, -jnp.inf, lg)
    m2 = jnp.max(lg2, axis=-1, keepdims=True)
    i2 = jnp.min(jnp.where(lg2 == m2, lane_f, float(LANES)), axis=-1, keepdims=True)
    e2 = jnp.exp(m2 - m1)
    w1 = 1.0 / (1.0 + e2)
    w2 = e2 / (1.0 + e2)
    idx_ref[...] = jnp.where(lane == 0, i1, jnp.where(lane == 1, i2, 0.0)).astype(jnp.int32)
    wt_ref[...] = jnp.where(lane == 0, w1, jnp.where(lane == 1, w2, 0.0))


def router_top2(x, g, w_router, *, tm=512):
    t, d = x.shape
    tm = min(tm, t)
    wp = jnp.zeros((d, LANES), F32).at[:, :N_EXPERTS].set(w_router.astype(F32))
    out = pl.BlockSpec((tm, LANES), lambda i: (i, 0))
    idx, wt, h = pl.pallas_call(
        _router_kernel,
        out_shape=(jax.ShapeDtypeStruct((t, LANES), jnp.int32),
                   jax.ShapeDtypeStruct((t, LANES), F32),
                   jax.ShapeDtypeStruct((t, d), F32)),
        grid=(t // tm,),
        in_specs=[pl.BlockSpec((tm, d), lambda i: (i, 0)),
                  pl.BlockSpec((1, d), lambda i: (0, 0)),
                  pl.BlockSpec((d, LANES), lambda i: (0, 0))],
        out_specs=(out, out, pl.BlockSpec((tm, d), lambda i: (i, 0))),
        compiler_params=_params("parallel"),
        name="router_top2",
    )(x, g.reshape(1, d), wp)
    return idx[:, :TOP_K], wt[:, :TOP_K], h


SC_INDEX_WINDOW = 128
SC_ROWS_PER_COPY = 16


def sc_gather_rows(src, idx):
    n = idx.shape[0]
    d = src.shape[1]
    assert n % SC_INDEX_WINDOW == 0
    mesh = plsc.VectorSubcoreMesh(core_axis_name="c", subcore_axis_name="s")
    dst = jnp.arange(n, dtype=jnp.int32).reshape(1, n)

    @pl.kernel(out_type=jax.ShapeDtypeStruct((n, d), src.dtype), mesh=mesh,
               scratch_types=[pltpu.VMEM((SC_ROWS_PER_COPY, d), src.dtype)])
    def gather(x_hbm, i_hbm, d_hbm, o_hbm, buf):
        def body(i_vmem, d_vmem):
            for s in range(SC_INDEX_WINDOW // SC_ROWS_PER_COPY):
                rows = pl.ds(s * SC_ROWS_PER_COPY, SC_ROWS_PER_COPY)
                pltpu.sync_copy(x_hbm.at[i_vmem.at[0, rows]], buf)
                pltpu.sync_copy(buf, o_hbm.at[d_vmem.at[0, rows]])

        window = pl.BlockSpec((1, SC_INDEX_WINDOW), lambda i: (0, i))
        pltpu.emit_pipeline(
            body, grid=(n // SC_INDEX_WINDOW,),
            in_specs=[window, window], out_specs=[],
            core_axis_name=("c", "s"),
            dimension_semantics=(pltpu.PARALLEL,),
        )(i_hbm, d_hbm)

    return gather(src, idx.reshape(1, n), dst)


def _moe_up_kernel(te_ref, na_ref, x_ref, wg_ref, wu_ref, o_ref):
    active = pl.program_id(1) < na_ref[0]

    @pl.when(active)
    def _():
        x = x_ref[...].astype(wg_ref.dtype)
        g = _dot(x, wg_ref[0])
        u = _dot(x, wu_ref[0])
        o_ref[...] = (g * _sigmoid(g) * u).astype(o_ref.dtype)

    @pl.when(jnp.logical_not(active))
    def _():
        o_ref[...] = jnp.zeros_like(o_ref)


def _moe_down_kernel(te_ref, na_ref, u_ref, wd_ref, o_ref):
    active = pl.program_id(1) < na_ref[0]

    @pl.when(active)
    def _():
        o_ref[...] = _dot(u_ref[...], wd_ref[0])

    @pl.when(jnp.logical_not(active))
    def _():
        o_ref[...] = jnp.zeros_like(o_ref)


def _active_tile(i, na_ref):
    return jnp.minimum(i, na_ref[0] - 1)


def moe_up(xs, tile_expert, n_active, wg, wu, *, tm, tn=1024):
    p, d = xs.shape
    f = wg.shape[2]
    tn = min(tn, f)
    assert f % tn == 0
    grid_spec = pltpu.PrefetchScalarGridSpec(
        num_scalar_prefetch=2,
        grid=(f // tn, p // tm),
        in_specs=[pl.BlockSpec((tm, d), lambda j, i, te, na: (_active_tile(i, na), 0)),
                  pl.BlockSpec((1, d, tn), lambda j, i, te, na: (te[_active_tile(i, na)], 0, j)),
                  pl.BlockSpec((1, d, tn), lambda j, i, te, na: (te[_active_tile(i, na)], 0, j))],
        out_specs=pl.BlockSpec((tm, tn), lambda j, i, te, na: (i, j)),
    )
    return pl.pallas_call(
        _moe_up_kernel,
        out_shape=jax.ShapeDtypeStruct((p, f), BF16),
        grid_spec=grid_spec,
        compiler_params=_params("arbitrary", "arbitrary"),
        name="moe_up",
    )(tile_expert, n_active, xs, wg, wu)


def moe_down(u, tile_expert, n_active, wd, *, tm, tn=512):
    p, f = u.shape
    d = wd.shape[2]
    tn = min(tn, d)
    grid_spec = pltpu.PrefetchScalarGridSpec(
        num_scalar_prefetch=2,
        grid=(d // tn, p // tm),
        in_specs=[pl.BlockSpec((tm, f), lambda j, i, te, na: (_active_tile(i, na), 0)),
                  pl.BlockSpec((1, f, tn), lambda j, i, te, na: (te[_active_tile(i, na)], 0, j))],
        out_specs=pl.BlockSpec((tm, tn), lambda j, i, te, na: (i, j)),
    )
    return pl.pallas_call(
        _moe_down_kernel,
        out_shape=jax.ShapeDtypeStruct((p, d), F32),
        grid_spec=grid_spec,
        compiler_params=_params("arbitrary", "arbitrary"),
        name="moe_down",
    )(tile_expert, n_active, u, wd)


def _combine_kernel(ya_ref, yb_ref, w_ref, x_ref, g_ref, o_ref):
    w = w_ref[...]
    y = w[:, 0:1] * ya_ref[...] + w[:, 1:2] * yb_ref[...]
    o_ref[...] = x_ref[...] + _rms(y, g_ref[...])


def combine_residual(y2, wts, x, g, *, tm=512):
    t, d = x.shape
    tm = min(tm, t)
    nt = t // tm
    blk = pl.BlockSpec((tm, d), lambda i: (i, 0))
    return pl.pallas_call(
        _combine_kernel,
        out_shape=jax.ShapeDtypeStruct((t, d), F32),
        grid=(nt,),
        in_specs=[blk, pl.BlockSpec((tm, d), lambda i: (i + nt, 0)),
                  pl.BlockSpec((tm, TOP_K), lambda i: (i, 0)), blk,
                  pl.BlockSpec((1, d), lambda i: (0, 0))],
        out_specs=blk,
        compiler_params=_params("parallel"),
        name="combine_residual",
    )(y2, y2, wts, x, g.reshape(1, d))


def _alibi_slopes(n):
    return jnp.exp2(-8.0 * (jnp.arange(n, dtype=F32) + 1.0) / n)


def _token_mixer(x2, h, batch, seq, layer_idx, w_in, b_forget, b_gate, sinks,
                 lq1, lk1, lq2, lk2, subln, w_br_a, w_br_b, w_br_c, w_out,
                 g_post, g_next):
    t, d = x2.shape
    a_w = A_Q_HEADS * HEAD_DIM
    kv_w = A_KV_HEADS * HEAD_DIM
    b_w = B_HEADS * 2 * HEAD_DIM
    c_w = C_HEADS * HEAD_DIM
    n_attn = a_w + 2 * kv_w + 3 * b_w + 3 * c_w

    slot_cols = np.concatenate([np.arange(HEAD_DIM) + HEAD_DIM * hd for hd in A_SLOT_HEAD])
    a_end = a_w + 2 * kv_w
    b_end = a_end + 3 * b_w
    w_in_t = jnp.transpose(w_in, (2, 0, 1))
    rows = lambda lo, hi: w_in_t[lo:hi, layer_idx, :]
    w_attn_t = jnp.concatenate([rows(b_end, n_attn), rows(a_end, b_end),
                                rows(0, a_w)[slot_cols], rows(a_w, a_end)],
                               axis=0).astype(BF16)
    w_f_t = rows(n_attn, n_attn + C_HEADS)
    w_g_t = rows(n_attn + C_HEADS, w_in.shape[2]).astype(BF16)

    z3 = matmul(h, w_attn_t).reshape(batch, seq, n_attn)
    gates = matmul(h, w_g_t, bias=b_gate)
    c = forget_cumsum(h.reshape(batch, seq, d), w_f_t, b_forget)
    c4 = c[:, :C_HEADS].reshape(batch, C_HEADS, 1, seq)

    oc = fox_attention(z3, c4, q_col=0, k_col=C_HEADS // 2, v_col=C_HEADS)

    lam_init = 0.8 - 0.6 * math.exp(-0.3 * layer_idx)
    lam = (jnp.exp(jnp.sum(lq1.astype(F32) * lk1.astype(F32)))
           - jnp.exp(jnp.sum(lq2.astype(F32) * lk2.astype(F32))) + lam_init)
    b0 = 3 * c_w // LANES
    ob = diff_attention(z3, _alibi_slopes(B_HEADS), lam.reshape(1), subln,
                        q_col=b0, k_col=b0 + B_HEADS, v_col=b0 + 2 * B_HEADS,
                        out_scale=1.0 - lam_init)

    slot = np.array(A_SLOT_HEAD)
    a0 = 3 * c_w + 3 * b_w
    oa = swa_attention(z3, _alibi_slopes(A_Q_HEADS)[slot], sinks.astype(F32)[slot],
                       q_tile=a0 // a_w, k_tile=(a0 + a_w) // kv_w,
                       v_tile=(a0 + a_w) // kv_w + 1)

    return merge_out_residual(oa.reshape(t, a_w), ob.reshape(t, b_w), oc.reshape(t, c_w), gates,
                              w_br_a[slot_cols].astype(BF16), w_br_b.astype(BF16),
                              w_br_c.astype(BF16), w_out.astype(BF16), x2, g_post, g_next)


COUNT_BLOCK = 256


def _moe_dispatch(idx, tm):
    t = idx.shape[0]
    n_rows = t * TOP_K
    n_tiles = (n_rows + N_EXPERTS * (tm - 1)) // tm
    e_flat = idx.reshape(n_rows)
    onehot = e_flat[:, None] == jnp.arange(N_EXPERTS)[None, :]
    blk = min(COUNT_BLOCK, n_rows)
    oh = onehot.astype(F32).reshape(n_rows // blk, blk, N_EXPERTS)
    lower = (jnp.arange(blk)[:, None] >= jnp.arange(blk)[None, :]).astype(F32)
    within = jnp.einsum("ij,bjk->bik", lower, oh)
    before = jnp.cumsum(within[:, -1, :], axis=0) - within[:, -1, :]
    running = (within + before[:, None, :]).reshape(n_rows, N_EXPERTS)
    rank = jnp.sum(jnp.where(onehot, running - 1.0, 0.0), axis=1).astype(jnp.int32)
    counts = jnp.sum(onehot.astype(jnp.int32), axis=0)
    padded = ((counts + tm - 1) // tm) * tm
    ends = jnp.cumsum(padded)
    starts = ends - padded
    pos = starts[e_flat] + rank
    row_token = jnp.zeros((n_tiles * tm,), jnp.int32).at[pos].set(
        jnp.arange(n_rows, dtype=jnp.int32) // TOP_K)
    tile_start = jnp.arange(n_tiles, dtype=jnp.int32) * tm
    tile_expert = jnp.minimum(jnp.sum(tile_start[:, None] >= ends[None, :], axis=1),
                              N_EXPERTS - 1).astype(jnp.int32)
    n_active = (ends[-1] // tm).astype(jnp.int32).reshape(1)
    return pos.reshape(t, TOP_K), row_token, tile_expert, n_active


def _moe_ffn(x2, g_pre, g_post, w_router, e_gate, e_up, e_down, *, tm=512):
    idx, wts, h = router_top2(x2, g_pre, w_router)
    t = x2.shape[0]
    tm = min(tm, t)
    pos, row_token, tile_expert, n_active = _moe_dispatch(idx, tm)
    xs = sc_gather_rows(h, row_token)
    w_gate, w_up = round_to_bf16(e_gate, e_up)
    (w_down,) = round_to_bf16(e_down)
    u = moe_up(xs, tile_expert, n_active, w_gate, w_up, tm=tm)
    ys = moe_down(u, tile_expert, n_active, w_down, tm=tm)
    y2 = sc_gather_rows(ys, pos.T.reshape(TOP_K * t))
    return combine_residual(y2, wts, x2, g_post)


def kernel(x, mix_pre_norm, w_in, b_forget, b_gate, attn_sinks, lam_q1, lam_k1, lam_q2, lam_k2, diff_subln, w_br_a, w_br_b, w_br_c, w_out, mix_post_norm, ffn_pre_norm, ffn_post_norm, dense_w_gate, dense_w_up, dense_w_down, w_router, moe_w_gate, moe_w_up, moe_w_down):
    batch, seq, d = x.shape
    depth = w_in.shape[0]
    x2 = x.reshape(batch * seq, d)
    h = rmsnorm_rows(x2, mix_pre_norm[0])
    for l in range(depth):
        x2, h = _token_mixer(x2, h, batch, seq, l, w_in, b_forget[l], b_gate[l],
                             attn_sinks[l], lam_q1[l], lam_k1[l], lam_q2[l], lam_k2[l],
                             diff_subln[l], w_br_a[l], w_br_b[l], w_br_c[l], w_out[l],
                             mix_post_norm[l], ffn_pre_norm[l])
        i = l // 2
        if l % 2 == 0:
            u = swiglu_up(h, dense_w_gate[i].astype(BF16), dense_w_up[i].astype(BF16))
            g_next = mix_pre_norm[l + 1] if l + 1 < depth else jnp.ones((d,), F32)
            f = dense_w_down.shape[1]
            x2, h = proj_residual(u, dense_w_down[i].astype(BF16), x2, ffn_post_norm[l],
                                  g_next, tm=256, tk=f)
        else:
            x2 = _moe_ffn(x2, ffn_pre_norm[l], ffn_post_norm[l], w_router[i],
                          moe_w_gate[i], moe_w_up[i], moe_w_down[i])
            if l + 1 < depth:
                h = rmsnorm_rows(x2, mix_pre_norm[l + 1])
    return x2.reshape(batch, seq, d)
```

```python
import functools
import math

import jax
import jax.numpy as jnp
import numpy as np
from jax import lax
from jax.experimental import pallas as pl
from jax.experimental.pallas import tpu as pltpu
from jax.experimental.pallas import tpu_sc as plsc

F32 = jnp.float32
BF16 = jnp.bfloat16

HEAD_DIM = 64
LANES = 128
A_Q_HEADS = 12
A_KV_HEADS = 4
A_GROUP = A_Q_HEADS // A_KV_HEADS
WINDOW = 128
B_HEADS = 4
C_HEADS = 12
N_BRANCH = 3
N_EXPERTS = 8
TOP_K = 2
RMS_EPS = 1e-6
NEG_INF = -1e30
VMEM_LIMIT = 56 * 1024 * 1024

A_SLOT_HEAD = (0, 3, 1, 4, 2, 5, 6, 9, 7, 10, 8, 11)


def _params(*sem):
    return pltpu.CompilerParams(dimension_semantics=sem, vmem_limit_bytes=VMEM_LIMIT)


def _rms(x, g):
    var = jnp.mean(x * x, axis=-1, keepdims=True)
    return x * lax.rsqrt(var + RMS_EPS) * g


def _sigmoid(x):
    return 0.5 * jnp.tanh(0.5 * x) + 0.5


def _dot(a, b):
    return jnp.dot(a, b, preferred_element_type=F32)


def _dot_nt(a, b):
    return lax.dot_general(a, b, (((1,), (1,)), ((), ())), preferred_element_type=F32)


def _round_kernel(*refs):
    n = len(refs) // 2
    for src, dst in zip(refs[:n], refs[n:]):
        dst[...] = src[...].astype(dst.dtype)


ROUND_BLOCK_ELEMS = 2 * 1024 * 1024


def round_to_bf16(*ws):
    shape = ws[0].shape
    flat = [w.reshape(-1, shape[-1]) for w in ws]
    n, c = flat[0].shape
    rows = min(n, 1 << ((ROUND_BLOCK_ELEMS // c).bit_length() - 1))
    assert n % rows == 0 and all(w.shape == shape for w in ws)
    blk = pl.BlockSpec((rows, c), lambda i: (i, 0))
    outs = pl.pallas_call(
        _round_kernel,
        out_shape=tuple(jax.ShapeDtypeStruct((n, c), BF16) for _ in ws),
        grid=(n // rows,),
        in_specs=[blk] * len(ws),
        out_specs=tuple(blk for _ in ws),
        compiler_params=_params("parallel"),
        name="round_to_bf16",
    )(*flat)
    return [o.reshape(shape) for o in outs]


def _rmsnorm_kernel(x_ref, g_ref, o_ref):
    o_ref[...] = _rms(x_ref[...], g_ref[...]).astype(o_ref.dtype)


def rmsnorm_rows(x, g, *, tm=512):
    t, d = x.shape
    tm = min(tm, t)
    return pl.pallas_call(
        _rmsnorm_kernel,
        out_shape=jax.ShapeDtypeStruct((t, d), BF16),
        grid=(t // tm,),
        in_specs=[pl.BlockSpec((tm, d), lambda i: (i, 0)),
                  pl.BlockSpec((1, d), lambda i: (0, 0))],
        out_specs=pl.BlockSpec((tm, d), lambda i: (i, 0)),
        compiler_params=_params("parallel"),
        name="rmsnorm_rows",
    )(x, g.reshape(1, d))


def _mm_kernel(a_ref, bt_ref, o_ref):
    o_ref[...] = _dot_nt(a_ref[...], bt_ref[...]).astype(o_ref.dtype)


def _mm_gate_kernel(a_ref, bt_ref, bias_ref, o_ref):
    z = _dot_nt(a_ref[...], bt_ref[...]) + bias_ref[...]
    o_ref[...] = _sigmoid(z).astype(o_ref.dtype)


def matmul(a, bt, *, bias=None, out_dtype=BF16, tm=1024, tn=1024):
    m, k = a.shape
    n = bt.shape[0]
    tm, tn = min(tm, m), min(tn, n)
    assert m % tm == 0 and n % tn == 0
    in_specs = [pl.BlockSpec((tm, k), lambda i, j: (i, 0)),
                pl.BlockSpec((tn, k), lambda i, j: (j, 0))]
    args = [a, bt]
    kern = _mm_kernel
    if bias is not None:
        in_specs.append(pl.BlockSpec((1, tn), lambda i, j: (0, j)))
        args.append(bias.reshape(1, n).astype(F32))
        kern = _mm_gate_kernel
    return pl.pallas_call(
        kern,
        out_shape=jax.ShapeDtypeStruct((m, n), out_dtype),
        grid=(m // tm, n // tn),
        in_specs=in_specs,
        out_specs=pl.BlockSpec((tm, tn), lambda i, j: (i, j)),
        compiler_params=_params("parallel", "parallel"),
        name="matmul_gate" if bias is not None else "matmul",
    )(*args)


def _forget_kernel(h_ref, wt_ref, b_ref, c_ref, *, seq, blk):
    z = _dot_nt(wt_ref[...], h_ref[0]) + b_ref[...]
    log_f = jnp.minimum(z, 0.0) - jnp.log1p(jnp.exp(-jnp.abs(z)))
    r = lax.broadcasted_iota(jnp.int32, (blk, blk), 0)
    c = lax.broadcasted_iota(jnp.int32, (blk, blk), 1)
    upper = (r <= c).astype(F32)
    carry = jnp.zeros((LANES, 1), F32)
    for j in range(seq // blk):
        part = lax.dot_general(log_f[:, j * blk:(j + 1) * blk], upper,
                               (((1,), (0,)), ((), ())),
                               precision=lax.Precision.HIGHEST,
                               preferred_element_type=F32) + carry
        c_ref[0, :, j * blk:(j + 1) * blk] = part
        carry = part[:, blk - 1:blk]


def forget_cumsum(h3, w_ft, b_f):
    b, s, d = h3.shape
    nh = w_ft.shape[0]
    wt = jnp.zeros((LANES, d), BF16).at[:nh].set(w_ft.astype(BF16))
    bias = jnp.zeros((LANES, 1), F32).at[:nh, 0].set(b_f.astype(F32))
    blk = min(256, s)
    return pl.pallas_call(
        functools.partial(_forget_kernel, seq=s, blk=blk),
        out_shape=jax.ShapeDtypeStruct((b, LANES, s), F32),
        grid=(b,),
        in_specs=[pl.BlockSpec((1, s, d), lambda i: (i, 0, 0)),
                  pl.BlockSpec((LANES, d), lambda i: (0, 0)),
                  pl.BlockSpec((LANES, 1), lambda i: (0, 0))],
        out_specs=pl.BlockSpec((1, LANES, s), lambda i: (i, 0, 0)),
        compiler_params=_params("parallel"),
        name="forget_cumsum",
    )(h3, wt, bias)


def _half_mask(tq, half):
    lane = lax.broadcasted_iota(jnp.int32, (tq, LANES), 1)
    return (lane >= HEAD_DIM * half) & (lane < HEAD_DIM * (half + 1))


def _causal_logits(qh, k_ref, lanes, qi, tq, bias):
    n0 = qi * tq
    r = lax.broadcasted_iota(jnp.int32, (tq, tq), 0)
    c = lax.broadcasted_iota(jnp.int32, (tq, tq), 1)
    diag = _dot_nt(qh, k_ref[0, n0:n0 + tq, lanes]) + bias(n0, tq)
    parts = [jnp.where(r >= c, diag, NEG_INF)]
    if qi > 0:
        parts.append(_dot_nt(qh, k_ref[0, :n0, lanes]) + bias(0, n0))
    return parts


def _softmax_parts(parts):
    m = functools.reduce(jnp.maximum, [s.max(axis=-1, keepdims=True) for s in parts])
    exps = [jnp.exp(s - m) for s in parts]
    denom = functools.reduce(jnp.add, [e.sum(axis=-1, keepdims=True) for e in exps])
    return exps, denom


def _value_parts(v_ref, lanes, qi, tq):
    n0 = qi * tq
    return [v_ref[0, n0:n0 + tq, lanes]] + ([v_ref[0, :n0, lanes]] if qi > 0 else [])


def _weighted_values(weights, values):
    return functools.reduce(jnp.add, [_dot(w.astype(v.dtype), v) for w, v in zip(weights, values)])


def _dispatch_on_tile(branch, nq):
    qi = pl.program_id(2)
    for i in range(nq):
        pl.when(qi == i)(functools.partial(branch, qi=i))


def _fox_kernel(q_ref, k_ref, v_ref, c_ref, o_ref, *, tq, nq, npair):
    masks = [_half_mask(tq, h) for h in range(2)]

    def branch(qi):
        for p in range(npair):
            lanes = slice(p * LANES, (p + 1) * LANES)
            q = q_ref[0, :, lanes] * jnp.asarray(HEAD_DIM ** -0.5, q_ref.dtype)
            values = _value_parts(v_ref, lanes, qi, tq)
            outs = []
            for h in range(2):
                qh = jnp.where(masks[h], q, jnp.zeros_like(q))
                bias = lambda start, size: -c_ref[0, 2 * p + h, :, start:start + size]
                exps, denom = _softmax_parts(_causal_logits(qh, k_ref, lanes, qi, tq, bias))
                outs.append(_weighted_values(exps, values) / denom)
            o_ref[0, :, lanes] = jnp.where(masks[0], outs[0], outs[1]).astype(o_ref.dtype)

    _dispatch_on_tile(branch, nq)


def _diff_kernel(slope_ref, lam_ref, q_ref, k_ref, v_ref, g_ref, o_ref, *,
                 tq, nq, nhead, out_scale):
    lam = lam_ref[0]
    masks = [_half_mask(tq, h) for h in range(2)]

    def branch(qi):
        for j in range(nhead):
            lanes = slice(j * LANES, (j + 1) * LANES)
            slope = slope_ref[pl.program_id(1) * nhead + j]
            q = q_ref[0, :, lanes] * jnp.asarray(HEAD_DIM ** -0.5, q_ref.dtype)

            def bias(start, size):
                pos = start + lax.broadcasted_iota(jnp.int32, (1, size), 1)
                return slope * pos.astype(F32)

            maps = []
            for h in range(2):
                qh = jnp.where(masks[h], q, jnp.zeros_like(q))
                maps.append(_softmax_parts(_causal_logits(qh, k_ref, lanes, qi, tq, bias)))
            (e0, d0), (e1, d1) = maps
            w0, w1 = 1.0 / d0, lam / d1
            weights = [a * w0 - b * w1 for a, b in zip(e0, e1)]
            o = _weighted_values(weights, _value_parts(v_ref, lanes, qi, tq))
            o_ref[0, :, lanes] = (_rms(o, g_ref[...]) * out_scale).astype(o_ref.dtype)

    _dispatch_on_tile(branch, nq)


def fox_attention(z3, c4, *, q_col, k_col, v_col, tq=256, npair=3):
    b, s, _ = z3.shape
    tq = min(tq, s)
    w = npair * LANES
    groups = C_HEADS // 2 // npair
    assert q_col % npair == 0 and k_col % npair == 0 and v_col % npair == 0
    return pl.pallas_call(
        functools.partial(_fox_kernel, tq=tq, nq=s // tq, npair=npair),
        out_shape=jax.ShapeDtypeStruct((b, s, C_HEADS // 2 * LANES), BF16),
        grid=(b, groups, s // tq),
        in_specs=[pl.BlockSpec((1, tq, w), lambda bi, g, qi: (bi, qi, q_col // npair + g)),
                  pl.BlockSpec((1, s, w), lambda bi, g, qi: (bi, 0, k_col // npair + g)),
                  pl.BlockSpec((1, s, w), lambda bi, g, qi: (bi, 0, v_col // npair + g)),
                  pl.BlockSpec((1, 2 * npair, 1, s), lambda bi, g, qi: (bi, g, 0, 0))],
        out_specs=pl.BlockSpec((1, tq, w), lambda bi, g, qi: (bi, qi, g)),
        compiler_params=_params("parallel", "parallel", "arbitrary"),
        name="fox_attention",
    )(z3, z3, z3, c4)


def diff_attention(z3, slopes, lam, subln, *, q_col, k_col, v_col, out_scale, tq=256, nhead=2):
    b, s, _ = z3.shape
    tq = min(tq, s)
    w = nhead * LANES
    assert q_col % nhead == 0 and k_col % nhead == 0 and v_col % nhead == 0
    smem = pl.BlockSpec(memory_space=pltpu.SMEM)
    return pl.pallas_call(
        functools.partial(_diff_kernel, tq=tq, nq=s // tq, nhead=nhead, out_scale=out_scale),
        out_shape=jax.ShapeDtypeStruct((b, s, B_HEADS * LANES), BF16),
        grid=(b, B_HEADS // nhead, s // tq),
        in_specs=[smem, smem,
                  pl.BlockSpec((1, tq, w), lambda bi, g, qi: (bi, qi, q_col // nhead + g)),
                  pl.BlockSpec((1, s, w), lambda bi, g, qi: (bi, 0, k_col // nhead + g)),
                  pl.BlockSpec((1, s, w), lambda bi, g, qi: (bi, 0, v_col // nhead + g)),
                  pl.BlockSpec((1, LANES), lambda bi, g, qi: (0, 0))],
        out_specs=pl.BlockSpec((1, tq, w), lambda bi, g, qi: (bi, qi, g)),
        compiler_params=_params("parallel", "parallel", "arbitrary"),
        name="diff_attention",
    )(slopes, lam, z3, z3, z3, subln.reshape(1, LANES).astype(F32))


def _swa_kernel(slope_ref, sink_ref, q_ref, k_ref, v_ref, o_ref, *, blk):
    qi = pl.program_id(1)
    cur = pl.multiple_of(qi * blk, blk)
    prev = pl.multiple_of(jnp.maximum(qi - 1, 0) * blk, blk)
    r = lax.broadcasted_iota(jnp.int32, (blk, 2 * blk), 0)
    c = lax.broadcasted_iota(jnp.int32, (blk, 2 * blk), 1)
    dist = r + blk - c
    keep = (dist >= 0) & (dist < WINDOW) & ((qi > 0) | (c >= blk))
    dist_f = dist.astype(F32)
    scale = jnp.asarray(HEAD_DIM ** -0.5, q_ref.dtype)
    masks = [_half_mask(blk, h) for h in range(2)]
    for tile in range(A_Q_HEADS // 2):
        q = q_ref[0, :, tile * LANES:(tile + 1) * LANES] * scale
        halves = []
        for half in range(2):
            slot = 2 * tile + half
            kv_tile = (A_SLOT_HEAD[slot] // A_GROUP) // 2
            lanes = slice(kv_tile * LANES, (kv_tile + 1) * LANES)
            k = jnp.concatenate([k_ref[0, pl.ds(prev, blk), lanes],
                                 k_ref[0, pl.ds(cur, blk), lanes]], axis=0)
            v = jnp.concatenate([v_ref[0, pl.ds(prev, blk), lanes],
                                 v_ref[0, pl.ds(cur, blk), lanes]], axis=0)
            qh = jnp.where(masks[half], q, jnp.zeros_like(q))
            s = _dot_nt(qh, k) - slope_ref[slot] * dist_f
            s = jnp.where(keep, s, NEG_INF)
            sink = sink_ref[slot]
            m = jnp.maximum(jnp.max(s, axis=-1, keepdims=True), sink)
            p = jnp.exp(s - m)
            denom = jnp.sum(p, axis=-1, keepdims=True) + jnp.exp(sink - m)
            halves.append(_dot(p.astype(v.dtype), v) / denom)
        o_ref[0, :, tile * LANES:(tile + 1) * LANES] = jnp.where(
            masks[0], halves[0], halves[1]).astype(o_ref.dtype)


def swa_attention(z3, slopes, sinks, *, q_tile, k_tile, v_tile):
    b, s, _ = z3.shape
    blk = WINDOW
    qw = A_Q_HEADS * HEAD_DIM
    kw = A_KV_HEADS * HEAD_DIM
    smem = pl.BlockSpec(memory_space=pltpu.SMEM)
    return pl.pallas_call(
        functools.partial(_swa_kernel, blk=blk),
        out_shape=jax.ShapeDtypeStruct((b, s, qw), BF16),
        grid=(b, s // blk),
        in_specs=[smem, smem,
                  pl.BlockSpec((1, blk, qw), lambda bi, qi: (bi, qi, q_tile)),
                  pl.BlockSpec((1, s, kw), lambda bi, qi: (bi, 0, k_tile)),
                  pl.BlockSpec((1, s, kw), lambda bi, qi: (bi, 0, v_tile))],
        out_specs=pl.BlockSpec((1, blk, qw), lambda bi, qi: (bi, qi, 0)),
        compiler_params=_params("parallel", "parallel"),
        name="swa_attention",
    )(slopes, sinks, z3, z3, z3)


def _merge_out_kernel(oa_ref, ob_ref, oc_ref, ga_ref, gb_ref, gc_ref, wa_ref, wb_ref, wc_ref,
                      wo_ref, x_ref, gp_ref, gn_ref, xo_ref, ho_ref):
    merged = ga_ref[...].astype(F32) * _dot(oa_ref[...], wa_ref[...])
    merged += gb_ref[...].astype(F32) * _dot(ob_ref[...], wb_ref[...])
    merged += gc_ref[...].astype(F32) * _dot(oc_ref[...], wc_ref[...])
    m = _dot(merged.astype(wo_ref.dtype), wo_ref[...])
    x_new = x_ref[...] + _rms(m, gp_ref[...])
    xo_ref[...] = x_new
    ho_ref[...] = _rms(x_new, gn_ref[...]).astype(ho_ref.dtype)


def merge_out_residual(oa, ob, oc, gates, wa, wb, wc, w_out, x, g_post, g_next, *, tm=256):
    t, d = x.shape
    tm = min(tm, t)
    row = lambda w: pl.BlockSpec((tm, w), lambda i: (i, 0))
    gate = lambda br: pl.BlockSpec((tm, d), lambda i: (i, br))
    whole = lambda w: pl.BlockSpec(w.shape, lambda i: (0, 0), pipeline_mode=pl.Buffered(1))
    vec = pl.BlockSpec((1, d), lambda i: (0, 0))
    return pl.pallas_call(
        _merge_out_kernel,
        out_shape=(jax.ShapeDtypeStruct((t, d), F32), jax.ShapeDtypeStruct((t, d), BF16)),
        grid=(t // tm,),
        in_specs=[row(oa.shape[1]), row(ob.shape[1]), row(oc.shape[1]),
                  gate(0), gate(1), gate(2),
                  whole(wa), whole(wb), whole(wc), whole(w_out), row(d), vec, vec],
        out_specs=(row(d), row(d)),
        compiler_params=_params("parallel"),
        name="merge_out_residual",
    )(oa, ob, oc, gates, gates, gates, wa, wb, wc, w_out, x,
      g_post.reshape(1, d), g_next.reshape(1, d))


def _proj_res_kernel(a_ref, w_ref, x_ref, gp_ref, gn_ref, xo_ref, ho_ref, acc_ref, *, nk):
    kk = pl.program_id(1)
    part = _dot(a_ref[...], w_ref[...])

    @pl.when(kk == 0)
    def _():
        acc_ref[...] = part

    @pl.when(kk > 0)
    def _():
        acc_ref[...] += part

    @pl.when(kk == nk - 1)
    def _():
        x_new = x_ref[...] + _rms(acc_ref[...], gp_ref[...])
        xo_ref[...] = x_new
        ho_ref[...] = _rms(x_new, gn_ref[...]).astype(ho_ref.dtype)


def proj_residual(a, w, x, g_post, g_next, *, tm, tk):
    t, k = a.shape
    d = w.shape[1]
    tm, tk = min(tm, t), min(tk, k)
    nk = k // tk
    vec = pl.BlockSpec((1, d), lambda i, kk: (0, 0))
    w_mode = dict(pipeline_mode=pl.Buffered(1)) if nk == 1 else {}
    return pl.pallas_call(
        functools.partial(_proj_res_kernel, nk=nk),
        out_shape=(jax.ShapeDtypeStruct((t, d), F32), jax.ShapeDtypeStruct((t, d), BF16)),
        grid=(t // tm, nk),
        in_specs=[pl.BlockSpec((tm, tk), lambda i, kk: (i, kk)),
                  pl.BlockSpec((tk, d), lambda i, kk: (kk, 0), **w_mode),
                  pl.BlockSpec((tm, d), lambda i, kk: (i, 0)),
                  vec, vec],
        out_specs=(pl.BlockSpec((tm, d), lambda i, kk: (i, 0)),
                   pl.BlockSpec((tm, d), lambda i, kk: (i, 0))),
        scratch_shapes=[pltpu.VMEM((tm, d), F32)],
        compiler_params=_params("parallel", "arbitrary"),
        name="proj_residual",
    )(a, w, x, g_post.reshape(1, d), g_next.reshape(1, d))


def _swiglu_up_kernel(h_ref, wg_ref, wu_ref, o_ref):
    h = h_ref[...]
    g = _dot(h, wg_ref[...])
    u = _dot(h, wu_ref[...])
    o_ref[...] = (g * _sigmoid(g) * u).astype(o_ref.dtype)


def swiglu_up(h, wg, wu, *, tm=1024, tn=512):
    t, d = h.shape
    f = wg.shape[1]
    tm, tn = min(tm, t), min(tn, f)
    return pl.pallas_call(
        _swiglu_up_kernel,
        out_shape=jax.ShapeDtypeStruct((t, f), BF16),
        grid=(t // tm, f // tn),
        in_specs=[pl.BlockSpec((tm, d), lambda i, j: (i, 0)),
                  pl.BlockSpec((d, tn), lambda i, j: (0, j)),
                  pl.BlockSpec((d, tn), lambda i, j: (0, j))],
        out_specs=pl.BlockSpec((tm, tn), lambda i, j: (i, j)),
        compiler_params=_params("parallel", "parallel"),
        name="swiglu_up",
    )(h, wg, wu)


def _router_kernel(x_ref, g_ref, w_ref, idx_ref, wt_ref, h_ref):
    h = _rms(x_ref[...], g_ref[...])
    h_ref[...] = h
    logits = jnp.dot(h, w_ref[...], precision=lax.Precision.HIGHEST,
                     preferred_element_type=F32)
    lane = lax.broadcasted_iota(jnp.int32, logits.shape, 1)
    lane_f = lane.astype(F32)
    lg = jnp.where(lane < N_EXPERTS, logits, -jnp.inf)
    m1 = jnp.max(lg, axis=-1, keepdims=True)
    i1 = jnp.min(jnp.where(lg == m1, lane_f, float(LANES)), axis=-1, keepdims=True)
    lg2 = jnp.where(lane_f == i1, -jnp.inf, lg)
    m2 = jnp.max(lg2, axis=-1, keepdims=True)
    i2 = jnp.min(jnp.where(lg2 == m2, lane_f, float(LANES)), axis=-1, keepdims=True)
    e2 = jnp.exp(m2 - m1)
    w1 = 1.0 / (1.0 + e2)
    w2 = e2 / (1.0 + e2)
    idx_ref[...] = jnp.where(lane == 0, i1, jnp.where(lane == 1, i2, 0.0)).astype(jnp.int32)
    wt_ref[...] = jnp.where(lane == 0, w1, jnp.where(lane == 1, w2, 0.0))


def router_top2(x, g, w_router, *, tm=512):
    t, d = x.shape
    tm = min(tm, t)
    wp = jnp.zeros((d, LANES), F32).at[:, :N_EXPERTS].set(w_router.astype(F32))
    out = pl.BlockSpec((tm, LANES), lambda i: (i, 0))
    idx, wt, h = pl.pallas_call(
        _router_kernel,
        out_shape=(jax.ShapeDtypeStruct((t, LANES), jnp.int32),
                   jax.ShapeDtypeStruct((t, LANES), F32),
                   jax.ShapeDtypeStruct((t, d), F32)),
        grid=(t // tm,),
        in_specs=[pl.BlockSpec((tm, d), lambda i: (i, 0)),
                  pl.BlockSpec((1, d), lambda i: (0, 0)),
                  pl.BlockSpec((d, LANES), lambda i: (0, 0))],
        out_specs=(out, out, pl.BlockSpec((tm, d), lambda i: (i, 0))),
        compiler_params=_params("parallel"),
        name="router_top2",
    )(x, g.reshape(1, d), wp)
    return idx[:, :TOP_K], wt[:, :TOP_K], h


SC_INDEX_WINDOW = 128
SC_ROWS_PER_COPY = 16


def sc_gather_rows(src, idx):
    n = idx.shape[0]
    d = src.shape[1]
    assert n % SC_INDEX_WINDOW == 0
    mesh = plsc.VectorSubcoreMesh(core_axis_name="c", subcore_axis_name="s")
    dst = jnp.arange(n, dtype=jnp.int32).reshape(1, n)

    @pl.kernel(out_type=jax.ShapeDtypeStruct((n, d), src.dtype), mesh=mesh,
               scratch_types=[pltpu.VMEM((SC_ROWS_PER_COPY, d), src.dtype)])
    def gather(x_hbm, i_hbm, d_hbm, o_hbm, buf):
        def body(i_vmem, d_vmem):
            for s in range(SC_INDEX_WINDOW // SC_ROWS_PER_COPY):
                rows = pl.ds(s * SC_ROWS_PER_COPY, SC_ROWS_PER_COPY)
                pltpu.sync_copy(x_hbm.at[i_vmem.at[0, rows]], buf)
                pltpu.sync_copy(buf, o_hbm.at[d_vmem.at[0, rows]])

        window = pl.BlockSpec((1, SC_INDEX_WINDOW), lambda i: (0, i))
        pltpu.emit_pipeline(
            body, grid=(n // SC_INDEX_WINDOW,),
            in_specs=[window, window], out_specs=[],
            core_axis_name=("c", "s"),
            dimension_semantics=(pltpu.PARALLEL,),
        )(i_hbm, d_hbm)

    return gather(src, idx.reshape(1, n), dst)


def _moe_up_kernel(te_ref, na_ref, x_ref, wg_ref, wu_ref, o_ref):
    active = pl.program_id(1) < na_ref[0]

    @pl.when(active)
    def _():
        x = x_ref[...].astype(BF16)
        g = _dot(x, wg_ref[0])
        u = _dot(x, wu_ref[0])
        o_ref[...] = (g * _sigmoid(g) * u).astype(o_ref.dtype)

    @pl.when(jnp.logical_not(active))
    def _():
        o_ref[...] = jnp.zeros_like(o_ref)


def _moe_down_kernel(te_ref, na_ref, u_ref, wd_ref, o_ref):
    active = pl.program_id(1) < na_ref[0]

    @pl.when(active)
    def _():
        o_ref[...] = _dot(u_ref[...], wd_ref[0])

    @pl.when(jnp.logical_not(active))
    def _():
        o_ref[...] = jnp.zeros_like(o_ref)


def _active_tile(i, na_ref):
    return jnp.minimum(i, na_ref[0] - 1)


def moe_up(xs, tile_expert, n_active, wg, wu, *, tm, tn=1024):
    p, d = xs.shape
    f = wg.shape[2]
    tn = min(tn, f)
    assert f % tn == 0
    grid_spec = pltpu.PrefetchScalarGridSpec(
        num_scalar_prefetch=2,
        grid=(f // tn, p // tm),
        in_specs=[pl.BlockSpec((tm, d), lambda j, i, te, na: (_active_tile(i, na), 0)),
                  pl.BlockSpec((1, d, tn), lambda j, i, te, na: (te[_active_tile(i, na)], 0, j)),
                  pl.BlockSpec((1, d, tn), lambda j, i, te, na: (te[_active_tile(i, na)], 0, j))],
        out_specs=pl.BlockSpec((tm, tn), lambda j, i, te, na: (i, j)),
    )
    return pl.pallas_call(
        _moe_up_kernel,
        out_shape=jax.ShapeDtypeStruct((p, f), BF16),
        grid_spec=grid_spec,
        compiler_params=_params("arbitrary", "arbitrary"),
        name="moe_up",
    )(tile_expert, n_active, xs, wg, wu)


def moe_down(u, tile_expert, n_active, wd, *, tm, tn=512):
    p, f = u.shape
    d = wd.shape[2]
    tn = min(tn, d)
    grid_spec = pltpu.PrefetchScalarGridSpec(
        num_scalar_prefetch=2,
        grid=(d // tn, p // tm),
        in_specs=[pl.BlockSpec((tm, f), lambda j, i, te, na: (_active_tile(i, na), 0)),
                  pl.BlockSpec((1, f, tn), lambda j, i, te, na: (te[_active_tile(i, na)], 0, j))],
        out_specs=pl.BlockSpec((tm, tn), lambda j, i, te, na: (i, j)),
    )
    return pl.pallas_call(
        _moe_down_kernel,
        out_shape=jax.ShapeDtypeStruct((p, d), F32),
        grid_spec=grid_spec,
        compiler_params=_params("arbitrary", "arbitrary"),
        name="moe_down",
    )(tile_expert, n_active, u, wd)


def _combine_kernel(ya_ref, yb_ref, w_ref, x_ref, g_ref, o_ref):
    w = w_ref[...]
    y = w[:, 0:1] * ya_ref[...] + w[:, 1:2] * yb_ref[...]
    o_ref[...] = x_ref[...] + _rms(y, g_ref[...])


def combine_residual(y2, wts, x, g, *, tm=512):
    t, d = x.shape
    tm = min(tm, t)
    nt = t // tm
    blk = pl.BlockSpec((tm, d), lambda i: (i, 0))
    return pl.pallas_call(
        _combine_kernel,
        out_shape=jax.ShapeDtypeStruct((t, d), F32),
        grid=(nt,),
        in_specs=[blk, pl.BlockSpec((tm, d), lambda i: (i + nt, 0)),
                  pl.BlockSpec((tm, TOP_K), lambda i: (i, 0)), blk,
                  pl.BlockSpec((1, d), lambda i: (0, 0))],
        out_specs=blk,
        compiler_params=_params("parallel"),
        name="combine_residual",
    )(y2, y2, wts, x, g.reshape(1, d))


def _alibi_slopes(n):
    return jnp.exp2(-8.0 * (jnp.arange(n, dtype=F32) + 1.0) / n)


def _token_mixer(x2, h, batch, seq, layer_idx, w_in, b_forget, b_gate, sinks,
                 lq1, lk1, lq2, lk2, subln, w_br_a, w_br_b, w_br_c, w_out,
                 g_post, g_next):
    t, d = x2.shape
    a_w = A_Q_HEADS * HEAD_DIM
    kv_w = A_KV_HEADS * HEAD_DIM
    b_w = B_HEADS * 2 * HEAD_DIM
    c_w = C_HEADS * HEAD_DIM
    n_attn = a_w + 2 * kv_w + 3 * b_w + 3 * c_w

    slot_cols = np.concatenate([np.arange(HEAD_DIM) + HEAD_DIM * hd for hd in A_SLOT_HEAD])
    a_end = a_w + 2 * kv_w
    b_end = a_end + 3 * b_w
    w_in_t = jnp.transpose(w_in, (2, 0, 1))
    rows = lambda lo, hi: w_in_t[lo:hi, layer_idx, :]
    w_attn_t = jnp.concatenate([rows(b_end, n_attn), rows(a_end, b_end),
                                rows(0, a_w)[slot_cols], rows(a_w, a_end)],
                               axis=0).astype(BF16)
    w_f_t = rows(n_attn, n_attn + C_HEADS)
    w_g_t = rows(n_attn + C_HEADS, w_in.shape[2]).astype(BF16)

    z3 = matmul(h, w_attn_t).reshape(batch, seq, n_attn)
    gates = matmul(h, w_g_t, bias=b_gate)
    c = forget_cumsum(h.reshape(batch, seq, d), w_f_t, b_forget)
    c4 = c[:, :C_HEADS].reshape(batch, C_HEADS, 1, seq)

    oc = fox_attention(z3, c4, q_col=0, k_col=C_HEADS // 2, v_col=C_HEADS)

    lam_init = 0.8 - 0.6 * math.exp(-0.3 * layer_idx)
    lam = (jnp.exp(jnp.sum(lq1.astype(F32) * lk1.astype(F32)))
           - jnp.exp(jnp.sum(lq2.astype(F32) * lk2.astype(F32))) + lam_init)
    b0 = 3 * c_w // LANES
    ob = diff_attention(z3, _alibi_slopes(B_HEADS), lam.reshape(1), subln,
                        q_col=b0, k_col=b0 + B_HEADS, v_col=b0 + 2 * B_HEADS,
                        out_scale=1.0 - lam_init)

    slot = np.array(A_SLOT_HEAD)
    a0 = 3 * c_w + 3 * b_w
    oa = swa_attention(z3, _alibi_slopes(A_Q_HEADS)[slot], sinks.astype(F32)[slot],
                       q_tile=a0 // a_w, k_tile=(a0 + a_w) // kv_w,
                       v_tile=(a0 + a_w) // kv_w + 1)

    return merge_out_residual(oa.reshape(t, a_w), ob.reshape(t, b_w), oc.reshape(t, c_w), gates,
                              w_br_a[slot_cols].astype(BF16), w_br_b.astype(BF16),
                              w_br_c.astype(BF16), w_out.astype(BF16), x2, g_post, g_next)


COUNT_BLOCK = 256


def _moe_dispatch(idx, tm):
    t = idx.shape[0]
    n_rows = t * TOP_K
    n_tiles = (n_rows + N_EXPERTS * (tm - 1)) // tm
    e_flat = idx.reshape(n_rows)
    onehot = e_flat[:, None] == jnp.arange(N_EXPERTS)[None, :]
    blk = min(COUNT_BLOCK, n_rows)
    oh = onehot.astype(F32).reshape(n_rows // blk, blk, N_EXPERTS)
    lower = (jnp.arange(blk)[:, None] >= jnp.arange(blk)[None, :]).astype(F32)
    within = jnp.einsum("ij,bjk->bik", lower, oh)
    before = jnp.cumsum(within[:, -1, :], axis=0) - within[:, -1, :]
    running = (within + before[:, None, :]).reshape(n_rows, N_EXPERTS)
    rank = jnp.sum(jnp.where(onehot, running - 1.0, 0.0), axis=1).astype(jnp.int32)
    counts = jnp.sum(onehot.astype(jnp.int32), axis=0)
    padded = ((counts + tm - 1) // tm) * tm
    ends = jnp.cumsum(padded)
    starts = ends - padded
    pos = starts[e_flat] + rank
    row_token = jnp.zeros((n_tiles * tm,), jnp.int32).at[pos].set(
        jnp.arange(n_rows, dtype=jnp.int32) // TOP_K)
    tile_start = jnp.arange(n_tiles, dtype=jnp.int32) * tm
    tile_expert = jnp.minimum(jnp.sum(tile_start[:, None] >= ends[None, :], axis=1),
                              N_EXPERTS - 1).astype(jnp.int32)
    n_active = (ends[-1] // tm).astype(jnp.int32).reshape(1)
    return pos.reshape(t, TOP_K), row_token, tile_expert, n_active


def _moe_ffn(x2, g_pre, g_post, w_router, e_gate, e_up, e_down, *, tm=512):
    idx, wts, h = router_top2(x2, g_pre, w_router)
    t = x2.shape[0]
    tm = min(tm, t)
    pos, row_token, tile_expert, n_active = _moe_dispatch(idx, tm)
    xs = sc_gather_rows(h, row_token)
    u = moe_up(xs, tile_expert, n_active, e_gate, e_up, tm=tm)
    ys = moe_down(u, tile_expert, n_active, e_down, tm=tm)
    y2 = sc_gather_rows(ys, pos.T.reshape(TOP_K * t))
    return combine_residual(y2, wts, x2, g_post)


def kernel(x, mix_pre_norm, w_in, b_forget, b_gate, attn_sinks, lam_q1, lam_k1, lam_q2, lam_k2, diff_subln, w_br_a, w_br_b, w_br_c, w_out, mix_post_norm, ffn_pre_norm, ffn_post_norm, dense_w_gate, dense_w_up, dense_w_down, w_router, moe_w_gate, moe_w_up, moe_w_down):
    batch, seq, d = x.shape
    depth = w_in.shape[0]
    x2 = x.reshape(batch * seq, d)
    h = rmsnorm_rows(x2, mix_pre_norm[0])
    for l in range(depth):
        x2, h = _token_mixer(x2, h, batch, seq, l, w_in, b_forget[l], b_gate[l],
                             attn_sinks[l], lam_q1[l], lam_k1[l], lam_q2[l], lam_k2[l],
                             diff_subln[l], w_br_a[l], w_br_b[l], w_br_c[l], w_out[l],
                             mix_post_norm[l], ffn_pre_norm[l])
        i = l // 2
        if l % 2 == 0:
            u = swiglu_up(h, dense_w_gate[i].astype(BF16), dense_w_up[i].astype(BF16))
            g_next = mix_pre_norm[l + 1] if l + 1 < depth else jnp.ones((d,), F32)
            f = dense_w_down.shape[1]
            x2, h = proj_residual(u, dense_w_down[i].astype(BF16), x2, ffn_post_norm[l],
                                  g_next, tm=256, tk=f)
        else:
            x2 = _moe_ffn(x2, ffn_pre_norm[l], ffn_post_norm[l], w_router[i],
                          moe_w_gate[i], moe_w_up[i], moe_w_down[i])
            if l + 1 < depth:
                h = rmsnorm_rows(x2, mix_pre_norm[l + 1])
    return x2.reshape(batch, seq, d)
```

```python
import functools
import math

import jax
import jax.numpy as jnp
import numpy as np
from jax import lax
from jax.experimental import pallas as pl
from jax.experimental.pallas import tpu as pltpu
from jax.experimental.pallas import tpu_sc as plsc

F32 = jnp.float32
BF16 = jnp.bfloat16

HEAD_DIM = 64
LANES = 128
A_Q_HEADS = 12
A_KV_HEADS = 4
A_GROUP = A_Q_HEADS // A_KV_HEADS
WINDOW = 128
B_HEADS = 4
C_HEADS = 12
N_BRANCH = 3
N_EXPERTS = 8
TOP_K = 2
RMS_EPS = 1e-6
NEG_INF = -1e30
VMEM_LIMIT = 56 * 1024 * 1024

A_SLOT_HEAD = (0, 3, 1, 4, 2, 5, 6, 9, 7, 10, 8, 11)


def _params(*sem):
    return pltpu.CompilerParams(dimension_semantics=sem, vmem_limit_bytes=VMEM_LIMIT)


def _rms(x, g):
    var = jnp.mean(x * x, axis=-1, keepdims=True)
    return x * lax.rsqrt(var + RMS_EPS) * g


def _sigmoid(x):
    return 0.5 * jnp.tanh(0.5 * x) + 0.5


def _dot(a, b):
    return jnp.dot(a, b, preferred_element_type=F32)


def _dot_nt(a, b):
    return lax.dot_general(a, b, (((1,), (1,)), ((), ())), preferred_element_type=F32)


def _round_kernel(*refs):
    n = len(refs) // 2
    for src, dst in zip(refs[:n], refs[n:]):
        dst[...] = src[...].astype(dst.dtype)


ROUND_BLOCK_ELEMS = 2 * 1024 * 1024


def round_to_bf16(*ws):
    shape = ws[0].shape
    flat = [w.reshape(-1, shape[-1]) for w in ws]
    n, c = flat[0].shape
    rows = min(n, 1 << ((ROUND_BLOCK_ELEMS // c).bit_length() - 1))
    assert n % rows == 0 and all(w.shape == shape for w in ws)
    blk = pl.BlockSpec((rows, c), lambda i: (i, 0))
    outs = pl.pallas_call(
        _round_kernel,
        out_shape=tuple(jax.ShapeDtypeStruct((n, c), BF16) for _ in ws),
        grid=(n // rows,),
        in_specs=[blk] * len(ws),
        out_specs=tuple(blk for _ in ws),
        compiler_params=_params("parallel"),
        name="round_to_bf16",
    )(*flat)
    return [o.reshape(shape) for o in outs]


def _rmsnorm_kernel(x_ref, g_ref, o_ref):
    o_ref[...] = _rms(x_ref[...], g_ref[...]).astype(o_ref.dtype)


def rmsnorm_rows(x, g, *, tm=512):
    t, d = x.shape
    tm = min(tm, t)
    return pl.pallas_call(
        _rmsnorm_kernel,
        out_shape=jax.ShapeDtypeStruct((t, d), BF16),
        grid=(t // tm,),
        in_specs=[pl.BlockSpec((tm, d), lambda i: (i, 0)),
                  pl.BlockSpec((1, d), lambda i: (0, 0))],
        out_specs=pl.BlockSpec((tm, d), lambda i: (i, 0)),
        compiler_params=_params("parallel"),
        name="rmsnorm_rows",
    )(x, g.reshape(1, d))


def _mm_kernel(a_ref, bt_ref, o_ref):
    o_ref[...] = _dot_nt(a_ref[...], bt_ref[...]).astype(o_ref.dtype)


def _mm_gate_kernel(a_ref, bt_ref, bias_ref, o_ref):
    z = _dot_nt(a_ref[...], bt_ref[...]) + bias_ref[...]
    o_ref[...] = _sigmoid(z).astype(o_ref.dtype)


def matmul(a, bt, *, bias=None, out_dtype=BF16, tm=1024, tn=1024):
    m, k = a.shape
    n = bt.shape[0]
    tm, tn = min(tm, m), min(tn, n)
    assert m % tm == 0 and n % tn == 0
    in_specs = [pl.BlockSpec((tm, k), lambda i, j: (i, 0)),
                pl.BlockSpec((tn, k), lambda i, j: (j, 0))]
    args = [a, bt]
    kern = _mm_kernel
    if bias is not None:
        in_specs.append(pl.BlockSpec((1, tn), lambda i, j: (0, j)))
        args.append(bias.reshape(1, n).astype(F32))
        kern = _mm_gate_kernel
    return pl.pallas_call(
        kern,
        out_shape=jax.ShapeDtypeStruct((m, n), out_dtype),
        grid=(m // tm, n // tn),
        in_specs=in_specs,
        out_specs=pl.BlockSpec((tm, tn), lambda i, j: (i, j)),
        compiler_params=_params("parallel", "parallel"),
        name="matmul_gate" if bias is not None else "matmul",
    )(*args)


def _forget_kernel(h_ref, wt_ref, b_ref, c_ref, *, seq, blk):
    z = _dot_nt(wt_ref[...], h_ref[0]) + b_ref[...]
    log_f = jnp.minimum(z, 0.0) - jnp.log1p(jnp.exp(-jnp.abs(z)))
    r = lax.broadcasted_iota(jnp.int32, (blk, blk), 0)
    c = lax.broadcasted_iota(jnp.int32, (blk, blk), 1)
    upper = (r <= c).astype(F32)
    carry = jnp.zeros((LANES, 1), F32)
    for j in range(seq // blk):
        part = lax.dot_general(log_f[:, j * blk:(j + 1) * blk], upper,
                               (((1,), (0,)), ((), ())),
                               precision=lax.Precision.HIGHEST,
                               preferred_element_type=F32) + carry
        c_ref[0, :, j * blk:(j + 1) * blk] = part
        carry = part[:, blk - 1:blk]


def forget_cumsum(h3, w_ft, b_f):
    b, s, d = h3.shape
    nh = w_ft.shape[0]
    wt = jnp.zeros((LANES, d), BF16).at[:nh].set(w_ft.astype(BF16))
    bias = jnp.zeros((LANES, 1), F32).at[:nh, 0].set(b_f.astype(F32))
    blk = min(256, s)
    return pl.pallas_call(
        functools.partial(_forget_kernel, seq=s, blk=blk),
        out_shape=jax.ShapeDtypeStruct((b, LANES, s), F32),
        grid=(b,),
        in_specs=[pl.BlockSpec((1, s, d), lambda i: (i, 0, 0)),
                  pl.BlockSpec((LANES, d), lambda i: (0, 0)),
                  pl.BlockSpec((LANES, 1), lambda i: (0, 0))],
        out_specs=pl.BlockSpec((1, LANES, s), lambda i: (i, 0, 0)),
        compiler_params=_params("parallel"),
        name="forget_cumsum",
    )(h3, wt, bias)


def _half_mask(tq, half):
    lane = lax.broadcasted_iota(jnp.int32, (tq, LANES), 1)
    return (lane >= HEAD_DIM * half) & (lane < HEAD_DIM * (half + 1))


def _causal_logits(qh, k_ref, lanes, qi, tq, bias):
    n0 = qi * tq
    r = lax.broadcasted_iota(jnp.int32, (tq, tq), 0)
    c = lax.broadcasted_iota(jnp.int32, (tq, tq), 1)
    diag = _dot_nt(qh, k_ref[0, n0:n0 + tq, lanes]) + bias(n0, tq)
    parts = [jnp.where(r >= c, diag, NEG_INF)]
    if qi > 0:
        parts.append(_dot_nt(qh, k_ref[0, :n0, lanes]) + bias(0, n0))
    return parts


def _softmax_parts(parts):
    m = functools.reduce(jnp.maximum, [s.max(axis=-1, keepdims=True) for s in parts])
    exps = [jnp.exp(s - m) for s in parts]
    denom = functools.reduce(jnp.add, [e.sum(axis=-1, keepdims=True) for e in exps])
    return exps, denom


def _value_parts(v_ref, lanes, qi, tq):
    n0 = qi * tq
    return [v_ref[0, n0:n0 + tq, lanes]] + ([v_ref[0, :n0, lanes]] if qi > 0 else [])


def _weighted_values(weights, values):
    return functools.reduce(jnp.add, [_dot(w.astype(v.dtype), v) for w, v in zip(weights, values)])


def _dispatch_on_tile(branch, nq):
    qi = pl.program_id(2)
    for i in range(nq):
        pl.when(qi == i)(functools.partial(branch, qi=i))


def _fox_kernel(q_ref, k_ref, v_ref, c_ref, o_ref, *, tq, nq, npair):
    masks = [_half_mask(tq, h) for h in range(2)]

    def branch(qi):
        for p in range(npair):
            lanes = slice(p * LANES, (p + 1) * LANES)
            q = q_ref[0, :, lanes] * jnp.asarray(HEAD_DIM ** -0.5, q_ref.dtype)
            values = _value_parts(v_ref, lanes, qi, tq)
            outs = []
            for h in range(2):
                qh = jnp.where(masks[h], q, jnp.zeros_like(q))
                bias = lambda start, size: -c_ref[0, 2 * p + h, :, start:start + size]
                exps, denom = _softmax_parts(_causal_logits(qh, k_ref, lanes, qi, tq, bias))
                outs.append(_weighted_values(exps, values) / denom)
            o_ref[0, :, lanes] = jnp.where(masks[0], outs[0], outs[1]).astype(o_ref.dtype)

    _dispatch_on_tile(branch, nq)


def _diff_kernel(slope_ref, lam_ref, q_ref, k_ref, v_ref, g_ref, o_ref, *,
                 tq, nq, nhead, out_scale):
    lam = lam_ref[0]
    masks = [_half_mask(tq, h) for h in range(2)]

    def branch(qi):
        for j in range(nhead):
            lanes = slice(j * LANES, (j + 1) * LANES)
            slope = slope_ref[pl.program_id(1) * nhead + j]
            q = q_ref[0, :, lanes] * jnp.asarray(HEAD_DIM ** -0.5, q_ref.dtype)

            def bias(start, size):
                pos = start + lax.broadcasted_iota(jnp.int32, (1, size), 1)
                return slope * pos.astype(F32)

            maps = []
            for h in range(2):
                qh = jnp.where(masks[h], q, jnp.zeros_like(q))
                maps.append(_softmax_parts(_causal_logits(qh, k_ref, lanes, qi, tq, bias)))
            (e0, d0), (e1, d1) = maps
            w0, w1 = 1.0 / d0, lam / d1
            weights = [a * w0 - b * w1 for a, b in zip(e0, e1)]
            o = _weighted_values(weights, _value_parts(v_ref, lanes, qi, tq))
            o_ref[0, :, lanes] = (_rms(o, g_ref[...]) * out_scale).astype(o_ref.dtype)

    _dispatch_on_tile(branch, nq)


def fox_attention(z3, c4, *, q_col, k_col, v_col, tq=256, npair=3):
    b, s, _ = z3.shape
    tq = min(tq, s)
    w = npair * LANES
    groups = C_HEADS // 2 // npair
    assert q_col % npair == 0 and k_col % npair == 0 and v_col % npair == 0
    return pl.pallas_call(
        functools.partial(_fox_kernel, tq=tq, nq=s // tq, npair=npair),
        out_shape=jax.ShapeDtypeStruct((b, s, C_HEADS // 2 * LANES), BF16),
        grid=(b, groups, s // tq),
        in_specs=[pl.BlockSpec((1, tq, w), lambda bi, g, qi: (bi, qi, q_col // npair + g)),
                  pl.BlockSpec((1, s, w), lambda bi, g, qi: (bi, 0, k_col // npair + g)),
                  pl.BlockSpec((1, s, w), lambda bi, g, qi: (bi, 0, v_col // npair + g)),
                  pl.BlockSpec((1, 2 * npair, 1, s), lambda bi, g, qi: (bi, g, 0, 0))],
        out_specs=pl.BlockSpec((1, tq, w), lambda bi, g, qi: (bi, qi, g)),
        compiler_params=_params("parallel", "parallel", "arbitrary"),
        name="fox_attention",
    )(z3, z3, z3, c4)


def diff_attention(z3, slopes, lam, subln, *, q_col, k_col, v_col, out_scale, tq=256, nhead=2):
    b, s, _ = z3.shape
    tq = min(tq, s)
    w = nhead * LANES
    assert q_col % nhead == 0 and k_col % nhead == 0 and v_col % nhead == 0
    smem = pl.BlockSpec(memory_space=pltpu.SMEM)
    return pl.pallas_call(
        functools.partial(_diff_kernel, tq=tq, nq=s // tq, nhead=nhead, out_scale=out_scale),
        out_shape=jax.ShapeDtypeStruct((b, s, B_HEADS * LANES), BF16),
        grid=(b, B_HEADS // nhead, s // tq),
        in_specs=[smem, smem,
                  pl.BlockSpec((1, tq, w), lambda bi, g, qi: (bi, qi, q_col // nhead + g)),
                  pl.BlockSpec((1, s, w), lambda bi, g, qi: (bi, 0, k_col // nhead + g)),
                  pl.BlockSpec((1, s, w), lambda bi, g, qi: (bi, 0, v_col // nhead + g)),
                  pl.BlockSpec((1, LANES), lambda bi, g, qi: (0, 0))],
        out_specs=pl.BlockSpec((1, tq, w), lambda bi, g, qi: (bi, qi, g)),
        compiler_params=_params("parallel", "parallel", "arbitrary"),
        name="diff_attention",
    )(slopes, lam, z3, z3, z3, subln.reshape(1, LANES).astype(F32))


def _swa_kernel(slope_ref, sink_ref, q_ref, k_ref, v_ref, o_ref, *, blk):
    qi = pl.program_id(1)
    cur = pl.multiple_of(qi * blk, blk)
    prev = pl.multiple_of(jnp.maximum(qi - 1, 0) * blk, blk)
    r = lax.broadcasted_iota(jnp.int32, (blk, 2 * blk), 0)
    c = lax.broadcasted_iota(jnp.int32, (blk, 2 * blk), 1)
    dist = r + blk - c
    keep = (dist >= 0) & (dist < WINDOW) & ((qi > 0) | (c >= blk))
    dist_f = dist.astype(F32)
    scale = jnp.asarray(HEAD_DIM ** -0.5, q_ref.dtype)
    masks = [_half_mask(blk, h) for h in range(2)]
    for tile in range(A_Q_HEADS // 2):
        q = q_ref[0, :, tile * LANES:(tile + 1) * LANES] * scale
        halves = []
        for half in range(2):
            slot = 2 * tile + half
            kv_tile = (A_SLOT_HEAD[slot] // A_GROUP) // 2
            lanes = slice(kv_tile * LANES, (kv_tile + 1) * LANES)
            k = jnp.concatenate([k_ref[0, pl.ds(prev, blk), lanes],
                                 k_ref[0, pl.ds(cur, blk), lanes]], axis=0)
            v = jnp.concatenate([v_ref[0, pl.ds(prev, blk), lanes],
                                 v_ref[0, pl.ds(cur, blk), lanes]], axis=0)
            qh = jnp.where(masks[half], q, jnp.zeros_like(q))
            s = _dot_nt(qh, k) - slope_ref[slot] * dist_f
            s = jnp.where(keep, s, NEG_INF)
            sink = sink_ref[slot]
            m = jnp.maximum(jnp.max(s, axis=-1, keepdims=True), sink)
            p = jnp.exp(s - m)
            denom = jnp.sum(p, axis=-1, keepdims=True) + jnp.exp(sink - m)
            halves.append(_dot(p.astype(v.dtype), v) / denom)
        o_ref[0, :, tile * LANES:(tile + 1) * LANES] = jnp.where(
            masks[0], halves[0], halves[1]).astype(o_ref.dtype)


def swa_attention(z3, slopes, sinks, *, q_tile, k_tile, v_tile):
    b, s, _ = z3.shape
    blk = WINDOW
    qw = A_Q_HEADS * HEAD_DIM
    kw = A_KV_HEADS * HEAD_DIM
    smem = pl.BlockSpec(memory_space=pltpu.SMEM)
    return pl.pallas_call(
        functools.partial(_swa_kernel, blk=blk),
        out_shape=jax.ShapeDtypeStruct((b, s, qw), BF16),
        grid=(b, s // blk),
        in_specs=[smem, smem,
                  pl.BlockSpec((1, blk, qw), lambda bi, qi: (bi, qi, q_tile)),
                  pl.BlockSpec((1, s, kw), lambda bi, qi: (bi, 0, k_tile)),
                  pl.BlockSpec((1, s, kw), lambda bi, qi: (bi, 0, v_tile))],
        out_specs=pl.BlockSpec((1, blk, qw), lambda bi, qi: (bi, qi, 0)),
        compiler_params=_params("parallel", "parallel"),
        name="swa_attention",
    )(slopes, sinks, z3, z3, z3)


def _merge_out_kernel(oa_ref, ob_ref, oc_ref, ga_ref, gb_ref, gc_ref, wa_ref, wb_ref, wc_ref,
                      wo_ref, x_ref, gp_ref, gn_ref, xo_ref, ho_ref):
    merged = ga_ref[...].astype(F32) * _dot(oa_ref[...], wa_ref[...])
    merged += gb_ref[...].astype(F32) * _dot(ob_ref[...], wb_ref[...])
    merged += gc_ref[...].astype(F32) * _dot(oc_ref[...], wc_ref[...])
    m = _dot(merged.astype(wo_ref.dtype), wo_ref[...])
    x_new = x_ref[...] + _rms(m, gp_ref[...])
    xo_ref[...] = x_new
    ho_ref[...] = _rms(x_new, gn_ref[...]).astype(ho_ref.dtype)


def merge_out_residual(oa, ob, oc, gates, wa, wb, wc, w_out, x, g_post, g_next, *, tm=256):
    t, d = x.shape
    tm = min(tm, t)
    row = lambda w: pl.BlockSpec((tm, w), lambda i: (i, 0))
    gate = lambda br: pl.BlockSpec((tm, d), lambda i: (i, br))
    whole = lambda w: pl.BlockSpec(w.shape, lambda i: (0, 0), pipeline_mode=pl.Buffered(1))
    vec = pl.BlockSpec((1, d), lambda i: (0, 0))
    return pl.pallas_call(
        _merge_out_kernel,
        out_shape=(jax.ShapeDtypeStruct((t, d), F32), jax.ShapeDtypeStruct((t, d), BF16)),
        grid=(t // tm,),
        in_specs=[row(oa.shape[1]), row(ob.shape[1]), row(oc.shape[1]),
                  gate(0), gate(1), gate(2),
                  whole(wa), whole(wb), whole(wc), whole(w_out), row(d), vec, vec],
        out_specs=(row(d), row(d)),
        compiler_params=_params("parallel"),
        name="merge_out_residual",
    )(oa, ob, oc, gates, gates, gates, wa, wb, wc, w_out, x,
      g_post.reshape(1, d), g_next.reshape(1, d))


def _proj_res_kernel(a_ref, w_ref, x_ref, gp_ref, gn_ref, xo_ref, ho_ref, acc_ref, *, nk):
    kk = pl.program_id(1)
    part = _dot(a_ref[...], w_ref[...])

    @pl.when(kk == 0)
    def _():
        acc_ref[...] = part

    @pl.when(kk > 0)
    def _():
        acc_ref[...] += part

    @pl.when(kk == nk - 1)
    def _():
        x_new = x_ref[...] + _rms(acc_ref[...], gp_ref[...])
        xo_ref[...] = x_new
        ho_ref[...] = _rms(x_new, gn_ref[...]).astype(ho_ref.dtype)


def proj_residual(a, w, x, g_post, g_next, *, tm, tk):
    t, k = a.shape
    d = w.shape[1]
    tm, tk = min(tm, t), min(tk, k)
    nk = k // tk
    vec = pl.BlockSpec((1, d), lambda i, kk: (0, 0))
    w_mode = dict(pipeline_mode=pl.Buffered(1)) if nk == 1 else {}
    return pl.pallas_call(
        functools.partial(_proj_res_kernel, nk=nk),
        out_shape=(jax.ShapeDtypeStruct((t, d), F32), jax.ShapeDtypeStruct((t, d), BF16)),
        grid=(t // tm, nk),
        in_specs=[pl.BlockSpec((tm, tk), lambda i, kk: (i, kk)),
                  pl.BlockSpec((tk, d), lambda i, kk: (kk, 0), **w_mode),
                  pl.BlockSpec((tm, d), lambda i, kk: (i, 0)),
                  vec, vec],
        out_specs=(pl.BlockSpec((tm, d), lambda i, kk: (i, 0)),
                   pl.BlockSpec((tm, d), lambda i, kk: (i, 0))),
        scratch_shapes=[pltpu.VMEM((tm, d), F32)],
        compiler_params=_params("parallel", "arbitrary"),
        name="proj_residual",
    )(a, w, x, g_post.reshape(1, d), g_next.reshape(1, d))


def _swiglu_up_kernel(h_ref, wg_ref, wu_ref, o_ref):
    h = h_ref[...]
    g = _dot(h, wg_ref[...])
    u = _dot(h, wu_ref[...])
    o_ref[...] = (g * _sigmoid(g) * u).astype(o_ref.dtype)


def swiglu_up(h, wg, wu, *, tm=1024, tn=512):
    t, d = h.shape
    f = wg.shape[1]
    tm, tn = min(tm, t), min(tn, f)
    return pl.pallas_call(
        _swiglu_up_kernel,
        out_shape=jax.ShapeDtypeStruct((t, f), BF16),
        grid=(t // tm, f // tn),
        in_specs=[pl.BlockSpec((tm, d), lambda i, j: (i, 0)),
                  pl.BlockSpec((d, tn), lambda i, j: (0, j)),
                  pl.BlockSpec((d, tn), lambda i, j: (0, j))],
        out_specs=pl.BlockSpec((tm, tn), lambda i, j: (i, j)),
        compiler_params=_params("parallel", "parallel"),
        name="swiglu_up",
    )(h, wg, wu)


def _router_kernel(x_ref, g_ref, w_ref, idx_ref, wt_ref, h_ref):
    h = _rms(x_ref[...], g_ref[...])
    bits = pltpu.bitcast(h.astype(BF16).astype(F32), jnp.uint32)
    half = h.shape[1] // 2
    h_ref[...] = (bits[:, :half] >> 16) | bits[:, half:]
    logits = jnp.dot(h, w_ref[...], precision=lax.Precision.HIGHEST,
                     preferred_element_type=F32)
    lane = lax.broadcasted_iota(jnp.int32, logits.shape, 1)
    lane_f = lane.astype(F32)
    lg = jnp.where(lane < N_EXPERTS, logits, -jnp.inf)
    m1 = jnp.max(lg, axis=-1, keepdims=True)
    i1 = jnp.min(jnp.where(lg == m1, lane_f, float(LANES)), axis=-1, keepdims=True)
    lg2 = jnp.where(lane_f == i1, -jnp.inf, lg)
    m2 = jnp.max(lg2, axis=-1, keepdims=True)
    i2 = jnp.min(jnp.where(lg2 == m2, lane_f, float(LANES)), axis=-1, keepdims=True)
    e2 = jnp.exp(m2 - m1)
    w1 = 1.0 / (1.0 + e2)
    w2 = e2 / (1.0 + e2)
    idx_ref[...] = jnp.where(lane == 0, i1, jnp.where(lane == 1, i2, 0.0)).astype(jnp.int32)
    wt_ref[...] = jnp.where(lane == 0, w1, jnp.where(lane == 1, w2, 0.0))


def router_top2(x, g, w_router, *, tm=512):
    t, d = x.shape
    tm = min(tm, t)
    wp = jnp.zeros((d, LANES), F32).at[:, :N_EXPERTS].set(w_router.astype(F32))
    out = pl.BlockSpec((tm, LANES), lambda i: (i, 0))
    idx, wt, h = pl.pallas_call(
        _router_kernel,
        out_shape=(jax.ShapeDtypeStruct((t, LANES), jnp.int32),
                   jax.ShapeDtypeStruct((t, LANES), F32),
                   jax.ShapeDtypeStruct((t, d // 2), jnp.uint32)),
        grid=(t // tm,),
        in_specs=[pl.BlockSpec((tm, d), lambda i: (i, 0)),
                  pl.BlockSpec((1, d), lambda i: (0, 0)),
                  pl.BlockSpec((d, LANES), lambda i: (0, 0))],
        out_specs=(out, out, pl.BlockSpec((tm, d // 2), lambda i: (i, 0))),
        compiler_params=_params("parallel"),
        name="router_top2",
    )(x, g.reshape(1, d), wp)
    return idx[:, :TOP_K], wt[:, :TOP_K], h


SC_INDEX_WINDOW = 128
SC_ROWS_PER_COPY = 16


def sc_gather_rows(src, idx):
    n = idx.shape[0]
    d = src.shape[1]
    assert n % SC_INDEX_WINDOW == 0
    mesh = plsc.VectorSubcoreMesh(core_axis_name="c", subcore_axis_name="s")
    dst = jnp.arange(n, dtype=jnp.int32).reshape(1, n)

    @pl.kernel(out_type=jax.ShapeDtypeStruct((n, d), src.dtype), mesh=mesh,
               scratch_types=[pltpu.VMEM((SC_ROWS_PER_COPY, d), src.dtype)])
    def gather(x_hbm, i_hbm, d_hbm, o_hbm, buf):
        def body(i_vmem, d_vmem):
            for s in range(SC_INDEX_WINDOW // SC_ROWS_PER_COPY):
                rows = pl.ds(s * SC_ROWS_PER_COPY, SC_ROWS_PER_COPY)
                pltpu.sync_copy(x_hbm.at[i_vmem.at[0, rows]], buf)
                pltpu.sync_copy(buf, o_hbm.at[d_vmem.at[0, rows]])

        window = pl.BlockSpec((1, SC_INDEX_WINDOW), lambda i: (0, i))
        pltpu.emit_pipeline(
            body, grid=(n // SC_INDEX_WINDOW,),
            in_specs=[window, window], out_specs=[],
            core_axis_name=("c", "s"),
            dimension_semantics=(pltpu.PARALLEL,),
        )(i_hbm, d_hbm)

    return gather(src, idx.reshape(1, n), dst)


def _moe_up_kernel(te_ref, na_ref, x_ref, wg_ref, wu_ref, o_ref):
    active = pl.program_id(1) < na_ref[0]

    @pl.when(active)
    def _():
        words = x_ref[...]
        half = words.shape[1]
        lo = pltpu.bitcast(words << 16, F32).astype(BF16)
        hi = pltpu.bitcast(words & jnp.uint32(0xFFFF0000), F32).astype(BF16)
        g = _dot(lo, wg_ref[0, :half]) + _dot(hi, wg_ref[0, half:])
        u = _dot(lo, wu_ref[0, :half]) + _dot(hi, wu_ref[0, half:])
        o_ref[...] = (g * _sigmoid(g) * u).astype(o_ref.dtype)

    @pl.when(jnp.logical_not(active))
    def _():
        o_ref[...] = jnp.zeros_like(o_ref)


def _moe_down_kernel(te_ref, na_ref, u_ref, wd_ref, o_ref):
    active = pl.program_id(1) < na_ref[0]

    @pl.when(active)
    def _():
        o_ref[...] = _dot(u_ref[...], wd_ref[0])

    @pl.when(jnp.logical_not(active))
    def _():
        o_ref[...] = jnp.zeros_like(o_ref)


def _active_tile(i, na_ref):
    return jnp.minimum(i, na_ref[0] - 1)


def moe_up(xs, tile_expert, n_active, wg, wu, *, tm, tn=1024):
    p = xs.shape[0]
    d, f = wg.shape[1:]
    tn = min(tn, f)
    assert f % tn == 0
    grid_spec = pltpu.PrefetchScalarGridSpec(
        num_scalar_prefetch=2,
        grid=(f // tn, p // tm),
        in_specs=[pl.BlockSpec((tm, d // 2), lambda j, i, te, na: (_active_tile(i, na), 0)),
                  pl.BlockSpec((1, d, tn), lambda j, i, te, na: (te[_active_tile(i, na)], 0, j)),
                  pl.BlockSpec((1, d, tn), lambda j, i, te, na: (te[_active_tile(i, na)], 0, j))],
        out_specs=pl.BlockSpec((tm, tn), lambda j, i, te, na: (i, j)),
    )
    return pl.pallas_call(
        _moe_up_kernel,
        out_shape=jax.ShapeDtypeStruct((p, f), BF16),
        grid_spec=grid_spec,
        compiler_params=_params("arbitrary", "arbitrary"),
        name="moe_up",
    )(tile_expert, n_active, xs, wg, wu)


def moe_down(u, tile_expert, n_active, wd, *, tm, tn=512):
    p, f = u.shape
    d = wd.shape[2]
    tn = min(tn, d)
    grid_spec = pltpu.PrefetchScalarGridSpec(
        num_scalar_prefetch=2,
        grid=(d // tn, p // tm),
        in_specs=[pl.BlockSpec((tm, f), lambda j, i, te, na: (_active_tile(i, na), 0)),
                  pl.BlockSpec((1, f, tn), lambda j, i, te, na: (te[_active_tile(i, na)], 0, j))],
        out_specs=pl.BlockSpec((tm, tn), lambda j, i, te, na: (i, j)),
    )
    return pl.pallas_call(
        _moe_down_kernel,
        out_shape=jax.ShapeDtypeStruct((p, d), F32),
        grid_spec=grid_spec,
        compiler_params=_params("arbitrary", "arbitrary"),
        name="moe_down",
    )(tile_expert, n_active, u, wd)


def _combine_kernel(ya_ref, yb_ref, w_ref, x_ref, g_ref, o_ref):
    w = w_ref[...]
    y = w[:, 0:1] * ya_ref[...] + w[:, 1:2] * yb_ref[...]
    o_ref[...] = x_ref[...] + _rms(y, g_ref[...])


def combine_residual(y2, wts, x, g, *, tm=512):
    t, d = x.shape
    tm = min(tm, t)
    nt = t // tm
    blk = pl.BlockSpec((tm, d), lambda i: (i, 0))
    return pl.pallas_call(
        _combine_kernel,
        out_shape=jax.ShapeDtypeStruct((t, d), F32),
        grid=(nt,),
        in_specs=[blk, pl.BlockSpec((tm, d), lambda i: (i + nt, 0)),
                  pl.BlockSpec((tm, TOP_K), lambda i: (i, 0)), blk,
                  pl.BlockSpec((1, d), lambda i: (0, 0))],
        out_specs=blk,
        compiler_params=_params("parallel"),
        name="combine_residual",
    )(y2, y2, wts, x, g.reshape(1, d))


def _alibi_slopes(n):
    return jnp.exp2(-8.0 * (jnp.arange(n, dtype=F32) + 1.0) / n)


def _token_mixer(x2, h, batch, seq, layer_idx, w_in, b_forget, b_gate, sinks,
                 lq1, lk1, lq2, lk2, subln, w_br_a, w_br_b, w_br_c, w_out,
                 g_post, g_next):
    t, d = x2.shape
    a_w = A_Q_HEADS * HEAD_DIM
    kv_w = A_KV_HEADS * HEAD_DIM
    b_w = B_HEADS * 2 * HEAD_DIM
    c_w = C_HEADS * HEAD_DIM
    n_attn = a_w + 2 * kv_w + 3 * b_w + 3 * c_w

    slot_cols = np.concatenate([np.arange(HEAD_DIM) + HEAD_DIM * hd for hd in A_SLOT_HEAD])
    a_end = a_w + 2 * kv_w
    b_end = a_end + 3 * b_w
    w_in_t = jnp.transpose(w_in, (2, 0, 1))
    rows = lambda lo, hi: w_in_t[lo:hi, layer_idx, :]
    w_attn_t = jnp.concatenate([rows(b_end, n_attn), rows(a_end, b_end),
                                rows(0, a_w)[slot_cols], rows(a_w, a_end)],
                               axis=0).astype(BF16)
    w_f_t = rows(n_attn, n_attn + C_HEADS)
    w_g_t = rows(n_attn + C_HEADS, w_in.shape[2]).astype(BF16)

    z3 = matmul(h, w_attn_t).reshape(batch, seq, n_attn)
    gates = matmul(h, w_g_t, bias=b_gate)
    c = forget_cumsum(h.reshape(batch, seq, d), w_f_t, b_forget)
    c4 = c[:, :C_HEADS].reshape(batch, C_HEADS, 1, seq)

    oc = fox_attention(z3, c4, q_col=0, k_col=C_HEADS // 2, v_col=C_HEADS)

    lam_init = 0.8 - 0.6 * math.exp(-0.3 * layer_idx)
    lam = (jnp.exp(jnp.sum(lq1.astype(F32) * lk1.astype(F32)))
           - jnp.exp(jnp.sum(lq2.astype(F32) * lk2.astype(F32))) + lam_init)
    b0 = 3 * c_w // LANES
    ob = diff_attention(z3, _alibi_slopes(B_HEADS), lam.reshape(1), subln,
                        q_col=b0, k_col=b0 + B_HEADS, v_col=b0 + 2 * B_HEADS,
                        out_scale=1.0 - lam_init)

    slot = np.array(A_SLOT_HEAD)
    a0 = 3 * c_w + 3 * b_w
    oa = swa_attention(z3, _alibi_slopes(A_Q_HEADS)[slot], sinks.astype(F32)[slot],
                       q_tile=a0 // a_w, k_tile=(a0 + a_w) // kv_w,
                       v_tile=(a0 + a_w) // kv_w + 1)

    return merge_out_residual(oa.reshape(t, a_w), ob.reshape(t, b_w), oc.reshape(t, c_w), gates,
                              w_br_a[slot_cols].astype(BF16), w_br_b.astype(BF16),
                              w_br_c.astype(BF16), w_out.astype(BF16), x2, g_post, g_next)


COUNT_BLOCK = 256


def _moe_dispatch(idx, tm):
    t = idx.shape[0]
    n_rows = t * TOP_K
    n_tiles = (n_rows + N_EXPERTS * (tm - 1)) // tm
    e_flat = idx.reshape(n_rows)
    onehot = e_flat[:, None] == jnp.arange(N_EXPERTS)[None, :]
    blk = min(COUNT_BLOCK, n_rows)
    oh = onehot.astype(F32).reshape(n_rows // blk, blk, N_EXPERTS)
    lower = (jnp.arange(blk)[:, None] >= jnp.arange(blk)[None, :]).astype(F32)
    within = jnp.einsum("ij,bjk->bik", lower, oh)
    before = jnp.cumsum(within[:, -1, :], axis=0) - within[:, -1, :]
    running = (within + before[:, None, :]).reshape(n_rows, N_EXPERTS)
    rank = jnp.sum(jnp.where(onehot, running - 1.0, 0.0), axis=1).astype(jnp.int32)
    counts = jnp.sum(onehot.astype(jnp.int32), axis=0)
    padded = ((counts + tm - 1) // tm) * tm
    ends = jnp.cumsum(padded)
    starts = ends - padded
    pos = starts[e_flat] + rank
    row_token = jnp.zeros((n_tiles * tm,), jnp.int32).at[pos].set(
        jnp.arange(n_rows, dtype=jnp.int32) // TOP_K)
    tile_start = jnp.arange(n_tiles, dtype=jnp.int32) * tm
    tile_expert = jnp.minimum(jnp.sum(tile_start[:, None] >= ends[None, :], axis=1),
                              N_EXPERTS - 1).astype(jnp.int32)
    n_active = (ends[-1] // tm).astype(jnp.int32).reshape(1)
    return pos.reshape(t, TOP_K), row_token, tile_expert, n_active


def _moe_ffn(x2, g_pre, g_post, w_router, e_gate, e_up, e_down, *, tm=512):
    idx, wts, h = router_top2(x2, g_pre, w_router)
    t = x2.shape[0]
    tm = min(tm, t)
    pos, row_token, tile_expert, n_active = _moe_dispatch(idx, tm)
    xs = sc_gather_rows(h, row_token)
    u = moe_up(xs, tile_expert, n_active, e_gate, e_up, tm=tm)
    ys = moe_down(u, tile_expert, n_active, e_down, tm=tm)
    y2 = sc_gather_rows(ys, pos.T.reshape(TOP_K * t))
    return combine_residual(y2, wts, x2, g_post)


def kernel(x, mix_pre_norm, w_in, b_forget, b_gate, attn_sinks, lam_q1, lam_k1, lam_q2, lam_k2, diff_subln, w_br_a, w_br_b, w_br_c, w_out, mix_post_norm, ffn_pre_norm, ffn_post_norm, dense_w_gate, dense_w_up, dense_w_down, w_router, moe_w_gate, moe_w_up, moe_w_down):
    batch, seq, d = x.shape
    depth = w_in.shape[0]
    x2 = x.reshape(batch * seq, d)
    h = rmsnorm_rows(x2, mix_pre_norm[0])
    for l in range(depth):
        x2, h = _token_mixer(x2, h, batch, seq, l, w_in, b_forget[l], b_gate[l],
                             attn_sinks[l], lam_q1[l], lam_k1[l], lam_q2[l], lam_k2[l],
                             diff_subln[l], w_br_a[l], w_br_b[l], w_br_c[l], w_out[l],
                             mix_post_norm[l], ffn_pre_norm[l])
        i = l // 2
        if l % 2 == 0:
            u = swiglu_up(h, dense_w_gate[i].astype(BF16), dense_w_up[i].astype(BF16))
            g_next = mix_pre_norm[l + 1] if l + 1 < depth else jnp.ones((d,), F32)
            f = dense_w_down.shape[1]
            x2, h = proj_residual(u, dense_w_down[i].astype(BF16), x2, ffn_post_norm[l],
                                  g_next, tm=256, tk=f)
        else:
            x2 = _moe_ffn(x2, ffn_pre_norm[l], ffn_post_norm[l], w_router[i],
                          moe_w_gate[i], moe_w_up[i], moe_w_down[i])
            if l + 1 < depth:
                h = rmsnorm_rows(x2, mix_pre_norm[l + 1])
    return x2.reshape(batch, seq, d)
```

```python
import functools
import math

import jax
import jax.numpy as jnp
import numpy as np
from jax import lax
from jax.experimental import pallas as pl
from jax.experimental.pallas import tpu as pltpu
from jax.experimental.pallas import tpu_sc as plsc

F32 = jnp.float32
BF16 = jnp.bfloat16

HEAD_DIM = 64
LANES = 128
A_Q_HEADS = 12
A_KV_HEADS = 4
A_GROUP = A_Q_HEADS // A_KV_HEADS
WINDOW = 128
B_HEADS = 4
C_HEADS = 12
N_BRANCH = 3
N_EXPERTS = 8
TOP_K = 2
RMS_EPS = 1e-6
NEG_INF = -1e30
VMEM_LIMIT = 56 * 1024 * 1024

A_SLOT_HEAD = (0, 3, 1, 4, 2, 5, 6, 9, 7, 10, 8, 11)


def _params(*sem):
    return pltpu.CompilerParams(dimension_semantics=sem, vmem_limit_bytes=VMEM_LIMIT)


def _rms(x, g):
    var = jnp.mean(x * x, axis=-1, keepdims=True)
    return x * lax.rsqrt(var + RMS_EPS) * g


def _sigmoid(x):
    return 0.5 * jnp.tanh(0.5 * x) + 0.5


def _dot(a, b):
    return jnp.dot(a, b, preferred_element_type=F32)


def _dot_nt(a, b):
    return lax.dot_general(a, b, (((1,), (1,)), ((), ())), preferred_element_type=F32)


def _round_kernel(*refs):
    n = len(refs) // 2
    for src, dst in zip(refs[:n], refs[n:]):
        dst[...] = src[...].astype(dst.dtype)


ROUND_BLOCK_ELEMS = 2 * 1024 * 1024


def round_to_bf16(*ws):
    shape = ws[0].shape
    flat = [w.reshape(-1, shape[-1]) for w in ws]
    n, c = flat[0].shape
    rows = min(n, 1 << ((ROUND_BLOCK_ELEMS // c).bit_length() - 1))
    assert n % rows == 0 and all(w.shape == shape for w in ws)
    blk = pl.BlockSpec((rows, c), lambda i: (i, 0))
    outs = pl.pallas_call(
        _round_kernel,
        out_shape=tuple(jax.ShapeDtypeStruct((n, c), BF16) for _ in ws),
        grid=(n // rows,),
        in_specs=[blk] * len(ws),
        out_specs=tuple(blk for _ in ws),
        compiler_params=_params("parallel"),
        name="round_to_bf16",
    )(*flat)
    return [o.reshape(shape) for o in outs]


def _rmsnorm_kernel(x_ref, g_ref, o_ref):
    o_ref[...] = _rms(x_ref[...], g_ref[...]).astype(o_ref.dtype)


def rmsnorm_rows(x, g, *, tm=512):
    t, d = x.shape
    tm = min(tm, t)
    return pl.pallas_call(
        _rmsnorm_kernel,
        out_shape=jax.ShapeDtypeStruct((t, d), BF16),
        grid=(t // tm,),
        in_specs=[pl.BlockSpec((tm, d), lambda i: (i, 0)),
                  pl.BlockSpec((1, d), lambda i: (0, 0))],
        out_specs=pl.BlockSpec((tm, d), lambda i: (i, 0)),
        compiler_params=_params("parallel"),
        name="rmsnorm_rows",
    )(x, g.reshape(1, d))


def _mm_kernel(a_ref, bt_ref, o_ref):
    o_ref[...] = _dot_nt(a_ref[...], bt_ref[...]).astype(o_ref.dtype)


def _mm_gate_kernel(a_ref, bt_ref, bias_ref, o_ref):
    z = _dot_nt(a_ref[...], bt_ref[...]) + bias_ref[...]
    o_ref[...] = _sigmoid(z).astype(o_ref.dtype)


def matmul(a, bt, *, bias=None, out_dtype=BF16, tm=1024, tn=1024):
    m, k = a.shape
    n = bt.shape[0]
    tm, tn = min(tm, m), min(tn, n)
    assert m % tm == 0 and n % tn == 0
    in_specs = [pl.BlockSpec((tm, k), lambda i, j: (i, 0)),
                pl.BlockSpec((tn, k), lambda i, j: (j, 0))]
    args = [a, bt]
    kern = _mm_kernel
    if bias is not None:
        in_specs.append(pl.BlockSpec((1, tn), lambda i, j: (0, j)))
        args.append(bias.reshape(1, n).astype(F32))
        kern = _mm_gate_kernel
    return pl.pallas_call(
        kern,
        out_shape=jax.ShapeDtypeStruct((m, n), out_dtype),
        grid=(m // tm, n // tn),
        in_specs=in_specs,
        out_specs=pl.BlockSpec((tm, tn), lambda i, j: (i, j)),
        compiler_params=_params("parallel", "parallel"),
        name="matmul_gate" if bias is not None else "matmul",
    )(*args)


def _forget_kernel(h_ref, wt_ref, b_ref, c_ref, *, seq, blk):
    z = _dot_nt(wt_ref[...], h_ref[0]) + b_ref[...]
    log_f = jnp.minimum(z, 0.0) - jnp.log1p(jnp.exp(-jnp.abs(z)))
    r = lax.broadcasted_iota(jnp.int32, (blk, blk), 0)
    c = lax.broadcasted_iota(jnp.int32, (blk, blk), 1)
    upper = (r <= c).astype(F32)
    carry = jnp.zeros((LANES, 1), F32)
    for j in range(seq // blk):
        part = lax.dot_general(log_f[:, j * blk:(j + 1) * blk], upper,
                               (((1,), (0,)), ((), ())),
                               precision=lax.Precision.HIGHEST,
                               preferred_element_type=F32) + carry
        c_ref[0, :, j * blk:(j + 1) * blk] = part
        carry = part[:, blk - 1:blk]


def forget_cumsum(h3, w_ft, b_f):
    b, s, d = h3.shape
    nh = w_ft.shape[0]
    wt = jnp.zeros((LANES, d), BF16).at[:nh].set(w_ft.astype(BF16))
    bias = jnp.zeros((LANES, 1), F32).at[:nh, 0].set(b_f.astype(F32))
    blk = min(256, s)
    return pl.pallas_call(
        functools.partial(_forget_kernel, seq=s, blk=blk),
        out_shape=jax.ShapeDtypeStruct((b, LANES, s), F32),
        grid=(b,),
        in_specs=[pl.BlockSpec((1, s, d), lambda i: (i, 0, 0)),
                  pl.BlockSpec((LANES, d), lambda i: (0, 0)),
                  pl.BlockSpec((LANES, 1), lambda i: (0, 0))],
        out_specs=pl.BlockSpec((1, LANES, s), lambda i: (i, 0, 0)),
        compiler_params=_params("parallel"),
        name="forget_cumsum",
    )(h3, wt, bias)


def _half_mask(tq, half):
    lane = lax.broadcasted_iota(jnp.int32, (tq, LANES), 1)
    return (lane >= HEAD_DIM * half) & (lane < HEAD_DIM * (half + 1))


def _causal_logits(qh, k_ref, lanes, qi, tq, bias):
    n0 = qi * tq
    r = lax.broadcasted_iota(jnp.int32, (tq, tq), 0)
    c = lax.broadcasted_iota(jnp.int32, (tq, tq), 1)
    diag = _dot_nt(qh, k_ref[0, n0:n0 + tq, lanes]) + bias(n0, tq)
    parts = [jnp.where(r >= c, diag, NEG_INF)]
    if qi > 0:
        parts.append(_dot_nt(qh, k_ref[0, :n0, lanes]) + bias(0, n0))
    return parts


def _softmax_parts(parts):
    m = functools.reduce(jnp.maximum, [s.max(axis=-1, keepdims=True) for s in parts])
    exps = [jnp.exp(s - m) for s in parts]
    denom = functools.reduce(jnp.add, [e.sum(axis=-1, keepdims=True) for e in exps])
    return exps, denom


def _value_parts(v_ref, lanes, qi, tq):
    n0 = qi * tq
    return [v_ref[0, n0:n0 + tq, lanes]] + ([v_ref[0, :n0, lanes]] if qi > 0 else [])


def _weighted_values(weights, values):
    return functools.reduce(jnp.add, [_dot(w.astype(v.dtype), v) for w, v in zip(weights, values)])


def _dispatch_on_tile(branch, nq):
    qi = pl.program_id(2)
    for i in range(nq):
        pl.when(qi == i)(functools.partial(branch, qi=i))


def _fox_kernel(q_ref, k_ref, v_ref, c_ref, o_ref, *, tq, nq, npair):
    masks = [_half_mask(tq, h) for h in range(2)]

    def branch(qi):
        for p in range(npair):
            lanes = slice(p * LANES, (p + 1) * LANES)
            q = q_ref[0, :, lanes] * jnp.asarray(HEAD_DIM ** -0.5, q_ref.dtype)
            values = _value_parts(v_ref, lanes, qi, tq)
            outs = []
            for h in range(2):
                qh = jnp.where(masks[h], q, jnp.zeros_like(q))
                bias = lambda start, size: -c_ref[0, 2 * p + h, :, start:start + size]
                exps, denom = _softmax_parts(_causal_logits(qh, k_ref, lanes, qi, tq, bias))
                outs.append(_weighted_values(exps, values) / denom)
            o_ref[0, :, lanes] = jnp.where(masks[0], outs[0], outs[1]).astype(o_ref.dtype)

    _dispatch_on_tile(branch, nq)


def _diff_kernel(slope_ref, lam_ref, q_ref, k_ref, v_ref, g_ref, o_ref, *,
                 tq, nq, nhead, out_scale):
    lam = lam_ref[0]
    masks = [_half_mask(tq, h) for h in range(2)]

    def branch(qi):
        for j in range(nhead):
            lanes = slice(j * LANES, (j + 1) * LANES)
            slope = slope_ref[pl.program_id(1) * nhead + j]
            q = q_ref[0, :, lanes] * jnp.asarray(HEAD_DIM ** -0.5, q_ref.dtype)

            def bias(start, size):
                pos = start + lax.broadcasted_iota(jnp.int32, (1, size), 1)
                return slope * pos.astype(F32)

            maps = []
            for h in range(2):
                qh = jnp.where(masks[h], q, jnp.zeros_like(q))
                maps.append(_softmax_parts(_causal_logits(qh, k_ref, lanes, qi, tq, bias)))
            (e0, d0), (e1, d1) = maps
            w0, w1 = 1.0 / d0, lam / d1
            weights = [a * w0 - b * w1 for a, b in zip(e0, e1)]
            o = _weighted_values(weights, _value_parts(v_ref, lanes, qi, tq))
            o_ref[0, :, lanes] = (_rms(o, g_ref[...]) * out_scale).astype(o_ref.dtype)

    _dispatch_on_tile(branch, nq)


def fox_attention(z3, c4, *, q_col, k_col, v_col, tq=256, npair=3):
    b, s, _ = z3.shape
    tq = min(tq, s)
    w = npair * LANES
    groups = C_HEADS // 2 // npair
    assert q_col % npair == 0 and k_col % npair == 0 and v_col % npair == 0
    return pl.pallas_call(
        functools.partial(_fox_kernel, tq=tq, nq=s // tq, npair=npair),
        out_shape=jax.ShapeDtypeStruct((b, s, C_HEADS // 2 * LANES), BF16),
        grid=(b, groups, s // tq),
        in_specs=[pl.BlockSpec((1, tq, w), lambda bi, g, qi: (bi, qi, q_col // npair + g)),
                  pl.BlockSpec((1, s, w), lambda bi, g, qi: (bi, 0, k_col // npair + g)),
                  pl.BlockSpec((1, s, w), lambda bi, g, qi: (bi, 0, v_col // npair + g)),
                  pl.BlockSpec((1, 2 * npair, 1, s), lambda bi, g, qi: (bi, g, 0, 0))],
        out_specs=pl.BlockSpec((1, tq, w), lambda bi, g, qi: (bi, qi, g)),
        compiler_params=_params("parallel", "parallel", "arbitrary"),
        name="fox_attention",
    )(z3, z3, z3, c4)


def diff_attention(z3, slopes, lam, subln, *, q_col, k_col, v_col, out_scale, tq=256, nhead=2):
    b, s, _ = z3.shape
    tq = min(tq, s)
    w = nhead * LANES
    assert q_col % nhead == 0 and k_col % nhead == 0 and v_col % nhead == 0
    smem = pl.BlockSpec(memory_space=pltpu.SMEM)
    return pl.pallas_call(
        functools.partial(_diff_kernel, tq=tq, nq=s // tq, nhead=nhead, out_scale=out_scale),
        out_shape=jax.ShapeDtypeStruct((b, s, B_HEADS * LANES), BF16),
        grid=(b, B_HEADS // nhead, s // tq),
        in_specs=[smem, smem,
                  pl.BlockSpec((1, tq, w), lambda bi, g, qi: (bi, qi, q_col // nhead + g)),
                  pl.BlockSpec((1, s, w), lambda bi, g, qi: (bi, 0, k_col // nhead + g)),
                  pl.BlockSpec((1, s, w), lambda bi, g, qi: (bi, 0, v_col // nhead + g)),
                  pl.BlockSpec((1, LANES), lambda bi, g, qi: (0, 0))],
        out_specs=pl.BlockSpec((1, tq, w), lambda bi, g, qi: (bi, qi, g)),
        compiler_params=_params("parallel", "parallel", "arbitrary"),
        name="diff_attention",
    )(slopes, lam, z3, z3, z3, subln.reshape(1, LANES).astype(F32))


def _swa_kernel(slope_ref, sink_ref, q_ref, k_ref, v_ref, o_ref, *, blk):
    qi = pl.program_id(1)
    cur = pl.multiple_of(qi * blk, blk)
    prev = pl.multiple_of(jnp.maximum(qi - 1, 0) * blk, blk)
    r = lax.broadcasted_iota(jnp.int32, (blk, 2 * blk), 0)
    c = lax.broadcasted_iota(jnp.int32, (blk, 2 * blk), 1)
    dist = r + blk - c
    keep = (dist >= 0) & (dist < WINDOW) & ((qi > 0) | (c >= blk))
    dist_f = dist.astype(F32)
    scale = jnp.asarray(HEAD_DIM ** -0.5, q_ref.dtype)
    masks = [_half_mask(blk, h) for h in range(2)]
    for tile in range(A_Q_HEADS // 2):
        q = q_ref[0, :, tile * LANES:(tile + 1) * LANES] * scale
        halves = []
        for half in range(2):
            slot = 2 * tile + half
            kv_tile = (A_SLOT_HEAD[slot] // A_GROUP) // 2
            lanes = slice(kv_tile * LANES, (kv_tile + 1) * LANES)
            k = jnp.concatenate([k_ref[0, pl.ds(prev, blk), lanes],
                                 k_ref[0, pl.ds(cur, blk), lanes]], axis=0)
            v = jnp.concatenate([v_ref[0, pl.ds(prev, blk), lanes],
                                 v_ref[0, pl.ds(cur, blk), lanes]], axis=0)
            qh = jnp.where(masks[half], q, jnp.zeros_like(q))
            s = _dot_nt(qh, k) - slope_ref[slot] * dist_f
            s = jnp.where(keep, s, NEG_INF)
            sink = sink_ref[slot]
            m = jnp.maximum(jnp.max(s, axis=-1, keepdims=True), sink)
            p = jnp.exp(s - m)
            denom = jnp.sum(p, axis=-1, keepdims=True) + jnp.exp(sink - m)
            halves.append(_dot(p.astype(v.dtype), v) / denom)
        o_ref[0, :, tile * LANES:(tile + 1) * LANES] = jnp.where(
            masks[0], halves[0], halves[1]).astype(o_ref.dtype)


def swa_attention(z3, slopes, sinks, *, q_tile, k_tile, v_tile):
    b, s, _ = z3.shape
    blk = WINDOW
    qw = A_Q_HEADS * HEAD_DIM
    kw = A_KV_HEADS * HEAD_DIM
    smem = pl.BlockSpec(memory_space=pltpu.SMEM)
    return pl.pallas_call(
        functools.partial(_swa_kernel, blk=blk),
        out_shape=jax.ShapeDtypeStruct((b, s, qw), BF16),
        grid=(b, s // blk),
        in_specs=[smem, smem,
                  pl.BlockSpec((1, blk, qw), lambda bi, qi: (bi, qi, q_tile)),
                  pl.BlockSpec((1, s, kw), lambda bi, qi: (bi, 0, k_tile)),
                  pl.BlockSpec((1, s, kw), lambda bi, qi: (bi, 0, v_tile))],
        out_specs=pl.BlockSpec((1, blk, qw), lambda bi, qi: (bi, qi, 0)),
        compiler_params=_params("parallel", "parallel"),
        name="swa_attention",
    )(slopes, sinks, z3, z3, z3)


def _merge_out_kernel(oa_ref, ob_ref, oc_ref, ga_ref, gb_ref, gc_ref, wa_ref, wb_ref, wc_ref,
                      wo_ref, x_ref, gp_ref, gn_ref, *rest, route):
    merged = ga_ref[...].astype(F32) * _dot(oa_ref[...], wa_ref[...])
    merged += gb_ref[...].astype(F32) * _dot(ob_ref[...], wb_ref[...])
    merged += gc_ref[...].astype(F32) * _dot(oc_ref[...], wc_ref[...])
    m = _dot(merged.astype(wo_ref.dtype), wo_ref[...])
    x_new = x_ref[...] + _rms(m, gp_ref[...])
    h = _rms(x_new, gn_ref[...])
    if route:
        wr_ref, xo_ref, idx_ref, wt_ref, hw_ref = rest
        _route(h, wr_ref, idx_ref, wt_ref, hw_ref)
    else:
        xo_ref, ho_ref = rest
        ho_ref[...] = h.astype(ho_ref.dtype)
    xo_ref[...] = x_new


def merge_out_residual(oa, ob, oc, gates, wa, wb, wc, w_out, x, g_post, g_next,
                       w_router=None, *, tm=256):
    t, d = x.shape
    tm = min(tm, t)
    row = lambda w: pl.BlockSpec((tm, w), lambda i: (i, 0))
    gate = lambda br: pl.BlockSpec((tm, d), lambda i: (i, br))
    whole = lambda w: pl.BlockSpec(w.shape, lambda i: (0, 0), pipeline_mode=pl.Buffered(1))
    vec = pl.BlockSpec((1, d), lambda i: (0, 0))
    in_specs = [row(oa.shape[1]), row(ob.shape[1]), row(oc.shape[1]),
                gate(0), gate(1), gate(2),
                whole(wa), whole(wb), whole(wc), whole(w_out), row(d), vec, vec]
    args = [oa, ob, oc, gates, gates, gates, wa, wb, wc, w_out, x,
            g_post.reshape(1, d), g_next.reshape(1, d)]
    route = w_router is not None
    if route:
        wr = jnp.zeros((d, LANES), F32).at[:, :N_EXPERTS].set(w_router.astype(F32))
        in_specs.append(whole(wr))
        args.append(wr)
        out_shape = (jax.ShapeDtypeStruct((t, d), F32),
                     jax.ShapeDtypeStruct((t, LANES), jnp.int32),
                     jax.ShapeDtypeStruct((t, LANES), F32),
                     jax.ShapeDtypeStruct((t, d // 2), jnp.uint32))
        out_specs = (row(d), row(LANES), row(LANES), row(d // 2))
    else:
        out_shape = (jax.ShapeDtypeStruct((t, d), F32), jax.ShapeDtypeStruct((t, d), BF16))
        out_specs = (row(d), row(d))
    outs = pl.pallas_call(
        functools.partial(_merge_out_kernel, route=route),
        out_shape=out_shape,
        grid=(t // tm,),
        in_specs=in_specs,
        out_specs=out_specs,
        compiler_params=_params("parallel"),
        name="merge_out_router" if route else "merge_out_residual",
    )(*args)
    if route:
        x_new, idx, wt, hw = outs
        return x_new, idx[:, :TOP_K], wt[:, :TOP_K], hw
    return outs


def _proj_res_kernel(a_ref, w_ref, x_ref, gp_ref, gn_ref, xo_ref, ho_ref, acc_ref, *, nk):
    kk = pl.program_id(1)
    part = _dot(a_ref[...], w_ref[...])

    @pl.when(kk == 0)
    def _():
        acc_ref[...] = part

    @pl.when(kk > 0)
    def _():
        acc_ref[...] += part

    @pl.when(kk == nk - 1)
    def _():
        x_new = x_ref[...] + _rms(acc_ref[...], gp_ref[...])
        xo_ref[...] = x_new
        ho_ref[...] = _rms(x_new, gn_ref[...]).astype(ho_ref.dtype)


def proj_residual(a, w, x, g_post, g_next, *, tm, tk):
    t, k = a.shape
    d = w.shape[1]
    tm, tk = min(tm, t), min(tk, k)
    nk = k // tk
    vec = pl.BlockSpec((1, d), lambda i, kk: (0, 0))
    w_mode = dict(pipeline_mode=pl.Buffered(1)) if nk == 1 else {}
    return pl.pallas_call(
        functools.partial(_proj_res_kernel, nk=nk),
        out_shape=(jax.ShapeDtypeStruct((t, d), F32), jax.ShapeDtypeStruct((t, d), BF16)),
        grid=(t // tm, nk),
        in_specs=[pl.BlockSpec((tm, tk), lambda i, kk: (i, kk)),
                  pl.BlockSpec((tk, d), lambda i, kk: (kk, 0), **w_mode),
                  pl.BlockSpec((tm, d), lambda i, kk: (i, 0)),
                  vec, vec],
        out_specs=(pl.BlockSpec((tm, d), lambda i, kk: (i, 0)),
                   pl.BlockSpec((tm, d), lambda i, kk: (i, 0))),
        scratch_shapes=[pltpu.VMEM((tm, d), F32)],
        compiler_params=_params("parallel", "arbitrary"),
        name="proj_residual",
    )(a, w, x, g_post.reshape(1, d), g_next.reshape(1, d))


def _swiglu_up_kernel(h_ref, wg_ref, wu_ref, o_ref):
    h = h_ref[...]
    g = _dot(h, wg_ref[...])
    u = _dot(h, wu_ref[...])
    o_ref[...] = (g * _sigmoid(g) * u).astype(o_ref.dtype)


def swiglu_up(h, wg, wu, *, tm=1024, tn=512):
    t, d = h.shape
    f = wg.shape[1]
    tm, tn = min(tm, t), min(tn, f)
    return pl.pallas_call(
        _swiglu_up_kernel,
        out_shape=jax.ShapeDtypeStruct((t, f), BF16),
        grid=(t // tm, f // tn),
        in_specs=[pl.BlockSpec((tm, d), lambda i, j: (i, 0)),
                  pl.BlockSpec((d, tn), lambda i, j: (0, j)),
                  pl.BlockSpec((d, tn), lambda i, j: (0, j))],
        out_specs=pl.BlockSpec((tm, tn), lambda i, j: (i, j)),
        compiler_params=_params("parallel", "parallel"),
        name="swiglu_up",
    )(h, wg, wu)


def _route(h, w_ref, idx_ref, wt_ref, h_ref):
    bits = pltpu.bitcast(h.astype(BF16).astype(F32), jnp.uint32)
    half = h.shape[1] // 2
    h_ref[...] = (bits[:, :half] >> 16) | bits[:, half:]
    logits = jnp.dot(h, w_ref[...], precision=lax.Precision.HIGHEST,
                     preferred_element_type=F32)
    lane = lax.broadcasted_iota(jnp.int32, logits.shape, 1)
    lane_f = lane.astype(F32)
    lg = jnp.where(lane < N_EXPERTS, logits, -jnp.inf)
    m1 = jnp.max(lg, axis=-1, keepdims=True)
    i1 = jnp.min(jnp.where(lg == m1, lane_f, float(LANES)), axis=-1, keepdims=True)
    lg2 = jnp.where(lane_f == i1, -jnp.inf, lg)
    m2 = jnp.max(lg2, axis=-1, keepdims=True)
    i2 = jnp.min(jnp.where(lg2 == m2, lane_f, float(LANES)), axis=-1, keepdims=True)
    e2 = jnp.exp(m2 - m1)
    w1 = 1.0 / (1.0 + e2)
    w2 = e2 / (1.0 + e2)
    idx_ref[...] = jnp.where(lane == 0, i1, jnp.where(lane == 1, i2, 0.0)).astype(jnp.int32)
    wt_ref[...] = jnp.where(lane == 0, w1, jnp.where(lane == 1, w2, 0.0))


SC_INDEX_WINDOW = 128
SC_COPY_BYTES = 256 * 1024


def sc_gather_rows(src, idx):
    n = idx.shape[0]
    d = src.shape[1]
    batch = min(SC_INDEX_WINDOW, SC_COPY_BYTES // (d * src.dtype.itemsize))
    assert n % SC_INDEX_WINDOW == 0 and SC_INDEX_WINDOW % batch == 0
    mesh = plsc.VectorSubcoreMesh(core_axis_name="c", subcore_axis_name="s")
    dst = jnp.arange(n, dtype=jnp.int32).reshape(1, n)

    @pl.kernel(out_type=jax.ShapeDtypeStruct((n, d), src.dtype), mesh=mesh,
               scratch_types=[pltpu.VMEM((batch, d), src.dtype)])
    def gather(x_hbm, i_hbm, d_hbm, o_hbm, buf):
        def body(i_vmem, d_vmem):
            for s in range(SC_INDEX_WINDOW // batch):
                rows = pl.ds(s * batch, batch)
                pltpu.sync_copy(x_hbm.at[i_vmem.at[0, rows]], buf)
                pltpu.sync_copy(buf, o_hbm.at[d_vmem.at[0, rows]])

        window = pl.BlockSpec((1, SC_INDEX_WINDOW), lambda i: (0, i))
        pltpu.emit_pipeline(
            body, grid=(n // SC_INDEX_WINDOW,),
            in_specs=[window, window], out_specs=[],
            core_axis_name=("c", "s"),
            dimension_semantics=(pltpu.PARALLEL,),
        )(i_hbm, d_hbm)

    return gather(src, idx.reshape(1, n), dst)


def _moe_up_kernel(te_ref, na_ref, x_ref, wg_ref, wu_ref, o_ref):
    active = pl.program_id(1) < na_ref[0]

    @pl.when(active)
    def _():
        words = x_ref[...]
        half = words.shape[1]
        lo = pltpu.bitcast(words << 16, F32).astype(BF16)
        hi = pltpu.bitcast(words & jnp.uint32(0xFFFF0000), F32).astype(BF16)
        g = _dot(lo, wg_ref[0, :half]) + _dot(hi, wg_ref[0, half:])
        u = _dot(lo, wu_ref[0, :half]) + _dot(hi, wu_ref[0, half:])
        o_ref[...] = (g * _sigmoid(g) * u).astype(o_ref.dtype)

    @pl.when(jnp.logical_not(active))
    def _():
        o_ref[...] = jnp.zeros_like(o_ref)


def _moe_down_kernel(te_ref, na_ref, u_ref, wd_ref, o_ref):
    active = pl.program_id(1) < na_ref[0]

    @pl.when(active)
    def _():
        o_ref[...] = _dot(u_ref[...], wd_ref[0])

    @pl.when(jnp.logical_not(active))
    def _():
        o_ref[...] = jnp.zeros_like(o_ref)


def _active_tile(i, na_ref):
    return jnp.minimum(i, na_ref[0] - 1)


def moe_up(xs, tile_expert, n_active, wg, wu, *, tm, tn=1024):
    p = xs.shape[0]
    d, f = wg.shape[1:]
    tn = min(tn, f)
    assert f % tn == 0
    grid_spec = pltpu.PrefetchScalarGridSpec(
        num_scalar_prefetch=2,
        grid=(f // tn, p // tm),
        in_specs=[pl.BlockSpec((tm, d // 2), lambda j, i, te, na: (_active_tile(i, na), 0)),
                  pl.BlockSpec((1, d, tn), lambda j, i, te, na: (te[_active_tile(i, na)], 0, j)),
                  pl.BlockSpec((1, d, tn), lambda j, i, te, na: (te[_active_tile(i, na)], 0, j))],
        out_specs=pl.BlockSpec((tm, tn), lambda j, i, te, na: (i, j)),
    )
    return pl.pallas_call(
        _moe_up_kernel,
        out_shape=jax.ShapeDtypeStruct((p, f), BF16),
        grid_spec=grid_spec,
        compiler_params=_params("arbitrary", "arbitrary"),
        name="moe_up",
    )(tile_expert, n_active, xs, wg, wu)


def moe_down(u, tile_expert, n_active, wd, *, tm, tn=512):
    p, f = u.shape
    d = wd.shape[2]
    tn = min(tn, d)
    grid_spec = pltpu.PrefetchScalarGridSpec(
        num_scalar_prefetch=2,
        grid=(d // tn, p // tm),
        in_specs=[pl.BlockSpec((tm, f), lambda j, i, te, na: (_active_tile(i, na), 0)),
                  pl.BlockSpec((1, f, tn), lambda j, i, te, na: (te[_active_tile(i, na)], 0, j))],
        out_specs=pl.BlockSpec((tm, tn), lambda j, i, te, na: (i, j)),
    )
    return pl.pallas_call(
        _moe_down_kernel,
        out_shape=jax.ShapeDtypeStruct((p, d), F32),
        grid_spec=grid_spec,
        compiler_params=_params("arbitrary", "arbitrary"),
        name="moe_down",
    )(tile_expert, n_active, u, wd)


def _combine_kernel(ya_ref, yb_ref, w_ref, x_ref, g_ref, o_ref):
    w = w_ref[...]
    y = w[:, 0:1] * ya_ref[...] + w[:, 1:2] * yb_ref[...]
    o_ref[...] = x_ref[...] + _rms(y, g_ref[...])


def combine_residual(y2, wts, x, g, *, tm=512):
    t, d = x.shape
    tm = min(tm, t)
    nt = t // tm
    blk = pl.BlockSpec((tm, d), lambda i: (i, 0))
    return pl.pallas_call(
        _combine_kernel,
        out_shape=jax.ShapeDtypeStruct((t, d), F32),
        grid=(nt,),
        in_specs=[blk, pl.BlockSpec((tm, d), lambda i: (i + nt, 0)),
                  pl.BlockSpec((tm, TOP_K), lambda i: (i, 0)), blk,
                  pl.BlockSpec((1, d), lambda i: (0, 0))],
        out_specs=blk,
        compiler_params=_params("parallel"),
        name="combine_residual",
    )(y2, y2, wts, x, g.reshape(1, d))


def _alibi_slopes(n):
    return jnp.exp2(-8.0 * (jnp.arange(n, dtype=F32) + 1.0) / n)


def _token_mixer(x2, h, batch, seq, layer_idx, w_in, b_forget, b_gate, sinks,
                 lq1, lk1, lq2, lk2, subln, w_br_a, w_br_b, w_br_c, w_out,
                 g_post, g_next, w_router=None):
    t, d = x2.shape
    a_w = A_Q_HEADS * HEAD_DIM
    kv_w = A_KV_HEADS * HEAD_DIM
    b_w = B_HEADS * 2 * HEAD_DIM
    c_w = C_HEADS * HEAD_DIM
    n_attn = a_w + 2 * kv_w + 3 * b_w + 3 * c_w

    slot_cols = np.concatenate([np.arange(HEAD_DIM) + HEAD_DIM * hd for hd in A_SLOT_HEAD])
    a_end = a_w + 2 * kv_w
    b_end = a_end + 3 * b_w
    w_in_t = jnp.transpose(w_in, (2, 0, 1))
    rows = lambda lo, hi: w_in_t[lo:hi, layer_idx, :]
    w_attn_t = jnp.concatenate([rows(b_end, n_attn), rows(a_end, b_end),
                                rows(0, a_w)[slot_cols], rows(a_w, a_end)],
                               axis=0).astype(BF16)
    w_f_t = rows(n_attn, n_attn + C_HEADS)
    w_g_t = rows(n_attn + C_HEADS, w_in.shape[2]).astype(BF16)

    z3 = matmul(h, w_attn_t).reshape(batch, seq, n_attn)
    gates = matmul(h, w_g_t, bias=b_gate)
    c = forget_cumsum(h.reshape(batch, seq, d), w_f_t, b_forget)
    c4 = c[:, :C_HEADS].reshape(batch, C_HEADS, 1, seq)

    oc = fox_attention(z3, c4, q_col=0, k_col=C_HEADS // 2, v_col=C_HEADS)

    lam_init = 0.8 - 0.6 * math.exp(-0.3 * layer_idx)
    lam = (jnp.exp(jnp.sum(lq1.astype(F32) * lk1.astype(F32)))
           - jnp.exp(jnp.sum(lq2.astype(F32) * lk2.astype(F32))) + lam_init)
    b0 = 3 * c_w // LANES
    ob = diff_attention(z3, _alibi_slopes(B_HEADS), lam.reshape(1), subln,
                        q_col=b0, k_col=b0 + B_HEADS, v_col=b0 + 2 * B_HEADS,
                        out_scale=1.0 - lam_init)

    slot = np.array(A_SLOT_HEAD)
    a0 = 3 * c_w + 3 * b_w
    oa = swa_attention(z3, _alibi_slopes(A_Q_HEADS)[slot], sinks.astype(F32)[slot],
                       q_tile=a0 // a_w, k_tile=(a0 + a_w) // kv_w,
                       v_tile=(a0 + a_w) // kv_w + 1)

    return merge_out_residual(oa.reshape(t, a_w), ob.reshape(t, b_w), oc.reshape(t, c_w), gates,
                              w_br_a[slot_cols].astype(BF16), w_br_b.astype(BF16),
                              w_br_c.astype(BF16), w_out.astype(BF16), x2, g_post, g_next,
                              w_router)


COUNT_BLOCK = 256


def _moe_dispatch(idx, tm):
    t = idx.shape[0]
    n_rows = t * TOP_K
    n_tiles = (n_rows + N_EXPERTS * (tm - 1)) // tm
    e_flat = idx.reshape(n_rows)
    onehot = e_flat[:, None] == jnp.arange(N_EXPERTS)[None, :]
    blk = min(COUNT_BLOCK, n_rows)
    oh = onehot.astype(F32).reshape(n_rows // blk, blk, N_EXPERTS)
    lower = (jnp.arange(blk)[:, None] >= jnp.arange(blk)[None, :]).astype(F32)
    within = jnp.einsum("ij,bjk->bik", lower, oh)
    before = jnp.cumsum(within[:, -1, :], axis=0) - within[:, -1, :]
    running = (within + before[:, None, :]).reshape(n_rows, N_EXPERTS)
    rank = jnp.sum(jnp.where(onehot, running - 1.0, 0.0), axis=1).astype(jnp.int32)
    counts = jnp.sum(onehot.astype(jnp.int32), axis=0)
    padded = ((counts + tm - 1) // tm) * tm
    ends = jnp.cumsum(padded)
    starts = ends - padded
    pos = starts[e_flat] + rank
    row_token = jnp.zeros((n_tiles * tm,), jnp.int32).at[pos].set(
        jnp.arange(n_rows, dtype=jnp.int32) // TOP_K)
    tile_start = jnp.arange(n_tiles, dtype=jnp.int32) * tm
    tile_expert = jnp.minimum(jnp.sum(tile_start[:, None] >= ends[None, :], axis=1),
                              N_EXPERTS - 1).astype(jnp.int32)
    n_active = (ends[-1] // tm).astype(jnp.int32).reshape(1)
    return pos.reshape(t, TOP_K), row_token, tile_expert, n_active


def _moe_ffn(x2, idx, wts, h, g_post, e_gate, e_up, e_down, *, tm=512):
    t = x2.shape[0]
    tm = min(tm, t)
    pos, row_token, tile_expert, n_active = _moe_dispatch(idx, tm)
    xs = sc_gather_rows(h, row_token)
    u = moe_up(xs, tile_expert, n_active, e_gate, e_up, tm=tm)
    ys = moe_down(u, tile_expert, n_active, e_down, tm=tm)
    y2 = sc_gather_rows(ys, pos.T.reshape(TOP_K * t))
    return combine_residual(y2, wts, x2, g_post)


def kernel(x, mix_pre_norm, w_in, b_forget, b_gate, attn_sinks, lam_q1, lam_k1, lam_q2, lam_k2, diff_subln, w_br_a, w_br_b, w_br_c, w_out, mix_post_norm, ffn_pre_norm, ffn_post_norm, dense_w_gate, dense_w_up, dense_w_down, w_router, moe_w_gate, moe_w_up, moe_w_down):
    batch, seq, d = x.shape
    depth = w_in.shape[0]
    x2 = x.reshape(batch * seq, d)
    h = rmsnorm_rows(x2, mix_pre_norm[0])
    for l in range(depth):
        i = l // 2
        x2, *mixed = _token_mixer(x2, h, batch, seq, l, w_in, b_forget[l], b_gate[l],
                                  attn_sinks[l], lam_q1[l], lam_k1[l], lam_q2[l], lam_k2[l],
                                  diff_subln[l], w_br_a[l], w_br_b[l], w_br_c[l], w_out[l],
                                  mix_post_norm[l], ffn_pre_norm[l],
                                  w_router[i] if l % 2 else None)
        if l % 2 == 0:
            (h,) = mixed
            u = swiglu_up(h, dense_w_gate[i].astype(BF16), dense_w_up[i].astype(BF16))
            g_next = mix_pre_norm[l + 1] if l + 1 < depth else jnp.ones((d,), F32)
            f = dense_w_down.shape[1]
            x2, h = proj_residual(u, dense_w_down[i].astype(BF16), x2, ffn_post_norm[l],
                                  g_next, tm=256, tk=f)
        else:
            x2 = _moe_ffn(x2, *mixed, ffn_post_norm[l],
                          moe_w_gate[i], moe_w_up[i], moe_w_down[i])
            if l + 1 < depth:
                h = rmsnorm_rows(x2, mix_pre_norm[l + 1])
    return x2.reshape(batch, seq, d)
```

```python
import functools
import math

import jax
import jax.numpy as jnp
import numpy as np
from jax import lax
from jax.experimental import pallas as pl
from jax.experimental.pallas import tpu as pltpu
from jax.experimental.pallas import tpu_sc as plsc

F32 = jnp.float32
BF16 = jnp.bfloat16

HEAD_DIM = 64
LANES = 128
A_Q_HEADS = 12
A_KV_HEADS = 4
A_GROUP = A_Q_HEADS // A_KV_HEADS
WINDOW = 128
B_HEADS = 4
C_HEADS = 12
N_BRANCH = 3
N_EXPERTS = 8
TOP_K = 2
RMS_EPS = 1e-6
NEG_INF = -1e30
VMEM_LIMIT = 56 * 1024 * 1024

A_SLOT_HEAD = (0, 3, 1, 4, 2, 5, 6, 9, 7, 10, 8, 11)


def _params(*sem):
    return pltpu.CompilerParams(dimension_semantics=sem, vmem_limit_bytes=VMEM_LIMIT)


def _rms(x, g):
    var = jnp.mean(x * x, axis=-1, keepdims=True)
    return x * lax.rsqrt(var + RMS_EPS) * g


def _sigmoid(x):
    return 0.5 * jnp.tanh(0.5 * x) + 0.5


def _dot(a, b):
    return jnp.dot(a, b, preferred_element_type=F32)


def _dot_nt(a, b):
    return lax.dot_general(a, b, (((1,), (1,)), ((), ())), preferred_element_type=F32)


def _round_kernel(*refs):
    n = len(refs) // 2
    for src, dst in zip(refs[:n], refs[n:]):
        dst[...] = src[...].astype(dst.dtype)


ROUND_BLOCK_ELEMS = 2 * 1024 * 1024


def round_to_bf16(*ws):
    shape = ws[0].shape
    flat = [w.reshape(-1, shape[-1]) for w in ws]
    n, c = flat[0].shape
    rows = min(n, 1 << ((ROUND_BLOCK_ELEMS // c).bit_length() - 1))
    assert n % rows == 0 and all(w.shape == shape for w in ws)
    blk = pl.BlockSpec((rows, c), lambda i: (i, 0))
    outs = pl.pallas_call(
        _round_kernel,
        out_shape=tuple(jax.ShapeDtypeStruct((n, c), BF16) for _ in ws),
        grid=(n // rows,),
        in_specs=[blk] * len(ws),
        out_specs=tuple(blk for _ in ws),
        compiler_params=_params("parallel"),
        name="round_to_bf16",
    )(*flat)
    return [o.reshape(shape) for o in outs]


def _rmsnorm_kernel(x_ref, g_ref, o_ref):
    o_ref[...] = _rms(x_ref[...], g_ref[...]).astype(o_ref.dtype)


def rmsnorm_rows(x, g, *, tm=512):
    t, d = x.shape
    tm = min(tm, t)
    return pl.pallas_call(
        _rmsnorm_kernel,
        out_shape=jax.ShapeDtypeStruct((t, d), BF16),
        grid=(t // tm,),
        in_specs=[pl.BlockSpec((tm, d), lambda i: (i, 0)),
                  pl.BlockSpec((1, d), lambda i: (0, 0))],
        out_specs=pl.BlockSpec((tm, d), lambda i: (i, 0)),
        compiler_params=_params("parallel"),
        name="rmsnorm_rows",
    )(x, g.reshape(1, d))


def _mm_kernel(a_ref, bt_ref, o_ref):
    o_ref[...] = _dot_nt(a_ref[...], bt_ref[...]).astype(o_ref.dtype)


def _mm_gate_kernel(a_ref, bt_ref, bias_ref, o_ref):
    z = _dot_nt(a_ref[...], bt_ref[...]) + bias_ref[...]
    o_ref[...] = _sigmoid(z).astype(o_ref.dtype)


def matmul(a, bt, *, bias=None, out_dtype=BF16, tm=1024, tn=1024):
    m, k = a.shape
    n = bt.shape[0]
    tm, tn = min(tm, m), min(tn, n)
    assert m % tm == 0 and n % tn == 0
    in_specs = [pl.BlockSpec((tm, k), lambda i, j: (i, 0)),
                pl.BlockSpec((tn, k), lambda i, j: (j, 0))]
    args = [a, bt]
    kern = _mm_kernel
    if bias is not None:
        in_specs.append(pl.BlockSpec((1, tn), lambda i, j: (0, j)))
        args.append(bias.reshape(1, n).astype(F32))
        kern = _mm_gate_kernel
    return pl.pallas_call(
        kern,
        out_shape=jax.ShapeDtypeStruct((m, n), out_dtype),
        grid=(m // tm, n // tn),
        in_specs=in_specs,
        out_specs=pl.BlockSpec((tm, tn), lambda i, j: (i, j)),
        compiler_params=_params("parallel", "parallel"),
        name="matmul_gate" if bias is not None else "matmul",
    )(*args)


def _forget_kernel(h_ref, wt_ref, b_ref, c_ref, *, seq, blk):
    z = _dot_nt(wt_ref[...], h_ref[0]) + b_ref[...]
    log_f = jnp.minimum(z, 0.0) - jnp.log1p(jnp.exp(-jnp.abs(z)))
    r = lax.broadcasted_iota(jnp.int32, (blk, blk), 0)
    c = lax.broadcasted_iota(jnp.int32, (blk, blk), 1)
    upper = (r <= c).astype(F32)
    carry = jnp.zeros((LANES, 1), F32)
    for j in range(seq // blk):
        part = lax.dot_general(log_f[:, j * blk:(j + 1) * blk], upper,
                               (((1,), (0,)), ((), ())),
                               precision=lax.Precision.HIGHEST,
                               preferred_element_type=F32) + carry
        c_ref[0, :, j * blk:(j + 1) * blk] = part
        carry = part[:, blk - 1:blk]


def forget_cumsum(h3, w_ft, b_f):
    b, s, d = h3.shape
    nh = w_ft.shape[0]
    wt = jnp.zeros((LANES, d), BF16).at[:nh].set(w_ft.astype(BF16))
    bias = jnp.zeros((LANES, 1), F32).at[:nh, 0].set(b_f.astype(F32))
    blk = min(256, s)
    return pl.pallas_call(
        functools.partial(_forget_kernel, seq=s, blk=blk),
        out_shape=jax.ShapeDtypeStruct((b, LANES, s), F32),
        grid=(b,),
        in_specs=[pl.BlockSpec((1, s, d), lambda i: (i, 0, 0)),
                  pl.BlockSpec((LANES, d), lambda i: (0, 0)),
                  pl.BlockSpec((LANES, 1), lambda i: (0, 0))],
        out_specs=pl.BlockSpec((1, LANES, s), lambda i: (i, 0, 0)),
        compiler_params=_params("parallel"),
        name="forget_cumsum",
    )(h3, wt, bias)


def _half_mask(tq, half):
    lane = lax.broadcasted_iota(jnp.int32, (tq, LANES), 1)
    return (lane >= HEAD_DIM * half) & (lane < HEAD_DIM * (half + 1))


def _causal_logits(qh, k_ref, lanes, qi, tq, bias):
    n0 = qi * tq
    r = lax.broadcasted_iota(jnp.int32, (tq, tq), 0)
    c = lax.broadcasted_iota(jnp.int32, (tq, tq), 1)
    diag = _dot_nt(qh, k_ref[0, n0:n0 + tq, lanes]) + bias(n0, tq)
    parts = [jnp.where(r >= c, diag, NEG_INF)]
    if qi > 0:
        parts.append(_dot_nt(qh, k_ref[0, :n0, lanes]) + bias(0, n0))
    return parts


def _softmax_parts(parts):
    m = functools.reduce(jnp.maximum, [s.max(axis=-1, keepdims=True) for s in parts])
    exps = [jnp.exp(s - m) for s in parts]
    denom = functools.reduce(jnp.add, [e.sum(axis=-1, keepdims=True) for e in exps])
    return exps, denom


def _value_parts(v_ref, lanes, qi, tq):
    n0 = qi * tq
    return [v_ref[0, n0:n0 + tq, lanes]] + ([v_ref[0, :n0, lanes]] if qi > 0 else [])


def _weighted_values(weights, values):
    return functools.reduce(jnp.add, [_dot(w.astype(v.dtype), v) for w, v in zip(weights, values)])


def _dispatch_on_tile(branch, nq):
    qi = pl.program_id(2)
    for i in range(nq):
        pl.when(qi == i)(functools.partial(branch, qi=i))


def _fox_kernel(q_ref, k_ref, v_ref, c_ref, o_ref, *, tq, nq, npair):
    masks = [_half_mask(tq, h) for h in range(2)]

    def branch(qi):
        for p in range(npair):
            lanes = slice(p * LANES, (p + 1) * LANES)
            q = q_ref[0, :, lanes] * jnp.asarray(HEAD_DIM ** -0.5, q_ref.dtype)
            values = _value_parts(v_ref, lanes, qi, tq)
            outs = []
            for h in range(2):
                qh = jnp.where(masks[h], q, jnp.zeros_like(q))
                bias = lambda start, size: -c_ref[0, 2 * p + h, :, start:start + size]
                exps, denom = _softmax_parts(_causal_logits(qh, k_ref, lanes, qi, tq, bias))
                outs.append(_weighted_values(exps, values) / denom)
            o_ref[0, :, lanes] = jnp.where(masks[0], outs[0], outs[1]).astype(o_ref.dtype)

    _dispatch_on_tile(branch, nq)


def _diff_kernel(slope_ref, lam_ref, q_ref, k_ref, v_ref, g_ref, o_ref, *,
                 tq, nq, nhead, out_scale):
    lam = lam_ref[0]
    masks = [_half_mask(tq, h) for h in range(2)]

    def branch(qi):
        for j in range(nhead):
            lanes = slice(j * LANES, (j + 1) * LANES)
            slope = slope_ref[pl.program_id(1) * nhead + j]
            q = q_ref[0, :, lanes] * jnp.asarray(HEAD_DIM ** -0.5, q_ref.dtype)

            def bias(start, size):
                pos = start + lax.broadcasted_iota(jnp.int32, (1, size), 1)
                return slope * pos.astype(F32)

            maps = []
            for h in range(2):
                qh = jnp.where(masks[h], q, jnp.zeros_like(q))
                maps.append(_softmax_parts(_causal_logits(qh, k_ref, lanes, qi, tq, bias)))
            (e0, d0), (e1, d1) = maps
            w0, w1 = 1.0 / d0, lam / d1
            weights = [a * w0 - b * w1 for a, b in zip(e0, e1)]
            o = _weighted_values(weights, _value_parts(v_ref, lanes, qi, tq))
            o_ref[0, :, lanes] = (_rms(o, g_ref[...]) * out_scale).astype(o_ref.dtype)

    _dispatch_on_tile(branch, nq)


def fox_attention(z3, c4, *, q_col, k_col, v_col, tq=256, npair=3):
    b, s, _ = z3.shape
    tq = min(tq, s)
    w = npair * LANES
    groups = C_HEADS // 2 // npair
    assert q_col % npair == 0 and k_col % npair == 0 and v_col % npair == 0
    return pl.pallas_call(
        functools.partial(_fox_kernel, tq=tq, nq=s // tq, npair=npair),
        out_shape=jax.ShapeDtypeStruct((b, s, C_HEADS // 2 * LANES), BF16),
        grid=(b, groups, s // tq),
        in_specs=[pl.BlockSpec((1, tq, w), lambda bi, g, qi: (bi, qi, q_col // npair + g)),
                  pl.BlockSpec((1, s, w), lambda bi, g, qi: (bi, 0, k_col // npair + g)),
                  pl.BlockSpec((1, s, w), lambda bi, g, qi: (bi, 0, v_col // npair + g)),
                  pl.BlockSpec((1, 2 * npair, 1, s), lambda bi, g, qi: (bi, g, 0, 0))],
        out_specs=pl.BlockSpec((1, tq, w), lambda bi, g, qi: (bi, qi, g)),
        compiler_params=_params("parallel", "parallel", "arbitrary"),
        name="fox_attention",
    )(z3, z3, z3, c4)


def diff_attention(z3, slopes, lam, subln, *, q_col, k_col, v_col, out_scale, tq=256, nhead=2):
    b, s, _ = z3.shape
    tq = min(tq, s)
    w = nhead * LANES
    assert q_col % nhead == 0 and k_col % nhead == 0 and v_col % nhead == 0
    smem = pl.BlockSpec(memory_space=pltpu.SMEM)
    return pl.pallas_call(
        functools.partial(_diff_kernel, tq=tq, nq=s // tq, nhead=nhead, out_scale=out_scale),
        out_shape=jax.ShapeDtypeStruct((b, s, B_HEADS * LANES), BF16),
        grid=(b, B_HEADS // nhead, s // tq),
        in_specs=[smem, smem,
                  pl.BlockSpec((1, tq, w), lambda bi, g, qi: (bi, qi, q_col // nhead + g)),
                  pl.BlockSpec((1, s, w), lambda bi, g, qi: (bi, 0, k_col // nhead + g)),
                  pl.BlockSpec((1, s, w), lambda bi, g, qi: (bi, 0, v_col // nhead + g)),
                  pl.BlockSpec((1, LANES), lambda bi, g, qi: (0, 0))],
        out_specs=pl.BlockSpec((1, tq, w), lambda bi, g, qi: (bi, qi, g)),
        compiler_params=_params("parallel", "parallel", "arbitrary"),
        name="diff_attention",
    )(slopes, lam, z3, z3, z3, subln.reshape(1, LANES).astype(F32))


def _swa_kernel(slope_ref, sink_ref, q_ref, k_ref, v_ref, o_ref, *, blk):
    qi = pl.program_id(1)
    cur = pl.multiple_of(qi * blk, blk)
    prev = pl.multiple_of(jnp.maximum(qi - 1, 0) * blk, blk)
    r = lax.broadcasted_iota(jnp.int32, (blk, 2 * blk), 0)
    c = lax.broadcasted_iota(jnp.int32, (blk, 2 * blk), 1)
    dist = r + blk - c
    keep = (dist >= 0) & (dist < WINDOW) & ((qi > 0) | (c >= blk))
    dist_f = dist.astype(F32)
    scale = jnp.asarray(HEAD_DIM ** -0.5, q_ref.dtype)
    masks = [_half_mask(blk, h) for h in range(2)]
    for tile in range(A_Q_HEADS // 2):
        q = q_ref[0, :, tile * LANES:(tile + 1) * LANES] * scale
        halves = []
        for half in range(2):
            slot = 2 * tile + half
            kv_tile = (A_SLOT_HEAD[slot] // A_GROUP) // 2
            lanes = slice(kv_tile * LANES, (kv_tile + 1) * LANES)
            k = jnp.concatenate([k_ref[0, pl.ds(prev, blk), lanes],
                                 k_ref[0, pl.ds(cur, blk), lanes]], axis=0)
            v = jnp.concatenate([v_ref[0, pl.ds(prev, blk), lanes],
                                 v_ref[0, pl.ds(cur, blk), lanes]], axis=0)
            qh = jnp.where(masks[half], q, jnp.zeros_like(q))
            s = _dot_nt(qh, k) - slope_ref[slot] * dist_f
            s = jnp.where(keep, s, NEG_INF)
            sink = sink_ref[slot]
            m = jnp.maximum(jnp.max(s, axis=-1, keepdims=True), sink)
            p = jnp.exp(s - m)
            denom = jnp.sum(p, axis=-1, keepdims=True) + jnp.exp(sink - m)
            halves.append(_dot(p.astype(v.dtype), v) / denom)
        o_ref[0, :, tile * LANES:(tile + 1) * LANES] = jnp.where(
            masks[0], halves[0], halves[1]).astype(o_ref.dtype)


def swa_attention(z3, slopes, sinks, *, q_tile, k_tile, v_tile):
    b, s, _ = z3.shape
    blk = WINDOW
    qw = A_Q_HEADS * HEAD_DIM
    kw = A_KV_HEADS * HEAD_DIM
    smem = pl.BlockSpec(memory_space=pltpu.SMEM)
    return pl.pallas_call(
        functools.partial(_swa_kernel, blk=blk),
        out_shape=jax.ShapeDtypeStruct((b, s, qw), BF16),
        grid=(b, s // blk),
        in_specs=[smem, smem,
                  pl.BlockSpec((1, blk, qw), lambda bi, qi: (bi, qi, q_tile)),
                  pl.BlockSpec((1, s, kw), lambda bi, qi: (bi, 0, k_tile)),
                  pl.BlockSpec((1, s, kw), lambda bi, qi: (bi, 0, v_tile))],
        out_specs=pl.BlockSpec((1, blk, qw), lambda bi, qi: (bi, qi, 0)),
        compiler_params=_params("parallel", "parallel"),
        name="swa_attention",
    )(slopes, sinks, z3, z3, z3)


def _merge_out_kernel(oa_ref, ob_ref, oc_ref, ga_ref, gb_ref, gc_ref, wa_ref, wb_ref, wc_ref,
                      wo_ref, x_ref, gp_ref, gn_ref, xo_ref, ho_ref):
    merged = ga_ref[...].astype(F32) * _dot(oa_ref[...], wa_ref[...])
    merged += gb_ref[...].astype(F32) * _dot(ob_ref[...], wb_ref[...])
    merged += gc_ref[...].astype(F32) * _dot(oc_ref[...], wc_ref[...])
    m = _dot(merged.astype(wo_ref.dtype), wo_ref[...])
    x_new = x_ref[...] + _rms(m, gp_ref[...])
    xo_ref[...] = x_new
    ho_ref[...] = _rms(x_new, gn_ref[...]).astype(ho_ref.dtype)


def merge_out_residual(oa, ob, oc, gates, wa, wb, wc, w_out, x, g_post, g_next, *, tm=256):
    t, d = x.shape
    tm = min(tm, t)
    row = lambda w: pl.BlockSpec((tm, w), lambda i: (i, 0))
    gate = lambda br: pl.BlockSpec((tm, d), lambda i: (i, br))
    whole = lambda w: pl.BlockSpec(w.shape, lambda i: (0, 0), pipeline_mode=pl.Buffered(1))
    vec = pl.BlockSpec((1, d), lambda i: (0, 0))
    return pl.pallas_call(
        _merge_out_kernel,
        out_shape=(jax.ShapeDtypeStruct((t, d), F32), jax.ShapeDtypeStruct((t, d), BF16)),
        grid=(t // tm,),
        in_specs=[row(oa.shape[1]), row(ob.shape[1]), row(oc.shape[1]),
                  gate(0), gate(1), gate(2),
                  whole(wa), whole(wb), whole(wc), whole(w_out), row(d), vec, vec],
        out_specs=(row(d), row(d)),
        compiler_params=_params("parallel"),
        name="merge_out_residual",
    )(oa, ob, oc, gates, gates, gates, wa, wb, wc, w_out, x,
      g_post.reshape(1, d), g_next.reshape(1, d))


def _proj_res_kernel(a_ref, w_ref, x_ref, gp_ref, gn_ref, xo_ref, ho_ref, acc_ref, *, nk):
    kk = pl.program_id(1)
    part = _dot(a_ref[...], w_ref[...])

    @pl.when(kk == 0)
    def _():
        acc_ref[...] = part

    @pl.when(kk > 0)
    def _():
        acc_ref[...] += part

    @pl.when(kk == nk - 1)
    def _():
        x_new = x_ref[...] + _rms(acc_ref[...], gp_ref[...])
        xo_ref[...] = x_new
        ho_ref[...] = _rms(x_new, gn_ref[...]).astype(ho_ref.dtype)


def proj_residual(a, w, x, g_post, g_next, *, tm, tk):
    t, k = a.shape
    d = w.shape[1]
    tm, tk = min(tm, t), min(tk, k)
    nk = k // tk
    vec = pl.BlockSpec((1, d), lambda i, kk: (0, 0))
    w_mode = dict(pipeline_mode=pl.Buffered(1)) if nk == 1 else {}
    return pl.pallas_call(
        functools.partial(_proj_res_kernel, nk=nk),
        out_shape=(jax.ShapeDtypeStruct((t, d), F32), jax.ShapeDtypeStruct((t, d), BF16)),
        grid=(t // tm, nk),
        in_specs=[pl.BlockSpec((tm, tk), lambda i, kk: (i, kk)),
                  pl.BlockSpec((tk, d), lambda i, kk: (kk, 0), **w_mode),
                  pl.BlockSpec((tm, d), lambda i, kk: (i, 0)),
                  vec, vec],
        out_specs=(pl.BlockSpec((tm, d), lambda i, kk: (i, 0)),
                   pl.BlockSpec((tm, d), lambda i, kk: (i, 0))),
        scratch_shapes=[pltpu.VMEM((tm, d), F32)],
        compiler_params=_params("parallel", "arbitrary"),
        name="proj_residual",
    )(a, w, x, g_post.reshape(1, d), g_next.reshape(1, d))


def _swiglu_up_kernel(h_ref, wg_ref, wu_ref, o_ref):
    h = h_ref[...]
    g = _dot(h, wg_ref[...])
    u = _dot(h, wu_ref[...])
    o_ref[...] = (g * _sigmoid(g) * u).astype(o_ref.dtype)


def swiglu_up(h, wg, wu, *, tm=1024, tn=512):
    t, d = h.shape
    f = wg.shape[1]
    tm, tn = min(tm, t), min(tn, f)
    return pl.pallas_call(
        _swiglu_up_kernel,
        out_shape=jax.ShapeDtypeStruct((t, f), BF16),
        grid=(t // tm, f // tn),
        in_specs=[pl.BlockSpec((tm, d), lambda i, j: (i, 0)),
                  pl.BlockSpec((d, tn), lambda i, j: (0, j)),
                  pl.BlockSpec((d, tn), lambda i, j: (0, j))],
        out_specs=pl.BlockSpec((tm, tn), lambda i, j: (i, j)),
        compiler_params=_params("parallel", "parallel"),
        name="swiglu_up",
    )(h, wg, wu)


def _router_kernel(x_ref, g_ref, w_ref, idx_ref, wt_ref, h_ref):
    h = _rms(x_ref[...], g_ref[...])
    bits = pltpu.bitcast(h.astype(BF16).astype(F32), jnp.uint32)
    half = h.shape[1] // 2
    h_ref[...] = (bits[:, :half] >> 16) | bits[:, half:]
    logits = jnp.dot(h, w_ref[...], precision=lax.Precision.HIGHEST,
                     preferred_element_type=F32)
    lane = lax.broadcasted_iota(jnp.int32, logits.shape, 1)
    lane_f = lane.astype(F32)
    lg = jnp.where(lane < N_EXPERTS, logits, -jnp.inf)
    m1 = jnp.max(lg, axis=-1, keepdims=True)
    i1 = jnp.min(jnp.where(lg == m1, lane_f, float(LANES)), axis=-1, keepdims=True)
    lg2 = jnp.where(lane_f == i1, -jnp.inf, lg)
    m2 = jnp.max(lg2, axis=-1, keepdims=True)
    i2 = jnp.min(jnp.where(lg2 == m2, lane_f, float(LANES)), axis=-1, keepdims=True)
    e2 = jnp.exp(m2 - m1)
    w1 = 1.0 / (1.0 + e2)
    w2 = e2 / (1.0 + e2)
    idx_ref[...] = jnp.where(lane == 0, i1, jnp.where(lane == 1, i2, 0.0)).astype(jnp.int32)
    wt_ref[...] = jnp.where(lane == 0, w1, jnp.where(lane == 1, w2, 0.0))


def router_top2(x, g, w_router, *, tm=512):
    t, d = x.shape
    tm = min(tm, t)
    wp = jnp.zeros((d, LANES), F32).at[:, :N_EXPERTS].set(w_router.astype(F32))
    out = pl.BlockSpec((tm, LANES), lambda i: (i, 0))
    idx, wt, h = pl.pallas_call(
        _router_kernel,
        out_shape=(jax.ShapeDtypeStruct((t, LANES), jnp.int32),
                   jax.ShapeDtypeStruct((t, LANES), F32),
                   jax.ShapeDtypeStruct((t, d // 2), jnp.uint32)),
        grid=(t // tm,),
        in_specs=[pl.BlockSpec((tm, d), lambda i: (i, 0)),
                  pl.BlockSpec((1, d), lambda i: (0, 0)),
                  pl.BlockSpec((d, LANES), lambda i: (0, 0))],
        out_specs=(out, out, pl.BlockSpec((tm, d // 2), lambda i: (i, 0))),
        compiler_params=_params("parallel"),
        name="router_top2",
    )(x, g.reshape(1, d), wp)
    return idx[:, :TOP_K], wt[:, :TOP_K], h


SC_INDEX_WINDOW = 128
SC_COPY_BYTES = 256 * 1024


def sc_gather_rows(src, idx):
    n = idx.shape[0]
    d = src.shape[1]
    batch = min(SC_INDEX_WINDOW, SC_COPY_BYTES // (d * src.dtype.itemsize))
    assert n % SC_INDEX_WINDOW == 0 and SC_INDEX_WINDOW % batch == 0
    mesh = plsc.VectorSubcoreMesh(core_axis_name="c", subcore_axis_name="s")
    dst = jnp.arange(n, dtype=jnp.int32).reshape(1, n)

    @pl.kernel(out_type=jax.ShapeDtypeStruct((n, d), src.dtype), mesh=mesh,
               scratch_types=[pltpu.VMEM((batch, d), src.dtype)])
    def gather(x_hbm, i_hbm, d_hbm, o_hbm, buf):
        def body(i_vmem, d_vmem):
            for s in range(SC_INDEX_WINDOW // batch):
                rows = pl.ds(s * batch, batch)
                pltpu.sync_copy(x_hbm.at[i_vmem.at[0, rows]], buf)
                pltpu.sync_copy(buf, o_hbm.at[d_vmem.at[0, rows]])

        window = pl.BlockSpec((1, SC_INDEX_WINDOW), lambda i: (0, i))
        pltpu.emit_pipeline(
            body, grid=(n // SC_INDEX_WINDOW,),
            in_specs=[window, window], out_specs=[],
            core_axis_name=("c", "s"),
            dimension_semantics=(pltpu.PARALLEL,),
        )(i_hbm, d_hbm)

    return gather(src, idx.reshape(1, n), dst)


def _moe_up_kernel(te_ref, na_ref, x_ref, wg_ref, wu_ref, *rest):
    o_ref = rest[-1]
    active = pl.program_id(1) < na_ref[0]

    @pl.when(active)
    def _():
        words = x_ref[...]
        half = words.shape[1]
        lo = pltpu.bitcast(words << 16, F32).astype(BF16)
        hi = pltpu.bitcast(words & jnp.uint32(0xFFFF0000), F32).astype(BF16)
        g = _dot(lo, wg_ref[0, :half]) + _dot(hi, wg_ref[0, half:])
        u = _dot(lo, wu_ref[0, :half]) + _dot(hi, wu_ref[0, half:])
        o_ref[...] = (g * _sigmoid(g) * u).astype(o_ref.dtype)

    @pl.when(jnp.logical_not(active))
    def _():
        o_ref[...] = jnp.zeros_like(o_ref)


def _moe_down_kernel(te_ref, na_ref, u_ref, wd_ref, o_ref):
    active = pl.program_id(1) < na_ref[0]

    @pl.when(active)
    def _():
        o_ref[...] = _dot(u_ref[...], wd_ref[0])

    @pl.when(jnp.logical_not(active))
    def _():
        o_ref[...] = jnp.zeros_like(o_ref)


def _active_tile(i, na_ref):
    return jnp.maximum(jnp.minimum(i, na_ref[0] - 1), 0)


def moe_up(xs, tile_expert, n_active, wg, wu, u_prev, *, tile0, n_tiles, tm, tn=1024):
    d, f = wg.shape[1:]
    tn = min(tn, f)
    assert f % tn == 0
    in_specs = [pl.BlockSpec((tm, d // 2), lambda j, i, te, na: (_active_tile(i, na), 0)),
                pl.BlockSpec((1, d, tn), lambda j, i, te, na: (te[_active_tile(i, na)], 0, j)),
                pl.BlockSpec((1, d, tn), lambda j, i, te, na: (te[_active_tile(i, na)], 0, j))]
    args = [tile_expert, n_active, xs, wg, wu]
    aliases = {}
    if u_prev is not None:
        in_specs.append(pl.BlockSpec(memory_space=pl.ANY))
        args.append(u_prev)
        aliases = {len(args) - 1: 0}
    grid_spec = pltpu.PrefetchScalarGridSpec(
        num_scalar_prefetch=2,
        grid=(f // tn, xs.shape[0] // tm),
        in_specs=in_specs,
        out_specs=pl.BlockSpec((tm, tn), lambda j, i, te, na: (i + tile0, j)),
    )
    return pl.pallas_call(
        _moe_up_kernel,
        out_shape=jax.ShapeDtypeStruct((n_tiles * tm, f), BF16),
        grid_spec=grid_spec,
        input_output_aliases=aliases,
        compiler_params=_params("arbitrary", "arbitrary"),
        name="moe_up",
    )(*args)


def moe_down(u, tile_expert, n_active, wd, *, tm, tn=512):
    p, f = u.shape
    d = wd.shape[2]
    tn = min(tn, d)
    grid_spec = pltpu.PrefetchScalarGridSpec(
        num_scalar_prefetch=2,
        grid=(d // tn, p // tm),
        in_specs=[pl.BlockSpec((tm, f), lambda j, i, te, na: (_active_tile(i, na), 0)),
                  pl.BlockSpec((1, f, tn), lambda j, i, te, na: (te[_active_tile(i, na)], 0, j))],
        out_specs=pl.BlockSpec((tm, tn), lambda j, i, te, na: (i, j)),
    )
    return pl.pallas_call(
        _moe_down_kernel,
        out_shape=jax.ShapeDtypeStruct((p, d), F32),
        grid_spec=grid_spec,
        compiler_params=_params("arbitrary", "arbitrary"),
        name="moe_down",
    )(tile_expert, n_active, u, wd)


def _combine_kernel(ya_ref, yb_ref, w_ref, x_ref, g_ref, o_ref):
    w = w_ref[...]
    y = w[:, 0:1] * ya_ref[...] + w[:, 1:2] * yb_ref[...]
    o_ref[...] = x_ref[...] + _rms(y, g_ref[...])


def combine_residual(y2, wts, x, g, *, tm=512):
    t, d = x.shape
    tm = min(tm, t)
    nt = t // tm
    blk = pl.BlockSpec((tm, d), lambda i: (i, 0))
    return pl.pallas_call(
        _combine_kernel,
        out_shape=jax.ShapeDtypeStruct((t, d), F32),
        grid=(nt,),
        in_specs=[blk, pl.BlockSpec((tm, d), lambda i: (i + nt, 0)),
                  pl.BlockSpec((tm, TOP_K), lambda i: (i, 0)), blk,
                  pl.BlockSpec((1, d), lambda i: (0, 0))],
        out_specs=blk,
        compiler_params=_params("parallel"),
        name="combine_residual",
    )(y2, y2, wts, x, g.reshape(1, d))


def _alibi_slopes(n):
    return jnp.exp2(-8.0 * (jnp.arange(n, dtype=F32) + 1.0) / n)


def _token_mixer(x2, h, batch, seq, layer_idx, w_in, b_forget, b_gate, sinks,
                 lq1, lk1, lq2, lk2, subln, w_br_a, w_br_b, w_br_c, w_out,
                 g_post, g_next):
    t, d = x2.shape
    a_w = A_Q_HEADS * HEAD_DIM
    kv_w = A_KV_HEADS * HEAD_DIM
    b_w = B_HEADS * 2 * HEAD_DIM
    c_w = C_HEADS * HEAD_DIM
    n_attn = a_w + 2 * kv_w + 3 * b_w + 3 * c_w

    slot_cols = np.concatenate([np.arange(HEAD_DIM) + HEAD_DIM * hd for hd in A_SLOT_HEAD])
    a_end = a_w + 2 * kv_w
    b_end = a_end + 3 * b_w
    w_in_t = jnp.transpose(w_in, (2, 0, 1))
    rows = lambda lo, hi: w_in_t[lo:hi, layer_idx, :]
    w_attn_t = jnp.concatenate([rows(b_end, n_attn), rows(a_end, b_end),
                                rows(0, a_w)[slot_cols], rows(a_w, a_end)],
                               axis=0).astype(BF16)
    w_f_t = rows(n_attn, n_attn + C_HEADS)
    w_g_t = rows(n_attn + C_HEADS, w_in.shape[2]).astype(BF16)

    z3 = matmul(h, w_attn_t).reshape(batch, seq, n_attn)
    gates = matmul(h, w_g_t, bias=b_gate)
    c = forget_cumsum(h.reshape(batch, seq, d), w_f_t, b_forget)
    c4 = c[:, :C_HEADS].reshape(batch, C_HEADS, 1, seq)

    oc = fox_attention(z3, c4, q_col=0, k_col=C_HEADS // 2, v_col=C_HEADS)

    lam_init = 0.8 - 0.6 * math.exp(-0.3 * layer_idx)
    lam = (jnp.exp(jnp.sum(lq1.astype(F32) * lk1.astype(F32)))
           - jnp.exp(jnp.sum(lq2.astype(F32) * lk2.astype(F32))) + lam_init)
    b0 = 3 * c_w // LANES
    ob = diff_attention(z3, _alibi_slopes(B_HEADS), lam.reshape(1), subln,
                        q_col=b0, k_col=b0 + B_HEADS, v_col=b0 + 2 * B_HEADS,
                        out_scale=1.0 - lam_init)

    slot = np.array(A_SLOT_HEAD)
    a0 = 3 * c_w + 3 * b_w
    oa = swa_attention(z3, _alibi_slopes(A_Q_HEADS)[slot], sinks.astype(F32)[slot],
                       q_tile=a0 // a_w, k_tile=(a0 + a_w) // kv_w,
                       v_tile=(a0 + a_w) // kv_w + 1)

    return merge_out_residual(oa.reshape(t, a_w), ob.reshape(t, b_w), oc.reshape(t, c_w), gates,
                              w_br_a[slot_cols].astype(BF16), w_br_b.astype(BF16),
                              w_br_c.astype(BF16), w_out.astype(BF16), x2, g_post, g_next)


COUNT_BLOCK = 256
MOE_PIECES = 4


def _moe_dispatch(idx, tm):
    t = idx.shape[0]
    n_rows = t * TOP_K
    n_tiles = (n_rows + N_EXPERTS * (tm - 1)) // tm
    e_flat = idx.reshape(n_rows)
    onehot = e_flat[:, None] == jnp.arange(N_EXPERTS)[None, :]
    blk = min(COUNT_BLOCK, n_rows)
    oh = onehot.astype(F32).reshape(n_rows // blk, blk, N_EXPERTS)
    lower = (jnp.arange(blk)[:, None] >= jnp.arange(blk)[None, :]).astype(F32)
    within = jnp.einsum("ij,bjk->bik", lower, oh)
    before = jnp.cumsum(within[:, -1, :], axis=0) - within[:, -1, :]
    running = (within + before[:, None, :]).reshape(n_rows, N_EXPERTS)
    rank = jnp.sum(jnp.where(onehot, running - 1.0, 0.0), axis=1).astype(jnp.int32)
    counts = jnp.sum(onehot.astype(jnp.int32), axis=0)
    padded = ((counts + tm - 1) // tm) * tm
    ends = jnp.cumsum(padded)
    starts = ends - padded
    pos = starts[e_flat] + rank
    row_token = jnp.zeros((n_tiles * tm,), jnp.int32).at[pos].set(
        jnp.arange(n_rows, dtype=jnp.int32) // TOP_K)
    tile_start = jnp.arange(n_tiles, dtype=jnp.int32) * tm
    tile_expert = jnp.minimum(jnp.sum(tile_start[:, None] >= ends[None, :], axis=1),
                              N_EXPERTS - 1).astype(jnp.int32)
    n_active = (ends[-1] // tm).astype(jnp.int32).reshape(1)
    return pos.reshape(t, TOP_K), row_token, tile_expert, n_active


def _moe_ffn(x2, g_pre, g_post, w_router, e_gate, e_up, e_down, *, tm=512):
    idx, wts, h = router_top2(x2, g_pre, w_router)
    t = x2.shape[0]
    tm = min(tm, t)
    pos, row_token, tile_expert, n_active = _moe_dispatch(idx, tm)
    n_tiles = tile_expert.shape[0]
    bounds = [k * n_tiles // MOE_PIECES for k in range(MOE_PIECES + 1)]
    u = None
    for lo, hi in zip(bounds[:-1], bounds[1:]):
        xs = sc_gather_rows(h, row_token[lo * tm:hi * tm])
        u = moe_up(xs, tile_expert[lo:hi], jnp.clip(n_active - lo, 0, hi - lo), e_gate, e_up, u,
                   tile0=lo, n_tiles=n_tiles, tm=tm)
    ys = moe_down(u, tile_expert, n_active, e_down, tm=tm)
    y2 = sc_gather_rows(ys, pos.T.reshape(TOP_K * t))
    return combine_residual(y2, wts, x2, g_post)


def kernel(x, mix_pre_norm, w_in, b_forget, b_gate, attn_sinks, lam_q1, lam_k1, lam_q2, lam_k2, diff_subln, w_br_a, w_br_b, w_br_c, w_out, mix_post_norm, ffn_pre_norm, ffn_post_norm, dense_w_gate, dense_w_up, dense_w_down, w_router, moe_w_gate, moe_w_up, moe_w_down):
    batch, seq, d = x.shape
    depth = w_in.shape[0]
    x2 = x.reshape(batch * seq, d)
    h = rmsnorm_rows(x2, mix_pre_norm[0])
    for l in range(depth):
        x2, h = _token_mixer(x2, h, batch, seq, l, w_in, b_forget[l], b_gate[l],
                             attn_sinks[l], lam_q1[l], lam_k1[l], lam_q2[l], lam_k2[l],
                             diff_subln[l], w_br_a[l], w_br_b[l], w_br_c[l], w_out[l],
                             mix_post_norm[l], ffn_pre_norm[l])
        i = l // 2
        if l % 2 == 0:
            u = swiglu_up(h, dense_w_gate[i].astype(BF16), dense_w_up[i].astype(BF16))
            g_next = mix_pre_norm[l + 1] if l + 1 < depth else jnp.ones((d,), F32)
            f = dense_w_down.shape[1]
            x2, h = proj_residual(u, dense_w_down[i].astype(BF16), x2, ffn_post_norm[l],
                                  g_next, tm=256, tk=f)
        else:
            x2 = _moe_ffn(x2, ffn_pre_norm[l], ffn_post_norm[l], w_router[i],
                          moe_w_gate[i], moe_w_up[i], moe_w_down[i])
            if l + 1 < depth:
                h = rmsnorm_rows(x2, mix_pre_norm[l + 1])
    return x2.reshape(batch, seq, d)
```

```python
import functools
import math

import jax
import jax.numpy as jnp
import numpy as np
from jax import lax
from jax.experimental import pallas as pl
from jax.experimental.pallas import tpu as pltpu
from jax.experimental.pallas import tpu_sc as plsc

F32 = jnp.float32
BF16 = jnp.bfloat16

HEAD_DIM = 64
LANES = 128
A_Q_HEADS = 12
A_KV_HEADS = 4
A_GROUP = A_Q_HEADS // A_KV_HEADS
WINDOW = 128
B_HEADS = 4
C_HEADS = 12
N_BRANCH = 3
N_EXPERTS = 8
TOP_K = 2
RMS_EPS = 1e-6
NEG_INF = -1e30
VMEM_LIMIT = 56 * 1024 * 1024

A_SLOT_HEAD = (0, 3, 1, 4, 2, 5, 6, 9, 7, 10, 8, 11)


def _params(*sem):
    return pltpu.CompilerParams(dimension_semantics=sem, vmem_limit_bytes=VMEM_LIMIT)


def _rms(x, g):
    var = jnp.mean(x * x, axis=-1, keepdims=True)
    return x * lax.rsqrt(var + RMS_EPS) * g


def _sigmoid(x):
    return 0.5 * jnp.tanh(0.5 * x) + 0.5


def _pack_bf16_pairs(x):
    bits = pltpu.bitcast(x.astype(BF16).astype(F32), jnp.uint32)
    c = x.shape[1] // 2
    return (bits[:, :c] >> 16) | bits[:, c:]


def _unpack_bf16_pairs(words):
    lo = pltpu.bitcast(words << 16, F32)
    hi = pltpu.bitcast(words & jnp.uint32(0xFFFF0000), F32)
    return lo, hi


def _dot(a, b):
    return jnp.dot(a, b, preferred_element_type=F32)


def _dot_nt(a, b):
    return lax.dot_general(a, b, (((1,), (1,)), ((), ())), preferred_element_type=F32)


def _round_kernel(*refs):
    n = len(refs) // 2
    for src, dst in zip(refs[:n], refs[n:]):
        dst[...] = src[...].astype(dst.dtype)


ROUND_BLOCK_ELEMS = 2 * 1024 * 1024


def round_to_bf16(*ws):
    shape = ws[0].shape
    flat = [w.reshape(-1, shape[-1]) for w in ws]
    n, c = flat[0].shape
    rows = min(n, 1 << ((ROUND_BLOCK_ELEMS // c).bit_length() - 1))
    assert n % rows == 0 and all(w.shape == shape for w in ws)
    blk = pl.BlockSpec((rows, c), lambda i: (i, 0))
    outs = pl.pallas_call(
        _round_kernel,
        out_shape=tuple(jax.ShapeDtypeStruct((n, c), BF16) for _ in ws),
        grid=(n // rows,),
        in_specs=[blk] * len(ws),
        out_specs=tuple(blk for _ in ws),
        compiler_params=_params("parallel"),
        name="round_to_bf16",
    )(*flat)
    return [o.reshape(shape) for o in outs]


def _rmsnorm_kernel(x_ref, g_ref, o_ref):
    o_ref[...] = _rms(x_ref[...], g_ref[...]).astype(o_ref.dtype)


def rmsnorm_rows(x, g, *, tm=512):
    t, d = x.shape
    tm = min(tm, t)
    return pl.pallas_call(
        _rmsnorm_kernel,
        out_shape=jax.ShapeDtypeStruct((t, d), BF16),
        grid=(t // tm,),
        in_specs=[pl.BlockSpec((tm, d), lambda i: (i, 0)),
                  pl.BlockSpec((1, d), lambda i: (0, 0))],
        out_specs=pl.BlockSpec((tm, d), lambda i: (i, 0)),
        compiler_params=_params("parallel"),
        name="rmsnorm_rows",
    )(x, g.reshape(1, d))


def _mm_kernel(a_ref, bt_ref, o_ref):
    o_ref[...] = _dot_nt(a_ref[...], bt_ref[...]).astype(o_ref.dtype)


def _mm_gate_kernel(a_ref, bt_ref, bias_ref, o_ref):
    z = _dot_nt(a_ref[...], bt_ref[...]) + bias_ref[...]
    o_ref[...] = _sigmoid(z).astype(o_ref.dtype)


def matmul(a, bt, *, bias=None, out_dtype=BF16, tm=1024, tn=1024):
    m, k = a.shape
    n = bt.shape[0]
    tm, tn = min(tm, m), min(tn, n)
    assert m % tm == 0 and n % tn == 0
    in_specs = [pl.BlockSpec((tm, k), lambda i, j: (i, 0)),
                pl.BlockSpec((tn, k), lambda i, j: (j, 0))]
    args = [a, bt]
    kern = _mm_kernel
    if bias is not None:
        in_specs.append(pl.BlockSpec((1, tn), lambda i, j: (0, j)))
        args.append(bias.reshape(1, n).astype(F32))
        kern = _mm_gate_kernel
    return pl.pallas_call(
        kern,
        out_shape=jax.ShapeDtypeStruct((m, n), out_dtype),
        grid=(m // tm, n // tn),
        in_specs=in_specs,
        out_specs=pl.BlockSpec((tm, tn), lambda i, j: (i, j)),
        compiler_params=_params("parallel", "parallel"),
        name="matmul_gate" if bias is not None else "matmul",
    )(*args)


def _forget_kernel(h_ref, wt_ref, b_ref, c_ref, *, seq, blk):
    z = _dot_nt(wt_ref[...], h_ref[0]) + b_ref[...]
    log_f = jnp.minimum(z, 0.0) - jnp.log1p(jnp.exp(-jnp.abs(z)))
    r = lax.broadcasted_iota(jnp.int32, (blk, blk), 0)
    c = lax.broadcasted_iota(jnp.int32, (blk, blk), 1)
    upper = (r <= c).astype(F32)
    carry = jnp.zeros((LANES, 1), F32)
    for j in range(seq // blk):
        part = lax.dot_general(log_f[:, j * blk:(j + 1) * blk], upper,
                               (((1,), (0,)), ((), ())),
                               precision=lax.Precision.HIGHEST,
                               preferred_element_type=F32) + carry
        c_ref[0, :, j * blk:(j + 1) * blk] = part
        carry = part[:, blk - 1:blk]


def forget_cumsum(h3, w_ft, b_f):
    b, s, d = h3.shape
    nh = w_ft.shape[0]
    wt = jnp.zeros((LANES, d), BF16).at[:nh].set(w_ft.astype(BF16))
    bias = jnp.zeros((LANES, 1), F32).at[:nh, 0].set(b_f.astype(F32))
    blk = min(256, s)
    return pl.pallas_call(
        functools.partial(_forget_kernel, seq=s, blk=blk),
        out_shape=jax.ShapeDtypeStruct((b, LANES, s), F32),
        grid=(b,),
        in_specs=[pl.BlockSpec((1, s, d), lambda i: (i, 0, 0)),
                  pl.BlockSpec((LANES, d), lambda i: (0, 0)),
                  pl.BlockSpec((LANES, 1), lambda i: (0, 0))],
        out_specs=pl.BlockSpec((1, LANES, s), lambda i: (i, 0, 0)),
        compiler_params=_params("parallel"),
        name="forget_cumsum",
    )(h3, wt, bias)


def _half_mask(tq, half):
    lane = lax.broadcasted_iota(jnp.int32, (tq, LANES), 1)
    return (lane >= HEAD_DIM * half) & (lane < HEAD_DIM * (half + 1))


def _causal_logits(qh, k_ref, lanes, qi, tq, bias):
    n0 = qi * tq
    r = lax.broadcasted_iota(jnp.int32, (tq, tq), 0)
    c = lax.broadcasted_iota(jnp.int32, (tq, tq), 1)
    diag = _dot_nt(qh, k_ref[0, n0:n0 + tq, lanes]) + bias(n0, tq)
    parts = [jnp.where(r >= c, diag, NEG_INF)]
    if qi > 0:
        parts.append(_dot_nt(qh, k_ref[0, :n0, lanes]) + bias(0, n0))
    return parts


def _softmax_parts(parts):
    m = functools.reduce(jnp.maximum, [s.max(axis=-1, keepdims=True) for s in parts])
    exps = [jnp.exp(s - m) for s in parts]
    denom = functools.reduce(jnp.add, [e.sum(axis=-1, keepdims=True) for e in exps])
    return exps, denom


def _value_parts(v_ref, lanes, qi, tq):
    n0 = qi * tq
    return [v_ref[0, n0:n0 + tq, lanes]] + ([v_ref[0, :n0, lanes]] if qi > 0 else [])


def _weighted_values(weights, values):
    return functools.reduce(jnp.add, [_dot(w.astype(v.dtype), v) for w, v in zip(weights, values)])


def _dispatch_on_tile(branch, nq):
    qi = pl.program_id(2)
    for i in range(nq):
        pl.when(qi == i)(functools.partial(branch, qi=i))


def _fox_kernel(q_ref, k_ref, v_ref, c_ref, o_ref, *, tq, nq, npair):
    masks = [_half_mask(tq, h) for h in range(2)]

    def branch(qi):
        for p in range(npair):
            lanes = slice(p * LANES, (p + 1) * LANES)
            q = q_ref[0, :, lanes] * jnp.asarray(HEAD_DIM ** -0.5, q_ref.dtype)
            values = _value_parts(v_ref, lanes, qi, tq)
            outs = []
            for h in range(2):
                qh = jnp.where(masks[h], q, jnp.zeros_like(q))
                bias = lambda start, size: -c_ref[0, 2 * p + h, :, start:start + size]
                exps, denom = _softmax_parts(_causal_logits(qh, k_ref, lanes, qi, tq, bias))
                outs.append(_weighted_values(exps, values) / denom)
            o_ref[0, :, lanes] = jnp.where(masks[0], outs[0], outs[1]).astype(o_ref.dtype)

    _dispatch_on_tile(branch, nq)


def _diff_kernel(slope_ref, lam_ref, q_ref, k_ref, v_ref, g_ref, o_ref, *,
                 tq, nq, nhead, out_scale):
    lam = lam_ref[0]
    masks = [_half_mask(tq, h) for h in range(2)]

    def branch(qi):
        for j in range(nhead):
            lanes = slice(j * LANES, (j + 1) * LANES)
            slope = slope_ref[pl.program_id(1) * nhead + j]
            q = q_ref[0, :, lanes] * jnp.asarray(HEAD_DIM ** -0.5, q_ref.dtype)

            def bias(start, size):
                pos = start + lax.broadcasted_iota(jnp.int32, (1, size), 1)
                return slope * pos.astype(F32)

            maps = []
            for h in range(2):
                qh = jnp.where(masks[h], q, jnp.zeros_like(q))
                maps.append(_softmax_parts(_causal_logits(qh, k_ref, lanes, qi, tq, bias)))
            (e0, d0), (e1, d1) = maps
            w0, w1 = 1.0 / d0, lam / d1
            weights = [a * w0 - b * w1 for a, b in zip(e0, e1)]
            o = _weighted_values(weights, _value_parts(v_ref, lanes, qi, tq))
            o_ref[0, :, lanes] = (_rms(o, g_ref[...]) * out_scale).astype(o_ref.dtype)

    _dispatch_on_tile(branch, nq)


def fox_attention(z3, c4, *, q_col, k_col, v_col, tq=256, npair=3):
    b, s, _ = z3.shape
    tq = min(tq, s)
    w = npair * LANES
    groups = C_HEADS // 2 // npair
    assert q_col % npair == 0 and k_col % npair == 0 and v_col % npair == 0
    return pl.pallas_call(
        functools.partial(_fox_kernel, tq=tq, nq=s // tq, npair=npair),
        out_shape=jax.ShapeDtypeStruct((b, s, C_HEADS // 2 * LANES), BF16),
        grid=(b, groups, s // tq),
        in_specs=[pl.BlockSpec((1, tq, w), lambda bi, g, qi: (bi, qi, q_col // npair + g)),
                  pl.BlockSpec((1, s, w), lambda bi, g, qi: (bi, 0, k_col // npair + g)),
                  pl.BlockSpec((1, s, w), lambda bi, g, qi: (bi, 0, v_col // npair + g)),
                  pl.BlockSpec((1, 2 * npair, 1, s), lambda bi, g, qi: (bi, g, 0, 0))],
        out_specs=pl.BlockSpec((1, tq, w), lambda bi, g, qi: (bi, qi, g)),
        compiler_params=_params("parallel", "parallel", "arbitrary"),
        name="fox_attention",
    )(z3, z3, z3, c4)


def diff_attention(z3, slopes, lam, subln, *, q_col, k_col, v_col, out_scale, tq=256, nhead=2):
    b, s, _ = z3.shape
    tq = min(tq, s)
    w = nhead * LANES
    assert q_col % nhead == 0 and k_col % nhead == 0 and v_col % nhead == 0
    smem = pl.BlockSpec(memory_space=pltpu.SMEM)
    return pl.pallas_call(
        functools.partial(_diff_kernel, tq=tq, nq=s // tq, nhead=nhead, out_scale=out_scale),
        out_shape=jax.ShapeDtypeStruct((b, s, B_HEADS * LANES), BF16),
        grid=(b, B_HEADS // nhead, s // tq),
        in_specs=[smem, smem,
                  pl.BlockSpec((1, tq, w), lambda bi, g, qi: (bi, qi, q_col // nhead + g)),
                  pl.BlockSpec((1, s, w), lambda bi, g, qi: (bi, 0, k_col // nhead + g)),
                  pl.BlockSpec((1, s, w), lambda bi, g, qi: (bi, 0, v_col // nhead + g)),
                  pl.BlockSpec((1, LANES), lambda bi, g, qi: (0, 0))],
        out_specs=pl.BlockSpec((1, tq, w), lambda bi, g, qi: (bi, qi, g)),
        compiler_params=_params("parallel", "parallel", "arbitrary"),
        name="diff_attention",
    )(slopes, lam, z3, z3, z3, subln.reshape(1, LANES).astype(F32))


def _swa_kernel(slope_ref, sink_ref, q_ref, k_ref, v_ref, o_ref, *, blk):
    qi = pl.program_id(1)
    cur = pl.multiple_of(qi * blk, blk)
    prev = pl.multiple_of(jnp.maximum(qi - 1, 0) * blk, blk)
    r = lax.broadcasted_iota(jnp.int32, (blk, 2 * blk), 0)
    c = lax.broadcasted_iota(jnp.int32, (blk, 2 * blk), 1)
    dist = r + blk - c
    keep = (dist >= 0) & (dist < WINDOW) & ((qi > 0) | (c >= blk))
    dist_f = dist.astype(F32)
    scale = jnp.asarray(HEAD_DIM ** -0.5, q_ref.dtype)
    masks = [_half_mask(blk, h) for h in range(2)]
    for tile in range(A_Q_HEADS // 2):
        q = q_ref[0, :, tile * LANES:(tile + 1) * LANES] * scale
        halves = []
        for half in range(2):
            slot = 2 * tile + half
            kv_tile = (A_SLOT_HEAD[slot] // A_GROUP) // 2
            lanes = slice(kv_tile * LANES, (kv_tile + 1) * LANES)
            k = jnp.concatenate([k_ref[0, pl.ds(prev, blk), lanes],
                                 k_ref[0, pl.ds(cur, blk), lanes]], axis=0)
            v = jnp.concatenate([v_ref[0, pl.ds(prev, blk), lanes],
                                 v_ref[0, pl.ds(cur, blk), lanes]], axis=0)
            qh = jnp.where(masks[half], q, jnp.zeros_like(q))
            s = _dot_nt(qh, k) - slope_ref[slot] * dist_f
            s = jnp.where(keep, s, NEG_INF)
            sink = sink_ref[slot]
            m = jnp.maximum(jnp.max(s, axis=-1, keepdims=True), sink)
            p = jnp.exp(s - m)
            denom = jnp.sum(p, axis=-1, keepdims=True) + jnp.exp(sink - m)
            halves.append(_dot(p.astype(v.dtype), v) / denom)
        o_ref[0, :, tile * LANES:(tile + 1) * LANES] = jnp.where(
            masks[0], halves[0], halves[1]).astype(o_ref.dtype)


def swa_attention(z3, slopes, sinks, *, q_tile, k_tile, v_tile):
    b, s, _ = z3.shape
    blk = WINDOW
    qw = A_Q_HEADS * HEAD_DIM
    kw = A_KV_HEADS * HEAD_DIM
    smem = pl.BlockSpec(memory_space=pltpu.SMEM)
    return pl.pallas_call(
        functools.partial(_swa_kernel, blk=blk),
        out_shape=jax.ShapeDtypeStruct((b, s, qw), BF16),
        grid=(b, s // blk),
        in_specs=[smem, smem,
                  pl.BlockSpec((1, blk, qw), lambda bi, qi: (bi, qi, q_tile)),
                  pl.BlockSpec((1, s, kw), lambda bi, qi: (bi, 0, k_tile)),
                  pl.BlockSpec((1, s, kw), lambda bi, qi: (bi, 0, v_tile))],
        out_specs=pl.BlockSpec((1, blk, qw), lambda bi, qi: (bi, qi, 0)),
        compiler_params=_params("parallel", "parallel"),
        name="swa_attention",
    )(slopes, sinks, z3, z3, z3)


def _merge_out_kernel(oa_ref, ob_ref, oc_ref, ga_ref, gb_ref, gc_ref, wa_ref, wb_ref, wc_ref,
                      wo_ref, x_ref, gp_ref, gn_ref, xo_ref, ho_ref):
    merged = ga_ref[...].astype(F32) * _dot(oa_ref[...], wa_ref[...])
    merged += gb_ref[...].astype(F32) * _dot(ob_ref[...], wb_ref[...])
    merged += gc_ref[...].astype(F32) * _dot(oc_ref[...], wc_ref[...])
    m = _dot(merged.astype(wo_ref.dtype), wo_ref[...])
    x_new = x_ref[...] + _rms(m, gp_ref[...])
    xo_ref[...] = x_new
    ho_ref[...] = _rms(x_new, gn_ref[...]).astype(ho_ref.dtype)


def merge_out_residual(oa, ob, oc, gates, wa, wb, wc, w_out, x, g_post, g_next, *, tm=256):
    t, d = x.shape
    tm = min(tm, t)
    row = lambda w: pl.BlockSpec((tm, w), lambda i: (i, 0))
    gate = lambda br: pl.BlockSpec((tm, d), lambda i: (i, br))
    whole = lambda w: pl.BlockSpec(w.shape, lambda i: (0, 0), pipeline_mode=pl.Buffered(1))
    vec = pl.BlockSpec((1, d), lambda i: (0, 0))
    return pl.pallas_call(
        _merge_out_kernel,
        out_shape=(jax.ShapeDtypeStruct((t, d), F32), jax.ShapeDtypeStruct((t, d), BF16)),
        grid=(t // tm,),
        in_specs=[row(oa.shape[1]), row(ob.shape[1]), row(oc.shape[1]),
                  gate(0), gate(1), gate(2),
                  whole(wa), whole(wb), whole(wc), whole(w_out), row(d), vec, vec],
        out_specs=(row(d), row(d)),
        compiler_params=_params("parallel"),
        name="merge_out_residual",
    )(oa, ob, oc, gates, gates, gates, wa, wb, wc, w_out, x,
      g_post.reshape(1, d), g_next.reshape(1, d))


def _proj_res_kernel(a_ref, w_ref, x_ref, gp_ref, gn_ref, xo_ref, ho_ref, acc_ref, *, nk):
    kk = pl.program_id(1)
    part = _dot(a_ref[...], w_ref[...])

    @pl.when(kk == 0)
    def _():
        acc_ref[...] = part

    @pl.when(kk > 0)
    def _():
        acc_ref[...] += part

    @pl.when(kk == nk - 1)
    def _():
        x_new = x_ref[...] + _rms(acc_ref[...], gp_ref[...])
        xo_ref[...] = x_new
        ho_ref[...] = _rms(x_new, gn_ref[...]).astype(ho_ref.dtype)


def proj_residual(a, w, x, g_post, g_next, *, tm, tk):
    t, k = a.shape
    d = w.shape[1]
    tm, tk = min(tm, t), min(tk, k)
    nk = k // tk
    vec = pl.BlockSpec((1, d), lambda i, kk: (0, 0))
    w_mode = dict(pipeline_mode=pl.Buffered(1)) if nk == 1 else {}
    return pl.pallas_call(
        functools.partial(_proj_res_kernel, nk=nk),
        out_shape=(jax.ShapeDtypeStruct((t, d), F32), jax.ShapeDtypeStruct((t, d), BF16)),
        grid=(t // tm, nk),
        in_specs=[pl.BlockSpec((tm, tk), lambda i, kk: (i, kk)),
                  pl.BlockSpec((tk, d), lambda i, kk: (kk, 0), **w_mode),
                  pl.BlockSpec((tm, d), lambda i, kk: (i, 0)),
                  vec, vec],
        out_specs=(pl.BlockSpec((tm, d), lambda i, kk: (i, 0)),
                   pl.BlockSpec((tm, d), lambda i, kk: (i, 0))),
        scratch_shapes=[pltpu.VMEM((tm, d), F32)],
        compiler_params=_params("parallel", "arbitrary"),
        name="proj_residual",
    )(a, w, x, g_post.reshape(1, d), g_next.reshape(1, d))


def _swiglu_up_kernel(h_ref, wg_ref, wu_ref, o_ref):
    h = h_ref[...]
    g = _dot(h, wg_ref[...])
    u = _dot(h, wu_ref[...])
    o_ref[...] = (g * _sigmoid(g) * u).astype(o_ref.dtype)


def swiglu_up(h, wg, wu, *, tm=1024, tn=512):
    t, d = h.shape
    f = wg.shape[1]
    tm, tn = min(tm, t), min(tn, f)
    return pl.pallas_call(
        _swiglu_up_kernel,
        out_shape=jax.ShapeDtypeStruct((t, f), BF16),
        grid=(t // tm, f // tn),
        in_specs=[pl.BlockSpec((tm, d), lambda i, j: (i, 0)),
                  pl.BlockSpec((d, tn), lambda i, j: (0, j)),
                  pl.BlockSpec((d, tn), lambda i, j: (0, j))],
        out_specs=pl.BlockSpec((tm, tn), lambda i, j: (i, j)),
        compiler_params=_params("parallel", "parallel"),
        name="swiglu_up",
    )(h, wg, wu)


def _router_kernel(x_ref, g_ref, w_ref, idx_ref, wt_ref, h_ref):
    h = _rms(x_ref[...], g_ref[...])
    h_ref[...] = _pack_bf16_pairs(h)
    logits = jnp.dot(h, w_ref[...], precision=lax.Precision.HIGHEST,
                     preferred_element_type=F32)
    lane = lax.broadcasted_iota(jnp.int32, logits.shape, 1)
    lane_f = lane.astype(F32)
    lg = jnp.where(lane < N_EXPERTS, logits, -jnp.inf)
    m1 = jnp.max(lg, axis=-1, keepdims=True)
    i1 = jnp.min(jnp.where(lg == m1, lane_f, float(LANES)), axis=-1, keepdims=True)
    lg2 = jnp.where(lane_f == i1, -jnp.inf, lg)
    m2 = jnp.max(lg2, axis=-1, keepdims=True)
    i2 = jnp.min(jnp.where(lg2 == m2, lane_f, float(LANES)), axis=-1, keepdims=True)
    e2 = jnp.exp(m2 - m1)
    w1 = 1.0 / (1.0 + e2)
    w2 = e2 / (1.0 + e2)
    idx_ref[...] = jnp.where(lane == 0, i1, jnp.where(lane == 1, i2, 0.0)).astype(jnp.int32)
    wt_ref[...] = jnp.where(lane == 0, w1, jnp.where(lane == 1, w2, 0.0))


def router_top2(x, g, w_router, *, tm=512):
    t, d = x.shape
    tm = min(tm, t)
    wp = jnp.zeros((d, LANES), F32).at[:, :N_EXPERTS].set(w_router.astype(F32))
    out = pl.BlockSpec((tm, LANES), lambda i: (i, 0))
    idx, wt, h = pl.pallas_call(
        _router_kernel,
        out_shape=(jax.ShapeDtypeStruct((t, LANES), jnp.int32),
                   jax.ShapeDtypeStruct((t, LANES), F32),
                   jax.ShapeDtypeStruct((t, d // 2), jnp.uint32)),
        grid=(t // tm,),
        in_specs=[pl.BlockSpec((tm, d), lambda i: (i, 0)),
                  pl.BlockSpec((1, d), lambda i: (0, 0)),
                  pl.BlockSpec((d, LANES), lambda i: (0, 0))],
        out_specs=(out, out, pl.BlockSpec((tm, d // 2), lambda i: (i, 0))),
        compiler_params=_params("parallel"),
        name="router_top2",
    )(x, g.reshape(1, d), wp)
    return idx[:, :TOP_K], wt[:, :TOP_K], h


SC_INDEX_WINDOW = 128
SC_COPY_BYTES = 256 * 1024


def sc_gather_rows(src, idx):
    n = idx.shape[0]
    d = src.shape[1]
    batch = min(SC_INDEX_WINDOW, SC_COPY_BYTES // (d * src.dtype.itemsize))
    assert n % SC_INDEX_WINDOW == 0 and SC_INDEX_WINDOW % batch == 0
    mesh = plsc.VectorSubcoreMesh(core_axis_name="c", subcore_axis_name="s")
    dst = jnp.arange(n, dtype=jnp.int32).reshape(1, n)

    @pl.kernel(out_type=jax.ShapeDtypeStruct((n, d), src.dtype), mesh=mesh,
               scratch_types=[pltpu.VMEM((batch, d), src.dtype)])
    def gather(x_hbm, i_hbm, d_hbm, o_hbm, buf):
        def body(i_vmem, d_vmem):
            for s in range(SC_INDEX_WINDOW // batch):
                rows = pl.ds(s * batch, batch)
                pltpu.sync_copy(x_hbm.at[i_vmem.at[0, rows]], buf)
                pltpu.sync_copy(buf, o_hbm.at[d_vmem.at[0, rows]])

        window = pl.BlockSpec((1, SC_INDEX_WINDOW), lambda i: (0, i))
        pltpu.emit_pipeline(
            body, grid=(n // SC_INDEX_WINDOW,),
            in_specs=[window, window], out_specs=[],
            core_axis_name=("c", "s"),
            dimension_semantics=(pltpu.PARALLEL,),
        )(i_hbm, d_hbm)

    return gather(src, idx.reshape(1, n), dst)


def _moe_up_kernel(te_ref, na_ref, x_ref, wg_ref, wu_ref, *rest):
    o_ref = rest[-1]
    active = pl.program_id(1) < na_ref[0]

    @pl.when(active)
    def _():
        half = x_ref.shape[1]
        lo, hi = (v.astype(BF16) for v in _unpack_bf16_pairs(x_ref[...]))
        g = _dot(lo, wg_ref[0, :half]) + _dot(hi, wg_ref[0, half:])
        u = _dot(lo, wu_ref[0, :half]) + _dot(hi, wu_ref[0, half:])
        o_ref[...] = (g * _sigmoid(g) * u).astype(o_ref.dtype)

    @pl.when(jnp.logical_not(active))
    def _():
        o_ref[...] = jnp.zeros_like(o_ref)


def _moe_down_kernel(te_ref, na_ref, u_ref, wd_ref, o_ref):
    active = pl.program_id(1) < na_ref[0]

    @pl.when(active)
    def _():
        o_ref[...] = _pack_bf16_pairs(_dot(u_ref[...], wd_ref[0]))

    @pl.when(jnp.logical_not(active))
    def _():
        o_ref[...] = jnp.zeros_like(o_ref)


def _active_tile(i, na_ref):
    return jnp.maximum(jnp.minimum(i, na_ref[0] - 1), 0)


def moe_up(xs, tile_expert, n_active, wg, wu, u_prev, *, tile0, n_tiles, tm, tn=1024):
    d, f = wg.shape[1:]
    tn = min(tn, f)
    assert f % tn == 0
    in_specs = [pl.BlockSpec((tm, d // 2), lambda j, i, te, na: (_active_tile(i, na), 0)),
                pl.BlockSpec((1, d, tn), lambda j, i, te, na: (te[_active_tile(i, na)], 0, j)),
                pl.BlockSpec((1, d, tn), lambda j, i, te, na: (te[_active_tile(i, na)], 0, j))]
    args = [tile_expert, n_active, xs, wg, wu]
    aliases = {}
    if u_prev is not None:
        in_specs.append(pl.BlockSpec(memory_space=pl.ANY))
        args.append(u_prev)
        aliases = {len(args) - 1: 0}
    grid_spec = pltpu.PrefetchScalarGridSpec(
        num_scalar_prefetch=2,
        grid=(f // tn, xs.shape[0] // tm),
        in_specs=in_specs,
        out_specs=pl.BlockSpec((tm, tn), lambda j, i, te, na: (i + tile0, j)),
    )
    return pl.pallas_call(
        _moe_up_kernel,
        out_shape=jax.ShapeDtypeStruct((n_tiles * tm, f), BF16),
        grid_spec=grid_spec,
        input_output_aliases=aliases,
        compiler_params=_params("arbitrary", "arbitrary"),
        name="moe_up",
    )(*args)


MOE_DOWN_TN = 512


def moe_down(u, tile_expert, n_active, wd, *, tm):
    p, f = u.shape
    d = wd.shape[2]
    tn = min(MOE_DOWN_TN, d)
    grid_spec = pltpu.PrefetchScalarGridSpec(
        num_scalar_prefetch=2,
        grid=(d // tn, p // tm),
        in_specs=[pl.BlockSpec((tm, f), lambda j, i, te, na: (_active_tile(i, na), 0)),
                  pl.BlockSpec((1, f, tn), lambda j, i, te, na: (te[_active_tile(i, na)], 0, j))],
        out_specs=pl.BlockSpec((tm, tn // 2), lambda j, i, te, na: (i, j)),
    )
    return pl.pallas_call(
        _moe_down_kernel,
        out_shape=jax.ShapeDtypeStruct((p, d // 2), jnp.uint32),
        grid_spec=grid_spec,
        compiler_params=_params("arbitrary", "arbitrary"),
        name="moe_down",
    )(tile_expert, n_active, u, wd)


def _unpack_expert_rows(words, half):
    lo, hi = _unpack_bf16_pairs(words)
    parts = []
    for b in range(words.shape[1] // half):
        parts += [lo[:, b * half:(b + 1) * half], hi[:, b * half:(b + 1) * half]]
    return jnp.concatenate(parts, axis=1)


def _combine_kernel(ya_ref, yb_ref, w_ref, x_ref, g_ref, o_ref, *, half):
    w = w_ref[...]
    y = (w[:, 0:1] * _unpack_expert_rows(ya_ref[...], half)
         + w[:, 1:2] * _unpack_expert_rows(yb_ref[...], half))
    o_ref[...] = x_ref[...] + _rms(y, g_ref[...])


def combine_residual(y2, wts, x, g, *, tm=512):
    t, d = x.shape
    tm = min(tm, t)
    nt = t // tm
    blk = pl.BlockSpec((tm, d), lambda i: (i, 0))
    return pl.pallas_call(
        functools.partial(_combine_kernel, half=min(MOE_DOWN_TN, d) // 2),
        out_shape=jax.ShapeDtypeStruct((t, d), F32),
        grid=(nt,),
        in_specs=[pl.BlockSpec((tm, d // 2), lambda i: (i, 0)),
                  pl.BlockSpec((tm, d // 2), lambda i: (i + nt, 0)),
                  pl.BlockSpec((tm, TOP_K), lambda i: (i, 0)), blk,
                  pl.BlockSpec((1, d), lambda i: (0, 0))],
        out_specs=blk,
        compiler_params=_params("parallel"),
        name="combine_residual",
    )(y2, y2, wts, x, g.reshape(1, d))


def _alibi_slopes(n):
    return jnp.exp2(-8.0 * (jnp.arange(n, dtype=F32) + 1.0) / n)


def _token_mixer(x2, h, batch, seq, layer_idx, w_in, b_forget, b_gate, sinks,
                 lq1, lk1, lq2, lk2, subln, w_br_a, w_br_b, w_br_c, w_out,
                 g_post, g_next):
    t, d = x2.shape
    a_w = A_Q_HEADS * HEAD_DIM
    kv_w = A_KV_HEADS * HEAD_DIM
    b_w = B_HEADS * 2 * HEAD_DIM
    c_w = C_HEADS * HEAD_DIM
    n_attn = a_w + 2 * kv_w + 3 * b_w + 3 * c_w

    slot_cols = np.concatenate([np.arange(HEAD_DIM) + HEAD_DIM * hd for hd in A_SLOT_HEAD])
    a_end = a_w + 2 * kv_w
    b_end = a_end + 3 * b_w
    w_in_t = jnp.transpose(w_in, (2, 0, 1))
    rows = lambda lo, hi: w_in_t[lo:hi, layer_idx, :]
    w_attn_t = jnp.concatenate([rows(b_end, n_attn), rows(a_end, b_end),
                                rows(0, a_w)[slot_cols], rows(a_w, a_end)],
                               axis=0).astype(BF16)
    w_f_t = rows(n_attn, n_attn + C_HEADS)
    w_g_t = rows(n_attn + C_HEADS, w_in.shape[2]).astype(BF16)

    z3 = matmul(h, w_attn_t).reshape(batch, seq, n_attn)
    gates = matmul(h, w_g_t, bias=b_gate)
    c = forget_cumsum(h.reshape(batch, seq, d), w_f_t, b_forget)
    c4 = c[:, :C_HEADS].reshape(batch, C_HEADS, 1, seq)

    oc = fox_attention(z3, c4, q_col=0, k_col=C_HEADS // 2, v_col=C_HEADS)

    lam_init = 0.8 - 0.6 * math.exp(-0.3 * layer_idx)
    lam = (jnp.exp(jnp.sum(lq1.astype(F32) * lk1.astype(F32)))
           - jnp.exp(jnp.sum(lq2.astype(F32) * lk2.astype(F32))) + lam_init)
    b0 = 3 * c_w // LANES
    ob = diff_attention(z3, _alibi_slopes(B_HEADS), lam.reshape(1), subln,
                        q_col=b0, k_col=b0 + B_HEADS, v_col=b0 + 2 * B_HEADS,
                        out_scale=1.0 - lam_init)

    slot = np.array(A_SLOT_HEAD)
    a0 = 3 * c_w + 3 * b_w
    oa = swa_attention(z3, _alibi_slopes(A_Q_HEADS)[slot], sinks.astype(F32)[slot],
                       q_tile=a0 // a_w, k_tile=(a0 + a_w) // kv_w,
                       v_tile=(a0 + a_w) // kv_w + 1)

    return merge_out_residual(oa.reshape(t, a_w), ob.reshape(t, b_w), oc.reshape(t, c_w), gates,
                              w_br_a[slot_cols].astype(BF16), w_br_b.astype(BF16),
                              w_br_c.astype(BF16), w_out.astype(BF16), x2, g_post, g_next)


COUNT_BLOCK = 256
MOE_PIECES = 4


def _moe_dispatch(idx, tm):
    t = idx.shape[0]
    n_rows = t * TOP_K
    n_tiles = (n_rows + N_EXPERTS * (tm - 1)) // tm
    e_flat = idx.reshape(n_rows)
    onehot = e_flat[:, None] == jnp.arange(N_EXPERTS)[None, :]
    blk = min(COUNT_BLOCK, n_rows)
    oh = onehot.astype(F32).reshape(n_rows // blk, blk, N_EXPERTS)
    lower = (jnp.arange(blk)[:, None] >= jnp.arange(blk)[None, :]).astype(F32)
    within = jnp.einsum("ij,bjk->bik", lower, oh)
    before = jnp.cumsum(within[:, -1, :], axis=0) - within[:, -1, :]
    running = (within + before[:, None, :]).reshape(n_rows, N_EXPERTS)
    rank = jnp.sum(jnp.where(onehot, running - 1.0, 0.0), axis=1).astype(jnp.int32)
    counts = jnp.sum(onehot.astype(jnp.int32), axis=0)
    padded = ((counts + tm - 1) // tm) * tm
    ends = jnp.cumsum(padded)
    starts = ends - padded
    pos = starts[e_flat] + rank
    row_token = jnp.zeros((n_tiles * tm,), jnp.int32).at[pos].set(
        jnp.arange(n_rows, dtype=jnp.int32) // TOP_K)
    tile_start = jnp.arange(n_tiles, dtype=jnp.int32) * tm
    tile_expert = jnp.minimum(jnp.sum(tile_start[:, None] >= ends[None, :], axis=1),
                              N_EXPERTS - 1).astype(jnp.int32)
    n_active = (ends[-1] // tm).astype(jnp.int32).reshape(1)
    return pos.reshape(t, TOP_K), row_token, tile_expert, n_active


def _moe_ffn(x2, g_pre, g_post, w_router, e_gate, e_up, e_down, *, tm=512):
    idx, wts, h = router_top2(x2, g_pre, w_router)
    t = x2.shape[0]
    tm = min(tm, t)
    pos, row_token, tile_expert, n_active = _moe_dispatch(idx, tm)
    n_tiles = tile_expert.shape[0]
    bounds = [k * n_tiles // MOE_PIECES for k in range(MOE_PIECES + 1)]
    u = None
    for lo, hi in zip(bounds[:-1], bounds[1:]):
        xs = sc_gather_rows(h, row_token[lo * tm:hi * tm])
        u = moe_up(xs, tile_expert[lo:hi], jnp.clip(n_active - lo, 0, hi - lo), e_gate, e_up, u,
                   tile0=lo, n_tiles=n_tiles, tm=tm)
    ys = moe_down(u, tile_expert, n_active, e_down, tm=tm)
    y2 = sc_gather_rows(ys, pos.T.reshape(TOP_K * t))
    return combine_residual(y2, wts, x2, g_post)


def kernel(x, mix_pre_norm, w_in, b_forget, b_gate, attn_sinks, lam_q1, lam_k1, lam_q2, lam_k2, diff_subln, w_br_a, w_br_b, w_br_c, w_out, mix_post_norm, ffn_pre_norm, ffn_post_norm, dense_w_gate, dense_w_up, dense_w_down, w_router, moe_w_gate, moe_w_up, moe_w_down):
    batch, seq, d = x.shape
    depth = w_in.shape[0]
    x2 = x.reshape(batch * seq, d)
    h = rmsnorm_rows(x2, mix_pre_norm[0])
    for l in range(depth):
        x2, h = _token_mixer(x2, h, batch, seq, l, w_in, b_forget[l], b_gate[l],
                             attn_sinks[l], lam_q1[l], lam_k1[l], lam_q2[l], lam_k2[l],
                             diff_subln[l], w_br_a[l], w_br_b[l], w_br_c[l], w_out[l],
                             mix_post_norm[l], ffn_pre_norm[l])
        i = l // 2
        if l % 2 == 0:
            u = swiglu_up(h, dense_w_gate[i].astype(BF16), dense_w_up[i].astype(BF16))
            g_next = mix_pre_norm[l + 1] if l + 1 < depth else jnp.ones((d,), F32)
            f = dense_w_down.shape[1]
            x2, h = proj_residual(u, dense_w_down[i].astype(BF16), x2, ffn_post_norm[l],
                                  g_next, tm=256, tk=f)
        else:
            x2 = _moe_ffn(x2, ffn_pre_norm[l], ffn_post_norm[l], w_router[i],
                          moe_w_gate[i], moe_w_up[i], moe_w_down[i])
            if l + 1 < depth:
                h = rmsnorm_rows(x2, mix_pre_norm[l + 1])
    return x2.reshape(batch, seq, d)
```

```python
import functools
import math

import jax
import jax.numpy as jnp
import numpy as np
from jax import lax
from jax.experimental import pallas as pl
from jax.experimental.pallas import tpu as pltpu
from jax.experimental.pallas import tpu_sc as plsc

F32 = jnp.float32
BF16 = jnp.bfloat16

HEAD_DIM = 64
LANES = 128
A_Q_HEADS = 12
A_KV_HEADS = 4
A_GROUP = A_Q_HEADS // A_KV_HEADS
WINDOW = 128
B_HEADS = 4
C_HEADS = 12
N_BRANCH = 3
N_EXPERTS = 8
TOP_K = 2
RMS_EPS = 1e-6
NEG_INF = -1e30
VMEM_LIMIT = 56 * 1024 * 1024

A_SLOT_HEAD = (0, 3, 1, 4, 2, 5, 6, 9, 7, 10, 8, 11)


def _params(*sem):
    return pltpu.CompilerParams(dimension_semantics=sem, vmem_limit_bytes=VMEM_LIMIT)


def _rms(x, g):
    var = jnp.mean(x * x, axis=-1, keepdims=True)
    return x * lax.rsqrt(var + RMS_EPS) * g


def _sigmoid(x):
    return 0.5 * jnp.tanh(0.5 * x) + 0.5


def _pack_bf16_pairs(x):
    bits = pltpu.bitcast(x.astype(BF16).astype(F32), jnp.uint32)
    c = x.shape[1] // 2
    return (bits[:, :c] >> 16) | bits[:, c:]


def _unpack_bf16_pairs(words):
    lo = pltpu.bitcast(words << 16, F32)
    hi = pltpu.bitcast(words & jnp.uint32(0xFFFF0000), F32)
    return lo, hi


def _dot(a, b):
    return jnp.dot(a, b, preferred_element_type=F32)


def _dot_nt(a, b):
    return lax.dot_general(a, b, (((1,), (1,)), ((), ())), preferred_element_type=F32)


def _round_kernel(*refs):
    n = len(refs) // 2
    for src, dst in zip(refs[:n], refs[n:]):
        dst[...] = src[...].astype(dst.dtype)


ROUND_BLOCK_ELEMS = 2 * 1024 * 1024


def round_to_bf16(*ws):
    shape = ws[0].shape
    flat = [w.reshape(-1, shape[-1]) for w in ws]
    n, c = flat[0].shape
    rows = min(n, 1 << ((ROUND_BLOCK_ELEMS // c).bit_length() - 1))
    assert n % rows == 0 and all(w.shape == shape for w in ws)
    blk = pl.BlockSpec((rows, c), lambda i: (i, 0))
    outs = pl.pallas_call(
        _round_kernel,
        out_shape=tuple(jax.ShapeDtypeStruct((n, c), BF16) for _ in ws),
        grid=(n // rows,),
        in_specs=[blk] * len(ws),
        out_specs=tuple(blk for _ in ws),
        compiler_params=_params("parallel"),
        name="round_to_bf16",
    )(*flat)
    return [o.reshape(shape) for o in outs]


def _rmsnorm_kernel(x_ref, g_ref, o_ref):
    o_ref[...] = _rms(x_ref[...], g_ref[...]).astype(o_ref.dtype)


def rmsnorm_rows(x, g, *, tm=512):
    t, d = x.shape
    tm = min(tm, t)
    return pl.pallas_call(
        _rmsnorm_kernel,
        out_shape=jax.ShapeDtypeStruct((t, d), BF16),
        grid=(t // tm,),
        in_specs=[pl.BlockSpec((tm, d), lambda i: (i, 0)),
                  pl.BlockSpec((1, d), lambda i: (0, 0))],
        out_specs=pl.BlockSpec((tm, d), lambda i: (i, 0)),
        compiler_params=_params("parallel"),
        name="rmsnorm_rows",
    )(x, g.reshape(1, d))


def _mm_kernel(a_ref, bt_ref, o_ref):
    o_ref[...] = _dot_nt(a_ref[...], bt_ref[...]).astype(o_ref.dtype)


def _mm_gate_kernel(a_ref, bt_ref, bias_ref, o_ref):
    z = _dot_nt(a_ref[...], bt_ref[...]) + bias_ref[...]
    o_ref[...] = _sigmoid(z).astype(o_ref.dtype)


def matmul(a, bt, *, bias=None, out_dtype=BF16, tm=1024, tn=1024):
    m, k = a.shape
    n = bt.shape[0]
    tm, tn = min(tm, m), min(tn, n)
    assert m % tm == 0 and n % tn == 0
    in_specs = [pl.BlockSpec((tm, k), lambda i, j: (i, 0)),
                pl.BlockSpec((tn, k), lambda i, j: (j, 0))]
    args = [a, bt]
    kern = _mm_kernel
    if bias is not None:
        in_specs.append(pl.BlockSpec((1, tn), lambda i, j: (0, j)))
        args.append(bias.reshape(1, n).astype(F32))
        kern = _mm_gate_kernel
    return pl.pallas_call(
        kern,
        out_shape=jax.ShapeDtypeStruct((m, n), out_dtype),
        grid=(m // tm, n // tn),
        in_specs=in_specs,
        out_specs=pl.BlockSpec((tm, tn), lambda i, j: (i, j)),
        compiler_params=_params("parallel", "parallel"),
        name="matmul_gate" if bias is not None else "matmul",
    )(*args)


def _forget_kernel(h_ref, wt_ref, b_ref, c_ref, *, seq, blk):
    z = _dot_nt(wt_ref[...], h_ref[0]) + b_ref[...]
    log_f = jnp.minimum(z, 0.0) - jnp.log1p(jnp.exp(-jnp.abs(z)))
    r = lax.broadcasted_iota(jnp.int32, (blk, blk), 0)
    c = lax.broadcasted_iota(jnp.int32, (blk, blk), 1)
    upper = (r <= c).astype(F32)
    carry = jnp.zeros((LANES, 1), F32)
    for j in range(seq // blk):
        part = lax.dot_general(log_f[:, j * blk:(j + 1) * blk], upper,
                               (((1,), (0,)), ((), ())),
                               precision=lax.Precision.HIGHEST,
                               preferred_element_type=F32) + carry
        c_ref[0, :, j * blk:(j + 1) * blk] = part
        carry = part[:, blk - 1:blk]


def forget_cumsum(h3, w_ft, b_f):
    b, s, d = h3.shape
    nh = w_ft.shape[0]
    wt = jnp.zeros((LANES, d), BF16).at[:nh].set(w_ft.astype(BF16))
    bias = jnp.zeros((LANES, 1), F32).at[:nh, 0].set(b_f.astype(F32))
    blk = min(256, s)
    return pl.pallas_call(
        functools.partial(_forget_kernel, seq=s, blk=blk),
        out_shape=jax.ShapeDtypeStruct((b, LANES, s), F32),
        grid=(b,),
        in_specs=[pl.BlockSpec((1, s, d), lambda i: (i, 0, 0)),
                  pl.BlockSpec((LANES, d), lambda i: (0, 0)),
                  pl.BlockSpec((LANES, 1), lambda i: (0, 0))],
        out_specs=pl.BlockSpec((1, LANES, s), lambda i: (i, 0, 0)),
        compiler_params=_params("parallel"),
        name="forget_cumsum",
    )(h3, wt, bias)


def _half_mask(tq, half):
    lane = lax.broadcasted_iota(jnp.int32, (tq, LANES), 1)
    return (lane >= HEAD_DIM * half) & (lane < HEAD_DIM * (half + 1))


def _causal_logits(qh, k_ref, lanes, qi, tq, bias):
    n0 = qi * tq
    r = lax.broadcasted_iota(jnp.int32, (tq, tq), 0)
    c = lax.broadcasted_iota(jnp.int32, (tq, tq), 1)
    diag = _dot_nt(qh, k_ref[0, n0:n0 + tq, lanes]) + bias(n0, tq)
    parts = [jnp.where(r >= c, diag, NEG_INF)]
    if qi > 0:
        parts.append(_dot_nt(qh, k_ref[0, :n0, lanes]) + bias(0, n0))
    return parts


def _softmax_parts(parts):
    m = functools.reduce(jnp.maximum, [s.max(axis=-1, keepdims=True) for s in parts])
    exps = [jnp.exp(s - m) for s in parts]
    denom = functools.reduce(jnp.add, [e.sum(axis=-1, keepdims=True) for e in exps])
    return exps, denom


def _value_parts(v_ref, lanes, qi, tq):
    n0 = qi * tq
    return [v_ref[0, n0:n0 + tq, lanes]] + ([v_ref[0, :n0, lanes]] if qi > 0 else [])


def _weighted_values(weights, values):
    return functools.reduce(jnp.add, [_dot(w.astype(v.dtype), v) for w, v in zip(weights, values)])


ATTN_TILES_PER_STEP = 2


def _dispatch_on_step(tile, nq, tq):
    tps = min(ATTN_TILES_PER_STEP, nq)

    def step_body(s):
        for sub in range(tps):
            tile(qi=s * tps + sub, rows=slice(sub * tq, (sub + 1) * tq))

    step = pl.program_id(2)
    for s in range(nq // tps):
        pl.when(step == s)(functools.partial(step_body, s))


def _fox_kernel(q_ref, k_ref, v_ref, c_ref, o_ref, *, tq, nq, npair):
    masks = [_half_mask(tq, h) for h in range(2)]

    def tile(qi, rows):
        for p in range(npair):
            lanes = slice(p * LANES, (p + 1) * LANES)
            q = q_ref[0, rows, lanes] * jnp.asarray(HEAD_DIM ** -0.5, q_ref.dtype)
            values = _value_parts(v_ref, lanes, qi, tq)
            outs = []
            for h in range(2):
                qh = jnp.where(masks[h], q, jnp.zeros_like(q))
                bias = lambda start, size: -c_ref[0, 2 * p + h, :, start:start + size]
                exps, denom = _softmax_parts(_causal_logits(qh, k_ref, lanes, qi, tq, bias))
                outs.append(_weighted_values(exps, values) / denom)
            o_ref[0, rows, lanes] = jnp.where(masks[0], outs[0], outs[1]).astype(o_ref.dtype)

    _dispatch_on_step(tile, nq, tq)


def _diff_kernel(slope_ref, lam_ref, q_ref, k_ref, v_ref, g_ref, o_ref, *,
                 tq, nq, nhead, out_scale):
    lam = lam_ref[0]
    masks = [_half_mask(tq, h) for h in range(2)]

    def tile(qi, rows):
        for j in range(nhead):
            lanes = slice(j * LANES, (j + 1) * LANES)
            slope = slope_ref[pl.program_id(1) * nhead + j]
            q = q_ref[0, rows, lanes] * jnp.asarray(HEAD_DIM ** -0.5, q_ref.dtype)

            def bias(start, size):
                pos = start + lax.broadcasted_iota(jnp.int32, (1, size), 1)
                return slope * pos.astype(F32)

            maps = []
            for h in range(2):
                qh = jnp.where(masks[h], q, jnp.zeros_like(q))
                maps.append(_softmax_parts(_causal_logits(qh, k_ref, lanes, qi, tq, bias)))
            (e0, d0), (e1, d1) = maps
            w0, w1 = 1.0 / d0, lam / d1
            weights = [a * w0 - b * w1 for a, b in zip(e0, e1)]
            o = _weighted_values(weights, _value_parts(v_ref, lanes, qi, tq))
            o_ref[0, rows, lanes] = (_rms(o, g_ref[...]) * out_scale).astype(o_ref.dtype)

    _dispatch_on_step(tile, nq, tq)


def fox_attention(z3, c4, *, q_col, k_col, v_col, tq=256, npair=3):
    b, s, _ = z3.shape
    tq = min(tq, s)
    rows = tq * min(ATTN_TILES_PER_STEP, s // tq)
    w = npair * LANES
    groups = C_HEADS // 2 // npair
    assert q_col % npair == 0 and k_col % npair == 0 and v_col % npair == 0
    return pl.pallas_call(
        functools.partial(_fox_kernel, tq=tq, nq=s // tq, npair=npair),
        out_shape=jax.ShapeDtypeStruct((b, s, C_HEADS // 2 * LANES), BF16),
        grid=(b, groups, s // rows),
        in_specs=[pl.BlockSpec((1, rows, w), lambda bi, g, qi: (bi, qi, q_col // npair + g)),
                  pl.BlockSpec((1, s, w), lambda bi, g, qi: (bi, 0, k_col // npair + g)),
                  pl.BlockSpec((1, s, w), lambda bi, g, qi: (bi, 0, v_col // npair + g)),
                  pl.BlockSpec((1, 2 * npair, 1, s), lambda bi, g, qi: (bi, g, 0, 0))],
        out_specs=pl.BlockSpec((1, rows, w), lambda bi, g, qi: (bi, qi, g)),
        compiler_params=_params("parallel", "parallel", "arbitrary"),
        name="fox_attention",
    )(z3, z3, z3, c4)


def diff_attention(z3, slopes, lam, subln, *, q_col, k_col, v_col, out_scale, tq=256, nhead=2):
    b, s, _ = z3.shape
    tq = min(tq, s)
    rows = tq * min(ATTN_TILES_PER_STEP, s // tq)
    w = nhead * LANES
    assert q_col % nhead == 0 and k_col % nhead == 0 and v_col % nhead == 0
    smem = pl.BlockSpec(memory_space=pltpu.SMEM)
    return pl.pallas_call(
        functools.partial(_diff_kernel, tq=tq, nq=s // tq, nhead=nhead, out_scale=out_scale),
        out_shape=jax.ShapeDtypeStruct((b, s, B_HEADS * LANES), BF16),
        grid=(b, B_HEADS // nhead, s // rows),
        in_specs=[smem, smem,
                  pl.BlockSpec((1, rows, w), lambda bi, g, qi: (bi, qi, q_col // nhead + g)),
                  pl.BlockSpec((1, s, w), lambda bi, g, qi: (bi, 0, k_col // nhead + g)),
                  pl.BlockSpec((1, s, w), lambda bi, g, qi: (bi, 0, v_col // nhead + g)),
                  pl.BlockSpec((1, LANES), lambda bi, g, qi: (0, 0))],
        out_specs=pl.BlockSpec((1, rows, w), lambda bi, g, qi: (bi, qi, g)),
        compiler_params=_params("parallel", "parallel", "arbitrary"),
        name="diff_attention",
    )(slopes, lam, z3, z3, z3, subln.reshape(1, LANES).astype(F32))


def _swa_kernel(slope_ref, sink_ref, q_ref, k_ref, v_ref, o_ref, *, blk, nsub):
    r = lax.broadcasted_iota(jnp.int32, (blk, 2 * blk), 0)
    c = lax.broadcasted_iota(jnp.int32, (blk, 2 * blk), 1)
    dist = r + blk - c
    in_window = (dist >= 0) & (dist < WINDOW)
    dist_f = dist.astype(F32)
    scale = jnp.asarray(HEAD_DIM ** -0.5, q_ref.dtype)
    masks = [_half_mask(blk, h) for h in range(2)]
    for sub in range(nsub):
        _swa_block(slope_ref, sink_ref, q_ref, k_ref, v_ref, o_ref, pl.program_id(1) * nsub + sub,
                   slice(sub * blk, (sub + 1) * blk), in_window, c, dist_f, scale, masks, blk)


def _swa_block(slope_ref, sink_ref, q_ref, k_ref, v_ref, o_ref, qi, rows,
               in_window, c, dist_f, scale, masks, blk):
    cur = pl.multiple_of(qi * blk, blk)
    prev = pl.multiple_of(jnp.maximum(qi - 1, 0) * blk, blk)
    keep = in_window & ((qi > 0) | (c >= blk))
    for tile in range(A_Q_HEADS // 2):
        q = q_ref[0, rows, tile * LANES:(tile + 1) * LANES] * scale
        halves = []
        for half in range(2):
            slot = 2 * tile + half
            kv_tile = (A_SLOT_HEAD[slot] // A_GROUP) // 2
            lanes = slice(kv_tile * LANES, (kv_tile + 1) * LANES)
            k = jnp.concatenate([k_ref[0, pl.ds(prev, blk), lanes],
                                 k_ref[0, pl.ds(cur, blk), lanes]], axis=0)
            v = jnp.concatenate([v_ref[0, pl.ds(prev, blk), lanes],
                                 v_ref[0, pl.ds(cur, blk), lanes]], axis=0)
            qh = jnp.where(masks[half], q, jnp.zeros_like(q))
            s = _dot_nt(qh, k) - slope_ref[slot] * dist_f
            s = jnp.where(keep, s, NEG_INF)
            sink = sink_ref[slot]
            m = jnp.maximum(jnp.max(s, axis=-1, keepdims=True), sink)
            p = jnp.exp(s - m)
            denom = jnp.sum(p, axis=-1, keepdims=True) + jnp.exp(sink - m)
            halves.append(_dot(p.astype(v.dtype), v) / denom)
        o_ref[0, rows, tile * LANES:(tile + 1) * LANES] = jnp.where(
            masks[0], halves[0], halves[1]).astype(o_ref.dtype)


SWA_BLOCKS_PER_STEP = 4


def swa_attention(z3, slopes, sinks, *, q_tile, k_tile, v_tile):
    b, s, _ = z3.shape
    blk = WINDOW
    nsub = min(SWA_BLOCKS_PER_STEP, s // blk)
    qw = A_Q_HEADS * HEAD_DIM
    kw = A_KV_HEADS * HEAD_DIM
    smem = pl.BlockSpec(memory_space=pltpu.SMEM)
    return pl.pallas_call(
        functools.partial(_swa_kernel, blk=blk, nsub=nsub),
        out_shape=jax.ShapeDtypeStruct((b, s, qw), BF16),
        grid=(b, s // (blk * nsub)),
        in_specs=[smem, smem,
                  pl.BlockSpec((1, blk * nsub, qw), lambda bi, qi: (bi, qi, q_tile)),
                  pl.BlockSpec((1, s, kw), lambda bi, qi: (bi, 0, k_tile)),
                  pl.BlockSpec((1, s, kw), lambda bi, qi: (bi, 0, v_tile))],
        out_specs=pl.BlockSpec((1, blk * nsub, qw), lambda bi, qi: (bi, qi, 0)),
        compiler_params=_params("parallel", "parallel"),
        name="swa_attention",
    )(slopes, sinks, z3, z3, z3)


def _merge_out_kernel(oa_ref, ob_ref, oc_ref, ga_ref, gb_ref, gc_ref, wa_ref, wb_ref, wc_ref,
                      wo_ref, x_ref, gp_ref, gn_ref, xo_ref, ho_ref):
    merged = ga_ref[...].astype(F32) * _dot(oa_ref[...], wa_ref[...])
    merged += gb_ref[...].astype(F32) * _dot(ob_ref[...], wb_ref[...])
    merged += gc_ref[...].astype(F32) * _dot(oc_ref[...], wc_ref[...])
    m = _dot(merged.astype(wo_ref.dtype), wo_ref[...])
    x_new = x_ref[...] + _rms(m, gp_ref[...])
    xo_ref[...] = x_new
    ho_ref[...] = _rms(x_new, gn_ref[...]).astype(ho_ref.dtype)


def merge_out_residual(oa, ob, oc, gates, wa, wb, wc, w_out, x, g_post, g_next, *, tm=256):
    t, d = x.shape
    tm = min(tm, t)
    row = lambda w: pl.BlockSpec((tm, w), lambda i: (i, 0))
    gate = lambda br: pl.BlockSpec((tm, d), lambda i: (i, br))
    whole = lambda w: pl.BlockSpec(w.shape, lambda i: (0, 0), pipeline_mode=pl.Buffered(1))
    vec = pl.BlockSpec((1, d), lambda i: (0, 0))
    return pl.pallas_call(
        _merge_out_kernel,
        out_shape=(jax.ShapeDtypeStruct((t, d), F32), jax.ShapeDtypeStruct((t, d), BF16)),
        grid=(t // tm,),
        in_specs=[row(oa.shape[1]), row(ob.shape[1]), row(oc.shape[1]),
                  gate(0), gate(1), gate(2),
                  whole(wa), whole(wb), whole(wc), whole(w_out), row(d), vec, vec],
        out_specs=(row(d), row(d)),
        compiler_params=_params("parallel"),
        name="merge_out_residual",
    )(oa, ob, oc, gates, gates, gates, wa, wb, wc, w_out, x,
      g_post.reshape(1, d), g_next.reshape(1, d))


def _proj_res_kernel(a_ref, w_ref, x_ref, gp_ref, gn_ref, xo_ref, ho_ref, acc_ref, *, nk):
    kk = pl.program_id(1)
    part = _dot(a_ref[...], w_ref[...])

    @pl.when(kk == 0)
    def _():
        acc_ref[...] = part

    @pl.when(kk > 0)
    def _():
        acc_ref[...] += part

    @pl.when(kk == nk - 1)
    def _():
        x_new = x_ref[...] + _rms(acc_ref[...], gp_ref[...])
        xo_ref[...] = x_new
        ho_ref[...] = _rms(x_new, gn_ref[...]).astype(ho_ref.dtype)


def proj_residual(a, w, x, g_post, g_next, *, tm, tk):
    t, k = a.shape
    d = w.shape[1]
    tm, tk = min(tm, t), min(tk, k)
    nk = k // tk
    vec = pl.BlockSpec((1, d), lambda i, kk: (0, 0))
    w_mode = dict(pipeline_mode=pl.Buffered(1)) if nk == 1 else {}
    return pl.pallas_call(
        functools.partial(_proj_res_kernel, nk=nk),
        out_shape=(jax.ShapeDtypeStruct((t, d), F32), jax.ShapeDtypeStruct((t, d), BF16)),
        grid=(t // tm, nk),
        in_specs=[pl.BlockSpec((tm, tk), lambda i, kk: (i, kk)),
                  pl.BlockSpec((tk, d), lambda i, kk: (kk, 0), **w_mode),
                  pl.BlockSpec((tm, d), lambda i, kk: (i, 0)),
                  vec, vec],
        out_specs=(pl.BlockSpec((tm, d), lambda i, kk: (i, 0)),
                   pl.BlockSpec((tm, d), lambda i, kk: (i, 0))),
        scratch_shapes=[pltpu.VMEM((tm, d), F32)],
        compiler_params=_params("parallel", "arbitrary"),
        name="proj_residual",
    )(a, w, x, g_post.reshape(1, d), g_next.reshape(1, d))


def _swiglu_up_kernel(h_ref, wg_ref, wu_ref, o_ref):
    h = h_ref[...]
    g = _dot(h, wg_ref[...])
    u = _dot(h, wu_ref[...])
    o_ref[...] = (g * _sigmoid(g) * u).astype(o_ref.dtype)


def swiglu_up(h, wg, wu, *, tm=1024, tn=512):
    t, d = h.shape
    f = wg.shape[1]
    tm, tn = min(tm, t), min(tn, f)
    return pl.pallas_call(
        _swiglu_up_kernel,
        out_shape=jax.ShapeDtypeStruct((t, f), BF16),
        grid=(t // tm, f // tn),
        in_specs=[pl.BlockSpec((tm, d), lambda i, j: (i, 0)),
                  pl.BlockSpec((d, tn), lambda i, j: (0, j)),
                  pl.BlockSpec((d, tn), lambda i, j: (0, j))],
        out_specs=pl.BlockSpec((tm, tn), lambda i, j: (i, j)),
        compiler_params=_params("parallel", "parallel"),
        name="swiglu_up",
    )(h, wg, wu)


def _router_kernel(x_ref, g_ref, w_ref, idx_ref, wt_ref, h_ref):
    h = _rms(x_ref[...], g_ref[...])
    h_ref[...] = _pack_bf16_pairs(h)
    logits = jnp.dot(h, w_ref[...], precision=lax.Precision.HIGHEST,
                     preferred_element_type=F32)
    lane = lax.broadcasted_iota(jnp.int32, logits.shape, 1)
    lane_f = lane.astype(F32)
    lg = jnp.where(lane < N_EXPERTS, logits, -jnp.inf)
    m1 = jnp.max(lg, axis=-1, keepdims=True)
    i1 = jnp.min(jnp.where(lg == m1, lane_f, float(LANES)), axis=-1, keepdims=True)
    lg2 = jnp.where(lane_f == i1, -jnp.inf, lg)
    m2 = jnp.max(lg2, axis=-1, keepdims=True)
    i2 = jnp.min(jnp.where(lg2 == m2, lane_f, float(LANES)), axis=-1, keepdims=True)
    e2 = jnp.exp(m2 - m1)
    w1 = 1.0 / (1.0 + e2)
    w2 = e2 / (1.0 + e2)
    idx_ref[...] = jnp.where(lane == 0, i1, jnp.where(lane == 1, i2, 0.0)).astype(jnp.int32)
    wt_ref[...] = jnp.where(lane == 0, w1, jnp.where(lane == 1, w2, 0.0))


def router_top2(x, g, w_router, *, tm=512):
    t, d = x.shape
    tm = min(tm, t)
    wp = jnp.zeros((d, LANES), F32).at[:, :N_EXPERTS].set(w_router.astype(F32))
    out = pl.BlockSpec((tm, LANES), lambda i: (i, 0))
    idx, wt, h = pl.pallas_call(
        _router_kernel,
        out_shape=(jax.ShapeDtypeStruct((t, LANES), jnp.int32),
                   jax.ShapeDtypeStruct((t, LANES), F32),
                   jax.ShapeDtypeStruct((t, d // 2), jnp.uint32)),
        grid=(t // tm,),
        in_specs=[pl.BlockSpec((tm, d), lambda i: (i, 0)),
                  pl.BlockSpec((1, d), lambda i: (0, 0)),
                  pl.BlockSpec((d, LANES), lambda i: (0, 0))],
        out_specs=(out, out, pl.BlockSpec((tm, d // 2), lambda i: (i, 0))),
        compiler_params=_params("parallel"),
        name="router_top2",
    )(x, g.reshape(1, d), wp)
    return idx[:, :TOP_K], wt[:, :TOP_K], h


SC_INDEX_WINDOW = 128
SC_COPY_BYTES = 256 * 1024


def sc_gather_rows(src, idx):
    n = idx.shape[0]
    d = src.shape[1]
    batch = min(SC_INDEX_WINDOW, SC_COPY_BYTES // (d * src.dtype.itemsize))
    assert n % SC_INDEX_WINDOW == 0 and SC_INDEX_WINDOW % batch == 0
    mesh = plsc.VectorSubcoreMesh(core_axis_name="c", subcore_axis_name="s")
    dst = jnp.arange(n, dtype=jnp.int32).reshape(1, n)

    @pl.kernel(out_type=jax.ShapeDtypeStruct((n, d), src.dtype), mesh=mesh,
               scratch_types=[pltpu.VMEM((batch, d), src.dtype)])
    def gather(x_hbm, i_hbm, d_hbm, o_hbm, buf):
        def body(i_vmem, d_vmem):
            for s in range(SC_INDEX_WINDOW // batch):
                rows = pl.ds(s * batch, batch)
                pltpu.sync_copy(x_hbm.at[i_vmem.at[0, rows]], buf)
                pltpu.sync_copy(buf, o_hbm.at[d_vmem.at[0, rows]])

        window = pl.BlockSpec((1, SC_INDEX_WINDOW), lambda i: (0, i))
        pltpu.emit_pipeline(
            body, grid=(n // SC_INDEX_WINDOW,),
            in_specs=[window, window], out_specs=[],
            core_axis_name=("c", "s"),
            dimension_semantics=(pltpu.PARALLEL,),
        )(i_hbm, d_hbm)

    return gather(src, idx.reshape(1, n), dst)


def _moe_up_kernel(te_ref, na_ref, x_ref, wg_ref, wu_ref, *rest):
    o_ref = rest[-1]
    active = pl.program_id(1) < na_ref[0]

    @pl.when(active)
    def _():
        half = x_ref.shape[1]
        lo, hi = (v.astype(BF16) for v in _unpack_bf16_pairs(x_ref[...]))
        g = _dot(lo, wg_ref[0, :half]) + _dot(hi, wg_ref[0, half:])
        u = _dot(lo, wu_ref[0, :half]) + _dot(hi, wu_ref[0, half:])
        o_ref[...] = (g * _sigmoid(g) * u).astype(o_ref.dtype)

    @pl.when(jnp.logical_not(active))
    def _():
        o_ref[...] = jnp.zeros_like(o_ref)


def _moe_down_kernel(te_ref, na_ref, u_ref, wd_ref, o_ref):
    active = pl.program_id(1) < na_ref[0]

    @pl.when(active)
    def _():
        o_ref[...] = _pack_bf16_pairs(_dot(u_ref[...], wd_ref[0]))

    @pl.when(jnp.logical_not(active))
    def _():
        o_ref[...] = jnp.zeros_like(o_ref)


def _active_tile(i, na_ref):
    return jnp.maximum(jnp.minimum(i, na_ref[0] - 1), 0)


def moe_up(xs, tile_expert, n_active, wg, wu, u_prev, *, tile0, n_tiles, tm, tn=1024):
    d, f = wg.shape[1:]
    tn = min(tn, f)
    assert f % tn == 0
    in_specs = [pl.BlockSpec((tm, d // 2), lambda j, i, te, na: (_active_tile(i, na), 0)),
                pl.BlockSpec((1, d, tn), lambda j, i, te, na: (te[_active_tile(i, na)], 0, j)),
                pl.BlockSpec((1, d, tn), lambda j, i, te, na: (te[_active_tile(i, na)], 0, j))]
    args = [tile_expert, n_active, xs, wg, wu]
    aliases = {}
    if u_prev is not None:
        in_specs.append(pl.BlockSpec(memory_space=pl.ANY))
        args.append(u_prev)
        aliases = {len(args) - 1: 0}
    grid_spec = pltpu.PrefetchScalarGridSpec(
        num_scalar_prefetch=2,
        grid=(f // tn, xs.shape[0] // tm),
        in_specs=in_specs,
        out_specs=pl.BlockSpec((tm, tn), lambda j, i, te, na: (i + tile0, j)),
    )
    return pl.pallas_call(
        _moe_up_kernel,
        out_shape=jax.ShapeDtypeStruct((n_tiles * tm, f), BF16),
        grid_spec=grid_spec,
        input_output_aliases=aliases,
        compiler_params=_params("arbitrary", "arbitrary"),
        name="moe_up",
    )(*args)


MOE_DOWN_TN = 512


def moe_down(u, tile_expert, n_active, wd, *, tm):
    p, f = u.shape
    d = wd.shape[2]
    tn = min(MOE_DOWN_TN, d)
    grid_spec = pltpu.PrefetchScalarGridSpec(
        num_scalar_prefetch=2,
        grid=(d // tn, p // tm),
        in_specs=[pl.BlockSpec((tm, f), lambda j, i, te, na: (_active_tile(i, na), 0)),
                  pl.BlockSpec((1, f, tn), lambda j, i, te, na: (te[_active_tile(i, na)], 0, j))],
        out_specs=pl.BlockSpec((tm, tn // 2), lambda j, i, te, na: (i, j)),
    )
    return pl.pallas_call(
        _moe_down_kernel,
        out_shape=jax.ShapeDtypeStruct((p, d // 2), jnp.uint32),
        grid_spec=grid_spec,
        compiler_params=_params("arbitrary", "arbitrary"),
        name="moe_down",
    )(tile_expert, n_active, u, wd)


def _unpack_expert_rows(words, half):
    lo, hi = _unpack_bf16_pairs(words)
    parts = []
    for b in range(words.shape[1] // half):
        parts += [lo[:, b * half:(b + 1) * half], hi[:, b * half:(b + 1) * half]]
    return jnp.concatenate(parts, axis=1)


def _combine_kernel(ya_ref, yb_ref, w_ref, x_ref, g_ref, o_ref, *, half):
    w = w_ref[...]
    y = (w[:, 0:1] * _unpack_expert_rows(ya_ref[...], half)
         + w[:, 1:2] * _unpack_expert_rows(yb_ref[...], half))
    o_ref[...] = x_ref[...] + _rms(y, g_ref[...])


def combine_residual(y2, wts, x, g, *, tm=512):
    t, d = x.shape
    tm = min(tm, t)
    nt = t // tm
    blk = pl.BlockSpec((tm, d), lambda i: (i, 0))
    return pl.pallas_call(
        functools.partial(_combine_kernel, half=min(MOE_DOWN_TN, d) // 2),
        out_shape=jax.ShapeDtypeStruct((t, d), F32),
        grid=(nt,),
        in_specs=[pl.BlockSpec((tm, d // 2), lambda i: (i, 0)),
                  pl.BlockSpec((tm, d // 2), lambda i: (i + nt, 0)),
                  pl.BlockSpec((tm, TOP_K), lambda i: (i, 0)), blk,
                  pl.BlockSpec((1, d), lambda i: (0, 0))],
        out_specs=blk,
        compiler_params=_params("parallel"),
        name="combine_residual",
    )(y2, y2, wts, x, g.reshape(1, d))


def _alibi_slopes(n):
    return jnp.exp2(-8.0 * (jnp.arange(n, dtype=F32) + 1.0) / n)


def _token_mixer(x2, h, batch, seq, layer_idx, w_in, b_forget, b_gate, sinks,
                 lq1, lk1, lq2, lk2, subln, w_br_a, w_br_b, w_br_c, w_out,
                 g_post, g_next):
    t, d = x2.shape
    a_w = A_Q_HEADS * HEAD_DIM
    kv_w = A_KV_HEADS * HEAD_DIM
    b_w = B_HEADS * 2 * HEAD_DIM
    c_w = C_HEADS * HEAD_DIM
    n_attn = a_w + 2 * kv_w + 3 * b_w + 3 * c_w

    slot_cols = np.concatenate([np.arange(HEAD_DIM) + HEAD_DIM * hd for hd in A_SLOT_HEAD])
    a_end = a_w + 2 * kv_w
    b_end = a_end + 3 * b_w
    w_in_t = jnp.transpose(w_in, (2, 0, 1))
    rows = lambda lo, hi: w_in_t[lo:hi, layer_idx, :]
    w_attn_t = jnp.concatenate([rows(b_end, n_attn), rows(a_end, b_end),
                                rows(0, a_w)[slot_cols], rows(a_w, a_end)],
                               axis=0).astype(BF16)
    w_f_t = rows(n_attn, n_attn + C_HEADS)
    w_g_t = rows(n_attn + C_HEADS, w_in.shape[2]).astype(BF16)

    z3 = matmul(h, w_attn_t).reshape(batch, seq, n_attn)
    gates = matmul(h, w_g_t, bias=b_gate)
    c = forget_cumsum(h.reshape(batch, seq, d), w_f_t, b_forget)
    c4 = c[:, :C_HEADS].reshape(batch, C_HEADS, 1, seq)

    oc = fox_attention(z3, c4, q_col=0, k_col=C_HEADS // 2, v_col=C_HEADS)

    lam_init = 0.8 - 0.6 * math.exp(-0.3 * layer_idx)
    lam = (jnp.exp(jnp.sum(lq1.astype(F32) * lk1.astype(F32)))
           - jnp.exp(jnp.sum(lq2.astype(F32) * lk2.astype(F32))) + lam_init)
    b0 = 3 * c_w // LANES
    ob = diff_attention(z3, _alibi_slopes(B_HEADS), lam.reshape(1), subln,
                        q_col=b0, k_col=b0 + B_HEADS, v_col=b0 + 2 * B_HEADS,
                        out_scale=1.0 - lam_init)

    slot = np.array(A_SLOT_HEAD)
    a0 = 3 * c_w + 3 * b_w
    oa = swa_attention(z3, _alibi_slopes(A_Q_HEADS)[slot], sinks.astype(F32)[slot],
                       q_tile=a0 // a_w, k_tile=(a0 + a_w) // kv_w,
                       v_tile=(a0 + a_w) // kv_w + 1)

    return merge_out_residual(oa.reshape(t, a_w), ob.reshape(t, b_w), oc.reshape(t, c_w), gates,
                              w_br_a[slot_cols].astype(BF16), w_br_b.astype(BF16),
                              w_br_c.astype(BF16), w_out.astype(BF16), x2, g_post, g_next)


COUNT_BLOCK = 256
MOE_PIECES = 4


def _moe_dispatch(idx, tm):
    t = idx.shape[0]
    n_rows = t * TOP_K
    n_tiles = (n_rows + N_EXPERTS * (tm - 1)) // tm
    e_flat = idx.reshape(n_rows)
    onehot = e_flat[:, None] == jnp.arange(N_EXPERTS)[None, :]
    blk = min(COUNT_BLOCK, n_rows)
    oh = onehot.astype(F32).reshape(n_rows // blk, blk, N_EXPERTS)
    lower = (jnp.arange(blk)[:, None] >= jnp.arange(blk)[None, :]).astype(F32)
    within = jnp.einsum("ij,bjk->bik", lower, oh)
    before = jnp.cumsum(within[:, -1, :], axis=0) - within[:, -1, :]
    running = (within + before[:, None, :]).reshape(n_rows, N_EXPERTS)
    rank = jnp.sum(jnp.where(onehot, running - 1.0, 0.0), axis=1).astype(jnp.int32)
    counts = jnp.sum(onehot.astype(jnp.int32), axis=0)
    padded = ((counts + tm - 1) // tm) * tm
    ends = jnp.cumsum(padded)
    starts = ends - padded
    pos = starts[e_flat] + rank
    row_token = jnp.zeros((n_tiles * tm,), jnp.int32).at[pos].set(
        jnp.arange(n_rows, dtype=jnp.int32) // TOP_K)
    tile_start = jnp.arange(n_tiles, dtype=jnp.int32) * tm
    tile_expert = jnp.minimum(jnp.sum(tile_start[:, None] >= ends[None, :], axis=1),
                              N_EXPERTS - 1).astype(jnp.int32)
    n_active = (ends[-1] // tm).astype(jnp.int32).reshape(1)
    return pos.reshape(t, TOP_K), row_token, tile_expert, n_active


def _moe_ffn(x2, g_pre, g_post, w_router, e_gate, e_up, e_down, *, tm=512):
    idx, wts, h = router_top2(x2, g_pre, w_router)
    t = x2.shape[0]
    tm = min(tm, t)
    pos, row_token, tile_expert, n_active = _moe_dispatch(idx, tm)
    n_tiles = tile_expert.shape[0]
    bounds = [k * n_tiles // MOE_PIECES for k in range(MOE_PIECES + 1)]
    u = None
    for lo, hi in zip(bounds[:-1], bounds[1:]):
        xs = sc_gather_rows(h, row_token[lo * tm:hi * tm])
        u = moe_up(xs, tile_expert[lo:hi], jnp.clip(n_active - lo, 0, hi - lo), e_gate, e_up, u,
                   tile0=lo, n_tiles=n_tiles, tm=tm)
    ys = moe_down(u, tile_expert, n_active, e_down, tm=tm)
    y2 = sc_gather_rows(ys, pos.T.reshape(TOP_K * t))
    return combine_residual(y2, wts, x2, g_post)


def kernel(x, mix_pre_norm, w_in, b_forget, b_gate, attn_sinks, lam_q1, lam_k1, lam_q2, lam_k2, diff_subln, w_br_a, w_br_b, w_br_c, w_out, mix_post_norm, ffn_pre_norm, ffn_post_norm, dense_w_gate, dense_w_up, dense_w_down, w_router, moe_w_gate, moe_w_up, moe_w_down):
    batch, seq, d = x.shape
    depth = w_in.shape[0]
    x2 = x.reshape(batch * seq, d)
    h = rmsnorm_rows(x2, mix_pre_norm[0])
    for l in range(depth):
        x2, h = _token_mixer(x2, h, batch, seq, l, w_in, b_forget[l], b_gate[l],
                             attn_sinks[l], lam_q1[l], lam_k1[l], lam_q2[l], lam_k2[l],
                             diff_subln[l], w_br_a[l], w_br_b[l], w_br_c[l], w_out[l],
                             mix_post_norm[l], ffn_pre_norm[l])
        i = l // 2
        if l % 2 == 0:
            u = swiglu_up(h, dense_w_gate[i].astype(BF16), dense_w_up[i].astype(BF16))
            g_next = mix_pre_norm[l + 1] if l + 1 < depth else jnp.ones((d,), F32)
            f = dense_w_down.shape[1]
            x2, h = proj_residual(u, dense_w_down[i].astype(BF16), x2, ffn_post_norm[l],
                                  g_next, tm=256, tk=f)
        else:
            x2 = _moe_ffn(x2, ffn_pre_norm[l], ffn_post_norm[l], w_router[i],
                          moe_w_gate[i], moe_w_up[i], moe_w_down[i])
            if l + 1 < depth:
                h = rmsnorm_rows(x2, mix_pre_norm[l + 1])
    return x2.reshape(batch, seq, d)
```

```python
import functools
import math

import jax
import jax.numpy as jnp
import numpy as np
from jax import lax
from jax.experimental import pallas as pl
from jax.experimental.pallas import tpu as pltpu
from jax.experimental.pallas import tpu_sc as plsc

F32 = jnp.float32
BF16 = jnp.bfloat16

HEAD_DIM = 64
LANES = 128
A_Q_HEADS = 12
A_KV_HEADS = 4
A_GROUP = A_Q_HEADS // A_KV_HEADS
WINDOW = 128
B_HEADS = 4
C_HEADS = 12
N_BRANCH = 3
N_EXPERTS = 8
TOP_K = 2
RMS_EPS = 1e-6
NEG_INF = -1e30
LOG2E = math.log2(math.e)
Q_SCALE = HEAD_DIM ** -0.5 * LOG2E
VMEM_LIMIT = 56 * 1024 * 1024

A_SLOT_HEAD = (0, 3, 1, 4, 2, 5, 6, 9, 7, 10, 8, 11)


def _params(*sem):
    return pltpu.CompilerParams(dimension_semantics=sem, vmem_limit_bytes=VMEM_LIMIT)


def _rms(x, g):
    var = jnp.mean(x * x, axis=-1, keepdims=True)
    return x * lax.rsqrt(var + RMS_EPS) * g


def _sigmoid(x):
    return 0.5 * jnp.tanh(0.5 * x) + 0.5


def _pack_bf16_pairs(x):
    bits = pltpu.bitcast(x.astype(BF16).astype(F32), jnp.uint32)
    c = x.shape[1] // 2
    return (bits[:, :c] >> 16) | bits[:, c:]


def _unpack_bf16_pairs(words):
    lo = pltpu.bitcast(words << 16, F32)
    hi = pltpu.bitcast(words & jnp.uint32(0xFFFF0000), F32)
    return lo, hi


def _dot(a, b):
    return jnp.dot(a, b, preferred_element_type=F32)


def _dot_nt(a, b):
    return lax.dot_general(a, b, (((1,), (1,)), ((), ())), preferred_element_type=F32)


def _round_kernel(*refs):
    n = len(refs) // 2
    for src, dst in zip(refs[:n], refs[n:]):
        dst[...] = src[...].astype(dst.dtype)


ROUND_BLOCK_ELEMS = 2 * 1024 * 1024


def round_to_bf16(*ws):
    shape = ws[0].shape
    flat = [w.reshape(-1, shape[-1]) for w in ws]
    n, c = flat[0].shape
    rows = min(n, 1 << ((ROUND_BLOCK_ELEMS // c).bit_length() - 1))
    assert n % rows == 0 and all(w.shape == shape for w in ws)
    blk = pl.BlockSpec((rows, c), lambda i: (i, 0))
    outs = pl.pallas_call(
        _round_kernel,
        out_shape=tuple(jax.ShapeDtypeStruct((n, c), BF16) for _ in ws),
        grid=(n // rows,),
        in_specs=[blk] * len(ws),
        out_specs=tuple(blk for _ in ws),
        compiler_params=_params("parallel"),
        name="round_to_bf16",
    )(*flat)
    return [o.reshape(shape) for o in outs]


def _rmsnorm_kernel(x_ref, g_ref, o_ref):
    o_ref[...] = _rms(x_ref[...], g_ref[...]).astype(o_ref.dtype)


def rmsnorm_rows(x, g, *, tm=512):
    t, d = x.shape
    tm = min(tm, t)
    return pl.pallas_call(
        _rmsnorm_kernel,
        out_shape=jax.ShapeDtypeStruct((t, d), BF16),
        grid=(t // tm,),
        in_specs=[pl.BlockSpec((tm, d), lambda i: (i, 0)),
                  pl.BlockSpec((1, d), lambda i: (0, 0))],
        out_specs=pl.BlockSpec((tm, d), lambda i: (i, 0)),
        compiler_params=_params("parallel"),
        name="rmsnorm_rows",
    )(x, g.reshape(1, d))


def _mm_kernel(a_ref, bt_ref, o_ref):
    o_ref[...] = _dot_nt(a_ref[...], bt_ref[...]).astype(o_ref.dtype)


def _mm_scale_kernel(a_ref, bt_ref, scale_ref, o_ref):
    o_ref[...] = (_dot_nt(a_ref[...], bt_ref[...]) * scale_ref[...]).astype(o_ref.dtype)


def _mm_gate_kernel(a_ref, bt_ref, bias_ref, o_ref):
    z = _dot_nt(a_ref[...], bt_ref[...]) + bias_ref[...]
    o_ref[...] = _sigmoid(z).astype(o_ref.dtype)


def matmul(a, bt, *, bias=None, col_scale=None, out_dtype=BF16, tm=1024, tn=1024):
    m, k = a.shape
    n = bt.shape[0]
    tm, tn = min(tm, m), min(tn, n)
    assert m % tm == 0 and n % tn == 0
    in_specs = [pl.BlockSpec((tm, k), lambda i, j: (i, 0)),
                pl.BlockSpec((tn, k), lambda i, j: (j, 0))]
    args = [a, bt]
    kern = _mm_kernel
    if bias is not None:
        in_specs.append(pl.BlockSpec((1, tn), lambda i, j: (0, j)))
        args.append(bias.reshape(1, n).astype(F32))
        kern = _mm_gate_kernel
    elif col_scale is not None:
        in_specs.append(pl.BlockSpec((1, tn), lambda i, j: (0, j)))
        args.append(col_scale.reshape(1, n).astype(F32))
        kern = _mm_scale_kernel
    return pl.pallas_call(
        kern,
        out_shape=jax.ShapeDtypeStruct((m, n), out_dtype),
        grid=(m // tm, n // tn),
        in_specs=in_specs,
        out_specs=pl.BlockSpec((tm, tn), lambda i, j: (i, j)),
        compiler_params=_params("parallel", "parallel"),
        name="matmul_gate" if bias is not None else "matmul",
    )(*args)


def _forget_kernel(h_ref, wt_ref, b_ref, c_ref, *, seq, blk):
    z = _dot_nt(wt_ref[...], h_ref[0]) + b_ref[...]
    log_f = jnp.minimum(z, 0.0) - jnp.log1p(jnp.exp(-jnp.abs(z)))
    r = lax.broadcasted_iota(jnp.int32, (blk, blk), 0)
    c = lax.broadcasted_iota(jnp.int32, (blk, blk), 1)
    upper = (r <= c).astype(F32)
    carry = jnp.zeros((LANES, 1), F32)
    for j in range(seq // blk):
        part = lax.dot_general(log_f[:, j * blk:(j + 1) * blk], upper,
                               (((1,), (0,)), ((), ())),
                               precision=lax.Precision.HIGHEST,
                               preferred_element_type=F32) + carry
        c_ref[0, :, j * blk:(j + 1) * blk] = part
        carry = part[:, blk - 1:blk]


def forget_cumsum(h3, w_ft, b_f):
    b, s, d = h3.shape
    nh = w_ft.shape[0]
    wt = jnp.zeros((LANES, d), BF16).at[:nh].set(w_ft.astype(BF16))
    bias = jnp.zeros((LANES, 1), F32).at[:nh, 0].set(b_f.astype(F32))
    blk = min(256, s)
    return pl.pallas_call(
        functools.partial(_forget_kernel, seq=s, blk=blk),
        out_shape=jax.ShapeDtypeStruct((b, LANES, s), F32),
        grid=(b,),
        in_specs=[pl.BlockSpec((1, s, d), lambda i: (i, 0, 0)),
                  pl.BlockSpec((LANES, d), lambda i: (0, 0)),
                  pl.BlockSpec((LANES, 1), lambda i: (0, 0))],
        out_specs=pl.BlockSpec((1, LANES, s), lambda i: (i, 0, 0)),
        compiler_params=_params("parallel"),
        name="forget_cumsum",
    )(h3, wt, bias)


def _half_mask(tq, half):
    lane = lax.broadcasted_iota(jnp.int32, (tq, LANES), 1)
    return (lane >= HEAD_DIM * half) & (lane < HEAD_DIM * (half + 1))


def _causal_logits(qh, k_ref, lanes, qi, tq, bias):
    n0 = qi * tq
    r = lax.broadcasted_iota(jnp.int32, (tq, tq), 0)
    c = lax.broadcasted_iota(jnp.int32, (tq, tq), 1)
    diag = _dot_nt(qh, k_ref[0, n0:n0 + tq, lanes]) + bias(n0, tq)
    parts = [jnp.where(r >= c, diag, NEG_INF)]
    if qi > 0:
        parts.append(_dot_nt(qh, k_ref[0, :n0, lanes]) + bias(0, n0))
    return parts


def _softmax_parts(parts):
    m = functools.reduce(jnp.maximum, [s.max(axis=-1, keepdims=True) for s in parts])
    exps = [jnp.exp2(s - m) for s in parts]
    denom = functools.reduce(jnp.add, [e.sum(axis=-1, keepdims=True) for e in exps])
    return exps, denom


def _value_parts(v_ref, lanes, qi, tq):
    n0 = qi * tq
    return [v_ref[0, n0:n0 + tq, lanes]] + ([v_ref[0, :n0, lanes]] if qi > 0 else [])


def _weighted_values(weights, values):
    return functools.reduce(jnp.add, [_dot(w.astype(v.dtype), v) for w, v in zip(weights, values)])


ATTN_TILES_PER_STEP = 2


def _dispatch_on_step(tile, nq, tq):
    tps = min(ATTN_TILES_PER_STEP, nq)

    def step_body(s):
        for sub in range(tps):
            tile(qi=s * tps + sub, rows=slice(sub * tq, (sub + 1) * tq))

    step = pl.program_id(2)
    for s in range(nq // tps):
        pl.when(step == s)(functools.partial(step_body, s))


def _fox_kernel(q_ref, k_ref, v_ref, c_ref, o_ref, *, tq, nq, npair):
    masks = [_half_mask(tq, h) for h in range(2)]

    def tile(qi, rows):
        for p in range(npair):
            lanes = slice(p * LANES, (p + 1) * LANES)
            q = q_ref[0, rows, lanes]
            values = _value_parts(v_ref, lanes, qi, tq)
            outs = []
            for h in range(2):
                qh = jnp.where(masks[h], q, jnp.zeros_like(q))
                bias = lambda start, size: c_ref[0, 2 * p + h, :, start:start + size] * -LOG2E
                exps, denom = _softmax_parts(_causal_logits(qh, k_ref, lanes, qi, tq, bias))
                outs.append(_weighted_values(exps, values) / denom)
            o_ref[0, rows, lanes] = jnp.where(masks[0], outs[0], outs[1]).astype(o_ref.dtype)

    _dispatch_on_step(tile, nq, tq)


def _diff_kernel(slope_ref, lam_ref, q_ref, k_ref, v_ref, g_ref, o_ref, *,
                 tq, nq, nhead, out_scale):
    lam = lam_ref[0]
    masks = [_half_mask(tq, h) for h in range(2)]

    def tile(qi, rows):
        for j in range(nhead):
            lanes = slice(j * LANES, (j + 1) * LANES)
            slope = slope_ref[pl.program_id(1) * nhead + j] * LOG2E
            q = q_ref[0, rows, lanes]

            def bias(start, size):
                pos = start + lax.broadcasted_iota(jnp.int32, (1, size), 1)
                return slope * pos.astype(F32)

            maps = []
            for h in range(2):
                qh = jnp.where(masks[h], q, jnp.zeros_like(q))
                maps.append(_softmax_parts(_causal_logits(qh, k_ref, lanes, qi, tq, bias)))
            (e0, d0), (e1, d1) = maps
            w0, w1 = 1.0 / d0, lam / d1
            weights = [a * w0 - b * w1 for a, b in zip(e0, e1)]
            o = _weighted_values(weights, _value_parts(v_ref, lanes, qi, tq))
            o_ref[0, rows, lanes] = (_rms(o, g_ref[...]) * out_scale).astype(o_ref.dtype)

    _dispatch_on_step(tile, nq, tq)


def fox_attention(z3, c4, *, q_col, k_col, v_col, tq=256, npair=3):
    b, s, _ = z3.shape
    tq = min(tq, s)
    rows = tq * min(ATTN_TILES_PER_STEP, s // tq)
    w = npair * LANES
    groups = C_HEADS // 2 // npair
    assert q_col % npair == 0 and k_col % npair == 0 and v_col % npair == 0
    return pl.pallas_call(
        functools.partial(_fox_kernel, tq=tq, nq=s // tq, npair=npair),
        out_shape=jax.ShapeDtypeStruct((b, s, C_HEADS // 2 * LANES), BF16),
        grid=(b, groups, s // rows),
        in_specs=[pl.BlockSpec((1, rows, w), lambda bi, g, qi: (bi, qi, q_col // npair + g)),
                  pl.BlockSpec((1, s, w), lambda bi, g, qi: (bi, 0, k_col // npair + g)),
                  pl.BlockSpec((1, s, w), lambda bi, g, qi: (bi, 0, v_col // npair + g)),
                  pl.BlockSpec((1, 2 * npair, 1, s), lambda bi, g, qi: (bi, g, 0, 0))],
        out_specs=pl.BlockSpec((1, rows, w), lambda bi, g, qi: (bi, qi, g)),
        compiler_params=_params("parallel", "parallel", "arbitrary"),
        name="fox_attention",
    )(z3, z3, z3, c4)


def diff_attention(z3, slopes, lam, subln, *, q_col, k_col, v_col, out_scale, tq=256, nhead=2):
    b, s, _ = z3.shape
    tq = min(tq, s)
    rows = tq * min(ATTN_TILES_PER_STEP, s // tq)
    w = nhead * LANES
    assert q_col % nhead == 0 and k_col % nhead == 0 and v_col % nhead == 0
    smem = pl.BlockSpec(memory_space=pltpu.SMEM)
    return pl.pallas_call(
        functools.partial(_diff_kernel, tq=tq, nq=s // tq, nhead=nhead, out_scale=out_scale),
        out_shape=jax.ShapeDtypeStruct((b, s, B_HEADS * LANES), BF16),
        grid=(b, B_HEADS // nhead, s // rows),
        in_specs=[smem, smem,
                  pl.BlockSpec((1, rows, w), lambda bi, g, qi: (bi, qi, q_col // nhead + g)),
                  pl.BlockSpec((1, s, w), lambda bi, g, qi: (bi, 0, k_col // nhead + g)),
                  pl.BlockSpec((1, s, w), lambda bi, g, qi: (bi, 0, v_col // nhead + g)),
                  pl.BlockSpec((1, LANES), lambda bi, g, qi: (0, 0))],
        out_specs=pl.BlockSpec((1, rows, w), lambda bi, g, qi: (bi, qi, g)),
        compiler_params=_params("parallel", "parallel", "arbitrary"),
        name="diff_attention",
    )(slopes, lam, z3, z3, z3, subln.reshape(1, LANES).astype(F32))


def _swa_kernel(slope_ref, sink_ref, q_ref, k_ref, v_ref, o_ref, *, blk, nsub):
    r = lax.broadcasted_iota(jnp.int32, (blk, 2 * blk), 0)
    c = lax.broadcasted_iota(jnp.int32, (blk, 2 * blk), 1)
    dist = r + blk - c
    in_window = (dist >= 0) & (dist < WINDOW)
    dist_f = dist.astype(F32)
    masks = [_half_mask(blk, h) for h in range(2)]
    for sub in range(nsub):
        _swa_block(slope_ref, sink_ref, q_ref, k_ref, v_ref, o_ref, pl.program_id(1) * nsub + sub,
                   slice(sub * blk, (sub + 1) * blk), in_window, c, dist_f, masks, blk)


def _swa_block(slope_ref, sink_ref, q_ref, k_ref, v_ref, o_ref, qi, rows,
               in_window, c, dist_f, masks, blk):
    cur = pl.multiple_of(qi * blk, blk)
    prev = pl.multiple_of(jnp.maximum(qi - 1, 0) * blk, blk)
    keep = in_window & ((qi > 0) | (c >= blk))
    for tile in range(A_Q_HEADS // 2):
        q = q_ref[0, rows, tile * LANES:(tile + 1) * LANES]
        halves = []
        for half in range(2):
            slot = 2 * tile + half
            kv_tile = (A_SLOT_HEAD[slot] // A_GROUP) // 2
            lanes = slice(kv_tile * LANES, (kv_tile + 1) * LANES)
            k = jnp.concatenate([k_ref[0, pl.ds(prev, blk), lanes],
                                 k_ref[0, pl.ds(cur, blk), lanes]], axis=0)
            v = jnp.concatenate([v_ref[0, pl.ds(prev, blk), lanes],
                                 v_ref[0, pl.ds(cur, blk), lanes]], axis=0)
            qh = jnp.where(masks[half], q, jnp.zeros_like(q))
            s = _dot_nt(qh, k) - (slope_ref[slot] * LOG2E) * dist_f
            s = jnp.where(keep, s, NEG_INF)
            sink = sink_ref[slot] * LOG2E
            m = jnp.maximum(jnp.max(s, axis=-1, keepdims=True), sink)
            p = jnp.exp2(s - m)
            denom = jnp.sum(p, axis=-1, keepdims=True) + jnp.exp2(sink - m)
            halves.append(_dot(p.astype(v.dtype), v) / denom)
        o_ref[0, rows, tile * LANES:(tile + 1) * LANES] = jnp.where(
            masks[0], halves[0], halves[1]).astype(o_ref.dtype)


SWA_BLOCKS_PER_STEP = 8


def swa_attention(z3, slopes, sinks, *, q_tile, k_tile, v_tile):
    b, s, _ = z3.shape
    blk = WINDOW
    nsub = min(SWA_BLOCKS_PER_STEP, s // blk)
    qw = A_Q_HEADS * HEAD_DIM
    kw = A_KV_HEADS * HEAD_DIM
    smem = pl.BlockSpec(memory_space=pltpu.SMEM)
    return pl.pallas_call(
        functools.partial(_swa_kernel, blk=blk, nsub=nsub),
        out_shape=jax.ShapeDtypeStruct((b, s, qw), BF16),
        grid=(b, s // (blk * nsub)),
        in_specs=[smem, smem,
                  pl.BlockSpec((1, blk * nsub, qw), lambda bi, qi: (bi, qi, q_tile)),
                  pl.BlockSpec((1, s, kw), lambda bi, qi: (bi, 0, k_tile)),
                  pl.BlockSpec((1, s, kw), lambda bi, qi: (bi, 0, v_tile))],
        out_specs=pl.BlockSpec((1, blk * nsub, qw), lambda bi, qi: (bi, qi, 0)),
        compiler_params=_params("parallel", "parallel"),
        name="swa_attention",
    )(slopes, sinks, z3, z3, z3)


def _merge_out_kernel(oa_ref, ob_ref, oc_ref, ga_ref, gb_ref, gc_ref, wa_ref, wb_ref, wc_ref,
                      wo_ref, x_ref, gp_ref, gn_ref, xo_ref, ho_ref):
    merged = ga_ref[...].astype(F32) * _dot(oa_ref[...], wa_ref[...])
    merged += gb_ref[...].astype(F32) * _dot(ob_ref[...], wb_ref[...])
    merged += gc_ref[...].astype(F32) * _dot(oc_ref[...], wc_ref[...])
    m = _dot(merged.astype(wo_ref.dtype), wo_ref[...])
    x_new = x_ref[...] + _rms(m, gp_ref[...])
    xo_ref[...] = x_new
    ho_ref[...] = _rms(x_new, gn_ref[...]).astype(ho_ref.dtype)


def merge_out_residual(oa, ob, oc, gates, wa, wb, wc, w_out, x, g_post, g_next, *, tm=256):
    t, d = x.shape
    tm = min(tm, t)
    row = lambda w: pl.BlockSpec((tm, w), lambda i: (i, 0))
    gate = lambda br: pl.BlockSpec((tm, d), lambda i: (i, br))
    whole = lambda w: pl.BlockSpec(w.shape, lambda i: (0, 0), pipeline_mode=pl.Buffered(1))
    vec = pl.BlockSpec((1, d), lambda i: (0, 0))
    return pl.pallas_call(
        _merge_out_kernel,
        out_shape=(jax.ShapeDtypeStruct((t, d), F32), jax.ShapeDtypeStruct((t, d), BF16)),
        grid=(t // tm,),
        in_specs=[row(oa.shape[1]), row(ob.shape[1]), row(oc.shape[1]),
                  gate(0), gate(1), gate(2),
                  whole(wa), whole(wb), whole(wc), whole(w_out), row(d), vec, vec],
        out_specs=(row(d), row(d)),
        compiler_params=_params("parallel"),
        name="merge_out_residual",
    )(oa, ob, oc, gates, gates, gates, wa, wb, wc, w_out, x,
      g_post.reshape(1, d), g_next.reshape(1, d))


def _proj_res_kernel(a_ref, w_ref, x_ref, gp_ref, gn_ref, xo_ref, ho_ref, acc_ref, *, nk):
    kk = pl.program_id(1)
    part = _dot(a_ref[...], w_ref[...])

    @pl.when(kk == 0)
    def _():
        acc_ref[...] = part

    @pl.when(kk > 0)
    def _():
        acc_ref[...] += part

    @pl.when(kk == nk - 1)
    def _():
        x_new = x_ref[...] + _rms(acc_ref[...], gp_ref[...])
        xo_ref[...] = x_new
        ho_ref[...] = _rms(x_new, gn_ref[...]).astype(ho_ref.dtype)


def proj_residual(a, w, x, g_post, g_next, *, tm, tk):
    t, k = a.shape
    d = w.shape[1]
    tm, tk = min(tm, t), min(tk, k)
    nk = k // tk
    vec = pl.BlockSpec((1, d), lambda i, kk: (0, 0))
    w_mode = dict(pipeline_mode=pl.Buffered(1)) if nk == 1 else {}
    return pl.pallas_call(
        functools.partial(_proj_res_kernel, nk=nk),
        out_shape=(jax.ShapeDtypeStruct((t, d), F32), jax.ShapeDtypeStruct((t, d), BF16)),
        grid=(t // tm, nk),
        in_specs=[pl.BlockSpec((tm, tk), lambda i, kk: (i, kk)),
                  pl.BlockSpec((tk, d), lambda i, kk: (kk, 0), **w_mode),
                  pl.BlockSpec((tm, d), lambda i, kk: (i, 0)),
                  vec, vec],
        out_specs=(pl.BlockSpec((tm, d), lambda i, kk: (i, 0)),
                   pl.BlockSpec((tm, d), lambda i, kk: (i, 0))),
        scratch_shapes=[pltpu.VMEM((tm, d), F32)],
        compiler_params=_params("parallel", "arbitrary"),
        name="proj_residual",
    )(a, w, x, g_post.reshape(1, d), g_next.reshape(1, d))


def _swiglu_up_kernel(h_ref, wg_ref, wu_ref, o_ref):
    h = h_ref[...]
    g = _dot(h, wg_ref[...])
    u = _dot(h, wu_ref[...])
    o_ref[...] = (g * _sigmoid(g) * u).astype(o_ref.dtype)


def swiglu_up(h, wg, wu, *, tm=1024, tn=512):
    t, d = h.shape
    f = wg.shape[1]
    tm, tn = min(tm, t), min(tn, f)
    return pl.pallas_call(
        _swiglu_up_kernel,
        out_shape=jax.ShapeDtypeStruct((t, f), BF16),
        grid=(t // tm, f // tn),
        in_specs=[pl.BlockSpec((tm, d), lambda i, j: (i, 0)),
                  pl.BlockSpec((d, tn), lambda i, j: (0, j)),
                  pl.BlockSpec((d, tn), lambda i, j: (0, j))],
        out_specs=pl.BlockSpec((tm, tn), lambda i, j: (i, j)),
        compiler_params=_params("parallel", "parallel"),
        name="swiglu_up",
    )(h, wg, wu)


def _router_kernel(x_ref, g_ref, w_ref, idx_ref, wt_ref, h_ref):
    h = _rms(x_ref[...], g_ref[...])
    h_ref[...] = _pack_bf16_pairs(h)
    logits = jnp.dot(h, w_ref[...], precision=lax.Precision.HIGHEST,
                     preferred_element_type=F32)
    lane = lax.broadcasted_iota(jnp.int32, logits.shape, 1)
    lane_f = lane.astype(F32)
    lg = jnp.where(lane < N_EXPERTS, logits, -jnp.inf)
    m1 = jnp.max(lg, axis=-1, keepdims=True)
    i1 = jnp.min(jnp.where(lg == m1, lane_f, float(LANES)), axis=-1, keepdims=True)
    lg2 = jnp.where(lane_f == i1, -jnp.inf, lg)
    m2 = jnp.max(lg2, axis=-1, keepdims=True)
    i2 = jnp.min(jnp.where(lg2 == m2, lane_f, float(LANES)), axis=-1, keepdims=True)
    e2 = jnp.exp(m2 - m1)
    w1 = 1.0 / (1.0 + e2)
    w2 = e2 / (1.0 + e2)
    idx_ref[...] = jnp.where(lane == 0, i1, jnp.where(lane == 1, i2, 0.0)).astype(jnp.int32)
    wt_ref[...] = jnp.where(lane == 0, w1, jnp.where(lane == 1, w2, 0.0))


def router_top2(x, g, w_router, *, tm=512):
    t, d = x.shape
    tm = min(tm, t)
    wp = jnp.zeros((d, LANES), F32).at[:, :N_EXPERTS].set(w_router.astype(F32))
    out = pl.BlockSpec((tm, LANES), lambda i: (i, 0))
    idx, wt, h = pl.pallas_call(
        _router_kernel,
        out_shape=(jax.ShapeDtypeStruct((t, LANES), jnp.int32),
                   jax.ShapeDtypeStruct((t, LANES), F32),
                   jax.ShapeDtypeStruct((t, d // 2), jnp.uint32)),
        grid=(t // tm,),
        in_specs=[pl.BlockSpec((tm, d), lambda i: (i, 0)),
                  pl.BlockSpec((1, d), lambda i: (0, 0)),
                  pl.BlockSpec((d, LANES), lambda i: (0, 0))],
        out_specs=(out, out, pl.BlockSpec((tm, d // 2), lambda i: (i, 0))),
        compiler_params=_params("parallel"),
        name="router_top2",
    )(x, g.reshape(1, d), wp)
    return idx[:, :TOP_K], wt[:, :TOP_K], h


SC_INDEX_WINDOW = 128
SC_COPY_BYTES = 256 * 1024


def sc_gather_rows(src, idx):
    n = idx.shape[0]
    d = src.shape[1]
    batch = min(SC_INDEX_WINDOW, SC_COPY_BYTES // (d * src.dtype.itemsize))
    assert n % SC_INDEX_WINDOW == 0 and SC_INDEX_WINDOW % batch == 0
    mesh = plsc.VectorSubcoreMesh(core_axis_name="c", subcore_axis_name="s")
    dst = jnp.arange(n, dtype=jnp.int32).reshape(1, n)

    @pl.kernel(out_type=jax.ShapeDtypeStruct((n, d), src.dtype), mesh=mesh,
               scratch_types=[pltpu.VMEM((batch, d), src.dtype)])
    def gather(x_hbm, i_hbm, d_hbm, o_hbm, buf):
        def body(i_vmem, d_vmem):
            for s in range(SC_INDEX_WINDOW // batch):
                rows = pl.ds(s * batch, batch)
                pltpu.sync_copy(x_hbm.at[i_vmem.at[0, rows]], buf)
                pltpu.sync_copy(buf, o_hbm.at[d_vmem.at[0, rows]])

        window = pl.BlockSpec((1, SC_INDEX_WINDOW), lambda i: (0, i))
        pltpu.emit_pipeline(
            body, grid=(n // SC_INDEX_WINDOW,),
            in_specs=[window, window], out_specs=[],
            core_axis_name=("c", "s"),
            dimension_semantics=(pltpu.PARALLEL,),
        )(i_hbm, d_hbm)

    return gather(src, idx.reshape(1, n), dst)


def _moe_up_kernel(te_ref, na_ref, x_ref, wg_ref, wu_ref, *rest):
    o_ref = rest[-1]
    active = pl.program_id(1) < na_ref[0]

    @pl.when(active)
    def _():
        half = x_ref.shape[1]
        lo, hi = (v.astype(BF16) for v in _unpack_bf16_pairs(x_ref[...]))
        g = _dot(lo, wg_ref[0, :half]) + _dot(hi, wg_ref[0, half:])
        u = _dot(lo, wu_ref[0, :half]) + _dot(hi, wu_ref[0, half:])
        o_ref[...] = (g * _sigmoid(g) * u).astype(o_ref.dtype)

    @pl.when(jnp.logical_not(active))
    def _():
        o_ref[...] = jnp.zeros_like(o_ref)


def _moe_down_kernel(te_ref, na_ref, u_ref, wd_ref, o_ref):
    active = pl.program_id(1) < na_ref[0]

    @pl.when(active)
    def _():
        o_ref[...] = _pack_bf16_pairs(_dot(u_ref[...], wd_ref[0]))

    @pl.when(jnp.logical_not(active))
    def _():
        o_ref[...] = jnp.zeros_like(o_ref)


def _active_tile(i, na_ref):
    return jnp.maximum(jnp.minimum(i, na_ref[0] - 1), 0)


def moe_up(xs, tile_expert, n_active, wg, wu, u_prev, *, tile0, n_tiles, tm, tn=1024):
    d, f = wg.shape[1:]
    tn = min(tn, f)
    assert f % tn == 0
    in_specs = [pl.BlockSpec((tm, d // 2), lambda j, i, te, na: (_active_tile(i, na), 0)),
                pl.BlockSpec((1, d, tn), lambda j, i, te, na: (te[_active_tile(i, na)], 0, j)),
                pl.BlockSpec((1, d, tn), lambda j, i, te, na: (te[_active_tile(i, na)], 0, j))]
    args = [tile_expert, n_active, xs, wg, wu]
    aliases = {}
    if u_prev is not None:
        in_specs.append(pl.BlockSpec(memory_space=pl.ANY))
        args.append(u_prev)
        aliases = {len(args) - 1: 0}
    grid_spec = pltpu.PrefetchScalarGridSpec(
        num_scalar_prefetch=2,
        grid=(f // tn, xs.shape[0] // tm),
        in_specs=in_specs,
        out_specs=pl.BlockSpec((tm, tn), lambda j, i, te, na: (i + tile0, j)),
    )
    return pl.pallas_call(
        _moe_up_kernel,
        out_shape=jax.ShapeDtypeStruct((n_tiles * tm, f), BF16),
        grid_spec=grid_spec,
        input_output_aliases=aliases,
        compiler_params=_params("arbitrary", "arbitrary"),
        name="moe_up",
    )(*args)


MOE_DOWN_TN = 512


def moe_down(u, tile_expert, n_active, wd, *, tm):
    p, f = u.shape
    d = wd.shape[2]
    tn = min(MOE_DOWN_TN, d)
    grid_spec = pltpu.PrefetchScalarGridSpec(
        num_scalar_prefetch=2,
        grid=(d // tn, p // tm),
        in_specs=[pl.BlockSpec((tm, f), lambda j, i, te, na: (_active_tile(i, na), 0)),
                  pl.BlockSpec((1, f, tn), lambda j, i, te, na: (te[_active_tile(i, na)], 0, j))],
        out_specs=pl.BlockSpec((tm, tn // 2), lambda j, i, te, na: (i, j)),
    )
    return pl.pallas_call(
        _moe_down_kernel,
        out_shape=jax.ShapeDtypeStruct((p, d // 2), jnp.uint32),
        grid_spec=grid_spec,
        compiler_params=_params("arbitrary", "arbitrary"),
        name="moe_down",
    )(tile_expert, n_active, u, wd)


def _unpack_expert_rows(words, half):
    lo, hi = _unpack_bf16_pairs(words)
    parts = []
    for b in range(words.shape[1] // half):
        parts += [lo[:, b * half:(b + 1) * half], hi[:, b * half:(b + 1) * half]]
    return jnp.concatenate(parts, axis=1)


def _combine_kernel(ya_ref, yb_ref, w_ref, x_ref, g_ref, o_ref, *, half):
    w = w_ref[...]
    y = (w[:, 0:1] * _unpack_expert_rows(ya_ref[...], half)
         + w[:, 1:2] * _unpack_expert_rows(yb_ref[...], half))
    o_ref[...] = x_ref[...] + _rms(y, g_ref[...])


def combine_residual(y2, wts, x, g, *, tm=512):
    t, d = x.shape
    tm = min(tm, t)
    nt = t // tm
    blk = pl.BlockSpec((tm, d), lambda i: (i, 0))
    return pl.pallas_call(
        functools.partial(_combine_kernel, half=min(MOE_DOWN_TN, d) // 2),
        out_shape=jax.ShapeDtypeStruct((t, d), F32),
        grid=(nt,),
        in_specs=[pl.BlockSpec((tm, d // 2), lambda i: (i, 0)),
                  pl.BlockSpec((tm, d // 2), lambda i: (i + nt, 0)),
                  pl.BlockSpec((tm, TOP_K), lambda i: (i, 0)), blk,
                  pl.BlockSpec((1, d), lambda i: (0, 0))],
        out_specs=blk,
        compiler_params=_params("parallel"),
        name="combine_residual",
    )(y2, y2, wts, x, g.reshape(1, d))


def _alibi_slopes(n):
    return jnp.exp2(-8.0 * (jnp.arange(n, dtype=F32) + 1.0) / n)


def _token_mixer(x2, h, batch, seq, layer_idx, w_in, b_forget, b_gate, sinks,
                 lq1, lk1, lq2, lk2, subln, w_br_a, w_br_b, w_br_c, w_out,
                 g_post, g_next):
    t, d = x2.shape
    a_w = A_Q_HEADS * HEAD_DIM
    kv_w = A_KV_HEADS * HEAD_DIM
    b_w = B_HEADS * 2 * HEAD_DIM
    c_w = C_HEADS * HEAD_DIM
    n_attn = a_w + 2 * kv_w + 3 * b_w + 3 * c_w

    slot_cols = np.concatenate([np.arange(HEAD_DIM) + HEAD_DIM * hd for hd in A_SLOT_HEAD])
    a_end = a_w + 2 * kv_w
    b_end = a_end + 3 * b_w
    w_in_t = jnp.transpose(w_in, (2, 0, 1))
    rows = lambda lo, hi: w_in_t[lo:hi, layer_idx, :]
    w_attn_t = jnp.concatenate([rows(b_end, n_attn), rows(a_end, b_end),
                                rows(0, a_w)[slot_cols], rows(a_w, a_end)],
                               axis=0).astype(BF16)
    w_f_t = rows(n_attn, n_attn + C_HEADS)
    w_g_t = rows(n_attn + C_HEADS, w_in.shape[2]).astype(BF16)

    q_cols = np.zeros((n_attn,), bool)
    for lo, width in ((0, c_w), (3 * c_w, b_w), (3 * c_w + 3 * b_w, a_w)):
        q_cols[lo:lo + width] = True
    z3 = matmul(h, w_attn_t, col_scale=jnp.where(q_cols, Q_SCALE, 1.0)).reshape(batch, seq, n_attn)
    gates = matmul(h, w_g_t, bias=b_gate)
    c = forget_cumsum(h.reshape(batch, seq, d), w_f_t, b_forget)
    c4 = c[:, :C_HEADS].reshape(batch, C_HEADS, 1, seq)

    oc = fox_attention(z3, c4, q_col=0, k_col=C_HEADS // 2, v_col=C_HEADS)

    lam_init = 0.8 - 0.6 * math.exp(-0.3 * layer_idx)
    lam = (jnp.exp(jnp.sum(lq1.astype(F32) * lk1.astype(F32)))
           - jnp.exp(jnp.sum(lq2.astype(F32) * lk2.astype(F32))) + lam_init)
    b0 = 3 * c_w // LANES
    ob = diff_attention(z3, _alibi_slopes(B_HEADS), lam.reshape(1), subln,
                        q_col=b0, k_col=b0 + B_HEADS, v_col=b0 + 2 * B_HEADS,
                        out_scale=1.0 - lam_init)

    slot = np.array(A_SLOT_HEAD)
    a0 = 3 * c_w + 3 * b_w
    oa = swa_attention(z3, _alibi_slopes(A_Q_HEADS)[slot], sinks.astype(F32)[slot],
                       q_tile=a0 // a_w, k_tile=(a0 + a_w) // kv_w,
                       v_tile=(a0 + a_w) // kv_w + 1)

    return merge_out_residual(oa.reshape(t, a_w), ob.reshape(t, b_w), oc.reshape(t, c_w), gates,
                              w_br_a[slot_cols].astype(BF16), w_br_b.astype(BF16),
                              w_br_c.astype(BF16), w_out.astype(BF16), x2, g_post, g_next)


COUNT_BLOCK = 256
MOE_PIECES = 4


def _moe_dispatch(idx, tm):
    t = idx.shape[0]
    n_rows = t * TOP_K
    n_tiles = (n_rows + N_EXPERTS * (tm - 1)) // tm
    e_flat = idx.reshape(n_rows)
    onehot = e_flat[:, None] == jnp.arange(N_EXPERTS)[None, :]
    blk = min(COUNT_BLOCK, n_rows)
    oh = onehot.astype(F32).reshape(n_rows // blk, blk, N_EXPERTS)
    lower = (jnp.arange(blk)[:, None] >= jnp.arange(blk)[None, :]).astype(F32)
    within = jnp.einsum("ij,bjk->bik", lower, oh)
    before = jnp.cumsum(within[:, -1, :], axis=0) - within[:, -1, :]
    running = (within + before[:, None, :]).reshape(n_rows, N_EXPERTS)
    rank = jnp.sum(jnp.where(onehot, running - 1.0, 0.0), axis=1).astype(jnp.int32)
    counts = jnp.sum(onehot.astype(jnp.int32), axis=0)
    padded = ((counts + tm - 1) // tm) * tm
    ends = jnp.cumsum(padded)
    starts = ends - padded
    pos = starts[e_flat] + rank
    row_token = jnp.zeros((n_tiles * tm,), jnp.int32).at[pos].set(
        jnp.arange(n_rows, dtype=jnp.int32) // TOP_K)
    tile_start = jnp.arange(n_tiles, dtype=jnp.int32) * tm
    tile_expert = jnp.minimum(jnp.sum(tile_start[:, None] >= ends[None, :], axis=1),
                              N_EXPERTS - 1).astype(jnp.int32)
    n_active = (ends[-1] // tm).astype(jnp.int32).reshape(1)
    return pos.reshape(t, TOP_K), row_token, tile_expert, n_active


def _moe_ffn(x2, g_pre, g_post, w_router, e_gate, e_up, e_down, *, tm=512):
    idx, wts, h = router_top2(x2, g_pre, w_router)
    t = x2.shape[0]
    tm = min(tm, t)
    pos, row_token, tile_expert, n_active = _moe_dispatch(idx, tm)
    n_tiles = tile_expert.shape[0]
    bounds = [k * n_tiles // MOE_PIECES for k in range(MOE_PIECES + 1)]
    u = None
    for lo, hi in zip(bounds[:-1], bounds[1:]):
        xs = sc_gather_rows(h, row_token[lo * tm:hi * tm])
        u = moe_up(xs, tile_expert[lo:hi], jnp.clip(n_active - lo, 0, hi - lo), e_gate, e_up, u,
                   tile0=lo, n_tiles=n_tiles, tm=tm)
    ys = moe_down(u, tile_expert, n_active, e_down, tm=tm)
    y2 = sc_gather_rows(ys, pos.T.reshape(TOP_K * t))
    return combine_residual(y2, wts, x2, g_post)


def kernel(x, mix_pre_norm, w_in, b_forget, b_gate, attn_sinks, lam_q1, lam_k1, lam_q2, lam_k2, diff_subln, w_br_a, w_br_b, w_br_c, w_out, mix_post_norm, ffn_pre_norm, ffn_post_norm, dense_w_gate, dense_w_up, dense_w_down, w_router, moe_w_gate, moe_w_up, moe_w_down):
    batch, seq, d = x.shape
    depth = w_in.shape[0]
    x2 = x.reshape(batch * seq, d)
    h = rmsnorm_rows(x2, mix_pre_norm[0])
    for l in range(depth):
        x2, h = _token_mixer(x2, h, batch, seq, l, w_in, b_forget[l], b_gate[l],
                             attn_sinks[l], lam_q1[l], lam_k1[l], lam_q2[l], lam_k2[l],
                             diff_subln[l], w_br_a[l], w_br_b[l], w_br_c[l], w_out[l],
                             mix_post_norm[l], ffn_pre_norm[l])
        i = l // 2
        if l % 2 == 0:
            u = swiglu_up(h, dense_w_gate[i].astype(BF16), dense_w_up[i].astype(BF16))
            g_next = mix_pre_norm[l + 1] if l + 1 < depth else jnp.ones((d,), F32)
            f = dense_w_down.shape[1]
            x2, h = proj_residual(u, dense_w_down[i].astype(BF16), x2, ffn_post_norm[l],
                                  g_next, tm=256, tk=f)
        else:
            x2 = _moe_ffn(x2, ffn_pre_norm[l], ffn_post_norm[l], w_router[i],
                          moe_w_gate[i], moe_w_up[i], moe_w_down[i])
            if l + 1 < depth:
                h = rmsnorm_rows(x2, mix_pre_norm[l + 1])
    return x2.reshape(batch, seq, d)
```

```python
import functools
import math

import jax
import jax.numpy as jnp
import numpy as np
from jax import lax
from jax.experimental import pallas as pl
from jax.experimental.pallas import tpu as pltpu
from jax.experimental.pallas import tpu_sc as plsc

F32 = jnp.float32
BF16 = jnp.bfloat16

HEAD_DIM = 64
LANES = 128
A_Q_HEADS = 12
A_KV_HEADS = 4
A_GROUP = A_Q_HEADS // A_KV_HEADS
WINDOW = 128
B_HEADS = 4
C_HEADS = 12
N_BRANCH = 3
N_EXPERTS = 8
TOP_K = 2
RMS_EPS = 1e-6
NEG_INF = -1e30
LOG2E = math.log2(math.e)
Q_SCALE = HEAD_DIM ** -0.5 * LOG2E
VMEM_LIMIT = 56 * 1024 * 1024

A_SLOT_HEAD = (0, 3, 1, 4, 2, 5, 6, 9, 7, 10, 8, 11)


def _params(*sem):
    return pltpu.CompilerParams(dimension_semantics=sem, vmem_limit_bytes=VMEM_LIMIT)


def _rms(x, g):
    var = jnp.mean(x * x, axis=-1, keepdims=True)
    return x * lax.rsqrt(var + RMS_EPS) * g


def _sigmoid(x):
    return 0.5 * jnp.tanh(0.5 * x) + 0.5


def _pack_bf16_pairs(x):
    bits = pltpu.bitcast(x.astype(BF16).astype(F32), jnp.uint32)
    c = x.shape[1] // 2
    return (bits[:, :c] >> 16) | bits[:, c:]


def _unpack_bf16_pairs(words):
    lo = pltpu.bitcast(words << 16, F32)
    hi = pltpu.bitcast(words & jnp.uint32(0xFFFF0000), F32)
    return lo, hi


def _dot(a, b):
    return jnp.dot(a, b, preferred_element_type=F32)


def _dot_nt(a, b):
    return lax.dot_general(a, b, (((1,), (1,)), ((), ())), preferred_element_type=F32)


def _rmsnorm_kernel(x_ref, g_ref, o_ref):
    o_ref[...] = _rms(x_ref[...], g_ref[...]).astype(o_ref.dtype)


def rmsnorm_rows(x, g, *, tm=512):
    t, d = x.shape
    tm = min(tm, t)
    return pl.pallas_call(
        _rmsnorm_kernel,
        out_shape=jax.ShapeDtypeStruct((t, d), BF16),
        grid=(t // tm,),
        in_specs=[pl.BlockSpec((tm, d), lambda i: (i, 0)),
                  pl.BlockSpec((1, d), lambda i: (0, 0))],
        out_specs=pl.BlockSpec((tm, d), lambda i: (i, 0)),
        compiler_params=_params("parallel"),
        name="rmsnorm_rows",
    )(x, g.reshape(1, d))


def _mm_kernel(a_ref, bt_ref, o_ref):
    o_ref[...] = _dot_nt(a_ref[...], bt_ref[...]).astype(o_ref.dtype)


def _mm_scale_kernel(a_ref, bt_ref, scale_ref, o_ref):
    o_ref[...] = (_dot_nt(a_ref[...], bt_ref[...]) * scale_ref[...]).astype(o_ref.dtype)


def _mm_gate_kernel(a_ref, bt_ref, bias_ref, o_ref):
    z = _dot_nt(a_ref[...], bt_ref[...]) + bias_ref[...]
    o_ref[...] = _sigmoid(z).astype(o_ref.dtype)


def matmul(a, bt, *, bias=None, col_scale=None, out_dtype=BF16, tm=1024, tn=1024):
    m, k = a.shape
    n = bt.shape[0]
    tm, tn = min(tm, m), min(tn, n)
    assert m % tm == 0 and n % tn == 0
    in_specs = [pl.BlockSpec((tm, k), lambda i, j: (i, 0)),
                pl.BlockSpec((tn, k), lambda i, j: (j, 0))]
    args = [a, bt]
    kern = _mm_kernel
    if bias is not None:
        in_specs.append(pl.BlockSpec((1, tn), lambda i, j: (0, j)))
        args.append(bias.reshape(1, n).astype(F32))
        kern = _mm_gate_kernel
    elif col_scale is not None:
        in_specs.append(pl.BlockSpec((1, tn), lambda i, j: (0, j)))
        args.append(col_scale.reshape(1, n).astype(F32))
        kern = _mm_scale_kernel
    return pl.pallas_call(
        kern,
        out_shape=jax.ShapeDtypeStruct((m, n), out_dtype),
        grid=(m // tm, n // tn),
        in_specs=in_specs,
        out_specs=pl.BlockSpec((tm, tn), lambda i, j: (i, j)),
        compiler_params=_params("parallel", "parallel"),
        name="matmul_gate" if bias is not None else "matmul",
    )(*args)


def _forget_kernel(h_ref, wt_ref, b_ref, c_ref, *, seq, blk):
    z = _dot_nt(wt_ref[...], h_ref[0]) + b_ref[...]
    log_f = jnp.minimum(z, 0.0) - jnp.log1p(jnp.exp(-jnp.abs(z)))
    r = lax.broadcasted_iota(jnp.int32, (blk, blk), 0)
    c = lax.broadcasted_iota(jnp.int32, (blk, blk), 1)
    upper = (r <= c).astype(F32)
    carry = jnp.zeros((LANES, 1), F32)
    for j in range(seq // blk):
        part = lax.dot_general(log_f[:, j * blk:(j + 1) * blk], upper,
                               (((1,), (0,)), ((), ())),
                               precision=lax.Precision.HIGHEST,
                               preferred_element_type=F32) + carry
        c_ref[0, :, j * blk:(j + 1) * blk] = part
        carry = part[:, blk - 1:blk]


def forget_cumsum(h3, w_ft, b_f):
    b, s, d = h3.shape
    nh = w_ft.shape[0]
    wt = jnp.zeros((LANES, d), BF16).at[:nh].set(w_ft.astype(BF16))
    bias = jnp.zeros((LANES, 1), F32).at[:nh, 0].set(b_f.astype(F32))
    blk = min(256, s)
    return pl.pallas_call(
        functools.partial(_forget_kernel, seq=s, blk=blk),
        out_shape=jax.ShapeDtypeStruct((b, LANES, s), F32),
        grid=(b,),
        in_specs=[pl.BlockSpec((1, s, d), lambda i: (i, 0, 0)),
                  pl.BlockSpec((LANES, d), lambda i: (0, 0)),
                  pl.BlockSpec((LANES, 1), lambda i: (0, 0))],
        out_specs=pl.BlockSpec((1, LANES, s), lambda i: (i, 0, 0)),
        compiler_params=_params("parallel"),
        name="forget_cumsum",
    )(h3, wt, bias)


def _half_mask(tq, half):
    lane = lax.broadcasted_iota(jnp.int32, (tq, LANES), 1)
    return (lane >= HEAD_DIM * half) & (lane < HEAD_DIM * (half + 1))


def _causal_logits(qh, k_ref, lanes, qi, tq, bias):
    n0 = qi * tq
    r = lax.broadcasted_iota(jnp.int32, (tq, tq), 0)
    c = lax.broadcasted_iota(jnp.int32, (tq, tq), 1)
    diag = _dot_nt(qh, k_ref[0, n0:n0 + tq, lanes]) + bias(n0, tq)
    parts = [jnp.where(r >= c, diag, NEG_INF)]
    if qi > 0:
        parts.append(_dot_nt(qh, k_ref[0, :n0, lanes]) + bias(0, n0))
    return parts


def _softmax_parts(parts):
    m = functools.reduce(jnp.maximum, [s.max(axis=-1, keepdims=True) for s in parts])
    exps = [jnp.exp2(s - m) for s in parts]
    denom = functools.reduce(jnp.add, [e.sum(axis=-1, keepdims=True) for e in exps])
    return exps, denom


def _value_parts(v_ref, lanes, qi, tq):
    n0 = qi * tq
    return [v_ref[0, n0:n0 + tq, lanes]] + ([v_ref[0, :n0, lanes]] if qi > 0 else [])


def _weighted_values(weights, values):
    return functools.reduce(jnp.add, [_dot(w.astype(v.dtype), v) for w, v in zip(weights, values)])


ATTN_TILES_PER_STEP = 2


def _dispatch_on_step(tile, nq, tq):
    tps = min(ATTN_TILES_PER_STEP, nq)

    def step_body(s):
        for sub in range(tps):
            tile(qi=s * tps + sub, rows=slice(sub * tq, (sub + 1) * tq))

    step = pl.program_id(2)
    for s in range(nq // tps):
        pl.when(step == s)(functools.partial(step_body, s))


def _fox_kernel(q_ref, k_ref, v_ref, c_ref, o_ref, *, tq, nq, npair):
    masks = [_half_mask(tq, h) for h in range(2)]

    def tile(qi, rows):
        for p in range(npair):
            lanes = slice(p * LANES, (p + 1) * LANES)
            q = q_ref[0, rows, lanes]
            values = _value_parts(v_ref, lanes, qi, tq)
            outs = []
            for h in range(2):
                qh = jnp.where(masks[h], q, jnp.zeros_like(q))
                bias = lambda start, size: c_ref[0, 2 * p + h, :, start:start + size] * -LOG2E
                exps, denom = _softmax_parts(_causal_logits(qh, k_ref, lanes, qi, tq, bias))
                outs.append(_weighted_values(exps, values) / denom)
            o_ref[0, rows, lanes] = jnp.where(masks[0], outs[0], outs[1]).astype(o_ref.dtype)

    _dispatch_on_step(tile, nq, tq)


def _diff_kernel(slope_ref, lam_ref, q_ref, k_ref, v_ref, g_ref, o_ref, *,
                 tq, nq, nhead, out_scale):
    lam = lam_ref[0]
    masks = [_half_mask(tq, h) for h in range(2)]

    def tile(qi, rows):
        for j in range(nhead):
            lanes = slice(j * LANES, (j + 1) * LANES)
            slope = slope_ref[pl.program_id(1) * nhead + j] * LOG2E
            q = q_ref[0, rows, lanes]

            def bias(start, size):
                pos = start + lax.broadcasted_iota(jnp.int32, (1, size), 1)
                return slope * pos.astype(F32)

            maps = []
            for h in range(2):
                qh = jnp.where(masks[h], q, jnp.zeros_like(q))
                maps.append(_softmax_parts(_causal_logits(qh, k_ref, lanes, qi, tq, bias)))
            (e0, d0), (e1, d1) = maps
            w0, w1 = 1.0 / d0, lam / d1
            weights = [a * w0 - b * w1 for a, b in zip(e0, e1)]
            o = _weighted_values(weights, _value_parts(v_ref, lanes, qi, tq))
            o_ref[0, rows, lanes] = (_rms(o, g_ref[...]) * out_scale).astype(o_ref.dtype)

    _dispatch_on_step(tile, nq, tq)


def fox_attention(z3, c4, *, q_col, k_col, v_col, tq=256, npair=3):
    b, s, _ = z3.shape
    tq = min(tq, s)
    rows = tq * min(ATTN_TILES_PER_STEP, s // tq)
    w = npair * LANES
    groups = C_HEADS // 2 // npair
    assert q_col % npair == 0 and k_col % npair == 0 and v_col % npair == 0
    return pl.pallas_call(
        functools.partial(_fox_kernel, tq=tq, nq=s // tq, npair=npair),
        out_shape=jax.ShapeDtypeStruct((b, s, C_HEADS // 2 * LANES), BF16),
        grid=(b, groups, s // rows),
        in_specs=[pl.BlockSpec((1, rows, w), lambda bi, g, qi: (bi, qi, q_col // npair + g)),
                  pl.BlockSpec((1, s, w), lambda bi, g, qi: (bi, 0, k_col // npair + g)),
                  pl.BlockSpec((1, s, w), lambda bi, g, qi: (bi, 0, v_col // npair + g)),
                  pl.BlockSpec((1, 2 * npair, 1, s), lambda bi, g, qi: (bi, g, 0, 0))],
        out_specs=pl.BlockSpec((1, rows, w), lambda bi, g, qi: (bi, qi, g)),
        compiler_params=_params("parallel", "parallel", "arbitrary"),
        name="fox_attention",
    )(z3, z3, z3, c4)


def diff_attention(z3, slopes, lam, subln, *, q_col, k_col, v_col, out_scale, tq=256, nhead=2):
    b, s, _ = z3.shape
    tq = min(tq, s)
    rows = tq * min(ATTN_TILES_PER_STEP, s // tq)
    w = nhead * LANES
    assert q_col % nhead == 0 and k_col % nhead == 0 and v_col % nhead == 0
    smem = pl.BlockSpec(memory_space=pltpu.SMEM)
    return pl.pallas_call(
        functools.partial(_diff_kernel, tq=tq, nq=s // tq, nhead=nhead, out_scale=out_scale),
        out_shape=jax.ShapeDtypeStruct((b, s, B_HEADS * LANES), BF16),
        grid=(b, B_HEADS // nhead, s // rows),
        in_specs=[smem, smem,
                  pl.BlockSpec((1, rows, w), lambda bi, g, qi: (bi, qi, q_col // nhead + g)),
                  pl.BlockSpec((1, s, w), lambda bi, g, qi: (bi, 0, k_col // nhead + g)),
                  pl.BlockSpec((1, s, w), lambda bi, g, qi: (bi, 0, v_col // nhead + g)),
                  pl.BlockSpec((1, LANES), lambda bi, g, qi: (0, 0))],
        out_specs=pl.BlockSpec((1, rows, w), lambda bi, g, qi: (bi, qi, g)),
        compiler_params=_params("parallel", "parallel", "arbitrary"),
        name="diff_attention",
    )(slopes, lam, z3, z3, z3, subln.reshape(1, LANES).astype(F32))


def _swa_kernel(slope_ref, sink_ref, q_ref, k_ref, v_ref, o_ref, *, blk, nsub):
    r = lax.broadcasted_iota(jnp.int32, (blk, 2 * blk), 0)
    c = lax.broadcasted_iota(jnp.int32, (blk, 2 * blk), 1)
    dist = r + blk - c
    in_window = (dist >= 0) & (dist < WINDOW)
    dist_f = dist.astype(F32)
    masks = [_half_mask(blk, h) for h in range(2)]
    for sub in range(nsub):
        _swa_block(slope_ref, sink_ref, q_ref, k_ref, v_ref, o_ref, pl.program_id(1) * nsub + sub,
                   slice(sub * blk, (sub + 1) * blk), in_window, c, dist_f, masks, blk)


def _swa_block(slope_ref, sink_ref, q_ref, k_ref, v_ref, o_ref, qi, rows,
               in_window, c, dist_f, masks, blk):
    cur = pl.multiple_of(qi * blk, blk)
    prev = pl.multiple_of(jnp.maximum(qi - 1, 0) * blk, blk)
    keep = in_window & ((qi > 0) | (c >= blk))
    for tile in range(A_Q_HEADS // 2):
        q = q_ref[0, rows, tile * LANES:(tile + 1) * LANES]
        halves = []
        for half in range(2):
            slot = 2 * tile + half
            kv_tile = (A_SLOT_HEAD[slot] // A_GROUP) // 2
            lanes = slice(kv_tile * LANES, (kv_tile + 1) * LANES)
            k = jnp.concatenate([k_ref[0, pl.ds(prev, blk), lanes],
                                 k_ref[0, pl.ds(cur, blk), lanes]], axis=0)
            v = jnp.concatenate([v_ref[0, pl.ds(prev, blk), lanes],
                                 v_ref[0, pl.ds(cur, blk), lanes]], axis=0)
            qh = jnp.where(masks[half], q, jnp.zeros_like(q))
            s = _dot_nt(qh, k) - (slope_ref[slot] * LOG2E) * dist_f
            s = jnp.where(keep, s, NEG_INF)
            sink = sink_ref[slot] * LOG2E
            m = jnp.maximum(jnp.max(s, axis=-1, keepdims=True), sink)
            p = jnp.exp2(s - m)
            denom = jnp.sum(p, axis=-1, keepdims=True) + jnp.exp2(sink - m)
            halves.append(_dot(p.astype(v.dtype), v) / denom)
        o_ref[0, rows, tile * LANES:(tile + 1) * LANES] = jnp.where(
            masks[0], halves[0], halves[1]).astype(o_ref.dtype)


SWA_BLOCKS_PER_STEP = 8


def swa_attention(z3, slopes, sinks, *, q_tile, k_tile, v_tile):
    b, s, _ = z3.shape
    blk = WINDOW
    nsub = min(SWA_BLOCKS_PER_STEP, s // blk)
    qw = A_Q_HEADS * HEAD_DIM
    kw = A_KV_HEADS * HEAD_DIM
    smem = pl.BlockSpec(memory_space=pltpu.SMEM)
    return pl.pallas_call(
        functools.partial(_swa_kernel, blk=blk, nsub=nsub),
        out_shape=jax.ShapeDtypeStruct((b, s, qw), BF16),
        grid=(b, s // (blk * nsub)),
        in_specs=[smem, smem,
                  pl.BlockSpec((1, blk * nsub, qw), lambda bi, qi: (bi, qi, q_tile)),
                  pl.BlockSpec((1, s, kw), lambda bi, qi: (bi, 0, k_tile)),
                  pl.BlockSpec((1, s, kw), lambda bi, qi: (bi, 0, v_tile))],
        out_specs=pl.BlockSpec((1, blk * nsub, qw), lambda bi, qi: (bi, qi, 0)),
        compiler_params=_params("parallel", "parallel"),
        name="swa_attention",
    )(slopes, sinks, z3, z3, z3)


def _merge_out_kernel(oa_ref, ob_ref, oc_ref, ga_ref, gb_ref, gc_ref, wa_ref, wb_ref, wc_ref,
                      wo_ref, x_ref, gp_ref, gn_ref, xo_ref, ho_ref):
    merged = ga_ref[...].astype(F32) * _dot(oa_ref[...], wa_ref[...])
    merged += gb_ref[...].astype(F32) * _dot(ob_ref[...], wb_ref[...])
    merged += gc_ref[...].astype(F32) * _dot(oc_ref[...], wc_ref[...])
    m = _dot(merged.astype(wo_ref.dtype), wo_ref[...])
    x_new = x_ref[...] + _rms(m, gp_ref[...])
    xo_ref[...] = x_new
    ho_ref[...] = _rms(x_new, gn_ref[...]).astype(ho_ref.dtype)


def merge_out_residual(oa, ob, oc, gates, wa, wb, wc, w_out, x, g_post, g_next, *, tm=256):
    t, d = x.shape
    tm = min(tm, t)
    row = lambda w: pl.BlockSpec((tm, w), lambda i: (i, 0))
    gate = lambda br: pl.BlockSpec((tm, d), lambda i: (i, br))
    whole = lambda w: pl.BlockSpec(w.shape, lambda i: (0, 0), pipeline_mode=pl.Buffered(1))
    vec = pl.BlockSpec((1, d), lambda i: (0, 0))
    return pl.pallas_call(
        _merge_out_kernel,
        out_shape=(jax.ShapeDtypeStruct((t, d), F32), jax.ShapeDtypeStruct((t, d), BF16)),
        grid=(t // tm,),
        in_specs=[row(oa.shape[1]), row(ob.shape[1]), row(oc.shape[1]),
                  gate(0), gate(1), gate(2),
                  whole(wa), whole(wb), whole(wc), whole(w_out), row(d), vec, vec],
        out_specs=(row(d), row(d)),
        compiler_params=_params("parallel"),
        name="merge_out_residual",
    )(oa, ob, oc, gates, gates, gates, wa, wb, wc, w_out, x,
      g_post.reshape(1, d), g_next.reshape(1, d))


def _proj_res_kernel(a_ref, w_ref, x_ref, gp_ref, gn_ref, xo_ref, ho_ref, acc_ref, *, nk):
    kk = pl.program_id(1)
    part = _dot(a_ref[...], w_ref[...])

    @pl.when(kk == 0)
    def _():
        acc_ref[...] = part

    @pl.when(kk > 0)
    def _():
        acc_ref[...] += part

    @pl.when(kk == nk - 1)
    def _():
        x_new = x_ref[...] + _rms(acc_ref[...], gp_ref[...])
        xo_ref[...] = x_new
        ho_ref[...] = _rms(x_new, gn_ref[...]).astype(ho_ref.dtype)


def proj_residual(a, w, x, g_post, g_next, *, tm, tk):
    t, k = a.shape
    d = w.shape[1]
    tm, tk = min(tm, t), min(tk, k)
    nk = k // tk
    vec = pl.BlockSpec((1, d), lambda i, kk: (0, 0))
    w_mode = dict(pipeline_mode=pl.Buffered(1)) if nk == 1 else {}
    return pl.pallas_call(
        functools.partial(_proj_res_kernel, nk=nk),
        out_shape=(jax.ShapeDtypeStruct((t, d), F32), jax.ShapeDtypeStruct((t, d), BF16)),
        grid=(t // tm, nk),
        in_specs=[pl.BlockSpec((tm, tk), lambda i, kk: (i, kk)),
                  pl.BlockSpec((tk, d), lambda i, kk: (kk, 0), **w_mode),
                  pl.BlockSpec((tm, d), lambda i, kk: (i, 0)),
                  vec, vec],
        out_specs=(pl.BlockSpec((tm, d), lambda i, kk: (i, 0)),
                   pl.BlockSpec((tm, d), lambda i, kk: (i, 0))),
        scratch_shapes=[pltpu.VMEM((tm, d), F32)],
        compiler_params=_params("parallel", "arbitrary"),
        name="proj_residual",
    )(a, w, x, g_post.reshape(1, d), g_next.reshape(1, d))


def _swiglu_up_kernel(h_ref, wg_ref, wu_ref, o_ref):
    h = h_ref[...]
    g = _dot(h, wg_ref[...])
    u = _dot(h, wu_ref[...])
    o_ref[...] = (g * _sigmoid(g) * u).astype(o_ref.dtype)


def swiglu_up(h, wg, wu, *, tm=1024, tn=512):
    t, d = h.shape
    f = wg.shape[1]
    tm, tn = min(tm, t), min(tn, f)
    return pl.pallas_call(
        _swiglu_up_kernel,
        out_shape=jax.ShapeDtypeStruct((t, f), BF16),
        grid=(t // tm, f // tn),
        in_specs=[pl.BlockSpec((tm, d), lambda i, j: (i, 0)),
                  pl.BlockSpec((d, tn), lambda i, j: (0, j)),
                  pl.BlockSpec((d, tn), lambda i, j: (0, j))],
        out_specs=pl.BlockSpec((tm, tn), lambda i, j: (i, j)),
        compiler_params=_params("parallel", "parallel"),
        name="swiglu_up",
    )(h, wg, wu)


def _router_kernel(x_ref, g_ref, w_ref, idx_ref, wt_ref, h_ref):
    h = _rms(x_ref[...], g_ref[...])
    h_ref[...] = _pack_bf16_pairs(h)
    logits = jnp.dot(h, w_ref[...], precision=lax.Precision.HIGHEST,
                     preferred_element_type=F32)
    lane = lax.broadcasted_iota(jnp.int32, logits.shape, 1)
    lane_f = lane.astype(F32)
    lg = jnp.where(lane < N_EXPERTS, logits, -jnp.inf)
    m1 = jnp.max(lg, axis=-1, keepdims=True)
    i1 = jnp.min(jnp.where(lg == m1, lane_f, float(LANES)), axis=-1, keepdims=True)
    lg2 = jnp.where(lane_f == i1, -jnp.inf, lg)
    m2 = jnp.max(lg2, axis=-1, keepdims=True)
    i2 = jnp.min(jnp.where(lg2 == m2, lane_f, float(LANES)), axis=-1, keepdims=True)
    e2 = jnp.exp(m2 - m1)
    w1 = 1.0 / (1.0 + e2)
    w2 = e2 / (1.0 + e2)
    idx_ref[...] = jnp.where(lane == 0, i1, jnp.where(lane == 1, i2, 0.0)).astype(jnp.int32)
    wt_ref[...] = jnp.where(lane == 0, w1, jnp.where(lane == 1, w2, 0.0))


def router_top2(x, g, w_router, *, tm=512):
    t, d = x.shape
    tm = min(tm, t)
    wp = jnp.zeros((d, LANES), F32).at[:, :N_EXPERTS].set(w_router.astype(F32))
    out = pl.BlockSpec((tm, LANES), lambda i: (i, 0))
    idx, wt, h = pl.pallas_call(
        _router_kernel,
        out_shape=(jax.ShapeDtypeStruct((t, LANES), jnp.int32),
                   jax.ShapeDtypeStruct((t, LANES), F32),
                   jax.ShapeDtypeStruct((t, d // 2), jnp.uint32)),
        grid=(t // tm,),
        in_specs=[pl.BlockSpec((tm, d), lambda i: (i, 0)),
                  pl.BlockSpec((1, d), lambda i: (0, 0)),
                  pl.BlockSpec((d, LANES), lambda i: (0, 0))],
        out_specs=(out, out, pl.BlockSpec((tm, d // 2), lambda i: (i, 0))),
        compiler_params=_params("parallel"),
        name="router_top2",
    )(x, g.reshape(1, d), wp)
    return idx[:, :TOP_K], wt[:, :TOP_K], h


SC_INDEX_WINDOW = 128
SC_COPY_BYTES = 256 * 1024


def sc_gather_rows(src, idx):
    n = idx.shape[0]
    d = src.shape[1]
    batch = min(SC_INDEX_WINDOW, SC_COPY_BYTES // (d * src.dtype.itemsize))
    assert n % SC_INDEX_WINDOW == 0 and SC_INDEX_WINDOW % batch == 0
    mesh = plsc.VectorSubcoreMesh(core_axis_name="c", subcore_axis_name="s")
    dst = jnp.arange(n, dtype=jnp.int32).reshape(1, n)

    @pl.kernel(out_type=jax.ShapeDtypeStruct((n, d), src.dtype), mesh=mesh,
               scratch_types=[pltpu.VMEM((batch, d), src.dtype)])
    def gather(x_hbm, i_hbm, d_hbm, o_hbm, buf):
        def body(i_vmem, d_vmem):
            for s in range(SC_INDEX_WINDOW // batch):
                rows = pl.ds(s * batch, batch)
                pltpu.sync_copy(x_hbm.at[i_vmem.at[0, rows]], buf)
                pltpu.sync_copy(buf, o_hbm.at[d_vmem.at[0, rows]])

        window = pl.BlockSpec((1, SC_INDEX_WINDOW), lambda i: (0, i))
        pltpu.emit_pipeline(
            body, grid=(n // SC_INDEX_WINDOW,),
            in_specs=[window, window], out_specs=[],
            core_axis_name=("c", "s"),
            dimension_semantics=(pltpu.PARALLEL,),
        )(i_hbm, d_hbm)

    return gather(src, idx.reshape(1, n), dst)


def _moe_up_kernel(te_ref, na_ref, x_ref, wg_ref, wu_ref, *rest):
    o_ref = rest[-1]
    active = pl.program_id(1) < na_ref[0]

    @pl.when(active)
    def _():
        half = x_ref.shape[1]
        lo, hi = (v.astype(BF16) for v in _unpack_bf16_pairs(x_ref[...]))
        g = _dot(lo, wg_ref[0, :half]) + _dot(hi, wg_ref[0, half:])
        u = _dot(lo, wu_ref[0, :half]) + _dot(hi, wu_ref[0, half:])
        o_ref[...] = (g * _sigmoid(g) * u).astype(o_ref.dtype)

    @pl.when(jnp.logical_not(active))
    def _():
        o_ref[...] = jnp.zeros_like(o_ref)


def _moe_down_kernel(te_ref, na_ref, u_ref, wd_ref, o_ref):
    active = pl.program_id(1) < na_ref[0]

    @pl.when(active)
    def _():
        o_ref[...] = _pack_bf16_pairs(_dot(u_ref[...], wd_ref[0]))

    @pl.when(jnp.logical_not(active))
    def _():
        o_ref[...] = jnp.zeros_like(o_ref)


def _active_tile(i, na_ref):
    return jnp.maximum(jnp.minimum(i, na_ref[0] - 1), 0)


def moe_up(xs, tile_expert, n_active, wg, wu, u_prev, *, tile0, n_tiles, tm, tn=1024):
    d, f = wg.shape[1:]
    tn = min(tn, f)
    assert f % tn == 0
    in_specs = [pl.BlockSpec((tm, d // 2), lambda j, i, te, na: (_active_tile(i, na), 0)),
                pl.BlockSpec((1, d, tn), lambda j, i, te, na: (te[_active_tile(i, na)], 0, j)),
                pl.BlockSpec((1, d, tn), lambda j, i, te, na: (te[_active_tile(i, na)], 0, j))]
    args = [tile_expert, n_active, xs, wg, wu]
    aliases = {}
    if u_prev is not None:
        in_specs.append(pl.BlockSpec(memory_space=pl.ANY))
        args.append(u_prev)
        aliases = {len(args) - 1: 0}
    grid_spec = pltpu.PrefetchScalarGridSpec(
        num_scalar_prefetch=2,
        grid=(f // tn, xs.shape[0] // tm),
        in_specs=in_specs,
        out_specs=pl.BlockSpec((tm, tn), lambda j, i, te, na: (i + tile0, j)),
    )
    return pl.pallas_call(
        _moe_up_kernel,
        out_shape=jax.ShapeDtypeStruct((n_tiles * tm, f), BF16),
        grid_spec=grid_spec,
        input_output_aliases=aliases,
        compiler_params=_params("arbitrary", "arbitrary"),
        name="moe_up",
    )(*args)


MOE_DOWN_TN = 512


def moe_down(u, tile_expert, n_active, wd, *, tm):
    p, f = u.shape
    d = wd.shape[2]
    tn = min(MOE_DOWN_TN, d)
    grid_spec = pltpu.PrefetchScalarGridSpec(
        num_scalar_prefetch=2,
        grid=(d // tn, p // tm),
        in_specs=[pl.BlockSpec((tm, f), lambda j, i, te, na: (_active_tile(i, na), 0)),
                  pl.BlockSpec((1, f, tn), lambda j, i, te, na: (te[_active_tile(i, na)], 0, j))],
        out_specs=pl.BlockSpec((tm, tn // 2), lambda j, i, te, na: (i, j)),
    )
    return pl.pallas_call(
        _moe_down_kernel,
        out_shape=jax.ShapeDtypeStruct((p, d // 2), jnp.uint32),
        grid_spec=grid_spec,
        compiler_params=_params("arbitrary", "arbitrary"),
        name="moe_down",
    )(tile_expert, n_active, u, wd)


def _unpack_expert_rows(words, half):
    lo, hi = _unpack_bf16_pairs(words)
    parts = []
    for b in range(words.shape[1] // half):
        parts += [lo[:, b * half:(b + 1) * half], hi[:, b * half:(b + 1) * half]]
    return jnp.concatenate(parts, axis=1)


def _combine_kernel(ya_ref, yb_ref, w_ref, x_ref, g_ref, o_ref, *, half):
    w = w_ref[...]
    y = (w[:, 0:1] * _unpack_expert_rows(ya_ref[...], half)
         + w[:, 1:2] * _unpack_expert_rows(yb_ref[...], half))
    o_ref[...] = x_ref[...] + _rms(y, g_ref[...])


def combine_residual(y2, wts, x, g, *, tm=512):
    t, d = x.shape
    tm = min(tm, t)
    nt = t // tm
    blk = pl.BlockSpec((tm, d), lambda i: (i, 0))
    return pl.pallas_call(
        functools.partial(_combine_kernel, half=min(MOE_DOWN_TN, d) // 2),
        out_shape=jax.ShapeDtypeStruct((t, d), F32),
        grid=(nt,),
        in_specs=[pl.BlockSpec((tm, d // 2), lambda i: (i, 0)),
                  pl.BlockSpec((tm, d // 2), lambda i: (i + nt, 0)),
                  pl.BlockSpec((tm, TOP_K), lambda i: (i, 0)), blk,
                  pl.BlockSpec((1, d), lambda i: (0, 0))],
        out_specs=blk,
        compiler_params=_params("parallel"),
        name="combine_residual",
    )(y2, y2, wts, x, g.reshape(1, d))


def _alibi_slopes(n):
    return jnp.exp2(-8.0 * (jnp.arange(n, dtype=F32) + 1.0) / n)


def _token_mixer(x2, h, batch, seq, layer_idx, w_in, b_forget, b_gate, sinks,
                 lq1, lk1, lq2, lk2, subln, w_br_a, w_br_b, w_br_c, w_out,
                 g_post, g_next):
    t, d = x2.shape
    a_w = A_Q_HEADS * HEAD_DIM
    kv_w = A_KV_HEADS * HEAD_DIM
    b_w = B_HEADS * 2 * HEAD_DIM
    c_w = C_HEADS * HEAD_DIM
    n_attn = a_w + 2 * kv_w + 3 * b_w + 3 * c_w

    slot_cols = np.concatenate([np.arange(HEAD_DIM) + HEAD_DIM * hd for hd in A_SLOT_HEAD])
    a_end = a_w + 2 * kv_w
    b_end = a_end + 3 * b_w
    w_in_t = jnp.transpose(w_in, (2, 0, 1))
    rows = lambda lo, hi: w_in_t[lo:hi, layer_idx, :]
    w_attn_t = jnp.concatenate([rows(b_end, n_attn), rows(a_end, b_end),
                                rows(0, a_w)[slot_cols], rows(a_w, a_end)],
                               axis=0).astype(BF16)
    w_f_t = rows(n_attn, n_attn + C_HEADS)
    w_g_t = rows(n_attn + C_HEADS, w_in.shape[2]).astype(BF16)

    q_cols = np.zeros((n_attn,), bool)
    for lo, width in ((0, c_w), (3 * c_w, b_w), (3 * c_w + 3 * b_w, a_w)):
        q_cols[lo:lo + width] = True
    z3 = matmul(h, w_attn_t, col_scale=jnp.where(q_cols, Q_SCALE, 1.0)).reshape(batch, seq, n_attn)
    gates = matmul(h, w_g_t, bias=b_gate)
    c = forget_cumsum(h.reshape(batch, seq, d), w_f_t, b_forget)
    c4 = c[:, :C_HEADS].reshape(batch, C_HEADS, 1, seq)

    oc = fox_attention(z3, c4, q_col=0, k_col=C_HEADS // 2, v_col=C_HEADS)

    lam_init = 0.8 - 0.6 * math.exp(-0.3 * layer_idx)
    lam = (jnp.exp(jnp.sum(lq1.astype(F32) * lk1.astype(F32)))
           - jnp.exp(jnp.sum(lq2.astype(F32) * lk2.astype(F32))) + lam_init)
    b0 = 3 * c_w // LANES
    ob = diff_attention(z3, _alibi_slopes(B_HEADS), lam.reshape(1), subln,
                        q_col=b0, k_col=b0 + B_HEADS, v_col=b0 + 2 * B_HEADS,
                        out_scale=1.0 - lam_init)

    slot = np.array(A_SLOT_HEAD)
    a0 = 3 * c_w + 3 * b_w
    oa = swa_attention(z3, _alibi_slopes(A_Q_HEADS)[slot], sinks.astype(F32)[slot],
                       q_tile=a0 // a_w, k_tile=(a0 + a_w) // kv_w,
                       v_tile=(a0 + a_w) // kv_w + 1)

    return merge_out_residual(oa.reshape(t, a_w), ob.reshape(t, b_w), oc.reshape(t, c_w), gates,
                              w_br_a[slot_cols].astype(BF16), w_br_b.astype(BF16),
                              w_br_c.astype(BF16), w_out.astype(BF16), x2, g_post, g_next)


COUNT_BLOCK = 256
MOE_PIECES = 4


def _moe_dispatch(idx, tm):
    t = idx.shape[0]
    n_rows = t * TOP_K
    n_tiles = (n_rows + N_EXPERTS * (tm - 1)) // tm
    e_flat = idx.reshape(n_rows)
    onehot = e_flat[:, None] == jnp.arange(N_EXPERTS)[None, :]
    blk = min(COUNT_BLOCK, n_rows)
    oh = onehot.astype(F32).reshape(n_rows // blk, blk, N_EXPERTS)
    lower = (jnp.arange(blk)[:, None] >= jnp.arange(blk)[None, :]).astype(F32)
    within = jnp.einsum("ij,bjk->bik", lower, oh)
    before = jnp.cumsum(within[:, -1, :], axis=0) - within[:, -1, :]
    running = (within + before[:, None, :]).reshape(n_rows, N_EXPERTS)
    rank = jnp.sum(jnp.where(onehot, running - 1.0, 0.0), axis=1).astype(jnp.int32)
    counts = jnp.sum(onehot.astype(jnp.int32), axis=0)
    padded = ((counts + tm - 1) // tm) * tm
    ends = jnp.cumsum(padded)
    starts = ends - padded
    pos = starts[e_flat] + rank
    row_token = jnp.zeros((n_tiles * tm,), jnp.int32).at[pos].set(
        jnp.arange(n_rows, dtype=jnp.int32) // TOP_K)
    tile_start = jnp.arange(n_tiles, dtype=jnp.int32) * tm
    tile_expert = jnp.minimum(jnp.sum(tile_start[:, None] >= ends[None, :], axis=1),
                              N_EXPERTS - 1).astype(jnp.int32)
    n_active = (ends[-1] // tm).astype(jnp.int32).reshape(1)
    return pos.reshape(t, TOP_K), row_token, tile_expert, n_active


def _moe_ffn(x2, g_pre, g_post, w_router, e_gate, e_up, e_down, *, tm=512):
    idx, wts, h = router_top2(x2, g_pre, w_router)
    t = x2.shape[0]
    tm = min(tm, t)
    pos, row_token, tile_expert, n_active = _moe_dispatch(idx, tm)
    n_tiles = tile_expert.shape[0]
    bounds = [k * n_tiles // MOE_PIECES for k in range(MOE_PIECES + 1)]
    u = None
    for lo, hi in zip(bounds[:-1], bounds[1:]):
        xs = sc_gather_rows(h, row_token[lo * tm:hi * tm])
        u = moe_up(xs, tile_expert[lo:hi], jnp.clip(n_active - lo, 0, hi - lo), e_gate, e_up, u,
                   tile0=lo, n_tiles=n_tiles, tm=tm)
    ys = moe_down(u, tile_expert, n_active, e_down, tm=tm)
    y2 = sc_gather_rows(ys, pos.T.reshape(TOP_K * t))
    return combine_residual(y2, wts, x2, g_post)


def kernel(x, mix_pre_norm, w_in, b_forget, b_gate, attn_sinks, lam_q1, lam_k1, lam_q2, lam_k2, diff_subln, w_br_a, w_br_b, w_br_c, w_out, mix_post_norm, ffn_pre_norm, ffn_post_norm, dense_w_gate, dense_w_up, dense_w_down, w_router, moe_w_gate, moe_w_up, moe_w_down):
    batch, seq, d = x.shape
    depth = w_in.shape[0]
    x2 = x.reshape(batch * seq, d)
    h = rmsnorm_rows(x2, mix_pre_norm[0])
    for l in range(depth):
        x2, h = _token_mixer(x2, h, batch, seq, l, w_in, b_forget[l], b_gate[l],
                             attn_sinks[l], lam_q1[l], lam_k1[l], lam_q2[l], lam_k2[l],
                             diff_subln[l], w_br_a[l], w_br_b[l], w_br_c[l], w_out[l],
                             mix_post_norm[l], ffn_pre_norm[l])
        i = l // 2
        if l % 2 == 0:
            u = swiglu_up(h, dense_w_gate[i].astype(BF16), dense_w_up[i].astype(BF16))
            g_next = mix_pre_norm[l + 1] if l + 1 < depth else jnp.ones((d,), F32)
            f = dense_w_down.shape[1]
            x2, h = proj_residual(u, dense_w_down[i].astype(BF16), x2, ffn_post_norm[l],
                                  g_next, tm=256, tk=f)
        else:
            x2 = _moe_ffn(x2, ffn_pre_norm[l], ffn_post_norm[l], w_router[i],
                          moe_w_gate[i], moe_w_up[i], moe_w_down[i])
            if l + 1 < depth:
                h = rmsnorm_rows(x2, mix_pre_norm[l + 1])
    return x2.reshape(batch, seq, d)
```

```python
import functools
import math

import jax
import jax.numpy as jnp
import numpy as np
from jax import lax
from jax.experimental import pallas as pl
from jax.experimental.pallas import tpu as pltpu
from jax.experimental.pallas import tpu_sc as plsc

F32 = jnp.float32
BF16 = jnp.bfloat16

HEAD_DIM = 64
LANES = 128
A_Q_HEADS = 12
A_KV_HEADS = 4
A_GROUP = A_Q_HEADS // A_KV_HEADS
WINDOW = 128
B_HEADS = 4
C_HEADS = 12
N_BRANCH = 3
N_EXPERTS = 8
TOP_K = 2
RMS_EPS = 1e-6
NEG_INF = -1e30
LOG2E = math.log2(math.e)
Q_SCALE = HEAD_DIM ** -0.5 * LOG2E
VMEM_LIMIT = 56 * 1024 * 1024

A_SLOT_HEAD = (0, 3, 1, 4, 2, 5, 6, 9, 7, 10, 8, 11)


def _params(*sem):
    return pltpu.CompilerParams(dimension_semantics=sem, vmem_limit_bytes=VMEM_LIMIT)


def _rms(x, g):
    var = jnp.mean(x * x, axis=-1, keepdims=True)
    return x * lax.rsqrt(var + RMS_EPS) * g


def _sigmoid(x):
    return 0.5 * jnp.tanh(0.5 * x) + 0.5


def _pack_bf16_pairs(x):
    bits = pltpu.bitcast(x.astype(BF16).astype(F32), jnp.uint32)
    c = x.shape[1] // 2
    return (bits[:, :c] >> 16) | bits[:, c:]


def _unpack_bf16_pairs(words):
    lo = pltpu.bitcast(words << 16, F32)
    hi = pltpu.bitcast(words & jnp.uint32(0xFFFF0000), F32)
    return lo, hi


def _dot(a, b):
    return jnp.dot(a, b, preferred_element_type=F32)


def _dot_nt(a, b):
    return lax.dot_general(a, b, (((1,), (1,)), ((), ())), preferred_element_type=F32)


def _rmsnorm_kernel(x_ref, g_ref, o_ref):
    o_ref[...] = _rms(x_ref[...], g_ref[...]).astype(o_ref.dtype)


def rmsnorm_rows(x, g, *, tm=512):
    t, d = x.shape
    tm = min(tm, t)
    return pl.pallas_call(
        _rmsnorm_kernel,
        out_shape=jax.ShapeDtypeStruct((t, d), BF16),
        grid=(t // tm,),
        in_specs=[pl.BlockSpec((tm, d), lambda i: (i, 0)),
                  pl.BlockSpec((1, d), lambda i: (0, 0))],
        out_specs=pl.BlockSpec((tm, d), lambda i: (i, 0)),
        compiler_params=_params("parallel"),
        name="rmsnorm_rows",
    )(x, g.reshape(1, d))


def _mm_kernel(a_ref, bt_ref, o_ref):
    o_ref[...] = _dot_nt(a_ref[...], bt_ref[...]).astype(o_ref.dtype)


def _mm_scale_kernel(a_ref, bt_ref, scale_ref, o_ref):
    o_ref[...] = (_dot_nt(a_ref[...], bt_ref[...]) * scale_ref[...]).astype(o_ref.dtype)


def _mm_gate_kernel(a_ref, bt_ref, bias_ref, o_ref):
    z = _dot_nt(a_ref[...], bt_ref[...]) + bias_ref[...]
    o_ref[...] = _sigmoid(z).astype(o_ref.dtype)


def matmul(a, bt, *, bias=None, col_scale=None, out_dtype=BF16, tm=2048, tn=1024):
    m, k = a.shape
    n = bt.shape[0]
    tm, tn = min(tm, m), min(tn, n)
    assert m % tm == 0 and n % tn == 0
    in_specs = [pl.BlockSpec((tm, k), lambda i, j: (i, 0)),
                pl.BlockSpec((tn, k), lambda i, j: (j, 0))]
    args = [a, bt]
    kern = _mm_kernel
    if bias is not None:
        in_specs.append(pl.BlockSpec((1, tn), lambda i, j: (0, j)))
        args.append(bias.reshape(1, n).astype(F32))
        kern = _mm_gate_kernel
    elif col_scale is not None:
        in_specs.append(pl.BlockSpec((1, tn), lambda i, j: (0, j)))
        args.append(col_scale.reshape(1, n).astype(F32))
        kern = _mm_scale_kernel
    return pl.pallas_call(
        kern,
        out_shape=jax.ShapeDtypeStruct((m, n), out_dtype),
        grid=(m // tm, n // tn),
        in_specs=in_specs,
        out_specs=pl.BlockSpec((tm, tn), lambda i, j: (i, j)),
        compiler_params=_params("parallel", "parallel"),
        name="matmul_gate" if bias is not None else "matmul",
    )(*args)


def _forget_kernel(h_ref, wt_ref, b_ref, c_ref, *, seq, blk):
    z = _dot_nt(wt_ref[...], h_ref[0]) + b_ref[...]
    log_f = jnp.minimum(z, 0.0) - jnp.log1p(jnp.exp(-jnp.abs(z)))
    r = lax.broadcasted_iota(jnp.int32, (blk, blk), 0)
    c = lax.broadcasted_iota(jnp.int32, (blk, blk), 1)
    upper = (r <= c).astype(F32)
    carry = jnp.zeros((LANES, 1), F32)
    for j in range(seq // blk):
        part = lax.dot_general(log_f[:, j * blk:(j + 1) * blk], upper,
                               (((1,), (0,)), ((), ())),
                               precision=lax.Precision.HIGHEST,
                               preferred_element_type=F32) + carry
        c_ref[0, :, j * blk:(j + 1) * blk] = part
        carry = part[:, blk - 1:blk]


def forget_cumsum(h3, w_ft, b_f):
    b, s, d = h3.shape
    nh = w_ft.shape[0]
    wt = jnp.zeros((LANES, d), BF16).at[:nh].set(w_ft.astype(BF16))
    bias = jnp.zeros((LANES, 1), F32).at[:nh, 0].set(b_f.astype(F32))
    blk = min(256, s)
    return pl.pallas_call(
        functools.partial(_forget_kernel, seq=s, blk=blk),
        out_shape=jax.ShapeDtypeStruct((b, LANES, s), F32),
        grid=(b,),
        in_specs=[pl.BlockSpec((1, s, d), lambda i: (i, 0, 0)),
                  pl.BlockSpec((LANES, d), lambda i: (0, 0)),
                  pl.BlockSpec((LANES, 1), lambda i: (0, 0))],
        out_specs=pl.BlockSpec((1, LANES, s), lambda i: (i, 0, 0)),
        compiler_params=_params("parallel"),
        name="forget_cumsum",
    )(h3, wt, bias)


def _half_mask(tq, half):
    lane = lax.broadcasted_iota(jnp.int32, (tq, LANES), 1)
    return (lane >= HEAD_DIM * half) & (lane < HEAD_DIM * (half + 1))


def _causal_logits(qh, k_ref, lanes, qi, tq, bias):
    n0 = qi * tq
    r = lax.broadcasted_iota(jnp.int32, (tq, tq), 0)
    c = lax.broadcasted_iota(jnp.int32, (tq, tq), 1)
    diag = _dot_nt(qh, k_ref[0, n0:n0 + tq, lanes]) + bias(n0, tq)
    parts = [jnp.where(r >= c, diag, NEG_INF)]
    if qi > 0:
        parts.append(_dot_nt(qh, k_ref[0, :n0, lanes]) + bias(0, n0))
    return parts


def _softmax_parts(parts):
    m = functools.reduce(jnp.maximum, [s.max(axis=-1, keepdims=True) for s in parts])
    exps = [jnp.exp2(s - m) for s in parts]
    denom = functools.reduce(jnp.add, [e.sum(axis=-1, keepdims=True) for e in exps])
    return exps, denom


def _value_parts(v_ref, lanes, qi, tq):
    n0 = qi * tq
    return [v_ref[0, n0:n0 + tq, lanes]] + ([v_ref[0, :n0, lanes]] if qi > 0 else [])


def _weighted_values(weights, values):
    return functools.reduce(jnp.add, [_dot(w.astype(v.dtype), v) for w, v in zip(weights, values)])


ATTN_TILES_PER_STEP = 2


def _dispatch_on_step(tile, nq, tq):
    tps = min(ATTN_TILES_PER_STEP, nq)

    def step_body(s):
        for sub in range(tps):
            tile(qi=s * tps + sub, rows=slice(sub * tq, (sub + 1) * tq))

    step = pl.program_id(2)
    for s in range(nq // tps):
        pl.when(step == s)(functools.partial(step_body, s))


def _fox_kernel(q_ref, k_ref, v_ref, c_ref, o_ref, *, tq, nq, npair):
    masks = [_half_mask(tq, h) for h in range(2)]

    def tile(qi, rows):
        for p in range(npair):
            lanes = slice(p * LANES, (p + 1) * LANES)
            q = q_ref[0, rows, lanes]
            values = _value_parts(v_ref, lanes, qi, tq)
            outs = []
            for h in range(2):
                qh = jnp.where(masks[h], q, jnp.zeros_like(q))
                bias = lambda start, size: c_ref[0, 2 * p + h, :, start:start + size] * -LOG2E
                exps, denom = _softmax_parts(_causal_logits(qh, k_ref, lanes, qi, tq, bias))
                outs.append(_weighted_values(exps, values) / denom)
            o_ref[0, rows, lanes] = jnp.where(masks[0], outs[0], outs[1]).astype(o_ref.dtype)

    _dispatch_on_step(tile, nq, tq)


def _diff_kernel(slope_ref, lam_ref, q_ref, k_ref, v_ref, g_ref, o_ref, *,
                 tq, nq, nhead, out_scale):
    lam = lam_ref[0]
    masks = [_half_mask(tq, h) for h in range(2)]

    def tile(qi, rows):
        for j in range(nhead):
            lanes = slice(j * LANES, (j + 1) * LANES)
            slope = slope_ref[pl.program_id(1) * nhead + j] * LOG2E
            q = q_ref[0, rows, lanes]

            def bias(start, size):
                pos = start + lax.broadcasted_iota(jnp.int32, (1, size), 1)
                return slope * pos.astype(F32)

            maps = []
            for h in range(2):
                qh = jnp.where(masks[h], q, jnp.zeros_like(q))
                maps.append(_softmax_parts(_causal_logits(qh, k_ref, lanes, qi, tq, bias)))
            (e0, d0), (e1, d1) = maps
            w0, w1 = 1.0 / d0, lam / d1
            weights = [a * w0 - b * w1 for a, b in zip(e0, e1)]
            o = _weighted_values(weights, _value_parts(v_ref, lanes, qi, tq))
            o_ref[0, rows, lanes] = (_rms(o, g_ref[...]) * out_scale).astype(o_ref.dtype)

    _dispatch_on_step(tile, nq, tq)


def fox_attention(z3, c4, *, q_col, k_col, v_col, tq=256, npair=3):
    b, s, _ = z3.shape
    tq = min(tq, s)
    rows = tq * min(ATTN_TILES_PER_STEP, s // tq)
    w = npair * LANES
    groups = C_HEADS // 2 // npair
    assert q_col % npair == 0 and k_col % npair == 0 and v_col % npair == 0
    return pl.pallas_call(
        functools.partial(_fox_kernel, tq=tq, nq=s // tq, npair=npair),
        out_shape=jax.ShapeDtypeStruct((b, s, C_HEADS // 2 * LANES), BF16),
        grid=(b, groups, s // rows),
        in_specs=[pl.BlockSpec((1, rows, w), lambda bi, g, qi: (bi, qi, q_col // npair + g)),
                  pl.BlockSpec((1, s, w), lambda bi, g, qi: (bi, 0, k_col // npair + g)),
                  pl.BlockSpec((1, s, w), lambda bi, g, qi: (bi, 0, v_col // npair + g)),
                  pl.BlockSpec((1, 2 * npair, 1, s), lambda bi, g, qi: (bi, g, 0, 0))],
        out_specs=pl.BlockSpec((1, rows, w), lambda bi, g, qi: (bi, qi, g)),
        compiler_params=_params("parallel", "parallel", "arbitrary"),
        name="fox_attention",
    )(z3, z3, z3, c4)


def diff_attention(z3, slopes, lam, subln, *, q_col, k_col, v_col, out_scale, tq=256, nhead=2):
    b, s, _ = z3.shape
    tq = min(tq, s)
    rows = tq * min(ATTN_TILES_PER_STEP, s // tq)
    w = nhead * LANES
    assert q_col % nhead == 0 and k_col % nhead == 0 and v_col % nhead == 0
    smem = pl.BlockSpec(memory_space=pltpu.SMEM)
    return pl.pallas_call(
        functools.partial(_diff_kernel, tq=tq, nq=s // tq, nhead=nhead, out_scale=out_scale),
        out_shape=jax.ShapeDtypeStruct((b, s, B_HEADS * LANES), BF16),
        grid=(b, B_HEADS // nhead, s // rows),
        in_specs=[smem, smem,
                  pl.BlockSpec((1, rows, w), lambda bi, g, qi: (bi, qi, q_col // nhead + g)),
                  pl.BlockSpec((1, s, w), lambda bi, g, qi: (bi, 0, k_col // nhead + g)),
                  pl.BlockSpec((1, s, w), lambda bi, g, qi: (bi, 0, v_col // nhead + g)),
                  pl.BlockSpec((1, LANES), lambda bi, g, qi: (0, 0))],
        out_specs=pl.BlockSpec((1, rows, w), lambda bi, g, qi: (bi, qi, g)),
        compiler_params=_params("parallel", "parallel", "arbitrary"),
        name="diff_attention",
    )(slopes, lam, z3, z3, z3, subln.reshape(1, LANES).astype(F32))


def _swa_kernel(slope_ref, sink_ref, q_ref, k_ref, v_ref, o_ref, *, blk, nsub):
    r = lax.broadcasted_iota(jnp.int32, (blk, 2 * blk), 0)
    c = lax.broadcasted_iota(jnp.int32, (blk, 2 * blk), 1)
    dist = r + blk - c
    in_window = (dist >= 0) & (dist < WINDOW)
    dist_f = dist.astype(F32)
    masks = [_half_mask(blk, h) for h in range(2)]
    for sub in range(nsub):
        _swa_block(slope_ref, sink_ref, q_ref, k_ref, v_ref, o_ref, pl.program_id(1) * nsub + sub,
                   slice(sub * blk, (sub + 1) * blk), in_window, c, dist_f, masks, blk)


def _swa_block(slope_ref, sink_ref, q_ref, k_ref, v_ref, o_ref, qi, rows,
               in_window, c, dist_f, masks, blk):
    cur = pl.multiple_of(qi * blk, blk)
    prev = pl.multiple_of(jnp.maximum(qi - 1, 0) * blk, blk)
    keep = in_window & ((qi > 0) | (c >= blk))
    for tile in range(A_Q_HEADS // 2):
        q = q_ref[0, rows, tile * LANES:(tile + 1) * LANES]
        halves = []
        for half in range(2):
            slot = 2 * tile + half
            kv_tile = (A_SLOT_HEAD[slot] // A_GROUP) // 2
            lanes = slice(kv_tile * LANES, (kv_tile + 1) * LANES)
            k = jnp.concatenate([k_ref[0, pl.ds(prev, blk), lanes],
                                 k_ref[0, pl.ds(cur, blk), lanes]], axis=0)
            v = jnp.concatenate([v_ref[0, pl.ds(prev, blk), lanes],
                                 v_ref[0, pl.ds(cur, blk), lanes]], axis=0)
            qh = jnp.where(masks[half], q, jnp.zeros_like(q))
            s = _dot_nt(qh, k) - (slope_ref[slot] * LOG2E) * dist_f
            s = jnp.where(keep, s, NEG_INF)
            sink = sink_ref[slot] * LOG2E
            m = jnp.maximum(jnp.max(s, axis=-1, keepdims=True), sink)
            p = jnp.exp2(s - m)
            denom = jnp.sum(p, axis=-1, keepdims=True) + jnp.exp2(sink - m)
            halves.append(_dot(p.astype(v.dtype), v) / denom)
        o_ref[0, rows, tile * LANES:(tile + 1) * LANES] = jnp.where(
            masks[0], halves[0], halves[1]).astype(o_ref.dtype)


SWA_BLOCKS_PER_STEP = 8


def swa_attention(z3, slopes, sinks, *, q_tile, k_tile, v_tile):
    b, s, _ = z3.shape
    blk = WINDOW
    nsub = min(SWA_BLOCKS_PER_STEP, s // blk)
    qw = A_Q_HEADS * HEAD_DIM
    kw = A_KV_HEADS * HEAD_DIM
    smem = pl.BlockSpec(memory_space=pltpu.SMEM)
    return pl.pallas_call(
        functools.partial(_swa_kernel, blk=blk, nsub=nsub),
        out_shape=jax.ShapeDtypeStruct((b, s, qw), BF16),
        grid=(b, s // (blk * nsub)),
        in_specs=[smem, smem,
                  pl.BlockSpec((1, blk * nsub, qw), lambda bi, qi: (bi, qi, q_tile)),
                  pl.BlockSpec((1, s, kw), lambda bi, qi: (bi, 0, k_tile)),
                  pl.BlockSpec((1, s, kw), lambda bi, qi: (bi, 0, v_tile))],
        out_specs=pl.BlockSpec((1, blk * nsub, qw), lambda bi, qi: (bi, qi, 0)),
        compiler_params=_params("parallel", "parallel"),
        name="swa_attention",
    )(slopes, sinks, z3, z3, z3)


def _merge_out_kernel(oa_ref, ob_ref, oc_ref, ga_ref, gb_ref, gc_ref, wa_ref, wb_ref, wc_ref,
                      wo_ref, x_ref, gp_ref, gn_ref, xo_ref, ho_ref):
    merged = ga_ref[...].astype(F32) * _dot(oa_ref[...], wa_ref[...])
    merged += gb_ref[...].astype(F32) * _dot(ob_ref[...], wb_ref[...])
    merged += gc_ref[...].astype(F32) * _dot(oc_ref[...], wc_ref[...])
    m = _dot(merged.astype(wo_ref.dtype), wo_ref[...])
    x_new = x_ref[...] + _rms(m, gp_ref[...])
    xo_ref[...] = x_new
    ho_ref[...] = _rms(x_new, gn_ref[...]).astype(ho_ref.dtype)


def merge_out_residual(oa, ob, oc, gates, wa, wb, wc, w_out, x, g_post, g_next, *, tm=256):
    t, d = x.shape
    tm = min(tm, t)
    row = lambda w: pl.BlockSpec((tm, w), lambda i: (i, 0))
    gate = lambda br: pl.BlockSpec((tm, d), lambda i: (i, br))
    whole = lambda w: pl.BlockSpec(w.shape, lambda i: (0, 0), pipeline_mode=pl.Buffered(1))
    vec = pl.BlockSpec((1, d), lambda i: (0, 0))
    return pl.pallas_call(
        _merge_out_kernel,
        out_shape=(jax.ShapeDtypeStruct((t, d), F32), jax.ShapeDtypeStruct((t, d), BF16)),
        grid=(t // tm,),
        in_specs=[row(oa.shape[1]), row(ob.shape[1]), row(oc.shape[1]),
                  gate(0), gate(1), gate(2),
                  whole(wa), whole(wb), whole(wc), whole(w_out), row(d), vec, vec],
        out_specs=(row(d), row(d)),
        compiler_params=_params("parallel"),
        name="merge_out_residual",
    )(oa, ob, oc, gates, gates, gates, wa, wb, wc, w_out, x,
      g_post.reshape(1, d), g_next.reshape(1, d))


def _proj_res_kernel(a_ref, w_ref, x_ref, gp_ref, gn_ref, xo_ref, ho_ref, acc_ref, *, nk):
    kk = pl.program_id(1)
    part = _dot(a_ref[...], w_ref[...])

    @pl.when(kk == 0)
    def _():
        acc_ref[...] = part

    @pl.when(kk > 0)
    def _():
        acc_ref[...] += part

    @pl.when(kk == nk - 1)
    def _():
        x_new = x_ref[...] + _rms(acc_ref[...], gp_ref[...])
        xo_ref[...] = x_new
        ho_ref[...] = _rms(x_new, gn_ref[...]).astype(ho_ref.dtype)


def proj_residual(a, w, x, g_post, g_next, *, tm, tk):
    t, k = a.shape
    d = w.shape[1]
    tm, tk = min(tm, t), min(tk, k)
    nk = k // tk
    vec = pl.BlockSpec((1, d), lambda i, kk: (0, 0))
    w_mode = dict(pipeline_mode=pl.Buffered(1)) if nk == 1 else {}
    return pl.pallas_call(
        functools.partial(_proj_res_kernel, nk=nk),
        out_shape=(jax.ShapeDtypeStruct((t, d), F32), jax.ShapeDtypeStruct((t, d), BF16)),
        grid=(t // tm, nk),
        in_specs=[pl.BlockSpec((tm, tk), lambda i, kk: (i, kk)),
                  pl.BlockSpec((tk, d), lambda i, kk: (kk, 0), **w_mode),
                  pl.BlockSpec((tm, d), lambda i, kk: (i, 0)),
                  vec, vec],
        out_specs=(pl.BlockSpec((tm, d), lambda i, kk: (i, 0)),
                   pl.BlockSpec((tm, d), lambda i, kk: (i, 0))),
        scratch_shapes=[pltpu.VMEM((tm, d), F32)],
        compiler_params=_params("parallel", "arbitrary"),
        name="proj_residual",
    )(a, w, x, g_post.reshape(1, d), g_next.reshape(1, d))


def _swiglu_up_kernel(h_ref, wg_ref, wu_ref, o_ref):
    h = h_ref[...]
    g = _dot(h, wg_ref[...])
    u = _dot(h, wu_ref[...])
    o_ref[...] = (g * _sigmoid(g) * u).astype(o_ref.dtype)


def swiglu_up(h, wg, wu, *, tm=1024, tn=512):
    t, d = h.shape
    f = wg.shape[1]
    tm, tn = min(tm, t), min(tn, f)
    return pl.pallas_call(
        _swiglu_up_kernel,
        out_shape=jax.ShapeDtypeStruct((t, f), BF16),
        grid=(t // tm, f // tn),
        in_specs=[pl.BlockSpec((tm, d), lambda i, j: (i, 0)),
                  pl.BlockSpec((d, tn), lambda i, j: (0, j)),
                  pl.BlockSpec((d, tn), lambda i, j: (0, j))],
        out_specs=pl.BlockSpec((tm, tn), lambda i, j: (i, j)),
        compiler_params=_params("parallel", "parallel"),
        name="swiglu_up",
    )(h, wg, wu)


def _router_kernel(x_ref, g_ref, w_ref, idx_ref, wt_ref, h_ref):
    h = _rms(x_ref[...], g_ref[...])
    h_ref[...] = _pack_bf16_pairs(h)
    w = w_ref[...]
    h_hi, w_hi = h.astype(BF16), w.astype(BF16)
    h_lo = (h - h_hi.astype(F32)).astype(BF16)
    w_lo = (w - w_hi.astype(F32)).astype(BF16)
    logits = _dot(h_hi, w_hi) + (_dot(h_hi, w_lo) + _dot(h_lo, w_hi))
    lane = lax.broadcasted_iota(jnp.int32, logits.shape, 1)
    lane_f = lane.astype(F32)
    lg = jnp.where(lane < N_EXPERTS, logits, -jnp.inf)
    m1 = jnp.max(lg, axis=-1, keepdims=True)
    i1 = jnp.min(jnp.where(lg == m1, lane_f, float(LANES)), axis=-1, keepdims=True)
    lg2 = jnp.where(lane_f == i1, -jnp.inf, lg)
    m2 = jnp.max(lg2, axis=-1, keepdims=True)
    i2 = jnp.min(jnp.where(lg2 == m2, lane_f, float(LANES)), axis=-1, keepdims=True)
    e2 = jnp.exp(m2 - m1)
    w1 = 1.0 / (1.0 + e2)
    w2 = e2 / (1.0 + e2)
    idx_ref[...] = jnp.where(lane == 0, i1, jnp.where(lane == 1, i2, 0.0)).astype(jnp.int32)
    wt_ref[...] = jnp.where(lane == 0, w1, jnp.where(lane == 1, w2, 0.0))


def router_top2(x, g, w_router, *, tm=512):
    t, d = x.shape
    tm = min(tm, t)
    wp = jnp.zeros((d, LANES), F32).at[:, :N_EXPERTS].set(w_router.astype(F32))
    out = pl.BlockSpec((tm, LANES), lambda i: (i, 0))
    idx, wt, h = pl.pallas_call(
        _router_kernel,
        out_shape=(jax.ShapeDtypeStruct((t, LANES), jnp.int32),
                   jax.ShapeDtypeStruct((t, LANES), F32),
                   jax.ShapeDtypeStruct((t, d // 2), jnp.uint32)),
        grid=(t // tm,),
        in_specs=[pl.BlockSpec((tm, d), lambda i: (i, 0)),
                  pl.BlockSpec((1, d), lambda i: (0, 0)),
                  pl.BlockSpec((d, LANES), lambda i: (0, 0))],
        out_specs=(out, out, pl.BlockSpec((tm, d // 2), lambda i: (i, 0))),
        compiler_params=_params("parallel"),
        name="router_top2",
    )(x, g.reshape(1, d), wp)
    return idx[:, :TOP_K], wt[:, :TOP_K], h


SC_INDEX_WINDOW = 128
SC_COPY_BYTES = 256 * 1024


def sc_gather_rows(src, idx):
    n = idx.shape[0]
    d = src.shape[1]
    batch = min(SC_INDEX_WINDOW, SC_COPY_BYTES // (d * src.dtype.itemsize))
    assert n % SC_INDEX_WINDOW == 0 and SC_INDEX_WINDOW % batch == 0
    mesh = plsc.VectorSubcoreMesh(core_axis_name="c", subcore_axis_name="s")
    dst = jnp.arange(n, dtype=jnp.int32).reshape(1, n)

    @pl.kernel(out_type=jax.ShapeDtypeStruct((n, d), src.dtype), mesh=mesh,
               scratch_types=[pltpu.VMEM((batch, d), src.dtype)])
    def gather(x_hbm, i_hbm, d_hbm, o_hbm, buf):
        def body(i_vmem, d_vmem):
            for s in range(SC_INDEX_WINDOW // batch):
                rows = pl.ds(s * batch, batch)
                pltpu.sync_copy(x_hbm.at[i_vmem.at[0, rows]], buf)
                pltpu.sync_copy(buf, o_hbm.at[d_vmem.at[0, rows]])

        window = pl.BlockSpec((1, SC_INDEX_WINDOW), lambda i: (0, i))
        pltpu.emit_pipeline(
            body, grid=(n // SC_INDEX_WINDOW,),
            in_specs=[window, window], out_specs=[],
            core_axis_name=("c", "s"),
            dimension_semantics=(pltpu.PARALLEL,),
        )(i_hbm, d_hbm)

    return gather(src, idx.reshape(1, n), dst)


def _moe_up_kernel(te_ref, na_ref, x_ref, wg_ref, wu_ref, *rest):
    o_ref = rest[-1]
    active = pl.program_id(1) < na_ref[0]

    @pl.when(active)
    def _():
        half = x_ref.shape[1]
        lo, hi = (v.astype(BF16) for v in _unpack_bf16_pairs(x_ref[...]))
        g = _dot(lo, wg_ref[0, :half]) + _dot(hi, wg_ref[0, half:])
        u = _dot(lo, wu_ref[0, :half]) + _dot(hi, wu_ref[0, half:])
        o_ref[...] = (g * _sigmoid(g) * u).astype(o_ref.dtype)

    @pl.when(jnp.logical_not(active))
    def _():
        o_ref[...] = jnp.zeros_like(o_ref)


def _moe_down_kernel(te_ref, na_ref, u_ref, wd_ref, o_ref):
    active = pl.program_id(1) < na_ref[0]

    @pl.when(active)
    def _():
        o_ref[...] = _pack_bf16_pairs(_dot(u_ref[...], wd_ref[0]))

    @pl.when(jnp.logical_not(active))
    def _():
        o_ref[...] = jnp.zeros_like(o_ref)


def _active_tile(i, na_ref):
    return jnp.maximum(jnp.minimum(i, na_ref[0] - 1), 0)


def moe_up(xs, tile_expert, n_active, wg, wu, u_prev, *, tile0, n_tiles, tm, tn=1024):
    d, f = wg.shape[1:]
    tn = min(tn, f)
    assert f % tn == 0
    in_specs = [pl.BlockSpec((tm, d // 2), lambda j, i, te, na: (_active_tile(i, na), 0)),
                pl.BlockSpec((1, d, tn), lambda j, i, te, na: (te[_active_tile(i, na)], 0, j)),
                pl.BlockSpec((1, d, tn), lambda j, i, te, na: (te[_active_tile(i, na)], 0, j))]
    args = [tile_expert, n_active, xs, wg, wu]
    aliases = {}
    if u_prev is not None:
        in_specs.append(pl.BlockSpec(memory_space=pl.ANY))
        args.append(u_prev)
        aliases = {len(args) - 1: 0}
    grid_spec = pltpu.PrefetchScalarGridSpec(
        num_scalar_prefetch=2,
        grid=(f // tn, xs.shape[0] // tm),
        in_specs=in_specs,
        out_specs=pl.BlockSpec((tm, tn), lambda j, i, te, na: (i + tile0, j)),
    )
    return pl.pallas_call(
        _moe_up_kernel,
        out_shape=jax.ShapeDtypeStruct((n_tiles * tm, f), BF16),
        grid_spec=grid_spec,
        input_output_aliases=aliases,
        compiler_params=_params("arbitrary", "arbitrary"),
        name="moe_up",
    )(*args)


MOE_DOWN_TN = 512


def moe_down(u, tile_expert, n_active, wd, *, tm):
    p, f = u.shape
    d = wd.shape[2]
    tn = min(MOE_DOWN_TN, d)
    grid_spec = pltpu.PrefetchScalarGridSpec(
        num_scalar_prefetch=2,
        grid=(d // tn, p // tm),
        in_specs=[pl.BlockSpec((tm, f), lambda j, i, te, na: (_active_tile(i, na), 0)),
                  pl.BlockSpec((1, f, tn), lambda j, i, te, na: (te[_active_tile(i, na)], 0, j))],
        out_specs=pl.BlockSpec((tm, tn // 2), lambda j, i, te, na: (i, j)),
    )
    return pl.pallas_call(
        _moe_down_kernel,
        out_shape=jax.ShapeDtypeStruct((p, d // 2), jnp.uint32),
        grid_spec=grid_spec,
        compiler_params=_params("arbitrary", "arbitrary"),
        name="moe_down",
    )(tile_expert, n_active, u, wd)


def _unpack_expert_rows(words, half):
    lo, hi = _unpack_bf16_pairs(words)
    parts = []
    for b in range(words.shape[1] // half):
        parts += [lo[:, b * half:(b + 1) * half], hi[:, b * half:(b + 1) * half]]
    return jnp.concatenate(parts, axis=1)


def _combine_kernel(ya_ref, yb_ref, w_ref, x_ref, g_ref, o_ref, *, half):
    w = w_ref[...]
    y = (w[:, 0:1] * _unpack_expert_rows(ya_ref[...], half)
         + w[:, 1:2] * _unpack_expert_rows(yb_ref[...], half))
    o_ref[...] = x_ref[...] + _rms(y, g_ref[...])


def combine_residual(y2, wts, x, g, *, tm=512):
    t, d = x.shape
    tm = min(tm, t)
    nt = t // tm
    blk = pl.BlockSpec((tm, d), lambda i: (i, 0))
    return pl.pallas_call(
        functools.partial(_combine_kernel, half=min(MOE_DOWN_TN, d) // 2),
        out_shape=jax.ShapeDtypeStruct((t, d), F32),
        grid=(nt,),
        in_specs=[pl.BlockSpec((tm, d // 2), lambda i: (i, 0)),
                  pl.BlockSpec((tm, d // 2), lambda i: (i + nt, 0)),
                  pl.BlockSpec((tm, TOP_K), lambda i: (i, 0)), blk,
                  pl.BlockSpec((1, d), lambda i: (0, 0))],
        out_specs=blk,
        compiler_params=_params("parallel"),
        name="combine_residual",
    )(y2, y2, wts, x, g.reshape(1, d))


def _alibi_slopes(n):
    return jnp.exp2(-8.0 * (jnp.arange(n, dtype=F32) + 1.0) / n)


def _token_mixer(x2, h, batch, seq, layer_idx, w_in, b_forget, b_gate, sinks,
                 lq1, lk1, lq2, lk2, subln, w_br_a, w_br_b, w_br_c, w_out,
                 g_post, g_next):
    t, d = x2.shape
    a_w = A_Q_HEADS * HEAD_DIM
    kv_w = A_KV_HEADS * HEAD_DIM
    b_w = B_HEADS * 2 * HEAD_DIM
    c_w = C_HEADS * HEAD_DIM
    n_attn = a_w + 2 * kv_w + 3 * b_w + 3 * c_w

    slot_cols = np.concatenate([np.arange(HEAD_DIM) + HEAD_DIM * hd for hd in A_SLOT_HEAD])
    a_end = a_w + 2 * kv_w
    b_end = a_end + 3 * b_w
    w_in_t = jnp.transpose(w_in, (2, 0, 1))
    rows = lambda lo, hi: w_in_t[lo:hi, layer_idx, :]
    w_attn_t = jnp.concatenate([rows(b_end, n_attn), rows(a_end, b_end),
                                rows(0, a_w)[slot_cols], rows(a_w, a_end)],
                               axis=0).astype(BF16)
    w_f_t = rows(n_attn, n_attn + C_HEADS)
    w_g_t = rows(n_attn + C_HEADS, w_in.shape[2]).astype(BF16)

    q_cols = np.zeros((n_attn,), bool)
    for lo, width in ((0, c_w), (3 * c_w, b_w), (3 * c_w + 3 * b_w, a_w)):
        q_cols[lo:lo + width] = True
    z3 = matmul(h, w_attn_t, col_scale=jnp.where(q_cols, Q_SCALE, 1.0)).reshape(batch, seq, n_attn)
    gates = matmul(h, w_g_t, bias=b_gate)
    c = forget_cumsum(h.reshape(batch, seq, d), w_f_t, b_forget)
    c4 = c[:, :C_HEADS].reshape(batch, C_HEADS, 1, seq)

    oc = fox_attention(z3, c4, q_col=0, k_col=C_HEADS // 2, v_col=C_HEADS)

    lam_init = 0.8 - 0.6 * math.exp(-0.3 * layer_idx)
    lam = (jnp.exp(jnp.sum(lq1.astype(F32) * lk1.astype(F32)))
           - jnp.exp(jnp.sum(lq2.astype(F32) * lk2.astype(F32))) + lam_init)
    b0 = 3 * c_w // LANES
    ob = diff_attention(z3, _alibi_slopes(B_HEADS), lam.reshape(1), subln,
                        q_col=b0, k_col=b0 + B_HEADS, v_col=b0 + 2 * B_HEADS,
                        out_scale=1.0 - lam_init)

    slot = np.array(A_SLOT_HEAD)
    a0 = 3 * c_w + 3 * b_w
    oa = swa_attention(z3, _alibi_slopes(A_Q_HEADS)[slot], sinks.astype(F32)[slot],
                       q_tile=a0 // a_w, k_tile=(a0 + a_w) // kv_w,
                       v_tile=(a0 + a_w) // kv_w + 1)

    return merge_out_residual(oa.reshape(t, a_w), ob.reshape(t, b_w), oc.reshape(t, c_w), gates,
                              w_br_a[slot_cols].astype(BF16), w_br_b.astype(BF16),
                              w_br_c.astype(BF16), w_out.astype(BF16), x2, g_post, g_next)


COUNT_BLOCK = 256
MOE_PIECES = 4


def _moe_dispatch(idx, tm):
    t = idx.shape[0]
    n_rows = t * TOP_K
    n_tiles = (n_rows + N_EXPERTS * (tm - 1)) // tm
    e_flat = idx.reshape(n_rows)
    onehot = e_flat[:, None] == jnp.arange(N_EXPERTS)[None, :]
    blk = min(COUNT_BLOCK, n_rows)
    oh = onehot.astype(F32).reshape(n_rows // blk, blk, N_EXPERTS)
    lower = (jnp.arange(blk)[:, None] >= jnp.arange(blk)[None, :]).astype(F32)
    within = jnp.einsum("ij,bjk->bik", lower, oh)
    before = jnp.cumsum(within[:, -1, :], axis=0) - within[:, -1, :]
    running = (within + before[:, None, :]).reshape(n_rows, N_EXPERTS)
    rank = jnp.sum(jnp.where(onehot, running - 1.0, 0.0), axis=1).astype(jnp.int32)
    counts = jnp.sum(onehot.astype(jnp.int32), axis=0)
    padded = ((counts + tm - 1) // tm) * tm
    ends = jnp.cumsum(padded)
    starts = ends - padded
    pos = starts[e_flat] + rank
    row_token = jnp.zeros((n_tiles * tm,), jnp.int32).at[pos].set(
        jnp.arange(n_rows, dtype=jnp.int32) // TOP_K, unique_indices=True)
    tile_start = jnp.arange(n_tiles, dtype=jnp.int32) * tm
    tile_expert = jnp.minimum(jnp.sum(tile_start[:, None] >= ends[None, :], axis=1),
                              N_EXPERTS - 1).astype(jnp.int32)
    n_active = (ends[-1] // tm).astype(jnp.int32).reshape(1)
    return pos.reshape(t, TOP_K), row_token, tile_expert, n_active


def _moe_ffn(x2, g_pre, g_post, w_router, e_gate, e_up, e_down, *, tm=512):
    idx, wts, h = router_top2(x2, g_pre, w_router)
    t = x2.shape[0]
    tm = min(tm, t)
    pos, row_token, tile_expert, n_active = _moe_dispatch(idx, tm)
    n_tiles = tile_expert.shape[0]
    bounds = [k * n_tiles // MOE_PIECES for k in range(MOE_PIECES + 1)]
    u = None
    for lo, hi in zip(bounds[:-1], bounds[1:]):
        xs = sc_gather_rows(h, row_token[lo * tm:hi * tm])
        u = moe_up(xs, tile_expert[lo:hi], jnp.clip(n_active - lo, 0, hi - lo), e_gate, e_up, u,
                   tile0=lo, n_tiles=n_tiles, tm=tm)
    ys = moe_down(u, tile_expert, n_active, e_down, tm=tm)
    y2 = sc_gather_rows(ys, pos.T.reshape(TOP_K * t))
    return combine_residual(y2, wts, x2, g_post)


def kernel(x, mix_pre_norm, w_in, b_forget, b_gate, attn_sinks, lam_q1, lam_k1, lam_q2, lam_k2, diff_subln, w_br_a, w_br_b, w_br_c, w_out, mix_post_norm, ffn_pre_norm, ffn_post_norm, dense_w_gate, dense_w_up, dense_w_down, w_router, moe_w_gate, moe_w_up, moe_w_down):
    batch, seq, d = x.shape
    depth = w_in.shape[0]
    x2 = x.reshape(batch * seq, d)
    h = rmsnorm_rows(x2, mix_pre_norm[0])
    for l in range(depth):
        x2, h = _token_mixer(x2, h, batch, seq, l, w_in, b_forget[l], b_gate[l],
                             attn_sinks[l], lam_q1[l], lam_k1[l], lam_q2[l], lam_k2[l],
                             diff_subln[l], w_br_a[l], w_br_b[l], w_br_c[l], w_out[l],
                             mix_post_norm[l], ffn_pre_norm[l])
        i = l // 2
        if l % 2 == 0:
            u = swiglu_up(h, dense_w_gate[i].astype(BF16), dense_w_up[i].astype(BF16))
            g_next = mix_pre_norm[l + 1] if l + 1 < depth else jnp.ones((d,), F32)
            f = dense_w_down.shape[1]
            x2, h = proj_residual(u, dense_w_down[i].astype(BF16), x2, ffn_post_norm[l],
                                  g_next, tm=256, tk=f)
        else:
            x2 = _moe_ffn(x2, ffn_pre_norm[l], ffn_post_norm[l], w_router[i],
                          moe_w_gate[i], moe_w_up[i], moe_w_down[i])
            if l + 1 < depth:
                h = rmsnorm_rows(x2, mix_pre_norm[l + 1])
    return x2.reshape(batch, seq, d)
```

```python
import functools
import math

import jax
import jax.numpy as jnp
import numpy as np
from jax import lax
from jax.experimental import pallas as pl
from jax.experimental.pallas import tpu as pltpu
from jax.experimental.pallas import tpu_sc as plsc

F32 = jnp.float32
BF16 = jnp.bfloat16

HEAD_DIM = 64
LANES = 128
A_Q_HEADS = 12
A_KV_HEADS = 4
A_GROUP = A_Q_HEADS // A_KV_HEADS
WINDOW = 128
B_HEADS = 4
C_HEADS = 12
N_BRANCH = 3
N_EXPERTS = 8
TOP_K = 2
RMS_EPS = 1e-6
NEG_INF = -1e30
LOG2E = math.log2(math.e)
Q_SCALE = HEAD_DIM ** -0.5 * LOG2E
VMEM_LIMIT = 56 * 1024 * 1024

A_SLOT_HEAD = (0, 3, 1, 4, 2, 5, 6, 9, 7, 10, 8, 11)


def _params(*sem):
    return pltpu.CompilerParams(dimension_semantics=sem, vmem_limit_bytes=VMEM_LIMIT)


def _rms(x, g):
    var = jnp.mean(x * x, axis=-1, keepdims=True)
    return x * lax.rsqrt(var + RMS_EPS) * g


def _sigmoid(x):
    return 0.5 * jnp.tanh(0.5 * x) + 0.5


def _pack_bf16_pairs(x):
    bits = pltpu.bitcast(x.astype(BF16).astype(F32), jnp.uint32)
    c = x.shape[1] // 2
    return (bits[:, :c] >> 16) | bits[:, c:]


def _unpack_bf16_pairs(words):
    lo = pltpu.bitcast(words << 16, F32)
    hi = pltpu.bitcast(words & jnp.uint32(0xFFFF0000), F32)
    return lo, hi


def _dot(a, b):
    return jnp.dot(a, b, preferred_element_type=F32)


def _dot_nt(a, b):
    return lax.dot_general(a, b, (((1,), (1,)), ((), ())), preferred_element_type=F32)


def _rmsnorm_kernel(x_ref, g_ref, o_ref):
    o_ref[...] = _rms(x_ref[...], g_ref[...]).astype(o_ref.dtype)


def rmsnorm_rows(x, g, *, tm=512):
    t, d = x.shape
    tm = min(tm, t)
    return pl.pallas_call(
        _rmsnorm_kernel,
        out_shape=jax.ShapeDtypeStruct((t, d), BF16),
        grid=(t // tm,),
        in_specs=[pl.BlockSpec((tm, d), lambda i: (i, 0)),
                  pl.BlockSpec((1, d), lambda i: (0, 0))],
        out_specs=pl.BlockSpec((tm, d), lambda i: (i, 0)),
        compiler_params=_params("parallel"),
        name="rmsnorm_rows",
    )(x, g.reshape(1, d))


def _mm_kernel(a_ref, bt_ref, o_ref):
    o_ref[...] = _dot_nt(a_ref[...], bt_ref[...]).astype(o_ref.dtype)


def _mm_scale_kernel(a_ref, bt_ref, scale_ref, o_ref):
    o_ref[...] = (_dot_nt(a_ref[...], bt_ref[...]) * scale_ref[...]).astype(o_ref.dtype)


def _mm_gate_kernel(a_ref, bt_ref, bias_ref, o_ref):
    z = _dot_nt(a_ref[...], bt_ref[...]) + bias_ref[...]
    o_ref[...] = _sigmoid(z).astype(o_ref.dtype)


def matmul(a, bt, *, bias=None, col_scale=None, out_dtype=BF16, tm=2048, tn=1024):
    m, k = a.shape
    n = bt.shape[0]
    tm, tn = min(tm, m), min(tn, n)
    assert m % tm == 0 and n % tn == 0
    in_specs = [pl.BlockSpec((tm, k), lambda i, j: (i, 0)),
                pl.BlockSpec((tn, k), lambda i, j: (j, 0))]
    args = [a, bt]
    kern = _mm_kernel
    if bias is not None:
        in_specs.append(pl.BlockSpec((1, tn), lambda i, j: (0, j)))
        args.append(bias.reshape(1, n).astype(F32))
        kern = _mm_gate_kernel
    elif col_scale is not None:
        in_specs.append(pl.BlockSpec((1, tn), lambda i, j: (0, j)))
        args.append(col_scale.reshape(1, n).astype(F32))
        kern = _mm_scale_kernel
    return pl.pallas_call(
        kern,
        out_shape=jax.ShapeDtypeStruct((m, n), out_dtype),
        grid=(m // tm, n // tn),
        in_specs=in_specs,
        out_specs=pl.BlockSpec((tm, tn), lambda i, j: (i, j)),
        compiler_params=_params("parallel", "parallel"),
        name="matmul_gate" if bias is not None else "matmul",
    )(*args)


def _forget_kernel(h_ref, wt_ref, b_ref, c_ref, *, seq, blk):
    z = _dot_nt(wt_ref[...], h_ref[0]) + b_ref[...]
    log_f = jnp.minimum(z, 0.0) - jnp.log1p(jnp.exp(-jnp.abs(z)))
    r = lax.broadcasted_iota(jnp.int32, (blk, blk), 0)
    c = lax.broadcasted_iota(jnp.int32, (blk, blk), 1)
    upper = (r <= c).astype(F32)
    carry = jnp.zeros((LANES, 1), F32)
    for j in range(seq // blk):
        part = lax.dot_general(log_f[:, j * blk:(j + 1) * blk], upper,
                               (((1,), (0,)), ((), ())),
                               precision=lax.Precision.HIGHEST,
                               preferred_element_type=F32) + carry
        c_ref[0, :, j * blk:(j + 1) * blk] = part
        carry = part[:, blk - 1:blk]


def forget_cumsum(h3, w_ft, b_f):
    b, s, d = h3.shape
    nh = w_ft.shape[0]
    wt = jnp.zeros((LANES, d), BF16).at[:nh].set(w_ft.astype(BF16))
    bias = jnp.zeros((LANES, 1), F32).at[:nh, 0].set(b_f.astype(F32))
    blk = min(256, s)
    return pl.pallas_call(
        functools.partial(_forget_kernel, seq=s, blk=blk),
        out_shape=jax.ShapeDtypeStruct((b, LANES, s), F32),
        grid=(b,),
        in_specs=[pl.BlockSpec((1, s, d), lambda i: (i, 0, 0)),
                  pl.BlockSpec((LANES, d), lambda i: (0, 0)),
                  pl.BlockSpec((LANES, 1), lambda i: (0, 0))],
        out_specs=pl.BlockSpec((1, LANES, s), lambda i: (i, 0, 0)),
        compiler_params=_params("parallel"),
        name="forget_cumsum",
    )(h3, wt, bias)


def _half_mask(tq, half):
    lane = lax.broadcasted_iota(jnp.int32, (tq, LANES), 1)
    return (lane >= HEAD_DIM * half) & (lane < HEAD_DIM * (half + 1))


def _causal_logits(qh, k_ref, lanes, qi, tq, bias):
    n0 = qi * tq
    r = lax.broadcasted_iota(jnp.int32, (tq, tq), 0)
    c = lax.broadcasted_iota(jnp.int32, (tq, tq), 1)
    diag = _dot_nt(qh, k_ref[0, n0:n0 + tq, lanes]) + bias(n0, tq)
    parts = [jnp.where(r >= c, diag, NEG_INF)]
    if qi > 0:
        parts.append(_dot_nt(qh, k_ref[0, :n0, lanes]) + bias(0, n0))
    return parts


def _softmax_parts(parts):
    m = functools.reduce(jnp.maximum, [s.max(axis=-1, keepdims=True) for s in parts])
    exps = [jnp.exp2(s - m) for s in parts]
    denom = functools.reduce(jnp.add, [e.sum(axis=-1, keepdims=True) for e in exps])
    return exps, denom


def _value_parts(v_ref, lanes, qi, tq):
    n0 = qi * tq
    return [v_ref[0, n0:n0 + tq, lanes]] + ([v_ref[0, :n0, lanes]] if qi > 0 else [])


def _weighted_values(weights, values):
    return functools.reduce(jnp.add, [_dot(w.astype(v.dtype), v) for w, v in zip(weights, values)])


ATTN_TILES_PER_STEP = 2


def _dispatch_on_step(tile, nq, tq):
    tps = min(ATTN_TILES_PER_STEP, nq)

    def step_body(s):
        for sub in range(tps):
            tile(qi=s * tps + sub, rows=slice(sub * tq, (sub + 1) * tq))

    step = pl.program_id(2)
    for s in range(nq // tps):
        pl.when(step == s)(functools.partial(step_body, s))


def _fox_kernel(q_ref, k_ref, v_ref, c_ref, o_ref, *, tq, nq, npair):
    masks = [_half_mask(tq, h) for h in range(2)]

    def tile(qi, rows):
        for p in range(npair):
            lanes = slice(p * LANES, (p + 1) * LANES)
            q = q_ref[0, rows, lanes]
            values = _value_parts(v_ref, lanes, qi, tq)
            outs = []
            for h in range(2):
                qh = jnp.where(masks[h], q, jnp.zeros_like(q))
                bias = lambda start, size: c_ref[0, 2 * p + h, :, start:start + size] * -LOG2E
                exps, denom = _softmax_parts(_causal_logits(qh, k_ref, lanes, qi, tq, bias))
                outs.append(_weighted_values(exps, values) / denom)
            o_ref[0, rows, lanes] = jnp.where(masks[0], outs[0], outs[1]).astype(o_ref.dtype)

    _dispatch_on_step(tile, nq, tq)


def _diff_kernel(slope_ref, lam_ref, q_ref, k_ref, v_ref, g_ref, o_ref, *,
                 tq, nq, nhead, out_scale):
    lam = lam_ref[0]
    masks = [_half_mask(tq, h) for h in range(2)]

    def tile(qi, rows):
        for j in range(nhead):
            lanes = slice(j * LANES, (j + 1) * LANES)
            slope = slope_ref[pl.program_id(1) * nhead + j] * LOG2E
            q = q_ref[0, rows, lanes]

            def bias(start, size):
                pos = start + lax.broadcasted_iota(jnp.int32, (1, size), 1)
                return slope * pos.astype(F32)

            maps = []
            for h in range(2):
                qh = jnp.where(masks[h], q, jnp.zeros_like(q))
                maps.append(_softmax_parts(_causal_logits(qh, k_ref, lanes, qi, tq, bias)))
            (e0, d0), (e1, d1) = maps
            w0, w1 = 1.0 / d0, lam / d1
            weights = [a * w0 - b * w1 for a, b in zip(e0, e1)]
            o = _weighted_values(weights, _value_parts(v_ref, lanes, qi, tq))
            o_ref[0, rows, lanes] = (_rms(o, g_ref[...]) * out_scale).astype(o_ref.dtype)

    _dispatch_on_step(tile, nq, tq)


def fox_attention(z3, c4, *, q_col, k_col, v_col, tq=256, npair=3):
    b, s, _ = z3.shape
    tq = min(tq, s)
    rows = tq * min(ATTN_TILES_PER_STEP, s // tq)
    w = npair * LANES
    groups = C_HEADS // 2 // npair
    assert q_col % npair == 0 and k_col % npair == 0 and v_col % npair == 0
    return pl.pallas_call(
        functools.partial(_fox_kernel, tq=tq, nq=s // tq, npair=npair),
        out_shape=jax.ShapeDtypeStruct((b, s, C_HEADS // 2 * LANES), BF16),
        grid=(b, groups, s // rows),
        in_specs=[pl.BlockSpec((1, rows, w), lambda bi, g, qi: (bi, qi, q_col // npair + g)),
                  pl.BlockSpec((1, s, w), lambda bi, g, qi: (bi, 0, k_col // npair + g)),
                  pl.BlockSpec((1, s, w), lambda bi, g, qi: (bi, 0, v_col // npair + g)),
                  pl.BlockSpec((1, 2 * npair, 1, s), lambda bi, g, qi: (bi, g, 0, 0))],
        out_specs=pl.BlockSpec((1, rows, w), lambda bi, g, qi: (bi, qi, g)),
        compiler_params=_params("parallel", "parallel", "arbitrary"),
        name="fox_attention",
    )(z3, z3, z3, c4)


def diff_attention(z3, slopes, lam, subln, *, q_col, k_col, v_col, out_scale, tq=512, nhead=2):
    b, s, _ = z3.shape
    tq = min(tq, s)
    rows = tq * min(ATTN_TILES_PER_STEP, s // tq)
    w = nhead * LANES
    assert q_col % nhead == 0 and k_col % nhead == 0 and v_col % nhead == 0
    smem = pl.BlockSpec(memory_space=pltpu.SMEM)
    return pl.pallas_call(
        functools.partial(_diff_kernel, tq=tq, nq=s // tq, nhead=nhead, out_scale=out_scale),
        out_shape=jax.ShapeDtypeStruct((b, s, B_HEADS * LANES), BF16),
        grid=(b, B_HEADS // nhead, s // rows),
        in_specs=[smem, smem,
                  pl.BlockSpec((1, rows, w), lambda bi, g, qi: (bi, qi, q_col // nhead + g)),
                  pl.BlockSpec((1, s, w), lambda bi, g, qi: (bi, 0, k_col // nhead + g)),
                  pl.BlockSpec((1, s, w), lambda bi, g, qi: (bi, 0, v_col // nhead + g)),
                  pl.BlockSpec((1, LANES), lambda bi, g, qi: (0, 0))],
        out_specs=pl.BlockSpec((1, rows, w), lambda bi, g, qi: (bi, qi, g)),
        compiler_params=_params("parallel", "parallel", "arbitrary"),
        name="diff_attention",
    )(slopes, lam, z3, z3, z3, subln.reshape(1, LANES).astype(F32))


def _swa_kernel(slope_ref, sink_ref, q_ref, k_ref, v_ref, o_ref, *, blk, nsub):
    r = lax.broadcasted_iota(jnp.int32, (blk, 2 * blk), 0)
    c = lax.broadcasted_iota(jnp.int32, (blk, 2 * blk), 1)
    dist = r + blk - c
    in_window = (dist >= 0) & (dist < WINDOW)
    dist_f = dist.astype(F32)
    masks = [_half_mask(blk, h) for h in range(2)]
    for sub in range(nsub):
        _swa_block(slope_ref, sink_ref, q_ref, k_ref, v_ref, o_ref, pl.program_id(1) * nsub + sub,
                   slice(sub * blk, (sub + 1) * blk), in_window, c, dist_f, masks, blk)


def _swa_block(slope_ref, sink_ref, q_ref, k_ref, v_ref, o_ref, qi, rows,
               in_window, c, dist_f, masks, blk):
    cur = pl.multiple_of(qi * blk, blk)
    prev = pl.multiple_of(jnp.maximum(qi - 1, 0) * blk, blk)
    keep = in_window & ((qi > 0) | (c >= blk))
    for tile in range(A_Q_HEADS // 2):
        q = q_ref[0, rows, tile * LANES:(tile + 1) * LANES]
        halves = []
        for half in range(2):
            slot = 2 * tile + half
            kv_tile = (A_SLOT_HEAD[slot] // A_GROUP) // 2
            lanes = slice(kv_tile * LANES, (kv_tile + 1) * LANES)
            k = jnp.concatenate([k_ref[0, pl.ds(prev, blk), lanes],
                                 k_ref[0, pl.ds(cur, blk), lanes]], axis=0)
            v = jnp.concatenate([v_ref[0, pl.ds(prev, blk), lanes],
                                 v_ref[0, pl.ds(cur, blk), lanes]], axis=0)
            qh = jnp.where(masks[half], q, jnp.zeros_like(q))
            s = _dot_nt(qh, k) - (slope_ref[slot] * LOG2E) * dist_f
            s = jnp.where(keep, s, NEG_INF)
            sink = sink_ref[slot] * LOG2E
            m = jnp.maximum(jnp.max(s, axis=-1, keepdims=True), sink)
            p = jnp.exp2(s - m)
            denom = jnp.sum(p, axis=-1, keepdims=True) + jnp.exp2(sink - m)
            halves.append(_dot(p.astype(v.dtype), v) / denom)
        o_ref[0, rows, tile * LANES:(tile + 1) * LANES] = jnp.where(
            masks[0], halves[0], halves[1]).astype(o_ref.dtype)


SWA_BLOCKS_PER_STEP = 8


def swa_attention(z3, slopes, sinks, *, q_tile, k_tile, v_tile):
    b, s, _ = z3.shape
    blk = WINDOW
    nsub = min(SWA_BLOCKS_PER_STEP, s // blk)
    qw = A_Q_HEADS * HEAD_DIM
    kw = A_KV_HEADS * HEAD_DIM
    smem = pl.BlockSpec(memory_space=pltpu.SMEM)
    return pl.pallas_call(
        functools.partial(_swa_kernel, blk=blk, nsub=nsub),
        out_shape=jax.ShapeDtypeStruct((b, s, qw), BF16),
        grid=(b, s // (blk * nsub)),
        in_specs=[smem, smem,
                  pl.BlockSpec((1, blk * nsub, qw), lambda bi, qi: (bi, qi, q_tile)),
                  pl.BlockSpec((1, s, kw), lambda bi, qi: (bi, 0, k_tile)),
                  pl.BlockSpec((1, s, kw), lambda bi, qi: (bi, 0, v_tile))],
        out_specs=pl.BlockSpec((1, blk * nsub, qw), lambda bi, qi: (bi, qi, 0)),
        compiler_params=_params("parallel", "parallel"),
        name="swa_attention",
    )(slopes, sinks, z3, z3, z3)


def _merge_out_kernel(oa_ref, ob_ref, oc_ref, ga_ref, gb_ref, gc_ref, wa_ref, wb_ref, wc_ref,
                      wo_ref, x_ref, gp_ref, gn_ref, xo_ref, ho_ref):
    merged = ga_ref[...].astype(F32) * _dot(oa_ref[...], wa_ref[...])
    merged += gb_ref[...].astype(F32) * _dot(ob_ref[...], wb_ref[...])
    merged += gc_ref[...].astype(F32) * _dot(oc_ref[...], wc_ref[...])
    m = _dot(merged.astype(wo_ref.dtype), wo_ref[...])
    x_new = x_ref[...] + _rms(m, gp_ref[...])
    xo_ref[...] = x_new
    ho_ref[...] = _rms(x_new, gn_ref[...]).astype(ho_ref.dtype)


def merge_out_residual(oa, ob, oc, gates, wa, wb, wc, w_out, x, g_post, g_next, *, tm=256):
    t, d = x.shape
    tm = min(tm, t)
    row = lambda w: pl.BlockSpec((tm, w), lambda i: (i, 0))
    gate = lambda br: pl.BlockSpec((tm, d), lambda i: (i, br))
    whole = lambda w: pl.BlockSpec(w.shape, lambda i: (0, 0), pipeline_mode=pl.Buffered(1))
    vec = pl.BlockSpec((1, d), lambda i: (0, 0))
    return pl.pallas_call(
        _merge_out_kernel,
        out_shape=(jax.ShapeDtypeStruct((t, d), F32), jax.ShapeDtypeStruct((t, d), BF16)),
        grid=(t // tm,),
        in_specs=[row(oa.shape[1]), row(ob.shape[1]), row(oc.shape[1]),
                  gate(0), gate(1), gate(2),
                  whole(wa), whole(wb), whole(wc), whole(w_out), row(d), vec, vec],
        out_specs=(row(d), row(d)),
        compiler_params=_params("parallel"),
        name="merge_out_residual",
    )(oa, ob, oc, gates, gates, gates, wa, wb, wc, w_out, x,
      g_post.reshape(1, d), g_next.reshape(1, d))


def _proj_res_kernel(a_ref, w_ref, x_ref, gp_ref, gn_ref, xo_ref, ho_ref, acc_ref, *, nk):
    kk = pl.program_id(1)
    part = _dot(a_ref[...], w_ref[...])

    @pl.when(kk == 0)
    def _():
        acc_ref[...] = part

    @pl.when(kk > 0)
    def _():
        acc_ref[...] += part

    @pl.when(kk == nk - 1)
    def _():
        x_new = x_ref[...] + _rms(acc_ref[...], gp_ref[...])
        xo_ref[...] = x_new
        ho_ref[...] = _rms(x_new, gn_ref[...]).astype(ho_ref.dtype)


def proj_residual(a, w, x, g_post, g_next, *, tm, tk):
    t, k = a.shape
    d = w.shape[1]
    tm, tk = min(tm, t), min(tk, k)
    nk = k // tk
    vec = pl.BlockSpec((1, d), lambda i, kk: (0, 0))
    w_mode = dict(pipeline_mode=pl.Buffered(1)) if nk == 1 else {}
    return pl.pallas_call(
        functools.partial(_proj_res_kernel, nk=nk),
        out_shape=(jax.ShapeDtypeStruct((t, d), F32), jax.ShapeDtypeStruct((t, d), BF16)),
        grid=(t // tm, nk),
        in_specs=[pl.BlockSpec((tm, tk), lambda i, kk: (i, kk)),
                  pl.BlockSpec((tk, d), lambda i, kk: (kk, 0), **w_mode),
                  pl.BlockSpec((tm, d), lambda i, kk: (i, 0)),
                  vec, vec],
        out_specs=(pl.BlockSpec((tm, d), lambda i, kk: (i, 0)),
                   pl.BlockSpec((tm, d), lambda i, kk: (i, 0))),
        scratch_shapes=[pltpu.VMEM((tm, d), F32)],
        compiler_params=_params("parallel", "arbitrary"),
        name="proj_residual",
    )(a, w, x, g_post.reshape(1, d), g_next.reshape(1, d))


def _swiglu_up_kernel(h_ref, wg_ref, wu_ref, o_ref):
    h = h_ref[...]
    g = _dot(h, wg_ref[...])
    u = _dot(h, wu_ref[...])
    o_ref[...] = (g * _sigmoid(g) * u).astype(o_ref.dtype)


def swiglu_up(h, wg, wu, *, tm=1024, tn=512):
    t, d = h.shape
    f = wg.shape[1]
    tm, tn = min(tm, t), min(tn, f)
    return pl.pallas_call(
        _swiglu_up_kernel,
        out_shape=jax.ShapeDtypeStruct((t, f), BF16),
        grid=(t // tm, f // tn),
        in_specs=[pl.BlockSpec((tm, d), lambda i, j: (i, 0)),
                  pl.BlockSpec((d, tn), lambda i, j: (0, j)),
                  pl.BlockSpec((d, tn), lambda i, j: (0, j))],
        out_specs=pl.BlockSpec((tm, tn), lambda i, j: (i, j)),
        compiler_params=_params("parallel", "parallel"),
        name="swiglu_up",
    )(h, wg, wu)


def _router_kernel(x_ref, g_ref, w_ref, idx_ref, wt_ref, h_ref):
    h = _rms(x_ref[...], g_ref[...])
    h_ref[...] = _pack_bf16_pairs(h)
    w = w_ref[...]
    h_hi, w_hi = h.astype(BF16), w.astype(BF16)
    h_lo = (h - h_hi.astype(F32)).astype(BF16)
    w_lo = (w - w_hi.astype(F32)).astype(BF16)
    logits = _dot(h_hi, w_hi) + (_dot(h_hi, w_lo) + _dot(h_lo, w_hi))
    lane = lax.broadcasted_iota(jnp.int32, logits.shape, 1)
    lane_f = lane.astype(F32)
    lg = jnp.where(lane < N_EXPERTS, logits, -jnp.inf)
    m1 = jnp.max(lg, axis=-1, keepdims=True)
    i1 = jnp.min(jnp.where(lg == m1, lane_f, float(LANES)), axis=-1, keepdims=True)
    lg2 = jnp.where(lane_f == i1, -jnp.inf, lg)
    m2 = jnp.max(lg2, axis=-1, keepdims=True)
    i2 = jnp.min(jnp.where(lg2 == m2, lane_f, float(LANES)), axis=-1, keepdims=True)
    e2 = jnp.exp(m2 - m1)
    w1 = 1.0 / (1.0 + e2)
    w2 = e2 / (1.0 + e2)
    idx_ref[...] = jnp.where(lane == 0, i1, jnp.where(lane == 1, i2, 0.0)).astype(jnp.int32)
    wt_ref[...] = jnp.where(lane == 0, w1, jnp.where(lane == 1, w2, 0.0))


def router_top2(x, g, w_router, *, tm=512):
    t, d = x.shape
    tm = min(tm, t)
    wp = jnp.zeros((d, LANES), F32).at[:, :N_EXPERTS].set(w_router.astype(F32))
    out = pl.BlockSpec((tm, LANES), lambda i: (i, 0))
    idx, wt, h = pl.pallas_call(
        _router_kernel,
        out_shape=(jax.ShapeDtypeStruct((t, LANES), jnp.int32),
                   jax.ShapeDtypeStruct((t, LANES), F32),
                   jax.ShapeDtypeStruct((t, d // 2), jnp.uint32)),
        grid=(t // tm,),
        in_specs=[pl.BlockSpec((tm, d), lambda i: (i, 0)),
                  pl.BlockSpec((1, d), lambda i: (0, 0)),
                  pl.BlockSpec((d, LANES), lambda i: (0, 0))],
        out_specs=(out, out, pl.BlockSpec((tm, d // 2), lambda i: (i, 0))),
        compiler_params=_params("parallel"),
        name="router_top2",
    )(x, g.reshape(1, d), wp)
    return idx[:, :TOP_K], wt[:, :TOP_K], h


SC_INDEX_WINDOW = 128
SC_COPY_BYTES = 256 * 1024


def sc_gather_rows(src, idx):
    n = idx.shape[0]
    d = src.shape[1]
    batch = min(SC_INDEX_WINDOW, SC_COPY_BYTES // (d * src.dtype.itemsize))
    assert n % SC_INDEX_WINDOW == 0 and SC_INDEX_WINDOW % batch == 0
    mesh = plsc.VectorSubcoreMesh(core_axis_name="c", subcore_axis_name="s")
    dst = jnp.arange(n, dtype=jnp.int32).reshape(1, n)

    @pl.kernel(out_type=jax.ShapeDtypeStruct((n, d), src.dtype), mesh=mesh,
               scratch_types=[pltpu.VMEM((batch, d), src.dtype)])
    def gather(x_hbm, i_hbm, d_hbm, o_hbm, buf):
        def body(i_vmem, d_vmem):
            for s in range(SC_INDEX_WINDOW // batch):
                rows = pl.ds(s * batch, batch)
                pltpu.sync_copy(x_hbm.at[i_vmem.at[0, rows]], buf)
                pltpu.sync_copy(buf, o_hbm.at[d_vmem.at[0, rows]])

        window = pl.BlockSpec((1, SC_INDEX_WINDOW), lambda i: (0, i))
        pltpu.emit_pipeline(
            body, grid=(n // SC_INDEX_WINDOW,),
            in_specs=[window, window], out_specs=[],
            core_axis_name=("c", "s"),
            dimension_semantics=(pltpu.PARALLEL,),
        )(i_hbm, d_hbm)

    return gather(src, idx.reshape(1, n), dst)


def _moe_up_kernel(te_ref, na_ref, x_ref, wg_ref, wu_ref, *rest):
    o_ref = rest[-1]
    active = pl.program_id(1) < na_ref[0]

    @pl.when(active)
    def _():
        half = x_ref.shape[1]
        lo, hi = (v.astype(BF16) for v in _unpack_bf16_pairs(x_ref[...]))
        g = _dot(lo, wg_ref[0, :half]) + _dot(hi, wg_ref[0, half:])
        u = _dot(lo, wu_ref[0, :half]) + _dot(hi, wu_ref[0, half:])
        o_ref[...] = (g * _sigmoid(g) * u).astype(o_ref.dtype)

    @pl.when(jnp.logical_not(active))
    def _():
        o_ref[...] = jnp.zeros_like(o_ref)


def _moe_down_kernel(te_ref, na_ref, u_ref, wd_ref, o_ref):
    active = pl.program_id(1) < na_ref[0]

    @pl.when(active)
    def _():
        o_ref[...] = _pack_bf16_pairs(_dot(u_ref[...], wd_ref[0]))

    @pl.when(jnp.logical_not(active))
    def _():
        o_ref[...] = jnp.zeros_like(o_ref)


def _active_tile(i, na_ref):
    return jnp.maximum(jnp.minimum(i, na_ref[0] - 1), 0)


def moe_up(xs, tile_expert, n_active, wg, wu, u_prev, *, tile0, n_tiles, tm, tn=1024):
    d, f = wg.shape[1:]
    tn = min(tn, f)
    assert f % tn == 0
    in_specs = [pl.BlockSpec((tm, d // 2), lambda j, i, te, na: (_active_tile(i, na), 0)),
                pl.BlockSpec((1, d, tn), lambda j, i, te, na: (te[_active_tile(i, na)], 0, j)),
                pl.BlockSpec((1, d, tn), lambda j, i, te, na: (te[_active_tile(i, na)], 0, j))]
    args = [tile_expert, n_active, xs, wg, wu]
    aliases = {}
    if u_prev is not None:
        in_specs.append(pl.BlockSpec(memory_space=pl.ANY))
        args.append(u_prev)
        aliases = {len(args) - 1: 0}
    grid_spec = pltpu.PrefetchScalarGridSpec(
        num_scalar_prefetch=2,
        grid=(f // tn, xs.shape[0] // tm),
        in_specs=in_specs,
        out_specs=pl.BlockSpec((tm, tn), lambda j, i, te, na: (i + tile0, j)),
    )
    return pl.pallas_call(
        _moe_up_kernel,
        out_shape=jax.ShapeDtypeStruct((n_tiles * tm, f), BF16),
        grid_spec=grid_spec,
        input_output_aliases=aliases,
        compiler_params=_params("arbitrary", "arbitrary"),
        name="moe_up",
    )(*args)


MOE_DOWN_TN = 512


def moe_down(u, tile_expert, n_active, wd, *, tm):
    p, f = u.shape
    d = wd.shape[2]
    tn = min(MOE_DOWN_TN, d)
    grid_spec = pltpu.PrefetchScalarGridSpec(
        num_scalar_prefetch=2,
        grid=(d // tn, p // tm),
        in_specs=[pl.BlockSpec((tm, f), lambda j, i, te, na: (_active_tile(i, na), 0)),
                  pl.BlockSpec((1, f, tn), lambda j, i, te, na: (te[_active_tile(i, na)], 0, j))],
        out_specs=pl.BlockSpec((tm, tn // 2), lambda j, i, te, na: (i, j)),
    )
    return pl.pallas_call(
        _moe_down_kernel,
        out_shape=jax.ShapeDtypeStruct((p, d // 2), jnp.uint32),
        grid_spec=grid_spec,
        compiler_params=_params("arbitrary", "arbitrary"),
        name="moe_down",
    )(tile_expert, n_active, u, wd)


def _unpack_expert_rows(words, half):
    lo, hi = _unpack_bf16_pairs(words)
    parts = []
    for b in range(words.shape[1] // half):
        parts += [lo[:, b * half:(b + 1) * half], hi[:, b * half:(b + 1) * half]]
    return jnp.concatenate(parts, axis=1)


def _combine_kernel(ya_ref, yb_ref, w_ref, x_ref, g_ref, o_ref, *, half):
    w = w_ref[...]
    y = (w[:, 0:1] * _unpack_expert_rows(ya_ref[...], half)
         + w[:, 1:2] * _unpack_expert_rows(yb_ref[...], half))
    o_ref[...] = x_ref[...] + _rms(y, g_ref[...])


def combine_residual(y2, wts, x, g, *, tm=512):
    t, d = x.shape
    tm = min(tm, t)
    nt = t // tm
    blk = pl.BlockSpec((tm, d), lambda i: (i, 0))
    return pl.pallas_call(
        functools.partial(_combine_kernel, half=min(MOE_DOWN_TN, d) // 2),
        out_shape=jax.ShapeDtypeStruct((t, d), F32),
        grid=(nt,),
        in_specs=[pl.BlockSpec((tm, d // 2), lambda i: (i, 0)),
                  pl.BlockSpec((tm, d // 2), lambda i: (i + nt, 0)),
                  pl.BlockSpec((tm, TOP_K), lambda i: (i, 0)), blk,
                  pl.BlockSpec((1, d), lambda i: (0, 0))],
        out_specs=blk,
        compiler_params=_params("parallel"),
        name="combine_residual",
    )(y2, y2, wts, x, g.reshape(1, d))


def _alibi_slopes(n):
    return jnp.exp2(-8.0 * (jnp.arange(n, dtype=F32) + 1.0) / n)


def _token_mixer(x2, h, batch, seq, layer_idx, w_in, b_forget, b_gate, sinks,
                 lq1, lk1, lq2, lk2, subln, w_br_a, w_br_b, w_br_c, w_out,
                 g_post, g_next):
    t, d = x2.shape
    a_w = A_Q_HEADS * HEAD_DIM
    kv_w = A_KV_HEADS * HEAD_DIM
    b_w = B_HEADS * 2 * HEAD_DIM
    c_w = C_HEADS * HEAD_DIM
    n_attn = a_w + 2 * kv_w + 3 * b_w + 3 * c_w

    slot_cols = np.concatenate([np.arange(HEAD_DIM) + HEAD_DIM * hd for hd in A_SLOT_HEAD])
    a_end = a_w + 2 * kv_w
    b_end = a_end + 3 * b_w
    w_in_t = jnp.transpose(w_in, (2, 0, 1))
    rows = lambda lo, hi: w_in_t[lo:hi, layer_idx, :]
    w_attn_t = jnp.concatenate([rows(b_end, n_attn), rows(a_end, b_end),
                                rows(0, a_w)[slot_cols], rows(a_w, a_end)],
                               axis=0).astype(BF16)
    w_f_t = rows(n_attn, n_attn + C_HEADS)
    w_g_t = rows(n_attn + C_HEADS, w_in.shape[2]).astype(BF16)

    q_cols = np.zeros((n_attn,), bool)
    for lo, width in ((0, c_w), (3 * c_w, b_w), (3 * c_w + 3 * b_w, a_w)):
        q_cols[lo:lo + width] = True
    z3 = matmul(h, w_attn_t, col_scale=jnp.where(q_cols, Q_SCALE, 1.0)).reshape(batch, seq, n_attn)
    gates = matmul(h, w_g_t, bias=b_gate)
    c = forget_cumsum(h.reshape(batch, seq, d), w_f_t, b_forget)
    c4 = c[:, :C_HEADS].reshape(batch, C_HEADS, 1, seq)

    oc = fox_attention(z3, c4, q_col=0, k_col=C_HEADS // 2, v_col=C_HEADS)

    lam_init = 0.8 - 0.6 * math.exp(-0.3 * layer_idx)
    lam = (jnp.exp(jnp.sum(lq1.astype(F32) * lk1.astype(F32)))
           - jnp.exp(jnp.sum(lq2.astype(F32) * lk2.astype(F32))) + lam_init)
    b0 = 3 * c_w // LANES
    ob = diff_attention(z3, _alibi_slopes(B_HEADS), lam.reshape(1), subln,
                        q_col=b0, k_col=b0 + B_HEADS, v_col=b0 + 2 * B_HEADS,
                        out_scale=1.0 - lam_init)

    slot = np.array(A_SLOT_HEAD)
    a0 = 3 * c_w + 3 * b_w
    oa = swa_attention(z3, _alibi_slopes(A_Q_HEADS)[slot], sinks.astype(F32)[slot],
                       q_tile=a0 // a_w, k_tile=(a0 + a_w) // kv_w,
                       v_tile=(a0 + a_w) // kv_w + 1)

    return merge_out_residual(oa.reshape(t, a_w), ob.reshape(t, b_w), oc.reshape(t, c_w), gates,
                              w_br_a[slot_cols].astype(BF16), w_br_b.astype(BF16),
                              w_br_c.astype(BF16), w_out.astype(BF16), x2, g_post, g_next)


COUNT_BLOCK = 256
MOE_PIECES = 4


def _moe_dispatch(idx, tm):
    t = idx.shape[0]
    n_rows = t * TOP_K
    n_tiles = (n_rows + N_EXPERTS * (tm - 1)) // tm
    e_flat = idx.reshape(n_rows)
    onehot = e_flat[:, None] == jnp.arange(N_EXPERTS)[None, :]
    blk = min(COUNT_BLOCK, n_rows)
    oh = onehot.astype(F32).reshape(n_rows // blk, blk, N_EXPERTS)
    lower = (jnp.arange(blk)[:, None] >= jnp.arange(blk)[None, :]).astype(F32)
    within = jnp.einsum("ij,bjk->bik", lower, oh)
    before = jnp.cumsum(within[:, -1, :], axis=0) - within[:, -1, :]
    running = (within + before[:, None, :]).reshape(n_rows, N_EXPERTS)
    rank = jnp.sum(jnp.where(onehot, running - 1.0, 0.0), axis=1).astype(jnp.int32)
    counts = jnp.sum(onehot.astype(jnp.int32), axis=0)
    padded = ((counts + tm - 1) // tm) * tm
    ends = jnp.cumsum(padded)
    starts = ends - padded
    pos = starts[e_flat] + rank
    row_token = jnp.zeros((n_tiles * tm,), jnp.int32).at[pos].set(
        jnp.arange(n_rows, dtype=jnp.int32) // TOP_K, unique_indices=True)
    tile_start = jnp.arange(n_tiles, dtype=jnp.int32) * tm
    tile_expert = jnp.minimum(jnp.sum(tile_start[:, None] >= ends[None, :], axis=1),
                              N_EXPERTS - 1).astype(jnp.int32)
    n_active = (ends[-1] // tm).astype(jnp.int32).reshape(1)
    return pos.reshape(t, TOP_K), row_token, tile_expert, n_active


def _moe_ffn(x2, g_pre, g_post, w_router, e_gate, e_up, e_down, *, tm=512):
    idx, wts, h = router_top2(x2, g_pre, w_router)
    t = x2.shape[0]
    tm = min(tm, t)
    pos, row_token, tile_expert, n_active = _moe_dispatch(idx, tm)
    n_tiles = tile_expert.shape[0]
    bounds = [k * n_tiles // MOE_PIECES for k in range(MOE_PIECES + 1)]
    u = None
    for lo, hi in zip(bounds[:-1], bounds[1:]):
        xs = sc_gather_rows(h, row_token[lo * tm:hi * tm])
        u = moe_up(xs, tile_expert[lo:hi], jnp.clip(n_active - lo, 0, hi - lo), e_gate, e_up, u,
                   tile0=lo, n_tiles=n_tiles, tm=tm)
    ys = moe_down(u, tile_expert, n_active, e_down, tm=tm)
    y2 = sc_gather_rows(ys, pos.T.reshape(TOP_K * t))
    return combine_residual(y2, wts, x2, g_post)


def kernel(x, mix_pre_norm, w_in, b_forget, b_gate, attn_sinks, lam_q1, lam_k1, lam_q2, lam_k2, diff_subln, w_br_a, w_br_b, w_br_c, w_out, mix_post_norm, ffn_pre_norm, ffn_post_norm, dense_w_gate, dense_w_up, dense_w_down, w_router, moe_w_gate, moe_w_up, moe_w_down):
    batch, seq, d = x.shape
    depth = w_in.shape[0]
    x2 = x.reshape(batch * seq, d)
    h = rmsnorm_rows(x2, mix_pre_norm[0])
    for l in range(depth):
        x2, h = _token_mixer(x2, h, batch, seq, l, w_in, b_forget[l], b_gate[l],
                             attn_sinks[l], lam_q1[l], lam_k1[l], lam_q2[l], lam_k2[l],
                             diff_subln[l], w_br_a[l], w_br_b[l], w_br_c[l], w_out[l],
                             mix_post_norm[l], ffn_pre_norm[l])
        i = l // 2
        if l % 2 == 0:
            u = swiglu_up(h, dense_w_gate[i].astype(BF16), dense_w_up[i].astype(BF16))
            g_next = mix_pre_norm[l + 1] if l + 1 < depth else jnp.ones((d,), F32)
            f = dense_w_down.shape[1]
            x2, h = proj_residual(u, dense_w_down[i].astype(BF16), x2, ffn_post_norm[l],
                                  g_next, tm=256, tk=f)
        else:
            x2 = _moe_ffn(x2, ffn_pre_norm[l], ffn_post_norm[l], w_router[i],
                          moe_w_gate[i], moe_w_up[i], moe_w_down[i])
            if l + 1 < depth:
                h = rmsnorm_rows(x2, mix_pre_norm[l + 1])
    return x2.reshape(batch, seq, d)
```
